```python
import jax, jax.numpy as jnp
from jax import lax
import numpy as np

D_MODEL = 1024
BATCH = 4
SEQ = 4096
DEPTH = 2

D_MIX = 1024
RWKV_HEAD_DIM = 64
RWKV_HEADS = 6
RWKV_DIM = 384
W_LORA = 32
A_LORA = 32
G_LORA = 64
RWKV_GN_EPS = 64e-5
CONV_DIM = 256
CONV_WIDTH = 31
LN_EPS = 1e-5
MLA_HEADS = 6
QK_NOPE_DIM = 64
QK_ROPE_DIM = 32
V_HEAD_DIM = 64
MLA_DIM = 384
Q_LORA = 256
KV_LORA = 256
ROPE_THETA = 10000.0
Q_BLOCK = 128
D_FF = 4 * D_MODEL
NORM_EPS = 1e-6

P_RWKV = 3 * RWKV_DIM + W_LORA + A_LORA + G_LORA
P_CONV = 2 * CONV_DIM
P_MLA = Q_LORA + KV_LORA + QK_ROPE_DIM
P_IN = P_RWKV + P_CONV + P_MLA

kernel_name = "hybrid_rwkv7_conformer_mla_block"


def rms_norm(x, g):
    xf = x.astype(jnp.float32)
    y = xf * lax.rsqrt(jnp.mean(xf * xf, axis=-1, keepdims=True) + NORM_EPS)
    return (y * g.astype(jnp.float32)).astype(x.dtype)


def layer_norm(x, w, b, eps):
    xf = x.astype(jnp.float32)
    mu = jnp.mean(xf, axis=-1, keepdims=True)
    var = jnp.mean(jnp.square(xf - mu), axis=-1, keepdims=True)
    y = (xf - mu) * lax.rsqrt(var + eps)
    return (y * w.astype(jnp.float32) + b.astype(jnp.float32)).astype(x.dtype)


def rwkv7_step(S, inp):
    r_t, w_t, k_t, v_t, a_t, b_t = inp
    Sa = jnp.einsum('bhvk,bhk->bhv', S, a_t)
    S = S * w_t[:, :, None, :] + Sa[..., None] * b_t[:, :, None, :] + v_t[..., None] * k_t[:, :, None, :]
    y = jnp.einsum('bhvk,bhk->bhv', S, r_t)
    return S, y


def rwkv7_time_mix(p, mu, w0, w2, a0, a2, g2, k_k, k_a, r_k, ln_w, ln_b):
    B, T, _ = p.shape
    H, N = RWKV_HEADS, RWKV_HEAD_DIM
    prev = jnp.pad(p, ((0, 0), (1, 0), (0, 0)))[:, :T]
    z = p + (prev - p) * mu
    cuts = [RWKV_DIM, 2 * RWKV_DIM, 3 * RWKV_DIM, 3 * RWKV_DIM + W_LORA, 3 * RWKV_DIM + W_LORA + A_LORA]
    r, k, v, w_lo, a_lo, g_lo = jnp.split(z, cuts, axis=-1)
    w = -jax.nn.softplus(-(w0 + jnp.tanh(w_lo) @ w2)) - 0.5
    a = jax.nn.sigmoid(a0 + a_lo @ a2)
    g = jax.nn.sigmoid(g_lo) @ g2
    kk = (k * k_k).reshape(B, T, H, N).astype(jnp.float32)
    kk = kk * lax.rsqrt(jnp.maximum(jnp.sum(kk * kk, axis=-1, keepdims=True), 1e-24))
    k = k * (1 + (a - 1) * k_a)

    def heads(t):
        return t.reshape(B, T, H, N).astype(jnp.float32)

    r_h, k_h, v_h, a_h = heads(r), heads(k), heads(v), heads(a)
    decay = jnp.exp(-jnp.exp(heads(w)))

    def tmaj(t):
        return jnp.moveaxis(t, 1, 0)

    xs = (tmaj(r_h), tmaj(decay), tmaj(k_h), tmaj(v_h), tmaj(-kk), tmaj(kk * a_h))
    S0 = jnp.zeros((B, H, N, N), jnp.float32)
    _, y = lax.scan(rwkv7_step, S0, xs)
    y = jnp.moveaxis(y, 0, 1)
    m = jnp.mean(y, axis=-1, keepdims=True)
    var = jnp.mean(jnp.square(y - m), axis=-1, keepdims=True)
    y = ((y - m) * lax.rsqrt(var + RWKV_GN_EPS)).reshape(B, T, RWKV_DIM)
    y = y * ln_w.astype(jnp.float32) + ln_b.astype(jnp.float32)
    bonus = jnp.sum(r_h * k_h * r_k.astype(jnp.float32), axis=-1, keepdims=True) * v_h
    y = y + bonus.reshape(B, T, RWKV_DIM)
    return y.astype(p.dtype) * g


def conformer_conv(p, conv_w, conv_b, ln_w, ln_b):
    u, gt = jnp.split(p, 2, axis=-1)
    h = u * jax.nn.sigmoid(gt)
    h = lax.conv_general_dilated(
        h, conv_w[:, None, :], window_strides=(1,), padding=((CONV_WIDTH - 1, 0),),
        dimension_numbers=('NWC', 'WIO', 'NWC'), feature_group_count=CONV_DIM) + conv_b
    h = layer_norm(h, ln_w, ln_b, LN_EPS)
    return jax.nn.silu(h)


def rope_cos_sin(positions):
    inv_freq = 1.0 / (ROPE_THETA ** (jnp.arange(0, QK_ROPE_DIM, 2, dtype=jnp.float32) / QK_ROPE_DIM))
    ang = positions.astype(jnp.float32)[..., None] * inv_freq
    ang = jnp.concatenate([ang, ang], axis=-1)
    return jnp.cos(ang), jnp.sin(ang)


def apply_rope(x, cos, sin):
    xf = x.astype(jnp.float32)
    x1, x2 = jnp.split(xf, 2, axis=-1)
    rot = jnp.concatenate([-x2, x1], axis=-1)
    return (xf * cos + rot * sin).astype(x.dtype)


def mla_attention(p, positions, q_norm, w_uq, kv_norm, w_ukv):
    B, T, _ = p.shape
    H = MLA_HEADS
    q_c, c_kv, k_pe = jnp.split(p, [Q_LORA, Q_LORA + KV_LORA], axis=-1)
    q = (rms_norm(q_c, q_norm) @ w_uq).reshape(B, T, H, QK_NOPE_DIM + QK_ROPE_DIM)
    q_nope, q_pe = q[..., :QK_NOPE_DIM], q[..., QK_NOPE_DIM:]
    kv = (rms_norm(c_kv, kv_norm) @ w_ukv).reshape(B, T, H, QK_NOPE_DIM + V_HEAD_DIM)
    k_nope, v = kv[..., :QK_NOPE_DIM], kv[..., QK_NOPE_DIM:]
    cos, sin = rope_cos_sin(positions)
    q_pe = apply_rope(q_pe, cos[:, :, None, :], sin[:, :, None, :])
    k_pe = apply_rope(k_pe, cos, sin)
    q = jnp.concatenate([q_nope, q_pe], axis=-1).transpose(0, 2, 1, 3)
    k = jnp.concatenate([k_nope, jnp.broadcast_to(k_pe[:, :, None, :], (B, T, H, QK_ROPE_DIM))],
                        axis=-1).transpose(0, 2, 1, 3)
    v = v.transpose(0, 2, 1, 3)
    scale = (QK_NOPE_DIM + QK_ROPE_DIM) ** -0.5
    outs = []
    for i in range(T // Q_BLOCK):
        qs, ke = i * Q_BLOCK, (i + 1) * Q_BLOCK
        s = jnp.einsum('bhqd,bhkd->bhqk', q[:, :, qs:ke], k[:, :, :ke]).astype(jnp.float32) * scale
        mask = (qs + jnp.arange(Q_BLOCK))[:, None] >= jnp.arange(ke)[None, :]
        s = jnp.where(mask, s, -jnp.inf)
        pr = jax.nn.softmax(s, axis=-1).astype(v.dtype)
        outs.append(jnp.einsum('bhqk,bhkd->bhqd', pr, v[:, :, :ke]))
    o = jnp.concatenate(outs, axis=2)
    return o.transpose(0, 2, 1, 3).reshape(B, T, MLA_DIM)


def setup_inputs(seed: int = 0) -> dict:
    key = jax.random.key(seed)
    keys = iter(jax.random.split(key, 40))
    L, D = DEPTH, D_MODEL
    f32 = jnp.float32

    def normal(shape, scale):
        return scale * jax.random.normal(next(keys), shape, f32)

    def gain(shape):
        return 1.0 + normal(shape, 0.05)

    def uniform(shape, lo, hi):
        return jax.random.uniform(next(keys), shape, f32, lo, hi)

    x = normal((BATCH, SEQ, D), 1.0)
    c = normal((BATCH, D), 1.0)
    offset = jax.random.randint(next(keys), (BATCH, 1), 0, 1024, dtype=jnp.int32)
    positions = offset + jnp.arange(SEQ, dtype=jnp.int32)[None, :]
    return {
        "x": x,
        "c": c,
        "positions": positions,
        "g_pre_mix": gain((L, D)),
        "g_post_mix": gain((L, D)),
        "g_pre_ffn": gain((L, D)),
        "g_post_ffn": gain((L, D)),
        "w_ada": normal((L, D, 6 * D), 0.5 * D ** -0.5),
        "b_ada": normal((L, 6 * D), 0.02),
        "w_in": normal((L, D, P_IN), D ** -0.5),
        "w_out": normal((L, D_MIX, D), D_MIX ** -0.5),
        "rwkv_mu": uniform((L, P_RWKV), 0.0, 1.0),
        "rwkv_w0": uniform((L, RWKV_DIM), -6.0, -1.0),
        "rwkv_w2": normal((L, W_LORA, RWKV_DIM), W_LORA ** -0.5),
        "rwkv_a0": normal((L, RWKV_DIM), 0.5),
        "rwkv_a2": normal((L, A_LORA, RWKV_DIM), A_LORA ** -0.5),
        "rwkv_g2": normal((L, G_LORA, RWKV_DIM), G_LORA ** -0.5),
        "rwkv_k_k": 0.85 + normal((L, RWKV_DIM), 0.05),
        "rwkv_k_a": gain((L, RWKV_DIM)),
        "rwkv_r_k": normal((L, RWKV_HEADS, RWKV_HEAD_DIM), 0.1),
        "rwkv_ln_w": gain((L, RWKV_DIM)),
        "rwkv_ln_b": normal((L, RWKV_DIM), 0.02),
        "conv_w": normal((L, CONV_WIDTH, CONV_DIM), CONV_WIDTH ** -0.5),
        "conv_b": normal((L, CONV_DIM), 0.02),
        "conv_ln_w": gain((L, CONV_DIM)),
        "conv_ln_b": normal((L, CONV_DIM), 0.02),
        "mla_q_norm": gain((L, Q_LORA)),
        "mla_w_uq": normal((L, Q_LORA, MLA_HEADS * (QK_NOPE_DIM + QK_ROPE_DIM)), Q_LORA ** -0.5),
        "mla_kv_norm": gain((L, KV_LORA)),
        "mla_w_ukv": normal((L, KV_LORA, MLA_HEADS * (QK_NOPE_DIM + V_HEAD_DIM)), KV_LORA ** -0.5),
        "w_ff1": normal((L, D, D_FF), D ** -0.5),
        "w_ff2": normal((L, D_FF, D), D_FF ** -0.5),
    }


def reference(x, c, positions, g_pre_mix, g_post_mix, g_pre_ffn, g_post_ffn, w_ada, b_ada,
              w_in, w_out, rwkv_mu, rwkv_w0, rwkv_w2, rwkv_a0, rwkv_a2, rwkv_g2, rwkv_k_k,
              rwkv_k_a, rwkv_r_k, rwkv_ln_w, rwkv_ln_b, conv_w, conv_b, conv_ln_w, conv_ln_b,
              mla_q_norm, mla_w_uq, mla_kv_norm, mla_w_ukv, w_ff1, w_ff2):
    cs = jax.nn.silu(c)
    for l in range(DEPTH):
        mod = (cs @ w_ada[l] + b_ada[l])[:, None, :]
        sh1, sc1, gt1, sh2, sc2, gt2 = jnp.split(mod, 6, axis=-1)

        h = rms_norm(x, g_pre_mix[l]) * (1 + sc1) + sh1
        p = h @ w_in[l]
        p_rwkv, p_conv, p_mla = jnp.split(p, [P_RWKV, P_RWKV + P_CONV], axis=-1)
        y_a = rwkv7_time_mix(p_rwkv, rwkv_mu[l], rwkv_w0[l], rwkv_w2[l], rwkv_a0[l], rwkv_a2[l],
                             rwkv_g2[l], rwkv_k_k[l], rwkv_k_a[l], rwkv_r_k[l], rwkv_ln_w[l], rwkv_ln_b[l])
        y_b = conformer_conv(p_conv, conv_w[l], conv_b[l], conv_ln_w[l], conv_ln_b[l])
        y_c = mla_attention(p_mla, positions, mla_q_norm[l], mla_w_uq[l], mla_kv_norm[l], mla_w_ukv[l])
        y = jnp.concatenate([y_a, y_b, y_c], axis=-1) @ w_out[l]
        x = x + gt1 * rms_norm(y, g_post_mix[l])

        h = rms_norm(x, g_pre_ffn[l]) * (1 + sc2) + sh2
        y = jnp.square(jax.nn.relu(h @ w_ff1[l])) @ w_ff2[l]
        x = x + gt2 * rms_norm(y, g_post_ffn[l])
    return x
```

```python
import functools

import jax
import jax.numpy as jnp
from jax import lax
from jax.experimental import pallas as pl
from jax.experimental.pallas import tpu as pltpu

F32 = jnp.float32
BF16 = jnp.bfloat16
HIGHEST = lax.Precision.HIGHEST

LANES = 128
HEAD = 64
NORM_EPS = 1e-6
LN_EPS = 1e-5
RWKV_GN_EPS = 64e-5
ROPE_THETA = 10000.0
CONV_WIDTH = 31
CONV_HALO = 32
CHUNK = 64
VMEM_LIMIT = 56 * 1024 * 1024


def _dot(a, b, precision=None):
    return jnp.dot(a, b, preferred_element_type=F32, precision=precision)


def _dot_nt(a, b, precision=None):
    return lax.dot_general(a, b, (((1,), (1,)), ((), ())), preferred_element_type=F32, precision=precision)


def _rms(x):
    return x * lax.rsqrt(jnp.mean(x * x, axis=-1, keepdims=True) + NORM_EPS)


def _sigmoid(x):
    return 1.0 / (1.0 + jnp.exp(-x))


def _params(*semantics):
    return pltpu.CompilerParams(dimension_semantics=semantics, vmem_limit_bytes=VMEM_LIMIT)


def _ada_kernel(c_ref, w_ref, b_ref, o_ref):
    c = c_ref[...]
    cs = c * _sigmoid(c)
    o_ref[0] = _dot(cs, w_ref[0], HIGHEST) + b_ref[0]


def _ada(c8, w_ada, b_ada):
    n_layers, d, d6 = w_ada.shape
    tn = 1536
    return pl.pallas_call(
        _ada_kernel,
        grid=(n_layers, d6 // tn),
        in_specs=[pl.BlockSpec((8, d), lambda l, j: (0, 0)),
                  pl.BlockSpec((1, d, tn), lambda l, j: (l, 0, j)),
                  pl.BlockSpec((1, 1, tn), lambda l, j: (l, 0, j))],
        out_specs=pl.BlockSpec((1, 8, tn), lambda l, j: (l, 0, j)),
        out_shape=jax.ShapeDtypeStruct((n_layers, 8, d6), F32),
        compiler_params=_params("parallel", "parallel"),
        name="ada_mod",
    )(c8, w_ada, b_ada)


def _rope_kernel(pos_ref, invf_ref, cos_ref, sin_ref):
    ang = pos_ref[...].astype(F32) * invf_ref[...]
    cos_ref[...] = jnp.cos(ang)
    sin_ref[...] = jnp.sin(ang)


def _rope_tables(pos_col, invf):
    n = pos_col.shape[0]
    tm = 2048
    spec = pl.BlockSpec((tm, LANES), lambda i: (i, 0))
    return pl.pallas_call(
        _rope_kernel,
        grid=(n // tm,),
        in_specs=[pl.BlockSpec((tm, 1), lambda i: (i, 0)), pl.BlockSpec((1, LANES), lambda i: (0, 0))],
        out_specs=[spec, spec],
        out_shape=[jax.ShapeDtypeStruct((n, LANES), F32)] * 2,
        compiler_params=_params("parallel"),
        name="rope_tables",
    )(pos_col, invf)


def _inproj_kernel(x_ref, mod_ref, g_ref, w_ref, pr_ref, pc_ref, pm_ref, *, n_r, n_c):
    h = _rms(x_ref[...]) * g_ref[...]
    h = (h * (1.0 + mod_ref[0, 1:2, :]) + mod_ref[0, 0:1, :]).astype(BF16)
    pr_ref[...] = _dot(h, w_ref[:, :n_r])
    pc_ref[...] = _dot(h, w_ref[:, n_r:n_r + n_c])
    pm_ref[...] = _dot(h, w_ref[:, n_r + n_c:])


def _inproj(x, mod, g, w_ext, seq, n_r, n_c, n_m):
    n, d = x.shape
    tm = 512
    per_b = seq // tm
    return pl.pallas_call(
        functools.partial(_inproj_kernel, n_r=n_r, n_c=n_c),
        grid=(n // tm,),
        in_specs=[pl.BlockSpec((tm, d), lambda i: (i, 0)),
                  pl.BlockSpec((1, 6, d), lambda i: (i // per_b, 0, 0)),
                  pl.BlockSpec((1, d), lambda i: (0, 0)),
                  pl.BlockSpec(w_ext.shape, lambda i: (0, 0))],
        out_specs=[pl.BlockSpec((tm, n_r), lambda i: (i, 0)),
                   pl.BlockSpec((tm, n_c), lambda i: (i, 0)),
                   pl.BlockSpec((tm, n_m), lambda i: (i, 0))],
        out_shape=[jax.ShapeDtypeStruct((n, n_r), F32),
                   jax.ShapeDtypeStruct((n, n_c), F32),
                   jax.ShapeDtypeStruct((n, n_m), F32)],
        compiler_params=_params("parallel"),
        name="in_proj",
    )(x, mod, g, w_ext)


def _rwkv_kernel(p_ref, mu_ref, w0_ref, a0_ref, kk_ref, ka_ref, rk_ref, lnw_ref, lnb_ref, lora_ref, bd_ref,
                 o_ref, zbuf, st_ref, r_s, k_s, v_s, a_s, b_s, ld_s, y_s, *, tt, dim):
    n_pairs = dim // LANES
    c = CHUNK

    @pl.when(pl.program_id(1) == 0)
    def _():
        zbuf[0:8, :] = jnp.zeros((8, zbuf.shape[1]), F32)
        st_ref[...] = jnp.zeros(st_ref.shape, F32)

    p = p_ref[...]
    zbuf[8:8 + tt, :] = p
    prev = zbuf[7:7 + tt, :]
    zbuf[7:8, :] = p_ref[tt - 1:tt, :]
    z = p + (prev - p) * mu_ref[...]

    r = z[:, 0:dim]
    k = z[:, dim:2 * dim]
    v = z[:, 2 * dim:3 * dim]
    lo = z[:, 3 * dim:3 * dim + LANES]
    lane = lax.broadcasted_iota(jnp.int32, lo.shape, 1)
    act = jnp.where(lane < 32, jnp.tanh(lo), jnp.where(lane < 64, lo, _sigmoid(lo)))
    lora = _dot(act, lora_ref[...], HIGHEST)
    x_w = w0_ref[...] + lora[:, 0:dim]
    w_log = -(jnp.maximum(-x_w, 0.0) + jnp.log(1.0 + jnp.exp(-jnp.abs(x_w)))) - 0.5
    a = _sigmoid(a0_ref[...] + lora[:, dim:2 * dim])
    g = lora[:, 2 * dim:3 * dim]
    bd = bd_ref[...]
    kk = k * kk_ref[...]
    kk = kk * lax.rsqrt(jnp.maximum(_dot(kk * kk, bd, HIGHEST), 1e-24))
    k = k * (1.0 + (a - 1.0) * ka_ref[...])
    r_s[...] = r
    k_s[...] = k
    v_s[...] = v
    a_s[...] = -kk
    b_s[...] = kk * a
    ld_s[...] = -jnp.exp(w_log)

    row_c = lax.broadcasted_iota(jnp.int32, (c, c), 0)
    col_c = lax.broadcasted_iota(jnp.int32, (c, c), 1)
    tri_incl = (row_c >= col_c).astype(F32)
    eye_c = (row_c == col_c).astype(F32)
    row_w = lax.broadcasted_iota(jnp.int32, (c, 2 * c), 0)
    col_w = lax.broadcasted_iota(jnp.int32, (c, 2 * c), 1)
    col_in = jnp.where(col_w >= c, col_w - c, col_w)
    strict_w = row_w > col_in
    incl_w = row_w >= col_in
    second_w = col_w >= c
    lane_c = lax.broadcasted_iota(jnp.int32, (c, LANES), 1)
    head0 = lane_c < HEAD
    lane_2 = lax.broadcasted_iota(jnp.int32, (c, 2 * LANES), 1)
    head0_2 = jnp.where(lane_2 >= LANES, lane_2 - LANES, lane_2) < HEAD
    row_p = lax.broadcasted_iota(jnp.int32, (LANES, LANES), 0)
    col_p = lax.broadcasted_iota(jnp.int32, (LANES, LANES), 1)
    same_head = (row_p < HEAD) == (col_p < HEAD)
    diag_p = row_p == col_p
    zeros_cl = jnp.zeros((c, LANES), F32)

    def chunk_body(ci, carry):
        rows = pl.ds(pl.multiple_of(ci * c, c), c)
        for j in range(n_pairs):
            lanes = slice(j * LANES, (j + 1) * LANES)
            r_c, k_c, v_c = r_s[rows, lanes], k_s[rows, lanes], v_s[rows, lanes]
            a_c, b_c, ld_c = a_s[rows, lanes], b_s[rows, lanes], ld_s[rows, lanes]
            cum = _dot(tri_incl, ld_c, HIGHEST)
            cum_end = cum[c - 1:c, :]
            e_pos = jnp.exp(cum)
            e_neg = jnp.exp(-cum)
            e_end = jnp.exp(cum_end - cum)
            at = a_c * jnp.exp(cum - ld_c)
            rt = r_c * e_pos
            bt = b_c * e_neg
            kt = k_c * e_neg
            lhs = jnp.concatenate([jnp.where(head0, at, 0.0), jnp.where(head0, 0.0, at),
                                   jnp.where(head0, rt, 0.0), jnp.where(head0, 0.0, rt)], axis=0)
            rhs = jnp.concatenate([bt, kt], axis=0)
            aa = _dot_nt(lhs, rhs, HIGHEST)
            vv = jnp.concatenate([v_c, v_c], axis=0)
            t_inv, q_mat, akv = [], [], []
            for h in range(2):
                t_h = jnp.where(strict_w, aa[h * c:(h + 1) * c, :], 0.0)
                q_mat.append(jnp.where(incl_w, aa[(2 + h) * c:(3 + h) * c, :], 0.0))
                akv.append(_dot(jnp.where(second_w, t_h, 0.0), vv, HIGHEST))
                l_pow = t_h[:, 0:c]
                inv = eye_c + l_pow
                for _ in range(5):
                    l_pow = _dot(l_pow, l_pow, HIGHEST)
                    inv = inv + _dot(inv, l_pow, HIGHEST)
                t_inv.append(inv)
            x_mat = jnp.concatenate([at, jnp.where(head0, akv[0], akv[1])], axis=1)
            tx = jnp.where(head0_2, _dot(t_inv[0], x_mat, HIGHEST), _dot(t_inv[1], x_mat, HIGHEST))
            w_mat = jnp.concatenate([tx, jnp.concatenate([zeros_cl, v_c], axis=1)], axis=0)
            qw = jnp.where(head0_2, _dot(q_mat[0], w_mat, HIGHEST), _dot(q_mat[1], w_mat, HIGHEST))
            r_eff = rt + qw[:, 0:LANES]
            y_0 = qw[:, LANES:]
            bk_t = jnp.concatenate([b_c * e_end, k_c * e_end], axis=0).T
            mn = _dot(bk_t, w_mat, HIGHEST)
            m_mat = jnp.where(same_head, mn[:, 0:LANES], 0.0) + jnp.where(diag_p, jnp.exp(cum_end), 0.0)
            n_mat = jnp.where(same_head, mn[:, LANES:], 0.0)
            st = st_ref[j]
            y_s[rows, lanes] = _dot(r_eff, st, HIGHEST) + y_0
            st_ref[j] = _dot(m_mat, st, HIGHEST) + n_mat
        return carry

    lax.fori_loop(0, tt // c, chunk_body, 0)

    y = y_s[...]
    inv_n = 1.0 / HEAD
    mean = _dot(y, bd, HIGHEST) * inv_n
    d = y - mean
    var = _dot(d * d, bd, HIGHEST) * inv_n
    y = d * lax.rsqrt(var + RWKV_GN_EPS) * lnw_ref[...] + lnb_ref[...]
    r = r_s[...]
    bonus = _dot(r * k_s[...] * rk_ref[...], bd, HIGHEST) * v_s[...]
    o_ref[...] = (y + bonus) * g


def _rwkv(p_rwkv, vecs, lora_w, bd, batch, seq, dim):
    n, width = p_rwkv.shape
    tt = 256
    n_t = seq // tt
    vec_spec = lambda a: pl.BlockSpec(a.shape, lambda b, t: (0, 0))
    act = pltpu.VMEM((tt, dim), F32)
    return pl.pallas_call(
        functools.partial(_rwkv_kernel, tt=tt, dim=dim),
        grid=(batch, n_t),
        in_specs=[pl.BlockSpec((tt, width), lambda b, t: (b * n_t + t, 0))]
                 + [vec_spec(a) for a in vecs] + [vec_spec(lora_w), vec_spec(bd)],
        out_specs=pl.BlockSpec((tt, dim), lambda b, t: (b * n_t + t, 0)),
        out_shape=jax.ShapeDtypeStruct((n, dim), F32),
        scratch_shapes=[pltpu.VMEM((tt + 8, width), F32),
                        pltpu.VMEM((dim // LANES, LANES, LANES), F32),
                        act, act, act, act, act, act, act],
        compiler_params=_params("parallel", "arbitrary"),
        name="rwkv7",
    )(p_rwkv, *vecs, lora_w, bd)


def _conv_kernel(p_ref, w_ref, b_ref, lnw_ref, lnb_ref, o_ref, hbuf, *, tt, dim):
    sub = 64

    @pl.when(pl.program_id(1) == 0)
    def _():
        hbuf[0:CONV_HALO, :] = jnp.zeros((CONV_HALO, dim), F32)

    hbuf[CONV_HALO:CONV_HALO + tt, :] = p_ref[:, 0:dim] * _sigmoid(p_ref[:, dim:2 * dim])
    first = CONV_HALO - (CONV_WIDTH - 1)
    for s in range(tt // sub):
        acc = jnp.zeros((sub, dim), F32) + b_ref[...]
        for j in range(CONV_WIDTH):
            lo = s * sub + first + j
            acc = acc + hbuf[lo:lo + sub, :] * w_ref[j:j + 1, :]
        mu = jnp.mean(acc, axis=-1, keepdims=True)
        d = acc - mu
        var = jnp.mean(d * d, axis=-1, keepdims=True)
        y = d * lax.rsqrt(var + LN_EPS) * lnw_ref[...] + lnb_ref[...]
        o_ref[s * sub:(s + 1) * sub, :] = y * _sigmoid(y)
    hbuf[0:CONV_HALO, :] = hbuf[tt:tt + CONV_HALO, :]


def _conv(p_conv, conv_w, conv_b, ln_w, ln_b, batch, seq):
    n, width = p_conv.shape
    dim = width // 2
    tt = 512
    n_t = seq // tt
    vec = lambda a: pl.BlockSpec(a.shape, lambda b, t: (0, 0))
    return pl.pallas_call(
        functools.partial(_conv_kernel, tt=tt, dim=dim),
        grid=(batch, n_t),
        in_specs=[pl.BlockSpec((tt, width), lambda b, t: (b * n_t + t, 0)),
                  vec(conv_w), vec(conv_b), vec(ln_w), vec(ln_b)],
        out_specs=pl.BlockSpec((tt, dim), lambda b, t: (b * n_t + t, 0)),
        out_shape=jax.ShapeDtypeStruct((n, dim), F32),
        scratch_shapes=[pltpu.VMEM((tt + CONV_HALO, dim), F32)],
        compiler_params=_params("parallel", "arbitrary"),
        name="conformer_conv",
    )(p_conv, conv_w, conv_b, ln_w, ln_b)


def _mlaprep_kernel(p_ref, cos_ref, sin_ref, qn_ref, kvn_ref, wuq_ref, wuk_ref, wuv_ref,
                    q_ref, k_ref, v_ref, *, lora, n_heads, scale):
    cos = cos_ref[...]
    sin = sin_ref[...]
    qn = (_rms(p_ref[:, 0:lora]) * qn_ref[...]).astype(BF16)
    kvn = (_rms(p_ref[:, lora:2 * lora]) * kvn_ref[...]).astype(BF16)
    kpe = p_ref[:, 2 * lora:2 * lora + LANES] * cos + p_ref[:, 2 * lora + LANES:2 * lora + 2 * LANES] * sin
    lane = lax.broadcasted_iota(jnp.int32, cos.shape, 1)
    q_mul = jnp.where(lane < HEAD, 1.0, jnp.where(lane < HEAD + 32, cos, sin)) * scale
    for h in range(n_heads):
        cols = slice(h * LANES, (h + 1) * LANES)
        q_ref[:, cols] = (_dot(qn, wuq_ref[:, cols]) * q_mul).astype(BF16)
        k_ref[:, cols] = (_dot(kvn, wuk_ref[:, cols]) + kpe).astype(BF16)
    v_ref[...] = _dot(kvn, wuv_ref[...]).astype(BF16)


def _mlaprep(p_mla, cos, sin, q_norm, kv_norm, wuq, wuk, wuv, lora, n_heads, scale):
    n, width = p_mla.shape
    tm = 512
    row = lambda w: pl.BlockSpec((tm, w), lambda i: (i, 0))
    full = lambda a: pl.BlockSpec(a.shape, lambda i: (0, 0))
    return pl.pallas_call(
        functools.partial(_mlaprep_kernel, lora=lora, n_heads=n_heads, scale=scale),
        grid=(n // tm,),
        in_specs=[row(width), row(LANES), row(LANES), full(q_norm), full(kv_norm), full(wuq), full(wuk), full(wuv)],
        out_specs=[row(n_heads * LANES), row(n_heads * LANES), row(n_heads * HEAD)],
        out_shape=[jax.ShapeDtypeStruct((n, n_heads * LANES), BF16),
                   jax.ShapeDtypeStruct((n, n_heads * LANES), BF16),
                   jax.ShapeDtypeStruct((n, n_heads * HEAD), BF16)],
        compiler_params=_params("parallel"),
        name="mla_prep",
    )(p_mla, cos, sin, q_norm, kv_norm, wuq, wuk, wuv)


def _flash_kernel(q_ref, k_ref, v_ref, o_ref, m_s, l_s, acc_s, *, tq):
    i = pl.program_id(2)
    m_s[...] = jnp.full(m_s.shape, -jnp.inf, F32)
    l_s[...] = jnp.zeros(l_s.shape, F32)
    acc_s[...] = jnp.zeros(acc_s.shape, F32)
    lane = lax.broadcasted_iota(jnp.int32, (tq, LANES), 1)
    head0 = lane < HEAD
    causal = lax.broadcasted_iota(jnp.int32, (tq, tq), 0) >= lax.broadcasted_iota(jnp.int32, (tq, tq), 1)

    def step(j, masked):
        keys = pl.ds(pl.multiple_of(j * tq, tq), tq)
        vb = v_ref[keys, :]
        alpha, pv = [], []
        for h in range(2):
            cols = slice(h * LANES, (h + 1) * LANES)
            s = _dot_nt(q_ref[:, cols], k_ref[keys, cols])
            if masked:
                s = jnp.where(causal, s, -jnp.inf)
            m_prev = m_s[h]
            m_new = jnp.maximum(m_prev, jnp.max(s, axis=-1, keepdims=True))
            al = jnp.exp(m_prev - m_new)
            p = jnp.exp(s - m_new)
            l_s[h] = al * l_s[h] + jnp.sum(p, axis=-1, keepdims=True)
            m_s[h] = m_new
            alpha.append(al)
            pv.append(_dot(p.astype(BF16), vb))
        acc_s[...] = jnp.where(head0, alpha[0], alpha[1]) * acc_s[...] + jnp.where(head0, pv[0], pv[1])

    def body(j, carry):
        step(j, False)
        return carry

    lax.fori_loop(0, i, body, 0)
    step(i, True)
    o_ref[...] = acc_s[...] / jnp.where(head0, l_s[0], l_s[1])


def _flash(q, k, v, batch, seq, n_heads):
    n = q.shape[0]
    tq = 256
    n_q = seq // tq
    return pl.pallas_call(
        functools.partial(_flash_kernel, tq=tq),
        grid=(batch, n_heads // 2, n_q),
        in_specs=[pl.BlockSpec((tq, 2 * LANES), lambda b, h, i: (b * n_q + i, h)),
                  pl.BlockSpec((seq, 2 * LANES), lambda b, h, i: (b, h)),
                  pl.BlockSpec((seq, LANES), lambda b, h, i: (b, h))],
        out_specs=pl.BlockSpec((tq, LANES), lambda b, h, i: (b * n_q + i, h)),
        out_shape=jax.ShapeDtypeStruct((n, n_heads * HEAD), F32),
        scratch_shapes=[pltpu.VMEM((2, tq, 1), F32), pltpu.VMEM((2, tq, 1), F32), pltpu.VMEM((tq, LANES), F32)],
        compiler_params=_params("parallel", "parallel", "arbitrary"),
        name="mla_flash",
    )(q, k, v)


def _outffn_kernel(ya_ref, yb_ref, yc_ref, x_ref, mod_ref, gpm_ref, gpf_ref, gqf_ref, wo_ref, w1_ref, w2_ref,
                   o_ref, x1_s, h_s, acc_s, *, da, db):
    f = pl.program_id(1)

    @pl.when(f == 0)
    def _():
        y = (_dot(ya_ref[...].astype(BF16), wo_ref[0:da, :])
             + _dot(yb_ref[...].astype(BF16), wo_ref[da:da + db, :])
             + _dot(yc_ref[...].astype(BF16), wo_ref[da + db:, :]))
        x1 = x_ref[...] + mod_ref[0, 2:3, :] * (_rms(y) * gpm_ref[...])
        x1_s[...] = x1
        h = _rms(x1) * gpf_ref[...]
        h_s[...] = (h * (1.0 + mod_ref[0, 4:5, :]) + mod_ref[0, 3:4, :]).astype(BF16)
        acc_s[...] = jnp.zeros(acc_s.shape, F32)

    hid = jnp.maximum(_dot(h_s[...], w1_ref[...]), 0.0)
    acc_s[...] += _dot((hid * hid).astype(BF16), w2_ref[...])

    @pl.when(f == pl.num_programs(1) - 1)
    def _():
        o_ref[...] = x1_s[...] + mod_ref[0, 5:6, :] * (_rms(acc_s[...]) * gqf_ref[...])


def _outffn(ya, yb, yc, x, mod, g_post_mix, g_pre_ffn, g_post_ffn, w_out, w_ff1, w_ff2, seq):
    n, d = x.shape
    d_ff = w_ff1.shape[1]
    tm, tf = 512, 512
    per_b = seq // tm
    row = lambda w: pl.BlockSpec((tm, w), lambda i, f: (i, 0))
    vec = pl.BlockSpec((1, d), lambda i, f: (0, 0))
    return pl.pallas_call(
        functools.partial(_outffn_kernel, da=ya.shape[1], db=yb.shape[1]),
        grid=(n // tm, d_ff // tf),
        in_specs=[row(ya.shape[1]), row(yb.shape[1]), row(yc.shape[1]), row(d),
                  pl.BlockSpec((1, 6, d), lambda i, f: (i // per_b, 0, 0)),
                  vec, vec, vec,
                  pl.BlockSpec(w_out.shape, lambda i, f: (0, 0)),
                  pl.BlockSpec((d, tf), lambda i, f: (0, f)),
                  pl.BlockSpec((tf, d), lambda i, f: (f, 0))],
        out_specs=row(d),
        out_shape=jax.ShapeDtypeStruct((n, d), F32),
        scratch_shapes=[pltpu.VMEM((tm, d), F32), pltpu.VMEM((tm, d), BF16), pltpu.VMEM((tm, d), F32)],
        compiler_params=_params("parallel", "arbitrary"),
        name="out_ffn",
    )(ya, yb, yc, x, mod, g_post_mix, g_pre_ffn, g_post_ffn, w_out, w_ff1, w_ff2)


def _rot_cols(w):
    half = w.shape[-1] // 2
    return jnp.concatenate([-w[..., half:], w[..., :half]], axis=-1)


def kernel(x, c, positions, g_pre_mix, g_post_mix, g_pre_ffn, g_post_ffn, w_ada, b_ada, w_in, w_out, rwkv_mu, rwkv_w0, rwkv_w2, rwkv_a0, rwkv_a2, rwkv_g2, rwkv_k_k, rwkv_k_a, rwkv_r_k, rwkv_ln_w, rwkv_ln_b, conv_w, conv_b, conv_ln_w, conv_ln_b, mla_q_norm, mla_w_uq, mla_kv_norm, mla_w_ukv, w_ff1, w_ff2):
    batch, seq, d = x.shape
    n = batch * seq
    n_layers = w_ada.shape[0]
    r_dim = rwkv_w0.shape[1]
    w_lora, a_lora, g_lora = rwkv_w2.shape[1], rwkv_a2.shape[1], rwkv_g2.shape[1]
    assert w_lora + a_lora + g_lora == LANES and r_dim % LANES == 0
    n_r = 3 * r_dim + LANES
    c_dim = conv_w.shape[2]
    n_c = 2 * c_dim
    lora = mla_q_norm.shape[1]
    n_heads = mla_w_ukv.shape[2] // (2 * HEAD)
    rope = mla_w_uq.shape[2] // n_heads - HEAD
    assert HEAD + 2 * rope == LANES
    n_m = 2 * lora + 2 * LANES
    scale = float(HEAD + rope) ** -0.5

    xf = x.reshape(n, d)
    c8 = jnp.pad(c, ((0, 8 - batch), (0, 0)))
    mod_all = _ada(c8, w_ada, b_ada.reshape(n_layers, 1, 6 * d))[:, :batch].reshape(n_layers, batch, 6, d)

    inv_freq = 1.0 / (ROPE_THETA ** (jnp.arange(0, rope, 2, dtype=F32) / rope))
    invf = jnp.tile(inv_freq, LANES // inv_freq.shape[0]).reshape(1, LANES)
    cos, sin = _rope_tables(positions.reshape(n, 1), invf)

    head_id = jnp.arange(r_dim) // HEAD
    bd = (head_id[:, None] == head_id[None, :]).astype(F32)
    row1 = lambda a: a.reshape(1, -1)

    for l in range(n_layers):
        wi = w_in[l]
        k_pe = wi[:, n_r + n_c + 2 * lora:]
        z_half = jnp.zeros((d, HEAD), F32)
        w_ext = jnp.concatenate([wi[:, :n_r + n_c + 2 * lora],
                                 z_half, k_pe, k_pe,
                                 z_half, _rot_cols(k_pe), _rot_cols(k_pe)], axis=1).astype(BF16)
        wq = mla_w_uq[l].reshape(lora, n_heads, HEAD + rope)
        wuq = jnp.concatenate([wq, _rot_cols(wq[..., HEAD:])], axis=-1).reshape(lora, n_heads * LANES).astype(BF16)
        wkv = mla_w_ukv[l].reshape(lora, n_heads, 2 * HEAD)
        wuk = jnp.concatenate([wkv[..., :HEAD], jnp.zeros_like(wkv[..., :HEAD])], axis=-1)
        wuk = wuk.reshape(lora, n_heads * LANES).astype(BF16)
        wuv = wkv[..., HEAD:].reshape(lora, n_heads * HEAD).astype(BF16)
        lora_w = jnp.zeros((LANES, 3 * r_dim), F32)
        lora_w = lora_w.at[0:w_lora, 0:r_dim].set(rwkv_w2[l])
        lora_w = lora_w.at[w_lora:w_lora + a_lora, r_dim:2 * r_dim].set(rwkv_a2[l])
        lora_w = lora_w.at[w_lora + a_lora:, 2 * r_dim:].set(rwkv_g2[l])

        mod = mod_all[l]
        p_rwkv, p_conv, p_mla = _inproj(xf, mod, row1(g_pre_mix[l]), w_ext, seq, n_r, n_c, n_m)
        vecs = [row1(rwkv_mu[l]), row1(rwkv_w0[l]), row1(rwkv_a0[l]), row1(rwkv_k_k[l]), row1(rwkv_k_a[l]),
                row1(rwkv_r_k[l]), row1(rwkv_ln_w[l]), row1(rwkv_ln_b[l])]
        y_a = _rwkv(p_rwkv, vecs, lora_w, bd, batch, seq, r_dim)
        y_b = _conv(p_conv, conv_w[l], row1(conv_b[l]), row1(conv_ln_w[l]), row1(conv_ln_b[l]), batch, seq)
        q, k, v = _mlaprep(p_mla, cos, sin, row1(mla_q_norm[l]), row1(mla_kv_norm[l]), wuq, wuk, wuv,
                           lora, n_heads, scale)
        y_c = _flash(q, k, v, batch, seq, n_heads)
        xf = _outffn(y_a, y_b, y_c, xf, mod, row1(g_post_mix[l]), row1(g_pre_ffn[l]), row1(g_post_ffn[l]),
                     w_out[l].astype(BF16), w_ff1[l].astype(BF16), w_ff2[l].astype(BF16), seq)
    return xf.reshape(batch, seq, d)
```

```python
import functools

import jax
import jax.numpy as jnp
from jax import lax
from jax.experimental import pallas as pl
from jax.experimental.pallas import tpu as pltpu

F32 = jnp.float32
BF16 = jnp.bfloat16
HIGHEST = lax.Precision.HIGHEST

LANES = 128
HEAD = 64
NORM_EPS = 1e-6
LN_EPS = 1e-5
RWKV_GN_EPS = 64e-5
ROPE_THETA = 10000.0
CONV_WIDTH = 31
CONV_HALO = 32
CHUNK = 64
GROUP_CHUNKS = 2
VMEM_LIMIT = 56 * 1024 * 1024


def _dot(a, b, precision=None):
    return jnp.dot(a, b, preferred_element_type=F32, precision=precision)


def _dot_nt(a, b, precision=None):
    return lax.dot_general(a, b, (((1,), (1,)), ((), ())), preferred_element_type=F32, precision=precision)


def _bf(x):
    return x.astype(BF16)


def _split(x):
    hi = x.astype(BF16)
    return hi, (x - hi.astype(F32)).astype(BF16)


def _dot_x2(a, b_exact):
    hi, lo = _split(a)
    return _dot(hi, b_exact) + _dot(lo, b_exact)


def _rms(x):
    return x * lax.rsqrt(jnp.mean(x * x, axis=-1, keepdims=True) + NORM_EPS)


def _sigmoid(x):
    return 1.0 / (1.0 + jnp.exp(-x))


def _params(*semantics):
    return pltpu.CompilerParams(dimension_semantics=semantics, vmem_limit_bytes=VMEM_LIMIT)


def _ada_kernel(c_ref, w_ref, b_ref, o_ref):
    c = c_ref[...]
    cs = c * _sigmoid(c)
    o_ref[0] = _dot(cs, w_ref[0], HIGHEST) + b_ref[0]


def _ada(c8, w_ada, b_ada):
    n_layers, d, d6 = w_ada.shape
    tn = 1536
    return pl.pallas_call(
        _ada_kernel,
        grid=(n_layers, d6 // tn),
        in_specs=[pl.BlockSpec((8, d), lambda l, j: (0, 0)),
                  pl.BlockSpec((1, d, tn), lambda l, j: (l, 0, j)),
                  pl.BlockSpec((1, 1, tn), lambda l, j: (l, 0, j))],
        out_specs=pl.BlockSpec((1, 8, tn), lambda l, j: (l, 0, j)),
        out_shape=jax.ShapeDtypeStruct((n_layers, 8, d6), F32),
        compiler_params=_params("parallel", "parallel"),
        name="ada_mod",
    )(c8, w_ada, b_ada)


def _rope_kernel(pos_ref, invf_ref, cos_ref, sin_ref):
    ang = pos_ref[...].astype(F32) * invf_ref[...]
    cos_ref[...] = jnp.cos(ang)
    sin_ref[...] = jnp.sin(ang)


def _rope_tables(pos_col, invf):
    n = pos_col.shape[0]
    tm = 2048
    spec = pl.BlockSpec((tm, LANES), lambda i: (i, 0))
    return pl.pallas_call(
        _rope_kernel,
        grid=(n // tm,),
        in_specs=[pl.BlockSpec((tm, 1), lambda i: (i, 0)), pl.BlockSpec((1, LANES), lambda i: (0, 0))],
        out_specs=[spec, spec],
        out_shape=[jax.ShapeDtypeStruct((n, LANES), F32)] * 2,
        compiler_params=_params("parallel"),
        name="rope_tables",
    )(pos_col, invf)


def _inproj_kernel(x_ref, mod_ref, g_ref, w_ref, pr_ref, pc_ref, pm_ref, *, n_r, n_c):
    h = _rms(x_ref[...]) * g_ref[...]
    h = (h * (1.0 + mod_ref[0, 1:2, :]) + mod_ref[0, 0:1, :]).astype(BF16)
    pr_ref[...] = _dot(h, w_ref[:, :n_r])
    pc_ref[...] = _dot(h, w_ref[:, n_r:n_r + n_c])
    pm_ref[...] = _dot(h, w_ref[:, n_r + n_c:])


def _inproj(x, mod, g, w_ext, seq, n_r, n_c, n_m):
    n, d = x.shape
    tm = 512
    per_b = seq // tm
    return pl.pallas_call(
        functools.partial(_inproj_kernel, n_r=n_r, n_c=n_c),
        grid=(n // tm,),
        in_specs=[pl.BlockSpec((tm, d), lambda i: (i, 0)),
                  pl.BlockSpec((1, 6, d), lambda i: (i // per_b, 0, 0)),
                  pl.BlockSpec((1, d), lambda i: (0, 0)),
                  pl.BlockSpec(w_ext.shape, lambda i: (0, 0))],
        out_specs=[pl.BlockSpec((tm, n_r), lambda i: (i, 0)),
                   pl.BlockSpec((tm, n_c), lambda i: (i, 0)),
                   pl.BlockSpec((tm, n_m), lambda i: (i, 0))],
        out_shape=[jax.ShapeDtypeStruct((n, n_r), F32),
                   jax.ShapeDtypeStruct((n, n_c), F32),
                   jax.ShapeDtypeStruct((n, n_m), F32)],
        compiler_params=_params("parallel"),
        name="in_proj",
    )(x, mod, g, w_ext)


def _rwkv_kernel(p_ref, mu_ref, w0_ref, a0_ref, kk_ref, ka_ref, rk_ref, lnw_ref, lnb_ref, lora_ref, bd_ref,
                 o_ref, zbuf, st_ref, r_s, k_s, v_s, a_s, b_s, ld_s, y_s, *, tt, dim):
    n_pairs = dim // LANES
    c = CHUNK

    @pl.when(pl.program_id(1) == 0)
    def _():
        zbuf[0:8, :] = jnp.zeros((8, zbuf.shape[1]), F32)
        st_ref[...] = jnp.zeros(st_ref.shape, F32)

    p = p_ref[...]
    zbuf[8:8 + tt, :] = p
    prev = zbuf[7:7 + tt, :]
    zbuf[7:8, :] = p_ref[tt - 1:tt, :]
    z = p + (prev - p) * mu_ref[...]

    r = z[:, 0:dim]
    k = z[:, dim:2 * dim]
    v = z[:, 2 * dim:3 * dim]
    lo = z[:, 3 * dim:3 * dim + LANES]
    lane = lax.broadcasted_iota(jnp.int32, lo.shape, 1)
    act = jnp.where(lane < 32, jnp.tanh(lo), jnp.where(lane < 64, lo, _sigmoid(lo)))
    lora = _dot(act, lora_ref[...], HIGHEST)
    x_w = w0_ref[...] + lora[:, 0:dim]
    w_log = -(jnp.maximum(-x_w, 0.0) + jnp.log(1.0 + jnp.exp(-jnp.abs(x_w)))) - 0.5
    a = _sigmoid(a0_ref[...] + lora[:, dim:2 * dim])
    g = lora[:, 2 * dim:3 * dim]
    bd = bd_ref[...]
    kk = k * kk_ref[...]
    kk = kk * lax.rsqrt(jnp.maximum(_dot_x2(kk * kk, bd), 1e-24))
    k = k * (1.0 + (a - 1.0) * ka_ref[...])
    r_s[...] = r
    k_s[...] = k
    v_s[...] = v
    a_s[...] = -kk
    b_s[...] = kk * a
    ld_s[...] = -jnp.exp(w_log)

    row_c = lax.broadcasted_iota(jnp.int32, (c, c), 0)
    col_c = lax.broadcasted_iota(jnp.int32, (c, c), 1)
    tri_incl = (row_c >= col_c).astype(BF16)
    eye_c = (row_c == col_c).astype(F32)
    row_w = lax.broadcasted_iota(jnp.int32, (c, 2 * c), 0)
    col_w = lax.broadcasted_iota(jnp.int32, (c, 2 * c), 1)
    col_in = jnp.where(col_w >= c, col_w - c, col_w)
    strict_w = row_w > col_in
    incl_w = row_w >= col_in
    second_w = col_w >= c
    lane_c = lax.broadcasted_iota(jnp.int32, (c, LANES), 1)
    head0 = lane_c < HEAD
    lane_2 = lax.broadcasted_iota(jnp.int32, (c, 2 * LANES), 1)
    head0_2 = jnp.where(lane_2 >= LANES, lane_2 - LANES, lane_2) < HEAD
    row_p = lax.broadcasted_iota(jnp.int32, (LANES, LANES), 0)
    col_p = lax.broadcasted_iota(jnp.int32, (LANES, LANES), 1)
    same_head = (row_p < HEAD) == (col_p < HEAD)
    diag_p = row_p == col_p
    zeros_cl = jnp.zeros((c, LANES), BF16)

    def group_body(gi, carry):
        base = gi * (GROUP_CHUNKS * c)
        items = [(pl.ds(pl.multiple_of(base + q * c, c), c), slice(j * LANES, (j + 1) * LANES), j)
                 for q in range(GROUP_CHUNKS) for j in range(n_pairs)]
        n_it = len(items)
        loaded = [[ref[rows, lanes] for ref in (r_s, k_s, v_s, a_s, b_s, ld_s)] for rows, lanes, _ in items]
        r_c, k_c, v_c, a_c, b_c, ld_c = [list(t) for t in zip(*loaded)]
        cum = []
        for i in range(n_it):
            ld_hi, ld_lo = _split(ld_c[i])
            cum.append(_dot(tri_incl, ld_hi) + _dot(tri_incl, ld_lo))
        cum_end = [x[c - 1:c, :] for x in cum]
        at = [a_c[i] * jnp.exp(cum[i] - ld_c[i]) for i in range(n_it)]
        rt = [r_c[i] * jnp.exp(cum[i]) for i in range(n_it)]
        v_bf = [_bf(x) for x in v_c]
        aa = []
        for i in range(n_it):
            e_neg = jnp.exp(-cum[i])
            lhs = _bf(jnp.concatenate([jnp.where(head0, at[i], 0.0), jnp.where(head0, 0.0, at[i]),
                                       jnp.where(head0, rt[i], 0.0), jnp.where(head0, 0.0, rt[i])], axis=0))
            rhs = _bf(jnp.concatenate([b_c[i] * e_neg, k_c[i] * e_neg], axis=0))
            aa.append(_dot_nt(lhs, rhs))
        heads = [(i, h) for i in range(n_it) for h in range(2)]
        t_mat = [jnp.where(strict_w, aa[i][h * c:(h + 1) * c, :], 0.0) for i, h in heads]
        q_mat = [_bf(jnp.where(incl_w, aa[i][(2 + h) * c:(3 + h) * c, :], 0.0)) for i, h in heads]
        akv = [_dot(_bf(jnp.where(second_w, t_mat[n], 0.0)), jnp.concatenate([v_bf[i], v_bf[i]], axis=0))
               for n, (i, h) in enumerate(heads)]
        l_pow = [t[:, 0:c] for t in t_mat]
        inv = [eye_c + l for l in l_pow]
        for _ in range(5):
            l_bf = [_bf(l) for l in l_pow]
            l_pow = [_dot(l, l) for l in l_bf]
            inv = [inv[n] + _dot(_bf(inv[n]), _bf(l_pow[n])) for n in range(len(heads))]
        t_inv = [_bf(x) for x in inv]
        x_mat = [_bf(jnp.concatenate([at[i], jnp.where(head0, akv[2 * i], akv[2 * i + 1])], axis=1))
                 for i in range(n_it)]
        tx = [jnp.where(head0_2, _dot(t_inv[2 * i], x_mat[i]), _dot(t_inv[2 * i + 1], x_mat[i])) for i in range(n_it)]
        w_mat = [jnp.concatenate([_bf(tx[i]), jnp.concatenate([zeros_cl, v_bf[i]], axis=1)], axis=0)
                 for i in range(n_it)]
        qw = [jnp.where(head0_2, _dot(q_mat[2 * i], w_mat[i]), _dot(q_mat[2 * i + 1], w_mat[i])) for i in range(n_it)]
        mn = []
        for i in range(n_it):
            e_end = jnp.exp(cum_end[i] - cum[i])
            bk_t = _bf(jnp.concatenate([b_c[i] * e_end, k_c[i] * e_end], axis=0).T)
            mn.append(_dot(bk_t, w_mat[i]))
        g_col = [jnp.sum(jnp.where(diag_p, jnp.exp(cum_end[i]), 0.0), axis=1, keepdims=True) for i in range(n_it)]
        states = [st_ref[j] for j in range(n_pairs)]
        for i, (rows, lanes, j) in enumerate(items):
            st = states[j]
            st_bf = _bf(st)
            y_s[rows, lanes] = _dot(_bf(rt[i] + qw[i][:, 0:LANES]), st_bf) + qw[i][:, LANES:]
            m_low = _bf(jnp.where(same_head, mn[i][:, 0:LANES], 0.0))
            states[j] = g_col[i] * st + _dot(m_low, st_bf) + jnp.where(same_head, mn[i][:, LANES:], 0.0)
        for j in range(n_pairs):
            st_ref[j] = states[j]
        return carry

    lax.fori_loop(0, tt // (c * GROUP_CHUNKS), group_body, 0)

    y = y_s[...]
    inv_n = 1.0 / HEAD
    mean = _dot_x2(y, bd) * inv_n
    d = y - mean
    var = _dot_x2(d * d, bd) * inv_n
    y = d * lax.rsqrt(var + RWKV_GN_EPS) * lnw_ref[...] + lnb_ref[...]
    r = r_s[...]
    bonus = _dot_x2(r * k_s[...] * rk_ref[...], bd) * v_s[...]
    o_ref[...] = (y + bonus) * g


def _rwkv(p_rwkv, vecs, lora_w, bd, batch, seq, dim):
    n, width = p_rwkv.shape
    tt = 256
    n_t = seq // tt
    vec_spec = lambda a: pl.BlockSpec(a.shape, lambda b, t: (0, 0))
    act = pltpu.VMEM((tt, dim), F32)
    return pl.pallas_call(
        functools.partial(_rwkv_kernel, tt=tt, dim=dim),
        grid=(batch, n_t),
        in_specs=[pl.BlockSpec((tt, width), lambda b, t: (b * n_t + t, 0))]
                 + [vec_spec(a) for a in vecs] + [vec_spec(lora_w), vec_spec(bd)],
        out_specs=pl.BlockSpec((tt, dim), lambda b, t: (b * n_t + t, 0)),
        out_shape=jax.ShapeDtypeStruct((n, dim), F32),
        scratch_shapes=[pltpu.VMEM((tt + 8, width), F32),
                        pltpu.VMEM((dim // LANES, LANES, LANES), F32),
                        act, act, act, act, act, act, act],
        compiler_params=_params("parallel", "arbitrary"),
        name="rwkv7",
    )(p_rwkv, *vecs, lora_w, bd)


def _conv_kernel(p_ref, w_ref, b_ref, lnw_ref, lnb_ref, o_ref, hbuf, *, tt, dim):
    sub = 64

    @pl.when(pl.program_id(1) == 0)
    def _():
        hbuf[0:CONV_HALO, :] = jnp.zeros((CONV_HALO, dim), F32)

    hbuf[CONV_HALO:CONV_HALO + tt, :] = p_ref[:, 0:dim] * _sigmoid(p_ref[:, dim:2 * dim])
    first = CONV_HALO - (CONV_WIDTH - 1)
    for s in range(tt // sub):
        acc = jnp.zeros((sub, dim), F32) + b_ref[...]
        for j in range(CONV_WIDTH):
            lo = s * sub + first + j
            acc = acc + hbuf[lo:lo + sub, :] * w_ref[j:j + 1, :]
        mu = jnp.mean(acc, axis=-1, keepdims=True)
        d = acc - mu
        var = jnp.mean(d * d, axis=-1, keepdims=True)
        y = d * lax.rsqrt(var + LN_EPS) * lnw_ref[...] + lnb_ref[...]
        o_ref[s * sub:(s + 1) * sub, :] = y * _sigmoid(y)
    hbuf[0:CONV_HALO, :] = hbuf[tt:tt + CONV_HALO, :]


def _conv(p_conv, conv_w, conv_b, ln_w, ln_b, batch, seq):
    n, width = p_conv.shape
    dim = width // 2
    tt = 512
    n_t = seq // tt
    vec = lambda a: pl.BlockSpec(a.shape, lambda b, t: (0, 0))
    return pl.pallas_call(
        functools.partial(_conv_kernel, tt=tt, dim=dim),
        grid=(batch, n_t),
        in_specs=[pl.BlockSpec((tt, width), lambda b, t: (b * n_t + t, 0)),
                  vec(conv_w), vec(conv_b), vec(ln_w), vec(ln_b)],
        out_specs=pl.BlockSpec((tt, dim), lambda b, t: (b * n_t + t, 0)),
        out_shape=jax.ShapeDtypeStruct((n, dim), F32),
        scratch_shapes=[pltpu.VMEM((tt + CONV_HALO, dim), F32)],
        compiler_params=_params("parallel", "arbitrary"),
        name="conformer_conv",
    )(p_conv, conv_w, conv_b, ln_w, ln_b)


def _mlaprep_kernel(p_ref, cos_ref, sin_ref, qn_ref, kvn_ref, wuq_ref, wuk_ref, wuv_ref,
                    q_ref, k_ref, v_ref, *, lora, n_heads, scale):
    cos = cos_ref[...]
    sin = sin_ref[...]
    qn = (_rms(p_ref[:, 0:lora]) * qn_ref[...]).astype(BF16)
    kvn = (_rms(p_ref[:, lora:2 * lora]) * kvn_ref[...]).astype(BF16)
    kpe = p_ref[:, 2 * lora:2 * lora + LANES] * cos + p_ref[:, 2 * lora + LANES:2 * lora + 2 * LANES] * sin
    lane = lax.broadcasted_iota(jnp.int32, cos.shape, 1)
    q_mul = jnp.where(lane < HEAD, 1.0, jnp.where(lane < HEAD + 32, cos, sin)) * scale
    for h in range(n_heads):
        cols = slice(h * LANES, (h + 1) * LANES)
        q_ref[:, cols] = (_dot(qn, wuq_ref[:, cols]) * q_mul).astype(BF16)
        k_ref[:, cols] = (_dot(kvn, wuk_ref[:, cols]) + kpe).astype(BF16)
    v_ref[...] = _dot(kvn, wuv_ref[...]).astype(BF16)


def _mlaprep(p_mla, cos, sin, q_norm, kv_norm, wuq, wuk, wuv, lora, n_heads, scale):
    n, width = p_mla.shape
    tm = 512
    row = lambda w: pl.BlockSpec((tm, w), lambda i: (i, 0))
    full = lambda a: pl.BlockSpec(a.shape, lambda i: (0, 0))
    return pl.pallas_call(
        functools.partial(_mlaprep_kernel, lora=lora, n_heads=n_heads, scale=scale),
        grid=(n // tm,),
        in_specs=[row(width), row(LANES), row(LANES), full(q_norm), full(kv_norm), full(wuq), full(wuk), full(wuv)],
        out_specs=[row(n_heads * LANES), row(n_heads * LANES), row(n_heads * HEAD)],
        out_shape=[jax.ShapeDtypeStruct((n, n_heads * LANES), BF16),
                   jax.ShapeDtypeStruct((n, n_heads * LANES), BF16),
                   jax.ShapeDtypeStruct((n, n_heads * HEAD), BF16)],
        compiler_params=_params("parallel"),
        name="mla_prep",
    )(p_mla, cos, sin, q_norm, kv_norm, wuq, wuk, wuv)


def _flash_kernel(q_ref, k_ref, v_ref, o_ref, m_s, l_s, acc_s, *, tq):
    i = pl.program_id(2)
    m_s[...] = jnp.full(m_s.shape, -jnp.inf, F32)
    l_s[...] = jnp.zeros(l_s.shape, F32)
    acc_s[...] = jnp.zeros(acc_s.shape, F32)
    lane = lax.broadcasted_iota(jnp.int32, (tq, LANES), 1)
    head0 = lane < HEAD
    causal = lax.broadcasted_iota(jnp.int32, (tq, tq), 0) >= lax.broadcasted_iota(jnp.int32, (tq, tq), 1)

    def step(j, masked):
        keys = pl.ds(pl.multiple_of(j * tq, tq), tq)
        vb = v_ref[keys, :]
        alpha, pv = [], []
        for h in range(2):
            cols = slice(h * LANES, (h + 1) * LANES)
            s = _dot_nt(q_ref[:, cols], k_ref[keys, cols])
            if masked:
                s = jnp.where(causal, s, -jnp.inf)
            m_prev = m_s[h]
            m_new = jnp.maximum(m_prev, jnp.max(s, axis=-1, keepdims=True))
            al = jnp.exp(m_prev - m_new)
            p = jnp.exp(s - m_new)
            l_s[h] = al * l_s[h] + jnp.sum(p, axis=-1, keepdims=True)
            m_s[h] = m_new
            alpha.append(al)
            pv.append(_dot(p.astype(BF16), vb))
        acc_s[...] = jnp.where(head0, alpha[0], alpha[1]) * acc_s[...] + jnp.where(head0, pv[0], pv[1])

    def body(j, carry):
        step(j, False)
        return carry

    lax.fori_loop(0, i, body, 0)
    step(i, True)
    o_ref[...] = acc_s[...] / jnp.where(head0, l_s[0], l_s[1])


def _flash(q, k, v, batch, seq, n_heads):
    n = q.shape[0]
    tq = 256
    n_q = seq // tq
    return pl.pallas_call(
        functools.partial(_flash_kernel, tq=tq),
        grid=(batch, n_heads // 2, n_q),
        in_specs=[pl.BlockSpec((tq, 2 * LANES), lambda b, h, i: (b * n_q + i, h)),
                  pl.BlockSpec((seq, 2 * LANES), lambda b, h, i: (b, h)),
                  pl.BlockSpec((seq, LANES), lambda b, h, i: (b, h))],
        out_specs=pl.BlockSpec((tq, LANES), lambda b, h, i: (b * n_q + i, h)),
        out_shape=jax.ShapeDtypeStruct((n, n_heads * HEAD), F32),
        scratch_shapes=[pltpu.VMEM((2, tq, 1), F32), pltpu.VMEM((2, tq, 1), F32), pltpu.VMEM((tq, LANES), F32)],
        compiler_params=_params("parallel", "parallel", "arbitrary"),
        name="mla_flash",
    )(q, k, v)


def _outffn_kernel(ya_ref, yb_ref, yc_ref, x_ref, mod_ref, gpm_ref, gpf_ref, gqf_ref, wo_ref, w1_ref, w2_ref,
                   o_ref, x1_s, h_s, acc_s, *, da, db):
    f = pl.program_id(1)

    @pl.when(f == 0)
    def _():
        y = (_dot(ya_ref[...].astype(BF16), wo_ref[0:da, :])
             + _dot(yb_ref[...].astype(BF16), wo_ref[da:da + db, :])
             + _dot(yc_ref[...].astype(BF16), wo_ref[da + db:, :]))
        x1 = x_ref[...] + mod_ref[0, 2:3, :] * (_rms(y) * gpm_ref[...])
        x1_s[...] = x1
        h = _rms(x1) * gpf_ref[...]
        h_s[...] = (h * (1.0 + mod_ref[0, 4:5, :]) + mod_ref[0, 3:4, :]).astype(BF16)
        acc_s[...] = jnp.zeros(acc_s.shape, F32)

    hid = jnp.maximum(_dot(h_s[...], w1_ref[...]), 0.0)
    acc_s[...] += _dot((hid * hid).astype(BF16), w2_ref[...])

    @pl.when(f == pl.num_programs(1) - 1)
    def _():
        o_ref[...] = x1_s[...] + mod_ref[0, 5:6, :] * (_rms(acc_s[...]) * gqf_ref[...])


def _outffn(ya, yb, yc, x, mod, g_post_mix, g_pre_ffn, g_post_ffn, w_out, w_ff1, w_ff2, seq):
    n, d = x.shape
    d_ff = w_ff1.shape[1]
    tm, tf = 512, 512
    per_b = seq // tm
    row = lambda w: pl.BlockSpec((tm, w), lambda i, f: (i, 0))
    vec = pl.BlockSpec((1, d), lambda i, f: (0, 0))
    return pl.pallas_call(
        functools.partial(_outffn_kernel, da=ya.shape[1], db=yb.shape[1]),
        grid=(n // tm, d_ff // tf),
        in_specs=[row(ya.shape[1]), row(yb.shape[1]), row(yc.shape[1]), row(d),
                  pl.BlockSpec((1, 6, d), lambda i, f: (i // per_b, 0, 0)),
                  vec, vec, vec,
                  pl.BlockSpec(w_out.shape, lambda i, f: (0, 0)),
                  pl.BlockSpec((d, tf), lambda i, f: (0, f)),
                  pl.BlockSpec((tf, d), lambda i, f: (f, 0))],
        out_specs=row(d),
        out_shape=jax.ShapeDtypeStruct((n, d), F32),
        scratch_shapes=[pltpu.VMEM((tm, d), F32), pltpu.VMEM((tm, d), BF16), pltpu.VMEM((tm, d), F32)],
        compiler_params=_params("parallel", "arbitrary"),
        name="out_ffn",
    )(ya, yb, yc, x, mod, g_post_mix, g_pre_ffn, g_post_ffn, w_out, w_ff1, w_ff2)


def _rot_cols(w):
    half = w.shape[-1] // 2
    return jnp.concatenate([-w[..., half:], w[..., :half]], axis=-1)


def kernel(x, c, positions, g_pre_mix, g_post_mix, g_pre_ffn, g_post_ffn, w_ada, b_ada, w_in, w_out, rwkv_mu, rwkv_w0, rwkv_w2, rwkv_a0, rwkv_a2, rwkv_g2, rwkv_k_k, rwkv_k_a, rwkv_r_k, rwkv_ln_w, rwkv_ln_b, conv_w, conv_b, conv_ln_w, conv_ln_b, mla_q_norm, mla_w_uq, mla_kv_norm, mla_w_ukv, w_ff1, w_ff2):
    batch, seq, d = x.shape
    n = batch * seq
    n_layers = w_ada.shape[0]
    r_dim = rwkv_w0.shape[1]
    w_lora, a_lora, g_lora = rwkv_w2.shape[1], rwkv_a2.shape[1], rwkv_g2.shape[1]
    assert w_lora + a_lora + g_lora == LANES and r_dim % LANES == 0
    n_r = 3 * r_dim + LANES
    c_dim = conv_w.shape[2]
    n_c = 2 * c_dim
    lora = mla_q_norm.shape[1]
    n_heads = mla_w_ukv.shape[2] // (2 * HEAD)
    rope = mla_w_uq.shape[2] // n_heads - HEAD
    assert HEAD + 2 * rope == LANES
    n_m = 2 * lora + 2 * LANES
    scale = float(HEAD + rope) ** -0.5

    xf = x.reshape(n, d)
    c8 = jnp.pad(c, ((0, 8 - batch), (0, 0)))
    mod_all = _ada(c8, w_ada, b_ada.reshape(n_layers, 1, 6 * d))[:, :batch].reshape(n_layers, batch, 6, d)

    inv_freq = 1.0 / (ROPE_THETA ** (jnp.arange(0, rope, 2, dtype=F32) / rope))
    invf = jnp.tile(inv_freq, LANES // inv_freq.shape[0]).reshape(1, LANES)
    cos, sin = _rope_tables(positions.reshape(n, 1), invf)

    head_id = jnp.arange(r_dim) // HEAD
    bd = (head_id[:, None] == head_id[None, :]).astype(BF16)
    row1 = lambda a: a.reshape(1, -1)

    for l in range(n_layers):
        wi = w_in[l]
        k_pe = wi[:, n_r + n_c + 2 * lora:]
        z_half = jnp.zeros((d, HEAD), F32)
        w_ext = jnp.concatenate([wi[:, :n_r + n_c + 2 * lora],
                                 z_half, k_pe, k_pe,
                                 z_half, _rot_cols(k_pe), _rot_cols(k_pe)], axis=1).astype(BF16)
        wq = mla_w_uq[l].reshape(lora, n_heads, HEAD + rope)
        wuq = jnp.concatenate([wq, _rot_cols(wq[..., HEAD:])], axis=-1).reshape(lora, n_heads * LANES).astype(BF16)
        wkv = mla_w_ukv[l].reshape(lora, n_heads, 2 * HEAD)
        wuk = jnp.concatenate([wkv[..., :HEAD], jnp.zeros_like(wkv[..., :HEAD])], axis=-1)
        wuk = wuk.reshape(lora, n_heads * LANES).astype(BF16)
        wuv = wkv[..., HEAD:].reshape(lora, n_heads * HEAD).astype(BF16)
        lora_w = jnp.zeros((LANES, 3 * r_dim), F32)
        lora_w = lora_w.at[0:w_lora, 0:r_dim].set(rwkv_w2[l])
        lora_w = lora_w.at[w_lora:w_lora + a_lora, r_dim:2 * r_dim].set(rwkv_a2[l])
        lora_w = lora_w.at[w_lora + a_lora:, 2 * r_dim:].set(rwkv_g2[l])

        mod = mod_all[l]
        p_rwkv, p_conv, p_mla = _inproj(xf, mod, row1(g_pre_mix[l]), w_ext, seq, n_r, n_c, n_m)
        vecs = [row1(rwkv_mu[l]), row1(rwkv_w0[l]), row1(rwkv_a0[l]), row1(rwkv_k_k[l]), row1(rwkv_k_a[l]),
                row1(rwkv_r_k[l]), row1(rwkv_ln_w[l]), row1(rwkv_ln_b[l])]
        y_a = _rwkv(p_rwkv, vecs, lora_w, bd, batch, seq, r_dim)
        y_b = _conv(p_conv, conv_w[l], row1(conv_b[l]), row1(conv_ln_w[l]), row1(conv_ln_b[l]), batch, seq)
        q, k, v = _mlaprep(p_mla, cos, sin, row1(mla_q_norm[l]), row1(mla_kv_norm[l]), wuq, wuk, wuv,
                           lora, n_heads, scale)
        y_c = _flash(q, k, v, batch, seq, n_heads)
        xf = _outffn(y_a, y_b, y_c, xf, mod, row1(g_post_mix[l]), row1(g_pre_ffn[l]), row1(g_post_ffn[l]),
                     w_out[l].astype(BF16), w_ff1[l].astype(BF16), w_ff2[l].astype(BF16), seq)
    return xf.reshape(batch, seq, d)
```

```python
import functools

import jax
import jax.numpy as jnp
from jax import lax
from jax.experimental import pallas as pl
from jax.experimental.pallas import tpu as pltpu

F32 = jnp.float32
BF16 = jnp.bfloat16
HIGHEST = lax.Precision.HIGHEST

LANES = 128
HEAD = 64
NORM_EPS = 1e-6
LN_EPS = 1e-5
RWKV_GN_EPS = 64e-5
ROPE_THETA = 10000.0
CONV_WIDTH = 31
CONV_HALO = 32
CHUNK = 64
FLASH_BLOCK = 256
FLASH_SUBS = 2
GROUP_CHUNKS = 2
VMEM_LIMIT = 56 * 1024 * 1024


def _dot(a, b, precision=None):
    return jnp.dot(a, b, preferred_element_type=F32, precision=precision)


def _dot_nt(a, b, precision=None):
    return lax.dot_general(a, b, (((1,), (1,)), ((), ())), preferred_element_type=F32, precision=precision)


def _bf(x):
    return x.astype(BF16)


def _split(x):
    hi = x.astype(BF16)
    return hi, (x - hi.astype(F32)).astype(BF16)


def _dot_x2(a, b_exact):
    hi, lo = _split(a)
    return _dot(hi, b_exact) + _dot(lo, b_exact)


def _rms(x):
    return x * lax.rsqrt(jnp.mean(x * x, axis=-1, keepdims=True) + NORM_EPS)


def _sigmoid(x):
    return 1.0 / (1.0 + jnp.exp(-x))


def _params(*semantics):
    return pltpu.CompilerParams(dimension_semantics=semantics, vmem_limit_bytes=VMEM_LIMIT)


def _ada_kernel(c_ref, w_ref, b_ref, o_ref):
    c = c_ref[...]
    cs = c * _sigmoid(c)
    o_ref[0] = _dot(cs, w_ref[0], HIGHEST) + b_ref[0]


def _ada(c8, w_ada, b_ada):
    n_layers, d, d6 = w_ada.shape
    tn = 1536
    return pl.pallas_call(
        _ada_kernel,
        grid=(n_layers, d6 // tn),
        in_specs=[pl.BlockSpec((8, d), lambda l, j: (0, 0)),
                  pl.BlockSpec((1, d, tn), lambda l, j: (l, 0, j)),
                  pl.BlockSpec((1, 1, tn), lambda l, j: (l, 0, j))],
        out_specs=pl.BlockSpec((1, 8, tn), lambda l, j: (l, 0, j)),
        out_shape=jax.ShapeDtypeStruct((n_layers, 8, d6), F32),
        compiler_params=_params("parallel", "parallel"),
        name="ada_mod",
    )(c8, w_ada, b_ada)


def _rope_kernel(pos_ref, invf_ref, cos_ref, sin_ref):
    ang = pos_ref[...].astype(F32) * invf_ref[...]
    cos_ref[...] = jnp.cos(ang)
    sin_ref[...] = jnp.sin(ang)


def _rope_tables(pos_col, invf):
    n = pos_col.shape[0]
    tm = 2048
    spec = pl.BlockSpec((tm, LANES), lambda i: (i, 0))
    return pl.pallas_call(
        _rope_kernel,
        grid=(n // tm,),
        in_specs=[pl.BlockSpec((tm, 1), lambda i: (i, 0)), pl.BlockSpec((1, LANES), lambda i: (0, 0))],
        out_specs=[spec, spec],
        out_shape=[jax.ShapeDtypeStruct((n, LANES), F32)] * 2,
        compiler_params=_params("parallel"),
        name="rope_tables",
    )(pos_col, invf)


def _inproj_kernel(x_ref, mod_ref, g_ref, w_ref, pr_ref, pc_ref, pm_ref, *, n_r, n_c):
    h = _rms(x_ref[...]) * g_ref[...]
    h = (h * (1.0 + mod_ref[0, 1:2, :]) + mod_ref[0, 0:1, :]).astype(BF16)
    pr_ref[...] = _dot(h, w_ref[:, :n_r])
    pc_ref[...] = _dot(h, w_ref[:, n_r:n_r + n_c])
    pm_ref[...] = _dot(h, w_ref[:, n_r + n_c:])


def _inproj(x, mod, g, w_ext, seq, n_r, n_c, n_m):
    n, d = x.shape
    tm = 512
    per_b = seq // tm
    return pl.pallas_call(
        functools.partial(_inproj_kernel, n_r=n_r, n_c=n_c),
        grid=(n // tm,),
        in_specs=[pl.BlockSpec((tm, d), lambda i: (i, 0)),
                  pl.BlockSpec((1, 6, d), lambda i: (i // per_b, 0, 0)),
                  pl.BlockSpec((1, d), lambda i: (0, 0)),
                  pl.BlockSpec(w_ext.shape, lambda i: (0, 0))],
        out_specs=[pl.BlockSpec((tm, n_r), lambda i: (i, 0)),
                   pl.BlockSpec((tm, n_c), lambda i: (i, 0)),
                   pl.BlockSpec((tm, n_m), lambda i: (i, 0))],
        out_shape=[jax.ShapeDtypeStruct((n, n_r), F32),
                   jax.ShapeDtypeStruct((n, n_c), F32),
                   jax.ShapeDtypeStruct((n, n_m), F32)],
        compiler_params=_params("parallel"),
        name="in_proj",
    )(x, mod, g, w_ext)


def _rwkv_kernel(p_ref, mu_ref, w0_ref, a0_ref, kk_ref, ka_ref, rk_ref, lnw_ref, lnb_ref, lora_ref, bd_ref,
                 o_ref, zbuf, st_ref, r_s, k_s, v_s, a_s, b_s, ld_s, y_s, *, tt, dim):
    n_pairs = dim // LANES
    c = CHUNK

    @pl.when(pl.program_id(1) == 0)
    def _():
        zbuf[0:8, :] = jnp.zeros((8, zbuf.shape[1]), F32)
        st_ref[...] = jnp.zeros(st_ref.shape, F32)

    p = p_ref[...]
    zbuf[8:8 + tt, :] = p
    prev = zbuf[7:7 + tt, :]
    zbuf[7:8, :] = p_ref[tt - 1:tt, :]
    z = p + (prev - p) * mu_ref[...]

    r = z[:, 0:dim]
    k = z[:, dim:2 * dim]
    v = z[:, 2 * dim:3 * dim]
    lo = z[:, 3 * dim:3 * dim + LANES]
    lane = lax.broadcasted_iota(jnp.int32, lo.shape, 1)
    act = jnp.where(lane < 32, jnp.tanh(lo), jnp.where(lane < 64, lo, _sigmoid(lo)))
    lora = _dot(act, lora_ref[...], HIGHEST)
    x_w = w0_ref[...] + lora[:, 0:dim]
    w_log = -(jnp.maximum(-x_w, 0.0) + jnp.log(1.0 + jnp.exp(-jnp.abs(x_w)))) - 0.5
    a = _sigmoid(a0_ref[...] + lora[:, dim:2 * dim])
    g = lora[:, 2 * dim:3 * dim]
    bd = bd_ref[...]
    kk = k * kk_ref[...]
    kk = kk * lax.rsqrt(jnp.maximum(_dot_x2(kk * kk, bd), 1e-24))
    k = k * (1.0 + (a - 1.0) * ka_ref[...])
    r_s[...] = r
    k_s[...] = k
    v_s[...] = v
    a_s[...] = -kk
    b_s[...] = kk * a
    ld_s[...] = -jnp.exp(w_log)

    row_c = lax.broadcasted_iota(jnp.int32, (c, c), 0)
    col_c = lax.broadcasted_iota(jnp.int32, (c, c), 1)
    tri_incl = (row_c >= col_c).astype(BF16)
    eye_c = (row_c == col_c).astype(F32)
    row_w = lax.broadcasted_iota(jnp.int32, (c, 2 * c), 0)
    col_w = lax.broadcasted_iota(jnp.int32, (c, 2 * c), 1)
    col_in = jnp.where(col_w >= c, col_w - c, col_w)
    strict_w = row_w > col_in
    incl_w = row_w >= col_in
    second_w = col_w >= c
    lane_c = lax.broadcasted_iota(jnp.int32, (c, LANES), 1)
    head0 = lane_c < HEAD
    lane_2 = lax.broadcasted_iota(jnp.int32, (c, 2 * LANES), 1)
    head0_2 = jnp.where(lane_2 >= LANES, lane_2 - LANES, lane_2) < HEAD
    row_p = lax.broadcasted_iota(jnp.int32, (LANES, LANES), 0)
    col_p = lax.broadcasted_iota(jnp.int32, (LANES, LANES), 1)
    same_head = (row_p < HEAD) == (col_p < HEAD)
    diag_p = row_p == col_p
    zeros_cl = jnp.zeros((c, LANES), BF16)

    def group_body(gi, carry):
        base = gi * (GROUP_CHUNKS * c)
        items = [(pl.ds(pl.multiple_of(base + q * c, c), c), slice(j * LANES, (j + 1) * LANES), j)
                 for q in range(GROUP_CHUNKS) for j in range(n_pairs)]
        n_it = len(items)
        loaded = [[ref[rows, lanes] for ref in (r_s, k_s, v_s, a_s, b_s, ld_s)] for rows, lanes, _ in items]
        r_c, k_c, v_c, a_c, b_c, ld_c = [list(t) for t in zip(*loaded)]
        cum = []
        for i in range(n_it):
            ld_hi, ld_lo = _split(ld_c[i])
            cum.append(_dot(tri_incl, ld_hi) + _dot(tri_incl, ld_lo))
        cum_end = [x[c - 1:c, :] for x in cum]
        at = [a_c[i] * jnp.exp(cum[i] - ld_c[i]) for i in range(n_it)]
        rt = [r_c[i] * jnp.exp(cum[i]) for i in range(n_it)]
        v_bf = [_bf(x) for x in v_c]
        aa = []
        for i in range(n_it):
            e_neg = jnp.exp(-cum[i])
            lhs = _bf(jnp.concatenate([jnp.where(head0, at[i], 0.0), jnp.where(head0, 0.0, at[i]),
                                       jnp.where(head0, rt[i], 0.0), jnp.where(head0, 0.0, rt[i])], axis=0))
            rhs = _bf(jnp.concatenate([b_c[i] * e_neg, k_c[i] * e_neg], axis=0))
            aa.append(_dot_nt(lhs, rhs))
        heads = [(i, h) for i in range(n_it) for h in range(2)]
        t_mat = [jnp.where(strict_w, aa[i][h * c:(h + 1) * c, :], 0.0) for i, h in heads]
        q_mat = [_bf(jnp.where(incl_w, aa[i][(2 + h) * c:(3 + h) * c, :], 0.0)) for i, h in heads]
        akv = [_dot(_bf(jnp.where(second_w, t_mat[n], 0.0)), jnp.concatenate([v_bf[i], v_bf[i]], axis=0))
               for n, (i, h) in enumerate(heads)]
        l_pow = [t[:, 0:c] for t in t_mat]
        inv = [eye_c + l for l in l_pow]
        for _ in range(5):
            l_bf = [_bf(l) for l in l_pow]
            l_pow = [_dot(l, l) for l in l_bf]
            inv = [inv[n] + _dot(_bf(inv[n]), _bf(l_pow[n])) for n in range(len(heads))]
        t_inv = [_bf(x) for x in inv]
        x_mat = [_bf(jnp.concatenate([at[i], jnp.where(head0, akv[2 * i], akv[2 * i + 1])], axis=1))
                 for i in range(n_it)]
        tx = [jnp.where(head0_2, _dot(t_inv[2 * i], x_mat[i]), _dot(t_inv[2 * i + 1], x_mat[i])) for i in range(n_it)]
        w_mat = [jnp.concatenate([_bf(tx[i]), jnp.concatenate([zeros_cl, v_bf[i]], axis=1)], axis=0)
                 for i in range(n_it)]
        qw = [jnp.where(head0_2, _dot(q_mat[2 * i], w_mat[i]), _dot(q_mat[2 * i + 1], w_mat[i])) for i in range(n_it)]
        mn = []
        for i in range(n_it):
            e_end = jnp.exp(cum_end[i] - cum[i])
            bk_t = _bf(jnp.concatenate([b_c[i] * e_end, k_c[i] * e_end], axis=0).T)
            mn.append(_dot(bk_t, w_mat[i]))
        g_col = [jnp.sum(jnp.where(diag_p, jnp.exp(cum_end[i]), 0.0), axis=1, keepdims=True) for i in range(n_it)]
        states = [st_ref[j] for j in range(n_pairs)]
        for i, (rows, lanes, j) in enumerate(items):
            st = states[j]
            st_bf = _bf(st)
            y_s[rows, lanes] = _dot(_bf(rt[i] + qw[i][:, 0:LANES]), st_bf) + qw[i][:, LANES:]
            m_low = _bf(jnp.where(same_head, mn[i][:, 0:LANES], 0.0))
            states[j] = g_col[i] * st + _dot(m_low, st_bf) + jnp.where(same_head, mn[i][:, LANES:], 0.0)
        for j in range(n_pairs):
            st_ref[j] = states[j]
        return carry

    lax.fori_loop(0, tt // (c * GROUP_CHUNKS), group_body, 0)

    y = y_s[...]
    inv_n = 1.0 / HEAD
    mean = _dot_x2(y, bd) * inv_n
    d = y - mean
    var = _dot_x2(d * d, bd) * inv_n
    y = d * lax.rsqrt(var + RWKV_GN_EPS) * lnw_ref[...] + lnb_ref[...]
    r = r_s[...]
    bonus = _dot_x2(r * k_s[...] * rk_ref[...], bd) * v_s[...]
    o_ref[...] = (y + bonus) * g


def _rwkv(p_rwkv, vecs, lora_w, bd, batch, seq, dim):
    n, width = p_rwkv.shape
    tt = 256
    n_t = seq // tt
    vec_spec = lambda a: pl.BlockSpec(a.shape, lambda b, t: (0, 0))
    act = pltpu.VMEM((tt, dim), F32)
    return pl.pallas_call(
        functools.partial(_rwkv_kernel, tt=tt, dim=dim),
        grid=(batch, n_t),
        in_specs=[pl.BlockSpec((tt, width), lambda b, t: (b * n_t + t, 0))]
                 + [vec_spec(a) for a in vecs] + [vec_spec(lora_w), vec_spec(bd)],
        out_specs=pl.BlockSpec((tt, dim), lambda b, t: (b * n_t + t, 0)),
        out_shape=jax.ShapeDtypeStruct((n, dim), F32),
        scratch_shapes=[pltpu.VMEM((tt + 8, width), F32),
                        pltpu.VMEM((dim // LANES, LANES, LANES), F32),
                        act, act, act, act, act, act, act],
        compiler_params=_params("parallel", "arbitrary"),
        name="rwkv7",
    )(p_rwkv, *vecs, lora_w, bd)


def _conv_kernel(p_ref, w_ref, b_ref, lnw_ref, lnb_ref, o_ref, hbuf, *, tt, dim):
    sub = 64

    @pl.when(pl.program_id(1) == 0)
    def _():
        hbuf[0:CONV_HALO, :] = jnp.zeros((CONV_HALO, dim), F32)

    hbuf[CONV_HALO:CONV_HALO + tt, :] = p_ref[:, 0:dim] * _sigmoid(p_ref[:, dim:2 * dim])
    first = CONV_HALO - (CONV_WIDTH - 1)
    for s in range(tt // sub):
        acc = jnp.zeros((sub, dim), F32) + b_ref[...]
        for j in range(CONV_WIDTH):
            lo = s * sub + first + j
            acc = acc + hbuf[lo:lo + sub, :] * w_ref[j:j + 1, :]
        mu = jnp.mean(acc, axis=-1, keepdims=True)
        d = acc - mu
        var = jnp.mean(d * d, axis=-1, keepdims=True)
        y = d * lax.rsqrt(var + LN_EPS) * lnw_ref[...] + lnb_ref[...]
        o_ref[s * sub:(s + 1) * sub, :] = y * _sigmoid(y)
    hbuf[0:CONV_HALO, :] = hbuf[tt:tt + CONV_HALO, :]


def _conv(p_conv, conv_w, conv_b, ln_w, ln_b, batch, seq):
    n, width = p_conv.shape
    dim = width // 2
    tt = 512
    n_t = seq // tt
    vec = lambda a: pl.BlockSpec(a.shape, lambda b, t: (0, 0))
    return pl.pallas_call(
        functools.partial(_conv_kernel, tt=tt, dim=dim),
        grid=(batch, n_t),
        in_specs=[pl.BlockSpec((tt, width), lambda b, t: (b * n_t + t, 0)),
                  vec(conv_w), vec(conv_b), vec(ln_w), vec(ln_b)],
        out_specs=pl.BlockSpec((tt, dim), lambda b, t: (b * n_t + t, 0)),
        out_shape=jax.ShapeDtypeStruct((n, dim), F32),
        scratch_shapes=[pltpu.VMEM((tt + CONV_HALO, dim), F32)],
        compiler_params=_params("parallel", "arbitrary"),
        name="conformer_conv",
    )(p_conv, conv_w, conv_b, ln_w, ln_b)


def _mlaprep_kernel(p_ref, cos_ref, sin_ref, qn_ref, kvn_ref, wuq_ref, wuk_ref, wuv_ref,
                    q_ref, k_ref, v_ref, *, lora, n_heads, scale):
    cos = cos_ref[...]
    sin = sin_ref[...]
    qn = (_rms(p_ref[:, 0:lora]) * qn_ref[...]).astype(BF16)
    kvn = (_rms(p_ref[:, lora:2 * lora]) * kvn_ref[...]).astype(BF16)
    kpe = p_ref[:, 2 * lora:2 * lora + LANES] * cos + p_ref[:, 2 * lora + LANES:2 * lora + 2 * LANES] * sin
    lane = lax.broadcasted_iota(jnp.int32, cos.shape, 1)
    q_mul = jnp.where(lane < HEAD, 1.0, jnp.where(lane < HEAD + 32, cos, sin)) * scale
    for h in range(n_heads):
        cols = slice(h * LANES, (h + 1) * LANES)
        q_ref[:, cols] = (_dot(qn, wuq_ref[:, cols]) * q_mul).astype(BF16)
        k_ref[:, cols] = (_dot(kvn, wuk_ref[:, cols]) + kpe).astype(BF16)
    v_ref[...] = _dot(kvn, wuv_ref[...]).astype(BF16)


def _mlaprep(p_mla, cos, sin, q_norm, kv_norm, wuq, wuk, wuv, lora, n_heads, scale):
    n, width = p_mla.shape
    tm = 512
    row = lambda w: pl.BlockSpec((tm, w), lambda i: (i, 0))
    full = lambda a: pl.BlockSpec(a.shape, lambda i: (0, 0))
    return pl.pallas_call(
        functools.partial(_mlaprep_kernel, lora=lora, n_heads=n_heads, scale=scale),
        grid=(n // tm,),
        in_specs=[row(width), row(LANES), row(LANES), full(q_norm), full(kv_norm), full(wuq), full(wuk), full(wuv)],
        out_specs=[row(n_heads * LANES), row(n_heads * LANES), row(n_heads * HEAD)],
        out_shape=[jax.ShapeDtypeStruct((n, n_heads * LANES), BF16),
                   jax.ShapeDtypeStruct((n, n_heads * LANES), BF16),
                   jax.ShapeDtypeStruct((n, n_heads * HEAD), BF16)],
        compiler_params=_params("parallel"),
        name="mla_prep",
    )(p_mla, cos, sin, q_norm, kv_norm, wuq, wuk, wuv)


def _flash_kernel(q_ref, k_ref, v_ref, o_ref, m_s, l_s, acc_s):
    i = pl.program_id(2)
    t = FLASH_BLOCK
    m_s[...] = jnp.full(m_s.shape, -jnp.inf, F32)
    l_s[...] = jnp.zeros(l_s.shape, F32)
    acc_s[...] = jnp.zeros(acc_s.shape, F32)
    head0 = lax.broadcasted_iota(jnp.int32, (t, LANES), 1) < HEAD
    causal = lax.broadcasted_iota(jnp.int32, (t, t), 0) >= lax.broadcasted_iota(jnp.int32, (t, t), 1)

    def step(j, subs):
        keys = pl.ds(pl.multiple_of(j * t, t), t)
        vb = v_ref[keys, :]
        chains = [(u, h, masked) for u, masked in subs for h in range(2)]
        s_all = [_dot_nt(q_ref[u * t:(u + 1) * t, h * LANES:(h + 1) * LANES], k_ref[keys, h * LANES:(h + 1) * LANES])
                 for u, h, _ in chains]
        alpha, probs = [], []
        for (u, h, masked), s in zip(chains, s_all):
            if masked:
                s = jnp.where(causal, s, -jnp.inf)
            halves = [s[:, n * LANES:(n + 1) * LANES] for n in range(t // LANES)]
            m_prev = m_s[u, h]
            m_new = jnp.maximum(m_prev, jnp.max(functools.reduce(jnp.maximum, halves), axis=-1, keepdims=True))
            al = jnp.exp2(m_prev - m_new)
            p = [jnp.exp2(x - m_new) for x in halves]
            l_s[u, h] = al * l_s[u, h] + jnp.sum(functools.reduce(jnp.add, p), axis=-1, keepdims=True)
            m_s[u, h] = m_new
            alpha.append(al)
            probs.append(jnp.concatenate(p, axis=1).astype(BF16))
        pv = [_dot(p, vb) for p in probs]
        for n, (u, _) in enumerate(subs):
            acc_s[u] = (jnp.where(head0, alpha[2 * n], alpha[2 * n + 1]) * acc_s[u]
                        + jnp.where(head0, pv[2 * n], pv[2 * n + 1]))

    def body(j, carry):
        step(j, [(u, False) for u in range(FLASH_SUBS)])
        return carry

    lax.fori_loop(0, FLASH_SUBS * i, body, 0)
    for d in range(FLASH_SUBS):
        step(FLASH_SUBS * i + d, [(d, True)] + [(u, False) for u in range(d + 1, FLASH_SUBS)])
    for u in range(FLASH_SUBS):
        o_ref[u * t:(u + 1) * t, :] = acc_s[u] / jnp.where(head0, l_s[u, 0], l_s[u, 1])


def _flash(q, k, v, batch, seq, n_heads):
    n = q.shape[0]
    tq = FLASH_SUBS * FLASH_BLOCK
    n_q = seq // tq
    stat = pltpu.VMEM((FLASH_SUBS, 2, FLASH_BLOCK, LANES), F32)
    return pl.pallas_call(
        _flash_kernel,
        grid=(batch, n_heads // 2, n_q),
        in_specs=[pl.BlockSpec((tq, 2 * LANES), lambda b, h, i: (b * n_q + i, h)),
                  pl.BlockSpec((seq, 2 * LANES), lambda b, h, i: (b, h)),
                  pl.BlockSpec((seq, LANES), lambda b, h, i: (b, h))],
        out_specs=pl.BlockSpec((tq, LANES), lambda b, h, i: (b * n_q + i, h)),
        out_shape=jax.ShapeDtypeStruct((n, n_heads * HEAD), F32),
        scratch_shapes=[stat, stat, pltpu.VMEM((FLASH_SUBS, FLASH_BLOCK, LANES), F32)],
        compiler_params=_params("parallel", "parallel", "arbitrary"),
        name="mla_flash",
    )(q, k, v)


def _outffn_kernel(ya_ref, yb_ref, yc_ref, x_ref, mod_ref, gpm_ref, gpf_ref, gqf_ref, wo_ref, w1_ref, w2_ref,
                   o_ref, x1_s, h_s, acc_s, *, da, db):
    f = pl.program_id(1)

    @pl.when(f == 0)
    def _():
        y = (_dot(ya_ref[...].astype(BF16), wo_ref[0:da, :])
             + _dot(yb_ref[...].astype(BF16), wo_ref[da:da + db, :])
             + _dot(yc_ref[...].astype(BF16), wo_ref[da + db:, :]))
        x1 = x_ref[...] + mod_ref[0, 2:3, :] * (_rms(y) * gpm_ref[...])
        x1_s[...] = x1
        h = _rms(x1) * gpf_ref[...]
        h_s[...] = (h * (1.0 + mod_ref[0, 4:5, :]) + mod_ref[0, 3:4, :]).astype(BF16)
        acc_s[...] = jnp.zeros(acc_s.shape, F32)

    hid = jnp.maximum(_dot(h_s[...], w1_ref[...]), 0.0)
    acc_s[...] += _dot((hid * hid).astype(BF16), w2_ref[...])

    @pl.when(f == pl.num_programs(1) - 1)
    def _():
        o_ref[...] = x1_s[...] + mod_ref[0, 5:6, :] * (_rms(acc_s[...]) * gqf_ref[...])


def _outffn(ya, yb, yc, x, mod, g_post_mix, g_pre_ffn, g_post_ffn, w_out, w_ff1, w_ff2, seq):
    n, d = x.shape
    d_ff = w_ff1.shape[1]
    tm, tf = 512, 512
    per_b = seq // tm
    row = lambda w: pl.BlockSpec((tm, w), lambda i, f: (i, 0))
    vec = pl.BlockSpec((1, d), lambda i, f: (0, 0))
    return pl.pallas_call(
        functools.partial(_outffn_kernel, da=ya.shape[1], db=yb.shape[1]),
        grid=(n // tm, d_ff // tf),
        in_specs=[row(ya.shape[1]), row(yb.shape[1]), row(yc.shape[1]), row(d),
                  pl.BlockSpec((1, 6, d), lambda i, f: (i // per_b, 0, 0)),
                  vec, vec, vec,
                  pl.BlockSpec(w_out.shape, lambda i, f: (0, 0)),
                  pl.BlockSpec((d, tf), lambda i, f: (0, f)),
                  pl.BlockSpec((tf, d), lambda i, f: (f, 0))],
        out_specs=row(d),
        out_shape=jax.ShapeDtypeStruct((n, d), F32),
        scratch_shapes=[pltpu.VMEM((tm, d), F32), pltpu.VMEM((tm, d), BF16), pltpu.VMEM((tm, d), F32)],
        compiler_params=_params("parallel", "arbitrary"),
        name="out_ffn",
    )(ya, yb, yc, x, mod, g_post_mix, g_pre_ffn, g_post_ffn, w_out, w_ff1, w_ff2)


def _rot_cols(w):
    half = w.shape[-1] // 2
    return jnp.concatenate([-w[..., half:], w[..., :half]], axis=-1)


def kernel(x, c, positions, g_pre_mix, g_post_mix, g_pre_ffn, g_post_ffn, w_ada, b_ada, w_in, w_out, rwkv_mu, rwkv_w0, rwkv_w2, rwkv_a0, rwkv_a2, rwkv_g2, rwkv_k_k, rwkv_k_a, rwkv_r_k, rwkv_ln_w, rwkv_ln_b, conv_w, conv_b, conv_ln_w, conv_ln_b, mla_q_norm, mla_w_uq, mla_kv_norm, mla_w_ukv, w_ff1, w_ff2):
    batch, seq, d = x.shape
    n = batch * seq
    n_layers = w_ada.shape[0]
    r_dim = rwkv_w0.shape[1]
    w_lora, a_lora, g_lora = rwkv_w2.shape[1], rwkv_a2.shape[1], rwkv_g2.shape[1]
    assert w_lora + a_lora + g_lora == LANES and r_dim % LANES == 0
    n_r = 3 * r_dim + LANES
    c_dim = conv_w.shape[2]
    n_c = 2 * c_dim
    lora = mla_q_norm.shape[1]
    n_heads = mla_w_ukv.shape[2] // (2 * HEAD)
    rope = mla_w_uq.shape[2] // n_heads - HEAD
    assert HEAD + 2 * rope == LANES
    n_m = 2 * lora + 2 * LANES
    scale = float(HEAD + rope) ** -0.5 * 1.4426950408889634

    xf = x.reshape(n, d)
    c8 = jnp.pad(c, ((0, 8 - batch), (0, 0)))
    mod_all = _ada(c8, w_ada, b_ada.reshape(n_layers, 1, 6 * d))[:, :batch].reshape(n_layers, batch, 6, d)

    inv_freq = 1.0 / (ROPE_THETA ** (jnp.arange(0, rope, 2, dtype=F32) / rope))
    invf = jnp.tile(inv_freq, LANES // inv_freq.shape[0]).reshape(1, LANES)
    cos, sin = _rope_tables(positions.reshape(n, 1), invf)

    head_id = jnp.arange(r_dim) // HEAD
    bd = (head_id[:, None] == head_id[None, :]).astype(BF16)
    row1 = lambda a: a.reshape(1, -1)

    for l in range(n_layers):
        wi = w_in[l]
        k_pe = wi[:, n_r + n_c + 2 * lora:]
        z_half = jnp.zeros((d, HEAD), F32)
        w_ext = jnp.concatenate([wi[:, :n_r + n_c + 2 * lora],
                                 z_half, k_pe, k_pe,
                                 z_half, _rot_cols(k_pe), _rot_cols(k_pe)], axis=1).astype(BF16)
        wq = mla_w_uq[l].reshape(lora, n_heads, HEAD + rope)
        wuq = jnp.concatenate([wq, _rot_cols(wq[..., HEAD:])], axis=-1).reshape(lora, n_heads * LANES).astype(BF16)
        wkv = mla_w_ukv[l].reshape(lora, n_heads, 2 * HEAD)
        wuk = jnp.concatenate([wkv[..., :HEAD], jnp.zeros_like(wkv[..., :HEAD])], axis=-1)
        wuk = wuk.reshape(lora, n_heads * LANES).astype(BF16)
        wuv = wkv[..., HEAD:].reshape(lora, n_heads * HEAD).astype(BF16)
        lora_w = jnp.zeros((LANES, 3 * r_dim), F32)
        lora_w = lora_w.at[0:w_lora, 0:r_dim].set(rwkv_w2[l])
        lora_w = lora_w.at[w_lora:w_lora + a_lora, r_dim:2 * r_dim].set(rwkv_a2[l])
        lora_w = lora_w.at[w_lora + a_lora:, 2 * r_dim:].set(rwkv_g2[l])

        mod = mod_all[l]
        p_rwkv, p_conv, p_mla = _inproj(xf, mod, row1(g_pre_mix[l]), w_ext, seq, n_r, n_c, n_m)
        vecs = [row1(rwkv_mu[l]), row1(rwkv_w0[l]), row1(rwkv_a0[l]), row1(rwkv_k_k[l]), row1(rwkv_k_a[l]),
                row1(rwkv_r_k[l]), row1(rwkv_ln_w[l]), row1(rwkv_ln_b[l])]
        y_a = _rwkv(p_rwkv, vecs, lora_w, bd, batch, seq, r_dim)
        y_b = _conv(p_conv, conv_w[l], row1(conv_b[l]), row1(conv_ln_w[l]), row1(conv_ln_b[l]), batch, seq)
        q, k, v = _mlaprep(p_mla, cos, sin, row1(mla_q_norm[l]), row1(mla_kv_norm[l]), wuq, wuk, wuv,
                           lora, n_heads, scale)
        y_c = _flash(q, k, v, batch, seq, n_heads)
        xf = _outffn(y_a, y_b, y_c, xf, mod, row1(g_post_mix[l]), row1(g_pre_ffn[l]), row1(g_post_ffn[l]),
                     w_out[l].astype(BF16), w_ff1[l].astype(BF16), w_ff2[l].astype(BF16), seq)
    return xf.reshape(batch, seq, d)
```

```python
import functools

import jax
import jax.numpy as jnp
from jax import lax
from jax.experimental import pallas as pl
from jax.experimental.pallas import tpu as pltpu

F32 = jnp.float32
BF16 = jnp.bfloat16
HIGHEST = lax.Precision.HIGHEST

LANES = 128
HEAD = 64
NORM_EPS = 1e-6
LN_EPS = 1e-5
RWKV_GN_EPS = 64e-5
ROPE_THETA = 10000.0
CONV_WIDTH = 31
CONV_HALO = 32
CHUNK = 64
FLASH_BLOCK = 256
FLASH_SUBS = 2
GROUP_CHUNKS = 2
VMEM_LIMIT = 56 * 1024 * 1024


def _dot(a, b, precision=None):
    return jnp.dot(a, b, preferred_element_type=F32, precision=precision)


def _dot_nt(a, b, precision=None):
    return lax.dot_general(a, b, (((1,), (1,)), ((), ())), preferred_element_type=F32, precision=precision)


def _bf(x):
    return x.astype(BF16)


def _split(x):
    hi = x.astype(BF16)
    return hi, (x - hi.astype(F32)).astype(BF16)


def _dot_x2(a, b_exact):
    hi, lo = _split(a)
    return _dot(hi, b_exact) + _dot(lo, b_exact)


def _rms(x):
    return x * lax.rsqrt(jnp.mean(x * x, axis=-1, keepdims=True) + NORM_EPS)


def _sigmoid(x):
    return 1.0 / (1.0 + jnp.exp(-x))


def _params(*semantics):
    return pltpu.CompilerParams(dimension_semantics=semantics, vmem_limit_bytes=VMEM_LIMIT)


def _ada_kernel(c_ref, w_ref, b_ref, o_ref):
    c = c_ref[...]
    cs = c * _sigmoid(c)
    o_ref[0] = _dot(cs, w_ref[0], HIGHEST) + b_ref[0]


def _ada(c8, w_ada, b_ada):
    n_layers, d, d6 = w_ada.shape
    tn = 1536
    return pl.pallas_call(
        _ada_kernel,
        grid=(n_layers, d6 // tn),
        in_specs=[pl.BlockSpec((8, d), lambda l, j: (0, 0)),
                  pl.BlockSpec((1, d, tn), lambda l, j: (l, 0, j)),
                  pl.BlockSpec((1, 1, tn), lambda l, j: (l, 0, j))],
        out_specs=pl.BlockSpec((1, 8, tn), lambda l, j: (l, 0, j)),
        out_shape=jax.ShapeDtypeStruct((n_layers, 8, d6), F32),
        compiler_params=_params("parallel", "parallel"),
        name="ada_mod",
    )(c8, w_ada, b_ada)


def _rope_kernel(pos_ref, invf_ref, cos_ref, sin_ref):
    ang = pos_ref[...].astype(F32) * invf_ref[...]
    cos_ref[...] = jnp.cos(ang)
    sin_ref[...] = jnp.sin(ang)


def _rope_tables(pos_col, invf):
    n = pos_col.shape[0]
    tm = 2048
    spec = pl.BlockSpec((tm, LANES), lambda i: (i, 0))
    return pl.pallas_call(
        _rope_kernel,
        grid=(n // tm,),
        in_specs=[pl.BlockSpec((tm, 1), lambda i: (i, 0)), pl.BlockSpec((1, LANES), lambda i: (0, 0))],
        out_specs=[spec, spec],
        out_shape=[jax.ShapeDtypeStruct((n, LANES), F32)] * 2,
        compiler_params=_params("parallel"),
        name="rope_tables",
    )(pos_col, invf)


def _inproj_kernel(x_ref, mod_ref, g_ref, w_ref, pr_ref, pc_ref, pm_ref, *, n_r, n_c):
    h = _rms(x_ref[...]) * g_ref[...]
    h = (h * (1.0 + mod_ref[0, 1:2, :]) + mod_ref[0, 0:1, :]).astype(BF16)
    pr_ref[...] = _dot(h, w_ref[:, :n_r])
    pc_ref[...] = _dot(h, w_ref[:, n_r:n_r + n_c])
    pm_ref[...] = _dot(h, w_ref[:, n_r + n_c:])


def _inproj(x, mod, g, w_ext, seq, n_r, n_c, n_m):
    n, d = x.shape
    tm = 512
    per_b = seq // tm
    return pl.pallas_call(
        functools.partial(_inproj_kernel, n_r=n_r, n_c=n_c),
        grid=(n // tm,),
        in_specs=[pl.BlockSpec((tm, d), lambda i: (i, 0)),
                  pl.BlockSpec((1, 6, d), lambda i: (i // per_b, 0, 0)),
                  pl.BlockSpec((1, d), lambda i: (0, 0)),
                  pl.BlockSpec(w_ext.shape, lambda i: (0, 0))],
        out_specs=[pl.BlockSpec((tm, n_r), lambda i: (i, 0)),
                   pl.BlockSpec((tm, n_c), lambda i: (i, 0)),
                   pl.BlockSpec((tm, n_m), lambda i: (i, 0))],
        out_shape=[jax.ShapeDtypeStruct((n, n_r), F32),
                   jax.ShapeDtypeStruct((n, n_c), F32),
                   jax.ShapeDtypeStruct((n, n_m), F32)],
        compiler_params=_params("parallel"),
        name="in_proj",
    )(x, mod, g, w_ext)


def _rwkv_kernel(p_ref, mu_ref, w0_ref, a0_ref, kk_ref, ka_ref, rk_ref, lnw_ref, lnb_ref, lora_ref, bd_ref,
                 o_ref, zbuf, st_ref, r_s, k_s, v_s, a_s, b_s, ld_s, y_s, *, tt, dim):
    n_pairs = dim // LANES
    c = CHUNK

    @pl.when(pl.program_id(1) == 0)
    def _():
        zbuf[0:8, :] = jnp.zeros((8, zbuf.shape[1]), F32)
        st_ref[...] = jnp.zeros(st_ref.shape, F32)

    p = p_ref[...]
    zbuf[8:8 + tt, :] = p
    prev = zbuf[7:7 + tt, :]
    zbuf[7:8, :] = p_ref[tt - 1:tt, :]
    z = p + (prev - p) * mu_ref[...]

    r = z[:, 0:dim]
    k = z[:, dim:2 * dim]
    v = z[:, 2 * dim:3 * dim]
    lo = z[:, 3 * dim:3 * dim + LANES]
    lane = lax.broadcasted_iota(jnp.int32, lo.shape, 1)
    act = jnp.where(lane < 32, jnp.tanh(lo), jnp.where(lane < 64, lo, _sigmoid(lo)))
    lora = _dot(act, lora_ref[...], HIGHEST)
    x_w = w0_ref[...] + lora[:, 0:dim]
    w_log = -(jnp.maximum(-x_w, 0.0) + jnp.log(1.0 + jnp.exp(-jnp.abs(x_w)))) - 0.5
    a = _sigmoid(a0_ref[...] + lora[:, dim:2 * dim])
    g = lora[:, 2 * dim:3 * dim]
    bd = bd_ref[...]
    kk = k * kk_ref[...]
    kk = kk * lax.rsqrt(jnp.maximum(_dot_x2(kk * kk, bd), 1e-24))
    k = k * (1.0 + (a - 1.0) * ka_ref[...])
    r_s[...] = r
    k_s[...] = k
    v_s[...] = v
    a_s[...] = -kk
    b_s[...] = kk * a
    ld_s[...] = -jnp.exp(w_log)

    row_c = lax.broadcasted_iota(jnp.int32, (c, c), 0)
    col_c = lax.broadcasted_iota(jnp.int32, (c, c), 1)
    tri_incl = (row_c >= col_c).astype(BF16)
    eye_c = (row_c == col_c).astype(F32)
    row_w = lax.broadcasted_iota(jnp.int32, (c, 2 * c), 0)
    col_w = lax.broadcasted_iota(jnp.int32, (c, 2 * c), 1)
    col_in = jnp.where(col_w >= c, col_w - c, col_w)
    strict_w = row_w > col_in
    incl_w = row_w >= col_in
    second_w = col_w >= c
    lane_c = lax.broadcasted_iota(jnp.int32, (c, LANES), 1)
    head0 = lane_c < HEAD
    lane_2 = lax.broadcasted_iota(jnp.int32, (c, 2 * LANES), 1)
    head0_2 = jnp.where(lane_2 >= LANES, lane_2 - LANES, lane_2) < HEAD
    row_p = lax.broadcasted_iota(jnp.int32, (LANES, LANES), 0)
    col_p = lax.broadcasted_iota(jnp.int32, (LANES, LANES), 1)
    same_head = (row_p < HEAD) == (col_p < HEAD)
    diag_p = row_p == col_p
    zeros_cl = jnp.zeros((c, LANES), BF16)

    def group_body(gi, carry):
        base = gi * (GROUP_CHUNKS * c)
        items = [(pl.ds(pl.multiple_of(base + q * c, c), c), slice(j * LANES, (j + 1) * LANES), j)
                 for q in range(GROUP_CHUNKS) for j in range(n_pairs)]
        n_it = len(items)
        loaded = [[ref[rows, lanes] for ref in (r_s, k_s, v_s, a_s, b_s, ld_s)] for rows, lanes, _ in items]
        r_c, k_c, v_c, a_c, b_c, ld_c = [list(t) for t in zip(*loaded)]
        cum = []
        for i in range(n_it):
            ld_hi, ld_lo = _split(ld_c[i])
            cum.append(_dot(tri_incl, ld_hi) + _dot(tri_incl, ld_lo))
        cum_end = [x[c - 1:c, :] for x in cum]
        at = [a_c[i] * jnp.exp(cum[i] - ld_c[i]) for i in range(n_it)]
        rt = [r_c[i] * jnp.exp(cum[i]) for i in range(n_it)]
        v_bf = [_bf(x) for x in v_c]
        aa = []
        for i in range(n_it):
            e_neg = jnp.exp(-cum[i])
            lhs = _bf(jnp.concatenate([jnp.where(head0, at[i], 0.0), jnp.where(head0, 0.0, at[i]),
                                       jnp.where(head0, rt[i], 0.0), jnp.where(head0, 0.0, rt[i])], axis=0))
            rhs = _bf(jnp.concatenate([b_c[i] * e_neg, k_c[i] * e_neg], axis=0))
            aa.append(_dot_nt(lhs, rhs))
        heads = [(i, h) for i in range(n_it) for h in range(2)]
        t_mat = [jnp.where(strict_w, aa[i][h * c:(h + 1) * c, :], 0.0) for i, h in heads]
        q_mat = [_bf(jnp.where(incl_w, aa[i][(2 + h) * c:(3 + h) * c, :], 0.0)) for i, h in heads]
        akv = [_dot(_bf(jnp.where(second_w, t_mat[n], 0.0)), jnp.concatenate([v_bf[i], v_bf[i]], axis=0))
               for n, (i, h) in enumerate(heads)]
        l_pow = [t[:, 0:c] for t in t_mat]
        inv = [eye_c + l for l in l_pow]
        for _ in range(5):
            l_bf = [_bf(l) for l in l_pow]
            l_pow = [_dot(l, l) for l in l_bf]
            inv = [inv[n] + _dot(_bf(inv[n]), _bf(l_pow[n])) for n in range(len(heads))]
        t_inv = [_bf(x) for x in inv]
        x_mat = [_bf(jnp.concatenate([at[i], jnp.where(head0, akv[2 * i], akv[2 * i + 1])], axis=1))
                 for i in range(n_it)]
        tx = [jnp.where(head0_2, _dot(t_inv[2 * i], x_mat[i]), _dot(t_inv[2 * i + 1], x_mat[i])) for i in range(n_it)]
        w_mat = [jnp.concatenate([_bf(tx[i]), jnp.concatenate([zeros_cl, v_bf[i]], axis=1)], axis=0)
                 for i in range(n_it)]
        qw = [jnp.where(head0_2, _dot(q_mat[2 * i], w_mat[i]), _dot(q_mat[2 * i + 1], w_mat[i])) for i in range(n_it)]
        mn = []
        for i in range(n_it):
            e_end = jnp.exp(cum_end[i] - cum[i])
            bk_t = _bf(jnp.concatenate([b_c[i] * e_end, k_c[i] * e_end], axis=0).T)
            mn.append(_dot(bk_t, w_mat[i]))
        g_col = [jnp.sum(jnp.where(diag_p, jnp.exp(cum_end[i]), 0.0), axis=1, keepdims=True) for i in range(n_it)]
        states = [st_ref[j] for j in range(n_pairs)]
        for i, (rows, lanes, j) in enumerate(items):
            st = states[j]
            st_bf = _bf(st)
            y_s[rows, lanes] = _dot(_bf(rt[i] + qw[i][:, 0:LANES]), st_bf) + qw[i][:, LANES:]
            m_low = _bf(jnp.where(same_head, mn[i][:, 0:LANES], 0.0))
            states[j] = g_col[i] * st + _dot(m_low, st_bf) + jnp.where(same_head, mn[i][:, LANES:], 0.0)
        for j in range(n_pairs):
            st_ref[j] = states[j]
        return carry

    lax.fori_loop(0, tt // (c * GROUP_CHUNKS), group_body, 0)

    y = y_s[...]
    inv_n = 1.0 / HEAD
    mean = _dot_x2(y, bd) * inv_n
    d = y - mean
    var = _dot_x2(d * d, bd) * inv_n
    y = d * lax.rsqrt(var + RWKV_GN_EPS) * lnw_ref[...] + lnb_ref[...]
    r = r_s[...]
    bonus = _dot_x2(r * k_s[...] * rk_ref[...], bd) * v_s[...]
    o_ref[...] = ((y + bonus) * g).astype(o_ref.dtype)


def _rwkv(p_rwkv, vecs, lora_w, bd, batch, seq, dim):
    n, width = p_rwkv.shape
    tt = 256
    n_t = seq // tt
    vec_spec = lambda a: pl.BlockSpec(a.shape, lambda b, t: (0, 0))
    act = pltpu.VMEM((tt, dim), F32)
    return pl.pallas_call(
        functools.partial(_rwkv_kernel, tt=tt, dim=dim),
        grid=(batch, n_t),
        in_specs=[pl.BlockSpec((tt, width), lambda b, t: (b * n_t + t, 0))]
                 + [vec_spec(a) for a in vecs] + [vec_spec(lora_w), vec_spec(bd)],
        out_specs=pl.BlockSpec((tt, dim), lambda b, t: (b * n_t + t, 0)),
        out_shape=jax.ShapeDtypeStruct((n, dim), BF16),
        scratch_shapes=[pltpu.VMEM((tt + 8, width), F32),
                        pltpu.VMEM((dim // LANES, LANES, LANES), F32),
                        act, act, act, act, act, act, act],
        compiler_params=_params("parallel", "arbitrary"),
        name="rwkv7",
    )(p_rwkv, *vecs, lora_w, bd)


def _conv_kernel(p_ref, w_ref, b_ref, lnw_ref, lnb_ref, o_ref, hbuf, *, tt, dim):
    sub = 64

    @pl.when(pl.program_id(1) == 0)
    def _():
        hbuf[0:CONV_HALO, :] = jnp.zeros((CONV_HALO, dim), F32)

    hbuf[CONV_HALO:CONV_HALO + tt, :] = p_ref[:, 0:dim] * _sigmoid(p_ref[:, dim:2 * dim])
    first = CONV_HALO - (CONV_WIDTH - 1)
    for s in range(tt // sub):
        acc = jnp.zeros((sub, dim), F32) + b_ref[...]
        for j in range(CONV_WIDTH):
            lo = s * sub + first + j
            acc = acc + hbuf[lo:lo + sub, :] * w_ref[j:j + 1, :]
        mu = jnp.mean(acc, axis=-1, keepdims=True)
        d = acc - mu
        var = jnp.mean(d * d, axis=-1, keepdims=True)
        y = d * lax.rsqrt(var + LN_EPS) * lnw_ref[...] + lnb_ref[...]
        o_ref[s * sub:(s + 1) * sub, :] = (y * _sigmoid(y)).astype(o_ref.dtype)
    hbuf[0:CONV_HALO, :] = hbuf[tt:tt + CONV_HALO, :]


def _conv(p_conv, conv_w, conv_b, ln_w, ln_b, batch, seq):
    n, width = p_conv.shape
    dim = width // 2
    tt = 512
    n_t = seq // tt
    vec = lambda a: pl.BlockSpec(a.shape, lambda b, t: (0, 0))
    return pl.pallas_call(
        functools.partial(_conv_kernel, tt=tt, dim=dim),
        grid=(batch, n_t),
        in_specs=[pl.BlockSpec((tt, width), lambda b, t: (b * n_t + t, 0)),
                  vec(conv_w), vec(conv_b), vec(ln_w), vec(ln_b)],
        out_specs=pl.BlockSpec((tt, dim), lambda b, t: (b * n_t + t, 0)),
        out_shape=jax.ShapeDtypeStruct((n, dim), BF16),
        scratch_shapes=[pltpu.VMEM((tt + CONV_HALO, dim), F32)],
        compiler_params=_params("parallel", "arbitrary"),
        name="conformer_conv",
    )(p_conv, conv_w, conv_b, ln_w, ln_b)


def _mlaprep_kernel(p_ref, cos_ref, sin_ref, qn_ref, kvn_ref, wuq_ref, wuk_ref, wuv_ref,
                    q_ref, k_ref, v_ref, *, lora, n_heads, scale):
    cos = cos_ref[...]
    sin = sin_ref[...]
    qn = (_rms(p_ref[:, 0:lora]) * qn_ref[...]).astype(BF16)
    kvn = (_rms(p_ref[:, lora:2 * lora]) * kvn_ref[...]).astype(BF16)
    kpe = p_ref[:, 2 * lora:2 * lora + LANES] * cos + p_ref[:, 2 * lora + LANES:2 * lora + 2 * LANES] * sin
    lane = lax.broadcasted_iota(jnp.int32, cos.shape, 1)
    q_mul = jnp.where(lane < HEAD, 1.0, jnp.where(lane < HEAD + 32, cos, sin)) * scale
    for h in range(n_heads):
        cols = slice(h * LANES, (h + 1) * LANES)
        q_ref[:, cols] = (_dot(qn, wuq_ref[:, cols]) * q_mul).astype(BF16)
        k_ref[:, cols] = (_dot(kvn, wuk_ref[:, cols]) + kpe).astype(BF16)
    v_ref[...] = _dot(kvn, wuv_ref[...]).astype(BF16)


def _mlaprep(p_mla, cos, sin, q_norm, kv_norm, wuq, wuk, wuv, lora, n_heads, scale):
    n, width = p_mla.shape
    tm = 512
    row = lambda w: pl.BlockSpec((tm, w), lambda i: (i, 0))
    full = lambda a: pl.BlockSpec(a.shape, lambda i: (0, 0))
    return pl.pallas_call(
        functools.partial(_mlaprep_kernel, lora=lora, n_heads=n_heads, scale=scale),
        grid=(n // tm,),
        in_specs=[row(width), row(LANES), row(LANES), full(q_norm), full(kv_norm), full(wuq), full(wuk), full(wuv)],
        out_specs=[row(n_heads * LANES), row(n_heads * LANES), row(n_heads * HEAD)],
        out_shape=[jax.ShapeDtypeStruct((n, n_heads * LANES), BF16),
                   jax.ShapeDtypeStruct((n, n_heads * LANES), BF16),
                   jax.ShapeDtypeStruct((n, n_heads * HEAD), BF16)],
        compiler_params=_params("parallel"),
        name="mla_prep",
    )(p_mla, cos, sin, q_norm, kv_norm, wuq, wuk, wuv)


def _flash_kernel(q_ref, k_ref, v_ref, o_ref, m_s, l_s, acc_s):
    i = pl.program_id(2)
    t = FLASH_BLOCK
    m_s[...] = jnp.full(m_s.shape, -jnp.inf, F32)
    l_s[...] = jnp.zeros(l_s.shape, F32)
    acc_s[...] = jnp.zeros(acc_s.shape, F32)
    head0 = lax.broadcasted_iota(jnp.int32, (t, LANES), 1) < HEAD
    causal = lax.broadcasted_iota(jnp.int32, (t, t), 0) >= lax.broadcasted_iota(jnp.int32, (t, t), 1)

    def step(j, subs):
        keys = pl.ds(pl.multiple_of(j * t, t), t)
        vb = v_ref[keys, :]
        chains = [(u, h, masked) for u, masked in subs for h in range(2)]
        s_all = [_dot_nt(q_ref[u * t:(u + 1) * t, h * LANES:(h + 1) * LANES], k_ref[keys, h * LANES:(h + 1) * LANES])
                 for u, h, _ in chains]
        alpha, probs = [], []
        for (u, h, masked), s in zip(chains, s_all):
            if masked:
                s = jnp.where(causal, s, -jnp.inf)
            halves = [s[:, n * LANES:(n + 1) * LANES] for n in range(t // LANES)]
            m_prev = m_s[u, h]
            m_new = jnp.maximum(m_prev, jnp.max(functools.reduce(jnp.maximum, halves), axis=-1, keepdims=True))
            al = jnp.exp2(m_prev - m_new)
            p = [jnp.exp2(x - m_new) for x in halves]
            l_s[u, h] = al * l_s[u, h] + jnp.sum(functools.reduce(jnp.add, p), axis=-1, keepdims=True)
            m_s[u, h] = m_new
            alpha.append(al)
            probs.append(jnp.concatenate(p, axis=1).astype(BF16))
        pv = [_dot(p, vb) for p in probs]
        for n, (u, _) in enumerate(subs):
            acc_s[u] = (jnp.where(head0, alpha[2 * n], alpha[2 * n + 1]) * acc_s[u]
                        + jnp.where(head0, pv[2 * n], pv[2 * n + 1]))

    def body(j, carry):
        step(j, [(u, False) for u in range(FLASH_SUBS)])
        return carry

    lax.fori_loop(0, FLASH_SUBS * i, body, 0)
    for d in range(FLASH_SUBS):
        step(FLASH_SUBS * i + d, [(d, True)] + [(u, False) for u in range(d + 1, FLASH_SUBS)])
    for u in range(FLASH_SUBS):
        o_ref[u * t:(u + 1) * t, :] = (acc_s[u] / jnp.where(head0, l_s[u, 0], l_s[u, 1])).astype(o_ref.dtype)


def _flash(q, k, v, batch, seq, n_heads):
    n = q.shape[0]
    tq = FLASH_SUBS * FLASH_BLOCK
    n_q = seq // tq
    stat = pltpu.VMEM((FLASH_SUBS, 2, FLASH_BLOCK, LANES), F32)
    return pl.pallas_call(
        _flash_kernel,
        grid=(batch, n_heads // 2, n_q),
        in_specs=[pl.BlockSpec((tq, 2 * LANES), lambda b, h, i: (b * n_q + i, h)),
                  pl.BlockSpec((seq, 2 * LANES), lambda b, h, i: (b, h)),
                  pl.BlockSpec((seq, LANES), lambda b, h, i: (b, h))],
        out_specs=pl.BlockSpec((tq, LANES), lambda b, h, i: (b * n_q + i, h)),
        out_shape=jax.ShapeDtypeStruct((n, n_heads * HEAD), BF16),
        scratch_shapes=[stat, stat, pltpu.VMEM((FLASH_SUBS, FLASH_BLOCK, LANES), F32)],
        compiler_params=_params("parallel", "parallel", "arbitrary"),
        name="mla_flash",
    )(q, k, v)


def _outffn_kernel(ya_ref, yb_ref, yc_ref, x_ref, mod_ref, gpm_ref, gpf_ref, gqf_ref, wo_ref, w1_ref, w2_ref,
                   o_ref, x1_s, hid_s, *, da, db, tf):
    y = (_dot(ya_ref[...], wo_ref[0:da, :])
         + _dot(yb_ref[...], wo_ref[da:da + db, :])
         + _dot(yc_ref[...], wo_ref[da + db:, :]))
    x1 = x_ref[...] + mod_ref[0, 2:3, :] * (_rms(y) * gpm_ref[...])
    x1_s[...] = x1
    h = _rms(x1) * gpf_ref[...]
    h = (h * (1.0 + mod_ref[0, 4:5, :]) + mod_ref[0, 3:4, :]).astype(BF16)
    for f in range(hid_s.shape[1] // tf):
        cols = slice(f * tf, (f + 1) * tf)
        hid = jnp.maximum(_dot(h, w1_ref[:, cols]), 0.0)
        hid_s[:, cols] = (hid * hid).astype(BF16)
    y2 = _dot(hid_s[...], w2_ref[...])
    o_ref[...] = x1_s[...] + mod_ref[0, 5:6, :] * (_rms(y2) * gqf_ref[...])


def _outffn(ya, yb, yc, x, mod, g_post_mix, g_pre_ffn, g_post_ffn, w_out, w_ff1, w_ff2, seq):
    n, d = x.shape
    d_ff = w_ff1.shape[1]
    tm, tf = 512, 512
    per_b = seq // tm
    row = lambda w: pl.BlockSpec((tm, w), lambda i: (i, 0))
    vec = pl.BlockSpec((1, d), lambda i: (0, 0))
    resident = lambda a: pl.BlockSpec(a.shape, lambda i: (0, 0), pipeline_mode=pl.Buffered(1))
    return pl.pallas_call(
        functools.partial(_outffn_kernel, da=ya.shape[1], db=yb.shape[1], tf=tf),
        grid=(n // tm,),
        in_specs=[row(ya.shape[1]), row(yb.shape[1]), row(yc.shape[1]), row(d),
                  pl.BlockSpec((1, 6, d), lambda i: (i // per_b, 0, 0)),
                  vec, vec, vec, resident(w_out), resident(w_ff1), resident(w_ff2)],
        out_specs=row(d),
        out_shape=jax.ShapeDtypeStruct((n, d), F32),
        scratch_shapes=[pltpu.VMEM((tm, d), F32), pltpu.VMEM((tm, d_ff), BF16)],
        compiler_params=_params("parallel"),
        name="out_ffn",
    )(ya, yb, yc, x, mod, g_post_mix, g_pre_ffn, g_post_ffn, w_out, w_ff1, w_ff2)


def _rot_cols(w):
    half = w.shape[-1] // 2
    return jnp.concatenate([-w[..., half:], w[..., :half]], axis=-1)


def kernel(x, c, positions, g_pre_mix, g_post_mix, g_pre_ffn, g_post_ffn, w_ada, b_ada, w_in, w_out, rwkv_mu, rwkv_w0, rwkv_w2, rwkv_a0, rwkv_a2, rwkv_g2, rwkv_k_k, rwkv_k_a, rwkv_r_k, rwkv_ln_w, rwkv_ln_b, conv_w, conv_b, conv_ln_w, conv_ln_b, mla_q_norm, mla_w_uq, mla_kv_norm, mla_w_ukv, w_ff1, w_ff2):
    batch, seq, d = x.shape
    n = batch * seq
    n_layers = w_ada.shape[0]
    r_dim = rwkv_w0.shape[1]
    w_lora, a_lora, g_lora = rwkv_w2.shape[1], rwkv_a2.shape[1], rwkv_g2.shape[1]
    assert w_lora + a_lora + g_lora == LANES and r_dim % LANES == 0
    n_r = 3 * r_dim + LANES
    c_dim = conv_w.shape[2]
    n_c = 2 * c_dim
    lora = mla_q_norm.shape[1]
    n_heads = mla_w_ukv.shape[2] // (2 * HEAD)
    rope = mla_w_uq.shape[2] // n_heads - HEAD
    assert HEAD + 2 * rope == LANES
    n_m = 2 * lora + 2 * LANES
    scale = float(HEAD + rope) ** -0.5 * 1.4426950408889634

    xf = x.reshape(n, d)
    c8 = jnp.pad(c, ((0, 8 - batch), (0, 0)))
    mod_all = _ada(c8, w_ada, b_ada.reshape(n_layers, 1, 6 * d))[:, :batch].reshape(n_layers, batch, 6, d)

    inv_freq = 1.0 / (ROPE_THETA ** (jnp.arange(0, rope, 2, dtype=F32) / rope))
    invf = jnp.tile(inv_freq, LANES // inv_freq.shape[0]).reshape(1, LANES)
    cos, sin = _rope_tables(positions.reshape(n, 1), invf)

    head_id = jnp.arange(r_dim) // HEAD
    bd = (head_id[:, None] == head_id[None, :]).astype(BF16)
    row1 = lambda a: a.reshape(1, -1)

    for l in range(n_layers):
        wi = w_in[l]
        k_pe = wi[:, n_r + n_c + 2 * lora:]
        z_half = jnp.zeros((d, HEAD), F32)
        w_ext = jnp.concatenate([wi[:, :n_r + n_c + 2 * lora],
                                 z_half, k_pe, k_pe,
                                 z_half, _rot_cols(k_pe), _rot_cols(k_pe)], axis=1).astype(BF16)
        wq = mla_w_uq[l].reshape(lora, n_heads, HEAD + rope)
        wuq = jnp.concatenate([wq, _rot_cols(wq[..., HEAD:])], axis=-1).reshape(lora, n_heads * LANES).astype(BF16)
        wkv = mla_w_ukv[l].reshape(lora, n_heads, 2 * HEAD)
        wuk = jnp.concatenate([wkv[..., :HEAD], jnp.zeros_like(wkv[..., :HEAD])], axis=-1)
        wuk = wuk.reshape(lora, n_heads * LANES).astype(BF16)
        wuv = wkv[..., HEAD:].reshape(lora, n_heads * HEAD).astype(BF16)
        lora_w = jnp.zeros((LANES, 3 * r_dim), F32)
        lora_w = lora_w.at[0:w_lora, 0:r_dim].set(rwkv_w2[l])
        lora_w = lora_w.at[w_lora:w_lora + a_lora, r_dim:2 * r_dim].set(rwkv_a2[l])
        lora_w = lora_w.at[w_lora + a_lora:, 2 * r_dim:].set(rwkv_g2[l])

        mod = mod_all[l]
        p_rwkv, p_conv, p_mla = _inproj(xf, mod, row1(g_pre_mix[l]), w_ext, seq, n_r, n_c, n_m)
        vecs = [row1(rwkv_mu[l]), row1(rwkv_w0[l]), row1(rwkv_a0[l]), row1(rwkv_k_k[l]), row1(rwkv_k_a[l]),
                row1(rwkv_r_k[l]), row1(rwkv_ln_w[l]), row1(rwkv_ln_b[l])]
        y_a = _rwkv(p_rwkv, vecs, lora_w, bd, batch, seq, r_dim)
        y_b = _conv(p_conv, conv_w[l], row1(conv_b[l]), row1(conv_ln_w[l]), row1(conv_ln_b[l]), batch, seq)
        q, k, v = _mlaprep(p_mla, cos, sin, row1(mla_q_norm[l]), row1(mla_kv_norm[l]), wuq, wuk, wuv,
                           lora, n_heads, scale)
        y_c = _flash(q, k, v, batch, seq, n_heads)
        xf = _outffn(y_a, y_b, y_c, xf, mod, row1(g_post_mix[l]), row1(g_pre_ffn[l]), row1(g_post_ffn[l]),
                     w_out[l].astype(BF16), w_ff1[l].astype(BF16), w_ff2[l].astype(BF16), seq)
    return xf.reshape(batch, seq, d)
```

```python
import functools

import jax
import jax.numpy as jnp
from jax import lax
from jax.experimental import pallas as pl
from jax.experimental.pallas import tpu as pltpu

F32 = jnp.float32
BF16 = jnp.bfloat16
HIGHEST = lax.Precision.HIGHEST

LANES = 128
HEAD = 64
NORM_EPS = 1e-6
LN_EPS = 1e-5
RWKV_GN_EPS = 64e-5
ROPE_THETA = 10000.0
CONV_WIDTH = 31
CONV_HALO = 32
CHUNK = 64
FLASH_BLOCK = 256
FLASH_SUBS = 2
GROUP_CHUNKS = 2
VMEM_LIMIT = 56 * 1024 * 1024


def _dot(a, b, precision=None):
    return jnp.dot(a, b, preferred_element_type=F32, precision=precision)


def _dot_nt(a, b, precision=None):
    return lax.dot_general(a, b, (((1,), (1,)), ((), ())), preferred_element_type=F32, precision=precision)


def _bf(x):
    return x.astype(BF16)


def _split(x):
    hi = x.astype(BF16)
    return hi, (x - hi.astype(F32)).astype(BF16)


def _head_sums(x, ones_bd):
    groups = [_dot(_bf(x[:, g * LANES:(g + 1) * LANES]), ones_bd) for g in range(x.shape[1] // LANES)]
    return jnp.concatenate(groups, axis=1)


def _rms(x):
    return x * lax.rsqrt(jnp.mean(x * x, axis=-1, keepdims=True) + NORM_EPS)


def _sigmoid(x):
    return 1.0 / (1.0 + jnp.exp(-x))


def _params(*semantics):
    return pltpu.CompilerParams(dimension_semantics=semantics, vmem_limit_bytes=VMEM_LIMIT)


def _ada_kernel(c_ref, w_ref, b_ref, o_ref):
    c = c_ref[...]
    cs = c * _sigmoid(c)
    o_ref[0] = _dot(cs, w_ref[0], HIGHEST) + b_ref[0]


def _ada(c8, w_ada, b_ada):
    n_layers, d, d6 = w_ada.shape
    tn = 1536
    return pl.pallas_call(
        _ada_kernel,
        grid=(n_layers, d6 // tn),
        in_specs=[pl.BlockSpec((8, d), lambda l, j: (0, 0)),
                  pl.BlockSpec((1, d, tn), lambda l, j: (l, 0, j)),
                  pl.BlockSpec((1, 1, tn), lambda l, j: (l, 0, j))],
        out_specs=pl.BlockSpec((1, 8, tn), lambda l, j: (l, 0, j)),
        out_shape=jax.ShapeDtypeStruct((n_layers, 8, d6), F32),
        compiler_params=_params("parallel", "parallel"),
        name="ada_mod",
    )(c8, w_ada, b_ada)


def _rope_kernel(pos_ref, invf_ref, cos_ref, sin_ref):
    ang = pos_ref[...].astype(F32) * invf_ref[...]
    cos_ref[...] = jnp.cos(ang)
    sin_ref[...] = jnp.sin(ang)


def _rope_tables(pos_col, invf):
    n = pos_col.shape[0]
    tm = 2048
    spec = pl.BlockSpec((tm, LANES), lambda i: (i, 0))
    return pl.pallas_call(
        _rope_kernel,
        grid=(n // tm,),
        in_specs=[pl.BlockSpec((tm, 1), lambda i: (i, 0)), pl.BlockSpec((1, LANES), lambda i: (0, 0))],
        out_specs=[spec, spec],
        out_shape=[jax.ShapeDtypeStruct((n, LANES), F32)] * 2,
        compiler_params=_params("parallel"),
        name="rope_tables",
    )(pos_col, invf)


def _inproj_kernel(x_ref, mod_ref, g_ref, w_ref, pr_ref, pc_ref, pm_ref, *, n_r, n_c):
    h = _rms(x_ref[...]) * g_ref[...]
    h = (h * (1.0 + mod_ref[0, 1:2, :]) + mod_ref[0, 0:1, :]).astype(BF16)
    pr_ref[...] = _dot(h, w_ref[:, :n_r])
    pc_ref[...] = _dot(h, w_ref[:, n_r:n_r + n_c])
    pm_ref[...] = _dot(h, w_ref[:, n_r + n_c:])


def _inproj(x, mod, g, w_ext, seq, n_r, n_c, n_m):
    n, d = x.shape
    tm = 512
    per_b = seq // tm
    return pl.pallas_call(
        functools.partial(_inproj_kernel, n_r=n_r, n_c=n_c),
        grid=(n // tm,),
        in_specs=[pl.BlockSpec((tm, d), lambda i: (i, 0)),
                  pl.BlockSpec((1, 6, d), lambda i: (i // per_b, 0, 0)),
                  pl.BlockSpec((1, d), lambda i: (0, 0)),
                  pl.BlockSpec(w_ext.shape, lambda i: (0, 0))],
        out_specs=[pl.BlockSpec((tm, n_r), lambda i: (i, 0)),
                   pl.BlockSpec((tm, n_c), lambda i: (i, 0)),
                   pl.BlockSpec((tm, n_m), lambda i: (i, 0))],
        out_shape=[jax.ShapeDtypeStruct((n, n_r), F32),
                   jax.ShapeDtypeStruct((n, n_c), F32),
                   jax.ShapeDtypeStruct((n, n_m), F32)],
        compiler_params=_params("parallel"),
        name="in_proj",
    )(x, mod, g, w_ext)


def _rwkv_kernel(p_ref, mu_ref, w0_ref, a0_ref, kk_ref, ka_ref, rk_ref, lnw_ref, lnb_ref, lora_ref, bd_ref,
                 o_ref, zbuf, st_ref, r_s, k_s, v_s, a_s, b_s, ld_s, y_s, *, tt, dim):
    n_pairs = dim // LANES
    c = CHUNK

    @pl.when(pl.program_id(1) == 0)
    def _():
        zbuf[0:8, :] = jnp.zeros((8, zbuf.shape[1]), F32)
        st_ref[...] = jnp.zeros(st_ref.shape, F32)

    p = p_ref[...]
    zbuf[8:8 + tt, :] = p
    prev = zbuf[7:7 + tt, :]
    zbuf[7:8, :] = p_ref[tt - 1:tt, :]
    z = p + (prev - p) * mu_ref[...]

    r = z[:, 0:dim]
    k = z[:, dim:2 * dim]
    v = z[:, 2 * dim:3 * dim]
    lo = z[:, 3 * dim:3 * dim + LANES]
    lane = lax.broadcasted_iota(jnp.int32, lo.shape, 1)
    act = jnp.where(lane < 32, jnp.tanh(lo), jnp.where(lane < 64, lo, _sigmoid(lo)))
    lora = _dot(_bf(act), lora_ref[...])
    x_w = w0_ref[...] + lora[:, 0:dim]
    w_log = -(jnp.maximum(-x_w, 0.0) + jnp.log(1.0 + jnp.exp(-jnp.abs(x_w)))) - 0.5
    a = _sigmoid(a0_ref[...] + lora[:, dim:2 * dim])
    g = lora[:, 2 * dim:3 * dim]
    bd = bd_ref[...]
    kk = k * kk_ref[...]
    kk = kk * lax.rsqrt(jnp.maximum(_head_sums(kk * kk, bd), 1e-24))
    k = k * (1.0 + (a - 1.0) * ka_ref[...])
    r_s[...] = r
    k_s[...] = k
    v_s[...] = v
    a_s[...] = -kk
    b_s[...] = kk * a
    ld_s[...] = -jnp.exp(w_log)

    row_c = lax.broadcasted_iota(jnp.int32, (c, c), 0)
    col_c = lax.broadcasted_iota(jnp.int32, (c, c), 1)
    tri_incl = (row_c >= col_c).astype(BF16)
    eye_c = (row_c == col_c).astype(F32)
    row_w = lax.broadcasted_iota(jnp.int32, (c, 2 * c), 0)
    col_w = lax.broadcasted_iota(jnp.int32, (c, 2 * c), 1)
    col_in = jnp.where(col_w >= c, col_w - c, col_w)
    strict_w = row_w > col_in
    incl_w = row_w >= col_in
    second_w = col_w >= c
    lane_c = lax.broadcasted_iota(jnp.int32, (c, LANES), 1)
    head0 = lane_c < HEAD
    lane_2 = lax.broadcasted_iota(jnp.int32, (c, 2 * LANES), 1)
    head0_2 = jnp.where(lane_2 >= LANES, lane_2 - LANES, lane_2) < HEAD
    row_p = lax.broadcasted_iota(jnp.int32, (LANES, LANES), 0)
    col_p = lax.broadcasted_iota(jnp.int32, (LANES, LANES), 1)
    same_head = (row_p < HEAD) == (col_p < HEAD)
    diag_p = row_p == col_p
    zeros_cl = jnp.zeros((c, LANES), BF16)

    def group_body(gi, carry):
        base = gi * (GROUP_CHUNKS * c)
        items = [(pl.ds(pl.multiple_of(base + q * c, c), c), slice(j * LANES, (j + 1) * LANES), j)
                 for q in range(GROUP_CHUNKS) for j in range(n_pairs)]
        n_it = len(items)
        loaded = [[ref[rows, lanes] for ref in (r_s, k_s, v_s, a_s, b_s, ld_s)] for rows, lanes, _ in items]
        r_c, k_c, v_c, a_c, b_c, ld_c = [list(t) for t in zip(*loaded)]
        cum = []
        for i in range(n_it):
            ld_hi, ld_lo = _split(ld_c[i])
            cum.append(_dot(tri_incl, ld_hi) + _dot(tri_incl, ld_lo))
        cum_end = [x[c - 1:c, :] for x in cum]
        at = [a_c[i] * jnp.exp(cum[i] - ld_c[i]) for i in range(n_it)]
        rt = [r_c[i] * jnp.exp(cum[i]) for i in range(n_it)]
        v_bf = [_bf(x) for x in v_c]
        aa = []
        for i in range(n_it):
            e_neg = jnp.exp(-cum[i])
            lhs = _bf(jnp.concatenate([jnp.where(head0, at[i], 0.0), jnp.where(head0, 0.0, at[i]),
                                       jnp.where(head0, rt[i], 0.0), jnp.where(head0, 0.0, rt[i])], axis=0))
            rhs = _bf(jnp.concatenate([b_c[i] * e_neg, k_c[i] * e_neg], axis=0))
            aa.append(_dot_nt(lhs, rhs))
        heads = [(i, h) for i in range(n_it) for h in range(2)]
        t_mat = [jnp.where(strict_w, aa[i][h * c:(h + 1) * c, :], 0.0) for i, h in heads]
        q_mat = [_bf(jnp.where(incl_w, aa[i][(2 + h) * c:(3 + h) * c, :], 0.0)) for i, h in heads]
        akv = [_dot(_bf(jnp.where(second_w, t_mat[n], 0.0)), jnp.concatenate([v_bf[i], v_bf[i]], axis=0))
               for n, (i, h) in enumerate(heads)]
        l_pow = [t[:, 0:c] for t in t_mat]
        inv = [eye_c + l for l in l_pow]
        for _ in range(5):
            l_bf = [_bf(l) for l in l_pow]
            l_pow = [_dot(l, l) for l in l_bf]
            inv = [inv[n] + _dot(_bf(inv[n]), _bf(l_pow[n])) for n in range(len(heads))]
        t_inv = [_bf(x) for x in inv]
        x_mat = [_bf(jnp.concatenate([at[i], jnp.where(head0, akv[2 * i], akv[2 * i + 1])], axis=1))
                 for i in range(n_it)]
        tx = [jnp.where(head0_2, _dot(t_inv[2 * i], x_mat[i]), _dot(t_inv[2 * i + 1], x_mat[i])) for i in range(n_it)]
        w_mat = [jnp.concatenate([_bf(tx[i]), jnp.concatenate([zeros_cl, v_bf[i]], axis=1)], axis=0)
                 for i in range(n_it)]
        qw = [jnp.where(head0_2, _dot(q_mat[2 * i], w_mat[i]), _dot(q_mat[2 * i + 1], w_mat[i])) for i in range(n_it)]
        mn = []
        for i in range(n_it):
            e_end = jnp.exp(cum_end[i] - cum[i])
            bk_t = _bf(jnp.concatenate([b_c[i] * e_end, k_c[i] * e_end], axis=0).T)
            mn.append(_dot(bk_t, w_mat[i]))
        g_col = [jnp.sum(jnp.where(diag_p, jnp.exp(cum_end[i]), 0.0), axis=1, keepdims=True) for i in range(n_it)]
        states = [st_ref[j] for j in range(n_pairs)]
        for i, (rows, lanes, j) in enumerate(items):
            st = states[j]
            st_bf = _bf(st)
            y_s[rows, lanes] = _dot(_bf(rt[i] + qw[i][:, 0:LANES]), st_bf) + qw[i][:, LANES:]
            m_low = _bf(jnp.where(same_head, mn[i][:, 0:LANES], 0.0))
            states[j] = g_col[i] * st + _dot(m_low, st_bf) + jnp.where(same_head, mn[i][:, LANES:], 0.0)
        for j in range(n_pairs):
            st_ref[j] = states[j]
        return carry

    lax.fori_loop(0, tt // (c * GROUP_CHUNKS), group_body, 0)

    y = y_s[...]
    inv_n = 1.0 / HEAD
    mean = _head_sums(y, bd) * inv_n
    d = y - mean
    var = _head_sums(d * d, bd) * inv_n
    y = d * lax.rsqrt(var + RWKV_GN_EPS) * lnw_ref[...] + lnb_ref[...]
    r = r_s[...]
    bonus = _head_sums(r * k_s[...] * rk_ref[...], bd) * v_s[...]
    o_ref[...] = ((y + bonus) * g).astype(o_ref.dtype)


def _rwkv(p_rwkv, vecs, lora_w, bd, batch, seq, dim):
    n, width = p_rwkv.shape
    tt = 256
    n_t = seq // tt
    vec_spec = lambda a: pl.BlockSpec(a.shape, lambda b, t: (0, 0))
    act = pltpu.VMEM((tt, dim), F32)
    return pl.pallas_call(
        functools.partial(_rwkv_kernel, tt=tt, dim=dim),
        grid=(batch, n_t),
        in_specs=[pl.BlockSpec((tt, width), lambda b, t: (b * n_t + t, 0))]
                 + [vec_spec(a) for a in vecs] + [vec_spec(lora_w), vec_spec(bd)],
        out_specs=pl.BlockSpec((tt, dim), lambda b, t: (b * n_t + t, 0)),
        out_shape=jax.ShapeDtypeStruct((n, dim), BF16),
        scratch_shapes=[pltpu.VMEM((tt + 8, width), F32),
                        pltpu.VMEM((dim // LANES, LANES, LANES), F32),
                        act, act, act, act, act, act, act],
        compiler_params=_params("parallel", "arbitrary"),
        name="rwkv7",
    )(p_rwkv, *vecs, lora_w, bd)


def _conv_kernel(p_ref, w_ref, b_ref, lnw_ref, lnb_ref, o_ref, hbuf, *, tt, dim):
    sub = 64

    @pl.when(pl.program_id(1) == 0)
    def _():
        hbuf[0:CONV_HALO, :] = jnp.zeros((CONV_HALO, dim), F32)

    hbuf[CONV_HALO:CONV_HALO + tt, :] = p_ref[:, 0:dim] * _sigmoid(p_ref[:, dim:2 * dim])
    first = CONV_HALO - (CONV_WIDTH - 1)
    for s in range(tt // sub):
        acc = jnp.zeros((sub, dim), F32) + b_ref[...]
        for j in range(CONV_WIDTH):
            lo = s * sub + first + j
            acc = acc + hbuf[lo:lo + sub, :] * w_ref[j:j + 1, :]
        mu = jnp.mean(acc, axis=-1, keepdims=True)
        d = acc - mu
        var = jnp.mean(d * d, axis=-1, keepdims=True)
        y = d * lax.rsqrt(var + LN_EPS) * lnw_ref[...] + lnb_ref[...]
        o_ref[s * sub:(s + 1) * sub, :] = (y * _sigmoid(y)).astype(o_ref.dtype)
    hbuf[0:CONV_HALO, :] = hbuf[tt:tt + CONV_HALO, :]


def _conv(p_conv, conv_w, conv_b, ln_w, ln_b, batch, seq):
    n, width = p_conv.shape
    dim = width // 2
    tt = 512
    n_t = seq // tt
    vec = lambda a: pl.BlockSpec(a.shape, lambda b, t: (0, 0))
    return pl.pallas_call(
        functools.partial(_conv_kernel, tt=tt, dim=dim),
        grid=(batch, n_t),
        in_specs=[pl.BlockSpec((tt, width), lambda b, t: (b * n_t + t, 0)),
                  vec(conv_w), vec(conv_b), vec(ln_w), vec(ln_b)],
        out_specs=pl.BlockSpec((tt, dim), lambda b, t: (b * n_t + t, 0)),
        out_shape=jax.ShapeDtypeStruct((n, dim), BF16),
        scratch_shapes=[pltpu.VMEM((tt + CONV_HALO, dim), F32)],
        compiler_params=_params("parallel", "arbitrary"),
        name="conformer_conv",
    )(p_conv, conv_w, conv_b, ln_w, ln_b)


def _mlaprep_kernel(p_ref, cos_ref, sin_ref, qn_ref, kvn_ref, wuq_ref, wuk_ref, wuv_ref,
                    q_ref, k_ref, v_ref, *, lora, n_heads, scale):
    cos = cos_ref[...]
    sin = sin_ref[...]
    qn = (_rms(p_ref[:, 0:lora]) * qn_ref[...]).astype(BF16)
    kvn = (_rms(p_ref[:, lora:2 * lora]) * kvn_ref[...]).astype(BF16)
    kpe = p_ref[:, 2 * lora:2 * lora + LANES] * cos + p_ref[:, 2 * lora + LANES:2 * lora + 2 * LANES] * sin
    lane = lax.broadcasted_iota(jnp.int32, cos.shape, 1)
    q_mul = jnp.where(lane < HEAD, 1.0, jnp.where(lane < HEAD + 32, cos, sin)) * scale
    for h in range(n_heads):
        cols = slice(h * LANES, (h + 1) * LANES)
        q_ref[:, cols] = (_dot(qn, wuq_ref[:, cols]) * q_mul).astype(BF16)
        k_ref[:, cols] = (_dot(kvn, wuk_ref[:, cols]) + kpe).astype(BF16)
        v_ref[:, cols] = jnp.where(lane < HEAD, _dot(kvn, wuv_ref[:, cols]), 1.0).astype(BF16)


def _mlaprep(p_mla, cos, sin, q_norm, kv_norm, wuq, wuk, wuv, lora, n_heads, scale):
    n, width = p_mla.shape
    tm = 512
    row = lambda w: pl.BlockSpec((tm, w), lambda i: (i, 0))
    full = lambda a: pl.BlockSpec(a.shape, lambda i: (0, 0))
    return pl.pallas_call(
        functools.partial(_mlaprep_kernel, lora=lora, n_heads=n_heads, scale=scale),
        grid=(n // tm,),
        in_specs=[row(width), row(LANES), row(LANES), full(q_norm), full(kv_norm), full(wuq), full(wuk), full(wuv)],
        out_specs=[row(n_heads * LANES)] * 3,
        out_shape=[jax.ShapeDtypeStruct((n, n_heads * LANES), BF16)] * 3,
        compiler_params=_params("parallel"),
        name="mla_prep",
    )(p_mla, cos, sin, q_norm, kv_norm, wuq, wuk, wuv)


def _flash_kernel(q_ref, k_ref, v_ref, o_ref, m_s, acc_s, s_buf, p_buf, al_buf):
    i = pl.program_id(2)
    t = FLASH_BLOCK
    m_s[...] = jnp.full(m_s.shape, -jnp.inf, F32)
    acc_s[...] = jnp.zeros(acc_s.shape, F32)
    p_buf[1] = jnp.zeros(p_buf.shape[1:], BF16)
    al_buf[1] = jnp.ones(al_buf.shape[1:], F32)
    causal = lax.broadcasted_iota(jnp.int32, (t, t), 0) >= lax.broadcasted_iota(jnp.int32, (t, t), 1)
    all_subs = list(range(FLASH_SUBS))

    def keys_of(j):
        return pl.ds(pl.multiple_of(j * t, t), t)

    def scores(j, slot, subs):
        for u in subs:
            for h in range(2):
                cols = slice(h * LANES, (h + 1) * LANES)
                s_buf[slot, 2 * u + h] = _dot_nt(q_ref[u * t:(u + 1) * t, cols], k_ref[keys_of(j), cols])

    def softmax(slot, subs, masked_sub):
        for u in subs:
            for h in range(2):
                s = s_buf[slot, 2 * u + h]
                if u == masked_sub:
                    s = jnp.where(causal, s, -jnp.inf)
                halves = [s[:, n * LANES:(n + 1) * LANES] for n in range(t // LANES)]
                m_prev = m_s[u, h]
                m_new = jnp.maximum(m_prev, jnp.max(functools.reduce(jnp.maximum, halves), axis=-1, keepdims=True))
                m_s[u, h] = m_new
                al_buf[slot, 2 * u + h] = jnp.exp2(m_prev - m_new)
                p_buf[slot, 2 * u + h] = jnp.concatenate([jnp.exp2((x - m_new).astype(BF16)) for x in halves], axis=1)

    def accumulate(j, slot, subs):
        for u in subs:
            for h in range(2):
                pv = _dot(p_buf[slot, 2 * u + h], v_ref[keys_of(j), h * LANES:(h + 1) * LANES])
                acc_s[u, h] = al_buf[slot, 2 * u + h] * acc_s[u, h] + pv

    n_full = FLASH_SUBS * i
    scores(0, 0, all_subs)

    def body(g, carry):
        for par in range(2):
            k = 2 * g + par
            scores(k + 1, 1 - par, all_subs)
            accumulate(jnp.maximum(k - 1, 0), 1 - par, all_subs)
            softmax(par, all_subs, None)
        return carry

    lax.fori_loop(0, n_full // 2, body, 0)
    accumulate(jnp.maximum(n_full - 1, 0), 1, all_subs)
    for d in range(FLASH_SUBS):
        if d + 1 < FLASH_SUBS:
            scores(n_full + d + 1, (d + 1) % 2, all_subs[d + 1:])
        softmax(d % 2, all_subs[d:], d)
        accumulate(n_full + d, d % 2, all_subs[d:])
    head0 = lax.broadcasted_iota(jnp.int32, (t, LANES), 1) < HEAD
    for u in range(FLASH_SUBS):
        a0, a1 = acc_s[u, 0], acc_s[u, 1]
        out = jnp.where(head0, a0 / pltpu.roll(a0, HEAD, 1), pltpu.roll(a1, HEAD, 1) / a1)
        o_ref[u * t:(u + 1) * t, :] = out.astype(o_ref.dtype)


def _flash(q, k, v, batch, seq, n_heads):
    n = q.shape[0]
    tq = FLASH_SUBS * FLASH_BLOCK
    n_q = seq // tq
    stat = pltpu.VMEM((FLASH_SUBS, 2, FLASH_BLOCK, LANES), F32)
    n_chain = 2 * FLASH_SUBS
    return pl.pallas_call(
        _flash_kernel,
        grid=(batch, n_heads // 2, n_q),
        in_specs=[pl.BlockSpec((tq, 2 * LANES), lambda b, h, i: (b * n_q + i, h)),
                  pl.BlockSpec((seq, 2 * LANES), lambda b, h, i: (b, h)),
                  pl.BlockSpec((seq, 2 * LANES), lambda b, h, i: (b, h))],
        out_specs=pl.BlockSpec((tq, LANES), lambda b, h, i: (b * n_q + i, h)),
        out_shape=jax.ShapeDtypeStruct((n, n_heads * HEAD), BF16),
        scratch_shapes=[stat, stat,
                        pltpu.VMEM((2, n_chain, FLASH_BLOCK, FLASH_BLOCK), F32),
                        pltpu.VMEM((2, n_chain, FLASH_BLOCK, FLASH_BLOCK), BF16),
                        pltpu.VMEM((2, n_chain, FLASH_BLOCK, LANES), F32)],
        compiler_params=_params("parallel", "parallel", "arbitrary"),
        name="mla_flash",
    )(q, k, v)


def _outffn_kernel(ya_ref, yb_ref, yc_ref, x_ref, mod_ref, gpm_ref, gpf_ref, gqf_ref, wo_ref, w1_ref, w2_ref,
                   o_ref, x1_s, hid_s, *, da, db, tf):
    y = (_dot(ya_ref[...], wo_ref[0:da, :])
         + _dot(yb_ref[...], wo_ref[da:da + db, :])
         + _dot(yc_ref[...], wo_ref[da + db:, :]))
    x1 = x_ref[...] + mod_ref[0, 2:3, :] * (_rms(y) * gpm_ref[...])
    x1_s[...] = x1
    h = _rms(x1) * gpf_ref[...]
    h = (h * (1.0 + mod_ref[0, 4:5, :]) + mod_ref[0, 3:4, :]).astype(BF16)
    for f in range(hid_s.shape[1] // tf):
        cols = slice(f * tf, (f + 1) * tf)
        hid = jnp.maximum(_dot(h, w1_ref[:, cols]), 0.0)
        hid_s[:, cols] = (hid * hid).astype(BF16)
    y2 = _dot(hid_s[...], w2_ref[...])
    o_ref[...] = x1_s[...] + mod_ref[0, 5:6, :] * (_rms(y2) * gqf_ref[...])


def _outffn(ya, yb, yc, x, mod, g_post_mix, g_pre_ffn, g_post_ffn, w_out, w_ff1, w_ff2, seq):
    n, d = x.shape
    d_ff = w_ff1.shape[1]
    tm, tf = 512, 512
    per_b = seq // tm
    row = lambda w: pl.BlockSpec((tm, w), lambda i: (i, 0))
    vec = pl.BlockSpec((1, d), lambda i: (0, 0))
    resident = lambda a: pl.BlockSpec(a.shape, lambda i: (0, 0), pipeline_mode=pl.Buffered(1))
    return pl.pallas_call(
        functools.partial(_outffn_kernel, da=ya.shape[1], db=yb.shape[1], tf=tf),
        grid=(n // tm,),
        in_specs=[row(ya.shape[1]), row(yb.shape[1]), row(yc.shape[1]), row(d),
                  pl.BlockSpec((1, 6, d), lambda i: (i // per_b, 0, 0)),
                  vec, vec, vec, resident(w_out), resident(w_ff1), resident(w_ff2)],
        out_specs=row(d),
        out_shape=jax.ShapeDtypeStruct((n, d), F32),
        scratch_shapes=[pltpu.VMEM((tm, d), F32), pltpu.VMEM((tm, d_ff), BF16)],
        compiler_params=_params("parallel"),
        name="out_ffn",
    )(ya, yb, yc, x, mod, g_post_mix, g_pre_ffn, g_post_ffn, w_out, w_ff1, w_ff2)


def _rot_cols(w):
    half = w.shape[-1] // 2
    return jnp.concatenate([-w[..., half:], w[..., :half]], axis=-1)


def kernel(x, c, positions, g_pre_mix, g_post_mix, g_pre_ffn, g_post_ffn, w_ada, b_ada, w_in, w_out, rwkv_mu, rwkv_w0, rwkv_w2, rwkv_a0, rwkv_a2, rwkv_g2, rwkv_k_k, rwkv_k_a, rwkv_r_k, rwkv_ln_w, rwkv_ln_b, conv_w, conv_b, conv_ln_w, conv_ln_b, mla_q_norm, mla_w_uq, mla_kv_norm, mla_w_ukv, w_ff1, w_ff2):
    batch, seq, d = x.shape
    n = batch * seq
    n_layers = w_ada.shape[0]
    r_dim = rwkv_w0.shape[1]
    w_lora, a_lora, g_lora = rwkv_w2.shape[1], rwkv_a2.shape[1], rwkv_g2.shape[1]
    assert w_lora + a_lora + g_lora == LANES and r_dim % LANES == 0
    n_r = 3 * r_dim + LANES
    c_dim = conv_w.shape[2]
    n_c = 2 * c_dim
    lora = mla_q_norm.shape[1]
    n_heads = mla_w_ukv.shape[2] // (2 * HEAD)
    rope = mla_w_uq.shape[2] // n_heads - HEAD
    assert HEAD + 2 * rope == LANES
    n_m = 2 * lora + 2 * LANES
    scale = float(HEAD + rope) ** -0.5 * 1.4426950408889634

    xf = x.reshape(n, d)
    c8 = jnp.pad(c, ((0, 8 - batch), (0, 0)))
    mod_all = _ada(c8, w_ada, b_ada.reshape(n_layers, 1, 6 * d))[:, :batch].reshape(n_layers, batch, 6, d)

    inv_freq = 1.0 / (ROPE_THETA ** (jnp.arange(0, rope, 2, dtype=F32) / rope))
    invf = jnp.tile(inv_freq, LANES // inv_freq.shape[0]).reshape(1, LANES)
    cos, sin = _rope_tables(positions.reshape(n, 1), invf)

    head_id = jnp.arange(LANES) // HEAD
    bd = (head_id[:, None] == head_id[None, :]).astype(BF16)
    row1 = lambda a: a.reshape(1, -1)

    for l in range(n_layers):
        wi = w_in[l]
        k_pe = wi[:, n_r + n_c + 2 * lora:]
        z_half = jnp.zeros((d, HEAD), F32)
        w_ext = jnp.concatenate([wi[:, :n_r + n_c + 2 * lora],
                                 z_half, k_pe, k_pe,
                                 z_half, _rot_cols(k_pe), _rot_cols(k_pe)], axis=1).astype(BF16)
        wq = mla_w_uq[l].reshape(lora, n_heads, HEAD + rope)
        wuq = jnp.concatenate([wq, _rot_cols(wq[..., HEAD:])], axis=-1).reshape(lora, n_heads * LANES).astype(BF16)
        wkv = mla_w_ukv[l].reshape(lora, n_heads, 2 * HEAD)
        wuk = jnp.concatenate([wkv[..., :HEAD], jnp.zeros_like(wkv[..., :HEAD])], axis=-1)
        wuk = wuk.reshape(lora, n_heads * LANES).astype(BF16)
        wuv = jnp.concatenate([wkv[..., HEAD:], jnp.zeros_like(wkv[..., HEAD:])], axis=-1)
        wuv = wuv.reshape(lora, n_heads * LANES).astype(BF16)
        lora_w = jnp.zeros((LANES, 3 * r_dim), F32)
        lora_w = lora_w.at[0:w_lora, 0:r_dim].set(rwkv_w2[l])
        lora_w = lora_w.at[w_lora:w_lora + a_lora, r_dim:2 * r_dim].set(rwkv_a2[l])
        lora_w = lora_w.at[w_lora + a_lora:, 2 * r_dim:].set(rwkv_g2[l]).astype(BF16)

        mod = mod_all[l]
        p_rwkv, p_conv, p_mla = _inproj(xf, mod, row1(g_pre_mix[l]), w_ext, seq, n_r, n_c, n_m)
        vecs = [row1(rwkv_mu[l]), row1(rwkv_w0[l]), row1(rwkv_a0[l]), row1(rwkv_k_k[l]), row1(rwkv_k_a[l]),
                row1(rwkv_r_k[l]), row1(rwkv_ln_w[l]), row1(rwkv_ln_b[l])]
        y_a = _rwkv(p_rwkv, vecs, lora_w, bd, batch, seq, r_dim)
        y_b = _conv(p_conv, conv_w[l], row1(conv_b[l]), row1(conv_ln_w[l]), row1(conv_ln_b[l]), batch, seq)
        q, k, v = _mlaprep(p_mla, cos, sin, row1(mla_q_norm[l]), row1(mla_kv_norm[l]), wuq, wuk, wuv,
                           lora, n_heads, scale)
        y_c = _flash(q, k, v, batch, seq, n_heads)
        xf = _outffn(y_a, y_b, y_c, xf, mod, row1(g_post_mix[l]), row1(g_pre_ffn[l]), row1(g_post_ffn[l]),
                     w_out[l].astype(BF16), w_ff1[l].astype(BF16), w_ff2[l].astype(BF16), seq)
    return xf.reshape(batch, seq, d)
```

```python
import functools

import jax
import jax.numpy as jnp
from jax import lax
from jax.experimental import pallas as pl
from jax.experimental.pallas import tpu as pltpu

F32 = jnp.float32
BF16 = jnp.bfloat16
HIGHEST = lax.Precision.HIGHEST

LANES = 128
HEAD = 64
NORM_EPS = 1e-6
LN_EPS = 1e-5
RWKV_GN_EPS = 64e-5
ROPE_THETA = 10000.0
CONV_WIDTH = 31
CONV_HALO = 32
CHUNK = 64
FLASH_BLOCK = 256
FLASH_SUBS = 2
GROUP_CHUNKS = 8
VMEM_LIMIT = 56 * 1024 * 1024


def _dot(a, b, precision=None):
    return jnp.dot(a, b, preferred_element_type=F32, precision=precision)


def _dot_nt(a, b, precision=None):
    return lax.dot_general(a, b, (((1,), (1,)), ((), ())), preferred_element_type=F32, precision=precision)


def _bf(x):
    return x.astype(BF16)


def _split(x):
    hi = x.astype(BF16)
    return hi, (x - hi.astype(F32)).astype(BF16)


def _head_sums(x, ones_bd):
    groups = [_dot(_bf(x[:, g * LANES:(g + 1) * LANES]), ones_bd) for g in range(x.shape[1] // LANES)]
    return jnp.concatenate(groups, axis=1)


def _rms(x):
    return x * lax.rsqrt(jnp.mean(x * x, axis=-1, keepdims=True) + NORM_EPS)


def _sigmoid(x):
    return 1.0 / (1.0 + jnp.exp(-x))


def _params(*semantics):
    return pltpu.CompilerParams(dimension_semantics=semantics, vmem_limit_bytes=VMEM_LIMIT)


def _ada_kernel(c_ref, w_ref, b_ref, o_ref):
    c = c_ref[...]
    cs = c * _sigmoid(c)
    o_ref[0] = _dot(cs, w_ref[0], HIGHEST) + b_ref[0]


def _ada(c8, w_ada, b_ada):
    n_layers, d, d6 = w_ada.shape
    tn = 1536
    return pl.pallas_call(
        _ada_kernel,
        grid=(n_layers, d6 // tn),
        in_specs=[pl.BlockSpec((8, d), lambda l, j: (0, 0)),
                  pl.BlockSpec((1, d, tn), lambda l, j: (l, 0, j)),
                  pl.BlockSpec((1, 1, tn), lambda l, j: (l, 0, j))],
        out_specs=pl.BlockSpec((1, 8, tn), lambda l, j: (l, 0, j)),
        out_shape=jax.ShapeDtypeStruct((n_layers, 8, d6), F32),
        compiler_params=_params("parallel", "parallel"),
        name="ada_mod",
    )(c8, w_ada, b_ada)


def _rope_kernel(pos_ref, invf_ref, cos_ref, sin_ref):
    ang = pos_ref[...].astype(F32) * invf_ref[...]
    cos_ref[...] = jnp.cos(ang)
    sin_ref[...] = jnp.sin(ang)


def _rope_tables(pos_col, invf):
    n = pos_col.shape[0]
    tm = 2048
    spec = pl.BlockSpec((tm, LANES), lambda i: (i, 0))
    return pl.pallas_call(
        _rope_kernel,
        grid=(n // tm,),
        in_specs=[pl.BlockSpec((tm, 1), lambda i: (i, 0)), pl.BlockSpec((1, LANES), lambda i: (0, 0))],
        out_specs=[spec, spec],
        out_shape=[jax.ShapeDtypeStruct((n, LANES), F32)] * 2,
        compiler_params=_params("parallel"),
        name="rope_tables",
    )(pos_col, invf)


def _inproj_kernel(x_ref, mod_ref, g_ref, w_ref, pr_ref, pc_ref, pm_ref, *, n_r, n_c):
    h = _rms(x_ref[...]) * g_ref[...]
    h = (h * (1.0 + mod_ref[0, 1:2, :]) + mod_ref[0, 0:1, :]).astype(BF16)
    pr_ref[...] = _dot(h, w_ref[:, :n_r]).astype(pr_ref.dtype)
    pc_ref[...] = _dot(h, w_ref[:, n_r:n_r + n_c]).astype(pc_ref.dtype)
    pm_ref[...] = _dot(h, w_ref[:, n_r + n_c:]).astype(pm_ref.dtype)


def _inproj(x, mod, g, w_ext, seq, n_r, n_c, n_m):
    n, d = x.shape
    tm = 512
    per_b = seq // tm
    return pl.pallas_call(
        functools.partial(_inproj_kernel, n_r=n_r, n_c=n_c),
        grid=(n // tm,),
        in_specs=[pl.BlockSpec((tm, d), lambda i: (i, 0)),
                  pl.BlockSpec((1, 6, d), lambda i: (i // per_b, 0, 0)),
                  pl.BlockSpec((1, d), lambda i: (0, 0)),
                  pl.BlockSpec(w_ext.shape, lambda i: (0, 0))],
        out_specs=[pl.BlockSpec((tm, n_r), lambda i: (i, 0)),
                   pl.BlockSpec((tm, n_c), lambda i: (i, 0)),
                   pl.BlockSpec((tm, n_m), lambda i: (i, 0))],
        out_shape=[jax.ShapeDtypeStruct((n, n_r), BF16),
                   jax.ShapeDtypeStruct((n, n_c), BF16),
                   jax.ShapeDtypeStruct((n, n_m), BF16)],
        compiler_params=_params("parallel"),
        name="in_proj",
    )(x, mod, g, w_ext)


def _rwkv_kernel(p_ref, mu_ref, w0_ref, a0_ref, kk_ref, ka_ref, rk_ref, lnw_ref, lnb_ref, lora_ref, bd_ref,
                 o_ref, zbuf, st_ref, r_s, k_s, v_s, a_s, b_s, ld_s, y_s, *, tt, dim):
    n_pairs = dim // LANES
    c = CHUNK

    @pl.when(pl.program_id(1) == 0)
    def _():
        zbuf[0:8, :] = jnp.zeros((8, zbuf.shape[1]), F32)
        st_ref[...] = jnp.zeros(st_ref.shape, F32)

    p = p_ref[...].astype(F32)
    zbuf[8:8 + tt, :] = p
    prev = zbuf[7:7 + tt, :]
    zbuf[7:8, :] = zbuf[tt + 7:tt + 8, :]
    z = p + (prev - p) * mu_ref[...]

    r = z[:, 0:dim]
    k = z[:, dim:2 * dim]
    v = z[:, 2 * dim:3 * dim]
    lo = z[:, 3 * dim:3 * dim + LANES]
    lane = lax.broadcasted_iota(jnp.int32, lo.shape, 1)
    act = jnp.where(lane < 32, jnp.tanh(lo), jnp.where(lane < 64, lo, _sigmoid(lo)))
    lora = _dot(_bf(act), lora_ref[...])
    x_w = w0_ref[...] + lora[:, 0:dim]
    w_log = -(jnp.maximum(-x_w, 0.0) + jnp.log(1.0 + jnp.exp(-jnp.abs(x_w)))) - 0.5
    a = _sigmoid(a0_ref[...] + lora[:, dim:2 * dim])
    g = lora[:, 2 * dim:3 * dim]
    bd = bd_ref[...]
    kk = k * kk_ref[...]
    kk = kk * lax.rsqrt(jnp.maximum(_head_sums(kk * kk, bd), 1e-24))
    k = k * (1.0 + (a - 1.0) * ka_ref[...])
    r_s[...] = r
    k_s[...] = k
    v_s[...] = v
    a_s[...] = -kk
    b_s[...] = kk * a
    ld_s[...] = -jnp.exp(w_log)

    row_c = lax.broadcasted_iota(jnp.int32, (c, c), 0)
    col_c = lax.broadcasted_iota(jnp.int32, (c, c), 1)
    tri_incl = (row_c >= col_c).astype(BF16)
    eye_c = (row_c == col_c).astype(F32)
    row_w = lax.broadcasted_iota(jnp.int32, (c, 2 * c), 0)
    col_w = lax.broadcasted_iota(jnp.int32, (c, 2 * c), 1)
    col_in = jnp.where(col_w >= c, col_w - c, col_w)
    strict_w = row_w > col_in
    incl_w = row_w >= col_in
    second_w = col_w >= c
    lane_c = lax.broadcasted_iota(jnp.int32, (c, LANES), 1)
    head0 = lane_c < HEAD
    lane_2 = lax.broadcasted_iota(jnp.int32, (c, 2 * LANES), 1)
    head0_2 = jnp.where(lane_2 >= LANES, lane_2 - LANES, lane_2) < HEAD
    row_p = lax.broadcasted_iota(jnp.int32, (LANES, LANES), 0)
    col_p = lax.broadcasted_iota(jnp.int32, (LANES, LANES), 1)
    same_head = (row_p < HEAD) == (col_p < HEAD)
    diag_p = row_p == col_p
    zeros_cl = jnp.zeros((c, LANES), BF16)

    def group_body(gi, carry):
        base = gi * (GROUP_CHUNKS * c)
        items = [(pl.ds(pl.multiple_of(base + q * c, c), c), slice(j * LANES, (j + 1) * LANES), j)
                 for q in range(GROUP_CHUNKS) for j in range(n_pairs)]
        n_it = len(items)
        loaded = [[ref[rows, lanes] for ref in (r_s, k_s, v_s, a_s, b_s, ld_s)] for rows, lanes, _ in items]
        r_c, k_c, v_c, a_c, b_c, ld_c = [list(t) for t in zip(*loaded)]
        cum = []
        for i in range(n_it):
            ld_hi, ld_lo = _split(ld_c[i])
            cum.append(_dot(tri_incl, ld_hi) + _dot(tri_incl, ld_lo))
        cum_end = [x[c - 1:c, :] for x in cum]
        at = [a_c[i] * jnp.exp(cum[i] - ld_c[i]) for i in range(n_it)]
        rt = [r_c[i] * jnp.exp(cum[i]) for i in range(n_it)]
        v_bf = [_bf(x) for x in v_c]
        aa = []
        for i in range(n_it):
            e_neg = jnp.exp(-cum[i])
            lhs = _bf(jnp.concatenate([jnp.where(head0, at[i], 0.0), jnp.where(head0, 0.0, at[i]),
                                       jnp.where(head0, rt[i], 0.0), jnp.where(head0, 0.0, rt[i])], axis=0))
            rhs = _bf(jnp.concatenate([b_c[i] * e_neg, k_c[i] * e_neg], axis=0))
            aa.append(_dot_nt(lhs, rhs))
        heads = [(i, h) for i in range(n_it) for h in range(2)]
        t_mat = [jnp.where(strict_w, aa[i][h * c:(h + 1) * c, :], 0.0) for i, h in heads]
        q_mat = [_bf(jnp.where(incl_w, aa[i][(2 + h) * c:(3 + h) * c, :], 0.0)) for i, h in heads]
        akv = [_dot(_bf(jnp.where(second_w, t_mat[n], 0.0)), jnp.concatenate([v_bf[i], v_bf[i]], axis=0))
               for n, (i, h) in enumerate(heads)]
        l_pow = [t[:, 0:c] for t in t_mat]
        inv = [eye_c + l for l in l_pow]
        for _ in range(5):
            l_bf = [_bf(l) for l in l_pow]
            l_pow = [_dot(l, l) for l in l_bf]
            inv = [inv[n] + _dot(_bf(inv[n]), _bf(l_pow[n])) for n in range(len(heads))]
        t_inv = [_bf(x) for x in inv]
        x_mat = [_bf(jnp.concatenate([at[i], jnp.where(head0, akv[2 * i], akv[2 * i + 1])], axis=1))
                 for i in range(n_it)]
        tx = [jnp.where(head0_2, _dot(t_inv[2 * i], x_mat[i]), _dot(t_inv[2 * i + 1], x_mat[i])) for i in range(n_it)]
        w_mat = [jnp.concatenate([_bf(tx[i]), jnp.concatenate([zeros_cl, v_bf[i]], axis=1)], axis=0)
                 for i in range(n_it)]
        qw = [jnp.where(head0_2, _dot(q_mat[2 * i], w_mat[i]), _dot(q_mat[2 * i + 1], w_mat[i])) for i in range(n_it)]
        mn = []
        for i in range(n_it):
            e_end = jnp.exp(cum_end[i] - cum[i])
            bk_t = _bf(jnp.concatenate([b_c[i] * e_end, k_c[i] * e_end], axis=0).T)
            mn.append(_dot(bk_t, w_mat[i]))
        g_col = [jnp.sum(jnp.where(diag_p, jnp.exp(cum_end[i]), 0.0), axis=1, keepdims=True) for i in range(n_it)]
        states = [st_ref[j] for j in range(n_pairs)]
        for i, (rows, lanes, j) in enumerate(items):
            st = states[j]
            st_bf = _bf(st)
            y_s[rows, lanes] = _dot(_bf(rt[i] + qw[i][:, 0:LANES]), st_bf) + qw[i][:, LANES:]
            m_low = _bf(jnp.where(same_head, mn[i][:, 0:LANES], 0.0))
            states[j] = g_col[i] * st + _dot(m_low, st_bf) + jnp.where(same_head, mn[i][:, LANES:], 0.0)
        for j in range(n_pairs):
            st_ref[j] = states[j]
        return carry

    lax.fori_loop(0, tt // (c * GROUP_CHUNKS), group_body, 0)

    y = y_s[...]
    inv_n = 1.0 / HEAD
    mean = _head_sums(y, bd) * inv_n
    d = y - mean
    var = _head_sums(d * d, bd) * inv_n
    y = d * lax.rsqrt(var + RWKV_GN_EPS) * lnw_ref[...] + lnb_ref[...]
    r = r_s[...]
    bonus = _head_sums(r * k_s[...] * rk_ref[...], bd) * v_s[...]
    o_ref[...] = ((y + bonus) * g).astype(o_ref.dtype)


def _rwkv(p_rwkv, vecs, lora_w, bd, batch, seq, dim):
    n, width = p_rwkv.shape
    tt = CHUNK * GROUP_CHUNKS
    n_t = seq // tt
    vec_spec = lambda a: pl.BlockSpec(a.shape, lambda b, t: (0, 0))
    act = pltpu.VMEM((tt, dim), F32)
    return pl.pallas_call(
        functools.partial(_rwkv_kernel, tt=tt, dim=dim),
        grid=(batch, n_t),
        in_specs=[pl.BlockSpec((tt, width), lambda b, t: (b * n_t + t, 0))]
                 + [vec_spec(a) for a in vecs] + [vec_spec(lora_w), vec_spec(bd)],
        out_specs=pl.BlockSpec((tt, dim), lambda b, t: (b * n_t + t, 0)),
        out_shape=jax.ShapeDtypeStruct((n, dim), BF16),
        scratch_shapes=[pltpu.VMEM((tt + 8, width), F32),
                        pltpu.VMEM((dim // LANES, LANES, LANES), F32),
                        act, act, act, act, act, act, act],
        compiler_params=_params("parallel", "arbitrary"),
        name="rwkv7",
    )(p_rwkv, *vecs, lora_w, bd)


def _conv_kernel(p_ref, w_ref, b_ref, lnw_ref, lnb_ref, o_ref, hbuf, *, tt, dim):
    sub = 64

    @pl.when(pl.program_id(1) == 0)
    def _():
        hbuf[0:CONV_HALO, :] = jnp.zeros((CONV_HALO, dim), F32)

    hbuf[CONV_HALO:CONV_HALO + tt, :] = p_ref[:, 0:dim].astype(F32) * _sigmoid(p_ref[:, dim:2 * dim].astype(F32))
    first = CONV_HALO - (CONV_WIDTH - 1)
    for s in range(tt // sub):
        acc = jnp.zeros((sub, dim), F32) + b_ref[...]
        for j in range(CONV_WIDTH):
            lo = s * sub + first + j
            acc = acc + hbuf[lo:lo + sub, :] * w_ref[j:j + 1, :]
        mu = jnp.mean(acc, axis=-1, keepdims=True)
        d = acc - mu
        var = jnp.mean(d * d, axis=-1, keepdims=True)
        y = d * lax.rsqrt(var + LN_EPS) * lnw_ref[...] + lnb_ref[...]
        o_ref[s * sub:(s + 1) * sub, :] = (y * _sigmoid(y)).astype(o_ref.dtype)
    hbuf[0:CONV_HALO, :] = hbuf[tt:tt + CONV_HALO, :]


def _conv(p_conv, conv_w, conv_b, ln_w, ln_b, batch, seq):
    n, width = p_conv.shape
    dim = width // 2
    tt = 512
    n_t = seq // tt
    vec = lambda a: pl.BlockSpec(a.shape, lambda b, t: (0, 0))
    return pl.pallas_call(
        functools.partial(_conv_kernel, tt=tt, dim=dim),
        grid=(batch, n_t),
        in_specs=[pl.BlockSpec((tt, width), lambda b, t: (b * n_t + t, 0)),
                  vec(conv_w), vec(conv_b), vec(ln_w), vec(ln_b)],
        out_specs=pl.BlockSpec((tt, dim), lambda b, t: (b * n_t + t, 0)),
        out_shape=jax.ShapeDtypeStruct((n, dim), BF16),
        scratch_shapes=[pltpu.VMEM((tt + CONV_HALO, dim), F32)],
        compiler_params=_params("parallel", "arbitrary"),
        name="conformer_conv",
    )(p_conv, conv_w, conv_b, ln_w, ln_b)


def _mlaprep_kernel(p_ref, cos_ref, sin_ref, qn_ref, kvn_ref, wuq_ref, wuk_ref, wuv_ref,
                    q_ref, k_ref, v_ref, *, lora, n_heads, scale):
    cos = cos_ref[...]
    sin = sin_ref[...]
    qn = (_rms(p_ref[:, 0:lora].astype(F32)) * qn_ref[...]).astype(BF16)
    kvn = (_rms(p_ref[:, lora:2 * lora].astype(F32)) * kvn_ref[...]).astype(BF16)
    kpe = (p_ref[:, 2 * lora:2 * lora + LANES].astype(F32) * cos
           + p_ref[:, 2 * lora + LANES:2 * lora + 2 * LANES].astype(F32) * sin)
    lane = lax.broadcasted_iota(jnp.int32, cos.shape, 1)
    q_mul = jnp.where(lane < HEAD, 1.0, jnp.where(lane < HEAD + 32, cos, sin)) * scale
    for h in range(n_heads):
        cols = slice(h * LANES, (h + 1) * LANES)
        q_ref[:, cols] = (_dot(qn, wuq_ref[:, cols]) * q_mul).astype(BF16)
        k_ref[:, cols] = (_dot(kvn, wuk_ref[:, cols]) + kpe).astype(BF16)
        v_ref[:, cols] = jnp.where(lane < HEAD, _dot(kvn, wuv_ref[:, cols]), 1.0).astype(BF16)


def _mlaprep(p_mla, cos, sin, q_norm, kv_norm, wuq, wuk, wuv, lora, n_heads, scale):
    n, width = p_mla.shape
    tm = 512
    row = lambda w: pl.BlockSpec((tm, w), lambda i: (i, 0))
    full = lambda a: pl.BlockSpec(a.shape, lambda i: (0, 0))
    return pl.pallas_call(
        functools.partial(_mlaprep_kernel, lora=lora, n_heads=n_heads, scale=scale),
        grid=(n // tm,),
        in_specs=[row(width), row(LANES), row(LANES), full(q_norm), full(kv_norm), full(wuq), full(wuk), full(wuv)],
        out_specs=[row(n_heads * LANES)] * 3,
        out_shape=[jax.ShapeDtypeStruct((n, n_heads * LANES), BF16)] * 3,
        compiler_params=_params("parallel"),
        name="mla_prep",
    )(p_mla, cos, sin, q_norm, kv_norm, wuq, wuk, wuv)


def _flash_kernel(q_ref, k_ref, v_ref, o_ref, m_s, acc_s, s_buf, p_buf, al_buf):
    i = pl.program_id(2)
    t = FLASH_BLOCK
    m_s[...] = jnp.full(m_s.shape, -jnp.inf, F32)
    acc_s[...] = jnp.zeros(acc_s.shape, F32)
    p_buf[1] = jnp.zeros(p_buf.shape[1:], BF16)
    al_buf[1] = jnp.ones(al_buf.shape[1:], F32)
    causal = lax.broadcasted_iota(jnp.int32, (t, t), 0) >= lax.broadcasted_iota(jnp.int32, (t, t), 1)
    all_subs = list(range(FLASH_SUBS))

    def keys_of(j):
        return pl.ds(pl.multiple_of(j * t, t), t)

    def scores(j, slot, subs):
        for u in subs:
            for h in range(2):
                cols = slice(h * LANES, (h + 1) * LANES)
                s_buf[slot, 2 * u + h] = _dot_nt(q_ref[u * t:(u + 1) * t, cols], k_ref[keys_of(j), cols])

    def softmax(slot, subs, masked_sub):
        for u in subs:
            for h in range(2):
                s = s_buf[slot, 2 * u + h]
                if u == masked_sub:
                    s = jnp.where(causal, s, -jnp.inf)
                halves = [s[:, n * LANES:(n + 1) * LANES] for n in range(t // LANES)]
                m_prev = m_s[u, h]
                m_new = jnp.maximum(m_prev, jnp.max(functools.reduce(jnp.maximum, halves), axis=-1, keepdims=True))
                m_s[u, h] = m_new
                al_buf[slot, 2 * u + h] = jnp.exp2(m_prev - m_new)
                p_buf[slot, 2 * u + h] = jnp.concatenate([jnp.exp2((x - m_new).astype(BF16)) for x in halves], axis=1)

    def accumulate(j, slot, subs):
        for u in subs:
            for h in range(2):
                pv = _dot(p_buf[slot, 2 * u + h], v_ref[keys_of(j), h * LANES:(h + 1) * LANES])
                acc_s[u, h] = al_buf[slot, 2 * u + h] * acc_s[u, h] + pv

    n_full = FLASH_SUBS * i
    scores(0, 0, all_subs)

    def body(g, carry):
        for par in range(2):
            k = 2 * g + par
            scores(k + 1, 1 - par, all_subs)
            accumulate(jnp.maximum(k - 1, 0), 1 - par, all_subs)
            softmax(par, all_subs, None)
        return carry

    lax.fori_loop(0, n_full // 2, body, 0)
    accumulate(jnp.maximum(n_full - 1, 0), 1, all_subs)
    for d in range(FLASH_SUBS):
        if d + 1 < FLASH_SUBS:
            scores(n_full + d + 1, (d + 1) % 2, all_subs[d + 1:])
        softmax(d % 2, all_subs[d:], d)
        accumulate(n_full + d, d % 2, all_subs[d:])
    head0 = lax.broadcasted_iota(jnp.int32, (t, LANES), 1) < HEAD
    for u in range(FLASH_SUBS):
        a0, a1 = acc_s[u, 0], acc_s[u, 1]
        out = jnp.where(head0, a0 / pltpu.roll(a0, HEAD, 1), pltpu.roll(a1, HEAD, 1) / a1)
        o_ref[u * t:(u + 1) * t, :] = out.astype(o_ref.dtype)


def _flash(q, k, v, batch, seq, n_heads):
    n = q.shape[0]
    tq = FLASH_SUBS * FLASH_BLOCK
    n_q = seq // tq
    stat = pltpu.VMEM((FLASH_SUBS, 2, FLASH_BLOCK, LANES), F32)
    n_chain = 2 * FLASH_SUBS
    return pl.pallas_call(
        _flash_kernel,
        grid=(batch, n_heads // 2, n_q),
        in_specs=[pl.BlockSpec((tq, 2 * LANES), lambda b, h, i: (b * n_q + i, h)),
                  pl.BlockSpec((seq, 2 * LANES), lambda b, h, i: (b, h)),
                  pl.BlockSpec((seq, 2 * LANES), lambda b, h, i: (b, h))],
        out_specs=pl.BlockSpec((tq, LANES), lambda b, h, i: (b * n_q + i, h)),
        out_shape=jax.ShapeDtypeStruct((n, n_heads * HEAD), BF16),
        scratch_shapes=[stat, stat,
                        pltpu.VMEM((2, n_chain, FLASH_BLOCK, FLASH_BLOCK), F32),
                        pltpu.VMEM((2, n_chain, FLASH_BLOCK, FLASH_BLOCK), BF16),
                        pltpu.VMEM((2, n_chain, FLASH_BLOCK, LANES), F32)],
        compiler_params=_params("parallel", "parallel", "arbitrary"),
        name="mla_flash",
    )(q, k, v)


def _outffn_kernel(ya_ref, yb_ref, yc_ref, x_ref, mod_ref, gpm_ref, gpf_ref, gqf_ref, wo_ref, w1_ref, w2_ref,
                   o_ref, x1_s, hid_s, *, da, db, tf):
    y = (_dot(ya_ref[...], wo_ref[0:da, :])
         + _dot(yb_ref[...], wo_ref[da:da + db, :])
         + _dot(yc_ref[...], wo_ref[da + db:, :]))
    x1 = x_ref[...] + mod_ref[0, 2:3, :] * (_rms(y) * gpm_ref[...])
    x1_s[...] = x1
    h = _rms(x1) * gpf_ref[...]
    h = (h * (1.0 + mod_ref[0, 4:5, :]) + mod_ref[0, 3:4, :]).astype(BF16)
    for f in range(hid_s.shape[1] // tf):
        cols = slice(f * tf, (f + 1) * tf)
        hid = jnp.maximum(_dot(h, w1_ref[:, cols]), 0.0)
        hid_s[:, cols] = (hid * hid).astype(BF16)
    y2 = _dot(hid_s[...], w2_ref[...])
    o_ref[...] = x1_s[...] + mod_ref[0, 5:6, :] * (_rms(y2) * gqf_ref[...])


def _outffn(ya, yb, yc, x, mod, g_post_mix, g_pre_ffn, g_post_ffn, w_out, w_ff1, w_ff2, seq):
    n, d = x.shape
    d_ff = w_ff1.shape[1]
    tm, tf = 512, 512
    per_b = seq // tm
    row = lambda w: pl.BlockSpec((tm, w), lambda i: (i, 0))
    vec = pl.BlockSpec((1, d), lambda i: (0, 0))
    resident = lambda a: pl.BlockSpec(a.shape, lambda i: (0, 0), pipeline_mode=pl.Buffered(1))
    return pl.pallas_call(
        functools.partial(_outffn_kernel, da=ya.shape[1], db=yb.shape[1], tf=tf),
        grid=(n // tm,),
        in_specs=[row(ya.shape[1]), row(yb.shape[1]), row(yc.shape[1]), row(d),
                  pl.BlockSpec((1, 6, d), lambda i: (i // per_b, 0, 0)),
                  vec, vec, vec, resident(w_out), resident(w_ff1), resident(w_ff2)],
        out_specs=row(d),
        out_shape=jax.ShapeDtypeStruct((n, d), F32),
        scratch_shapes=[pltpu.VMEM((tm, d), F32), pltpu.VMEM((tm, d_ff), BF16)],
        compiler_params=_params("parallel"),
        name="out_ffn",
    )(ya, yb, yc, x, mod, g_post_mix, g_pre_ffn, g_post_ffn, w_out, w_ff1, w_ff2)


def _rot_cols(w):
    half = w.shape[-1] // 2
    return jnp.concatenate([-w[..., half:], w[..., :half]], axis=-1)


def kernel(x, c, positions, g_pre_mix, g_post_mix, g_pre_ffn, g_post_ffn, w_ada, b_ada, w_in, w_out, rwkv_mu, rwkv_w0, rwkv_w2, rwkv_a0, rwkv_a2, rwkv_g2, rwkv_k_k, rwkv_k_a, rwkv_r_k, rwkv_ln_w, rwkv_ln_b, conv_w, conv_b, conv_ln_w, conv_ln_b, mla_q_norm, mla_w_uq, mla_kv_norm, mla_w_ukv, w_ff1, w_ff2):
    batch, seq, d = x.shape
    n = batch * seq
    n_layers = w_ada.shape[0]
    r_dim = rwkv_w0.shape[1]
    w_lora, a_lora, g_lora = rwkv_w2.shape[1], rwkv_a2.shape[1], rwkv_g2.shape[1]
    assert w_lora + a_lora + g_lora == LANES and r_dim % LANES == 0
    n_r = 3 * r_dim + LANES
    c_dim = conv_w.shape[2]
    n_c = 2 * c_dim
    lora = mla_q_norm.shape[1]
    n_heads = mla_w_ukv.shape[2] // (2 * HEAD)
    rope = mla_w_uq.shape[2] // n_heads - HEAD
    assert HEAD + 2 * rope == LANES
    n_m = 2 * lora + 2 * LANES
    scale = float(HEAD + rope) ** -0.5 * 1.4426950408889634

    xf = x.reshape(n, d)
    c8 = jnp.pad(c, ((0, 8 - batch), (0, 0)))
    mod_all = _ada(c8, w_ada, b_ada.reshape(n_layers, 1, 6 * d))[:, :batch].reshape(n_layers, batch, 6, d)

    inv_freq = 1.0 / (ROPE_THETA ** (jnp.arange(0, rope, 2, dtype=F32) / rope))
    invf = jnp.tile(inv_freq, LANES // inv_freq.shape[0]).reshape(1, LANES)
    cos, sin = _rope_tables(positions.reshape(n, 1), invf)

    head_id = jnp.arange(LANES) // HEAD
    bd = (head_id[:, None] == head_id[None, :]).astype(BF16)
    row1 = lambda a: a.reshape(1, -1)

    for l in range(n_layers):
        wi = w_in[l]
        k_pe = wi[:, n_r + n_c + 2 * lora:]
        z_half = jnp.zeros((d, HEAD), F32)
        w_ext = jnp.concatenate([wi[:, :n_r + n_c + 2 * lora],
                                 z_half, k_pe, k_pe,
                                 z_half, _rot_cols(k_pe), _rot_cols(k_pe)], axis=1).astype(BF16)
        wq = mla_w_uq[l].reshape(lora, n_heads, HEAD + rope)
        wuq = jnp.concatenate([wq, _rot_cols(wq[..., HEAD:])], axis=-1).reshape(lora, n_heads * LANES).astype(BF16)
        wkv = mla_w_ukv[l].reshape(lora, n_heads, 2 * HEAD)
        wuk = jnp.concatenate([wkv[..., :HEAD], jnp.zeros_like(wkv[..., :HEAD])], axis=-1)
        wuk = wuk.reshape(lora, n_heads * LANES).astype(BF16)
        wuv = jnp.concatenate([wkv[..., HEAD:], jnp.zeros_like(wkv[..., HEAD:])], axis=-1)
        wuv = wuv.reshape(lora, n_heads * LANES).astype(BF16)
        lora_w = jnp.zeros((LANES, 3 * r_dim), F32)
        lora_w = lora_w.at[0:w_lora, 0:r_dim].set(rwkv_w2[l])
        lora_w = lora_w.at[w_lora:w_lora + a_lora, r_dim:2 * r_dim].set(rwkv_a2[l])
        lora_w = lora_w.at[w_lora + a_lora:, 2 * r_dim:].set(rwkv_g2[l]).astype(BF16)

        mod = mod_all[l]
        p_rwkv, p_conv, p_mla = _inproj(xf, mod, row1(g_pre_mix[l]), w_ext, seq, n_r, n_c, n_m)
        vecs = [row1(rwkv_mu[l]), row1(rwkv_w0[l]), row1(rwkv_a0[l]), row1(rwkv_k_k[l]), row1(rwkv_k_a[l]),
                row1(rwkv_r_k[l]), row1(rwkv_ln_w[l]), row1(rwkv_ln_b[l])]
        y_a = _rwkv(p_rwkv, vecs, lora_w, bd, batch, seq, r_dim)
        y_b = _conv(p_conv, conv_w[l], row1(conv_b[l]), row1(conv_ln_w[l]), row1(conv_ln_b[l]), batch, seq)
        q, k, v = _mlaprep(p_mla, cos, sin, row1(mla_q_norm[l]), row1(mla_kv_norm[l]), wuq, wuk, wuv,
                           lora, n_heads, scale)
        y_c = _flash(q, k, v, batch, seq, n_heads)
        xf = _outffn(y_a, y_b, y_c, xf, mod, row1(g_post_mix[l]), row1(g_pre_ffn[l]), row1(g_post_ffn[l]),
                     w_out[l].astype(BF16), w_ff1[l].astype(BF16), w_ff2[l].astype(BF16), seq)
    return xf.reshape(batch, seq, d)
```

```python
import functools

import jax
import jax.numpy as jnp
from jax import lax
from jax.experimental import pallas as pl
from jax.experimental.pallas import tpu as pltpu

F32 = jnp.float32
BF16 = jnp.bfloat16

LANES = 128
HEAD = 64
NORM_EPS = 1e-6
LN_EPS = 1e-5
RWKV_GN_EPS = 64e-5
ROPE_THETA = 10000.0
CONV_WIDTH = 31
CONV_HALO = 32
CONV_SUB = 128
CHUNK = 64
FLASH_BLOCK = 256
FLASH_SUBS = 2
GROUP_CHUNKS = 8
VMEM_LIMIT = 56 * 1024 * 1024


def _dot(a, b, precision=None):
    return jnp.dot(a, b, preferred_element_type=F32, precision=precision)


def _dot_nt(a, b, precision=None):
    return lax.dot_general(a, b, (((1,), (1,)), ((), ())), preferred_element_type=F32, precision=precision)


def _bf(x):
    return x.astype(BF16)


def _split(x):
    hi = x.astype(BF16)
    return hi, (x - hi.astype(F32)).astype(BF16)


def _head_sums(x, ones_bd):
    groups = [_dot(_bf(x[:, g * LANES:(g + 1) * LANES]), ones_bd) for g in range(x.shape[1] // LANES)]
    return jnp.concatenate(groups, axis=1)


def _rms(x):
    return x * lax.rsqrt(jnp.mean(x * x, axis=-1, keepdims=True) + NORM_EPS)


def _sigmoid(x):
    return 1.0 / (1.0 + jnp.exp(-x))


def _params(*semantics):
    return pltpu.CompilerParams(dimension_semantics=semantics, vmem_limit_bytes=VMEM_LIMIT)


def _ada_kernel(c_ref, w_ref, b_ref, o_ref):
    c = c_ref[...]
    cs = c * _sigmoid(c)
    cs_hi, cs_lo = _split(cs)
    w_hi, w_lo = _split(w_ref[0])
    o_ref[0] = _dot(cs_hi, w_hi) + _dot(cs_hi, w_lo) + _dot(cs_lo, w_hi) + b_ref[0]


def _ada(c8, w_ada, b_ada):
    n_layers, d, d6 = w_ada.shape
    tn = 1536
    return pl.pallas_call(
        _ada_kernel,
        grid=(n_layers, d6 // tn),
        in_specs=[pl.BlockSpec((8, d), lambda l, j: (0, 0)),
                  pl.BlockSpec((1, d, tn), lambda l, j: (l, 0, j)),
                  pl.BlockSpec((1, 1, tn), lambda l, j: (l, 0, j))],
        out_specs=pl.BlockSpec((1, 8, tn), lambda l, j: (l, 0, j)),
        out_shape=jax.ShapeDtypeStruct((n_layers, 8, d6), F32),
        compiler_params=_params("parallel", "parallel"),
        name="ada_mod",
    )(c8, w_ada, b_ada)


def _rope_kernel(pos_ref, invf_ref, cos_ref, sin_ref):
    ang = pos_ref[...].astype(F32) * invf_ref[...]
    cos_ref[...] = jnp.cos(ang)
    sin_ref[...] = jnp.sin(ang)


def _rope_tables(pos_col, invf):
    n = pos_col.shape[0]
    tm = 2048
    spec = pl.BlockSpec((tm, LANES), lambda i: (i, 0))
    return pl.pallas_call(
        _rope_kernel,
        grid=(n // tm,),
        in_specs=[pl.BlockSpec((tm, 1), lambda i: (i, 0)), pl.BlockSpec((1, LANES), lambda i: (0, 0))],
        out_specs=[spec, spec],
        out_shape=[jax.ShapeDtypeStruct((n, LANES), F32)] * 2,
        compiler_params=_params("parallel"),
        name="rope_tables",
    )(pos_col, invf)


def _inproj_kernel(x_ref, mod_ref, g_ref, w_ref, pr_ref, pc_ref, pm_ref, *, n_r, n_c):
    h = _rms(x_ref[...]) * g_ref[...]
    h = (h * (1.0 + mod_ref[0, 1:2, :]) + mod_ref[0, 0:1, :]).astype(BF16)
    pr_ref[...] = _dot(h, w_ref[:, :n_r]).astype(pr_ref.dtype)
    pc_ref[...] = _dot(h, w_ref[:, n_r:n_r + n_c]).astype(pc_ref.dtype)
    pm_ref[...] = _dot(h, w_ref[:, n_r + n_c:]).astype(pm_ref.dtype)


def _inproj(x, mod, g, w_ext, seq, n_r, n_c, n_m):
    n, d = x.shape
    tm = 512
    per_b = seq // tm
    return pl.pallas_call(
        functools.partial(_inproj_kernel, n_r=n_r, n_c=n_c),
        grid=(n // tm,),
        in_specs=[pl.BlockSpec((tm, d), lambda i: (i, 0)),
                  pl.BlockSpec((1, 6, d), lambda i: (i // per_b, 0, 0)),
                  pl.BlockSpec((1, d), lambda i: (0, 0)),
                  pl.BlockSpec(w_ext.shape, lambda i: (0, 0))],
        out_specs=[pl.BlockSpec((tm, n_r), lambda i: (i, 0)),
                   pl.BlockSpec((tm, n_c), lambda i: (i, 0)),
                   pl.BlockSpec((tm, n_m), lambda i: (i, 0))],
        out_shape=[jax.ShapeDtypeStruct((n, n_r), BF16),
                   jax.ShapeDtypeStruct((n, n_c), BF16),
                   jax.ShapeDtypeStruct((n, n_m), BF16)],
        compiler_params=_params("parallel"),
        name="in_proj",
    )(x, mod, g, w_ext)


def _rwkv_kernel(p_ref, mu_ref, w0_ref, a0_ref, kk_ref, ka_ref, rk_ref, lnw_ref, lnb_ref, lora_ref, bd_ref,
                 o_ref, zbuf, st_ref, r_s, k_s, v_s, a_s, b_s, ld_s, y_s, *, tt, dim):
    n_pairs = dim // LANES
    c = CHUNK

    @pl.when(pl.program_id(1) == 0)
    def _():
        zbuf[0:8, :] = jnp.zeros((8, zbuf.shape[1]), F32)
        st_ref[...] = jnp.zeros(st_ref.shape, F32)

    p = p_ref[...].astype(F32)
    zbuf[8:8 + tt, :] = p
    prev = zbuf[7:7 + tt, :]
    zbuf[7:8, :] = zbuf[tt + 7:tt + 8, :]
    z = p + (prev - p) * mu_ref[...]

    r = z[:, 0:dim]
    k = z[:, dim:2 * dim]
    v = z[:, 2 * dim:3 * dim]
    lo = z[:, 3 * dim:3 * dim + LANES]
    lane = lax.broadcasted_iota(jnp.int32, lo.shape, 1)
    act = jnp.where(lane < 32, jnp.tanh(lo), jnp.where(lane < 64, lo, _sigmoid(lo)))
    lora = _dot(_bf(act), lora_ref[...])
    x_w = w0_ref[...] + lora[:, 0:dim]
    w_log = -(jnp.maximum(-x_w, 0.0) + jnp.log(1.0 + jnp.exp(-jnp.abs(x_w)))) - 0.5
    a = _sigmoid(a0_ref[...] + lora[:, dim:2 * dim])
    g = lora[:, 2 * dim:3 * dim]
    bd = bd_ref[...]
    kk = k * kk_ref[...]
    kk = kk * lax.rsqrt(jnp.maximum(_head_sums(kk * kk, bd), 1e-24))
    k = k * (1.0 + (a - 1.0) * ka_ref[...])
    r_s[...] = r
    k_s[...] = k
    v_s[...] = v
    a_s[...] = -kk
    b_s[...] = kk * a
    ld_s[...] = -jnp.exp(w_log)

    row_c = lax.broadcasted_iota(jnp.int32, (c, c), 0)
    col_c = lax.broadcasted_iota(jnp.int32, (c, c), 1)
    tri_incl = (row_c >= col_c).astype(BF16)
    eye_c = (row_c == col_c).astype(F32)
    row_w = lax.broadcasted_iota(jnp.int32, (2 * c, 2 * c), 0)
    col_w = lax.broadcasted_iota(jnp.int32, (2 * c, 2 * c), 1)
    row_in = jnp.where(row_w >= c, row_w - c, row_w)
    col_in = jnp.where(col_w >= c, col_w - c, col_w)
    strict_w = row_in > col_in
    incl_w = row_in >= col_in
    second_w = col_w >= c
    lane_c = lax.broadcasted_iota(jnp.int32, (c, LANES), 1)
    head0 = lane_c < HEAD
    lane_2 = lax.broadcasted_iota(jnp.int32, (c, 2 * LANES), 1)
    head0_2 = jnp.where(lane_2 >= LANES, lane_2 - LANES, lane_2) < HEAD
    row_p = lax.broadcasted_iota(jnp.int32, (LANES, LANES), 0)
    col_p = lax.broadcasted_iota(jnp.int32, (LANES, LANES), 1)
    same_head = (row_p < HEAD) == (col_p < HEAD)
    diag_p = row_p == col_p
    zeros_cl = jnp.zeros((c, LANES), BF16)

    def group_body(gi, carry):
        base = gi * (GROUP_CHUNKS * c)
        items = [(pl.ds(pl.multiple_of(base + q * c, c), c), slice(j * LANES, (j + 1) * LANES), j)
                 for q in range(GROUP_CHUNKS) for j in range(n_pairs)]
        n_it = len(items)
        loaded = [[ref[rows, lanes] for ref in (r_s, k_s, v_s, a_s, b_s, ld_s)] for rows, lanes, _ in items]
        r_c, k_c, v_c, a_c, b_c, ld_c = [list(t) for t in zip(*loaded)]
        cum = []
        for i in range(n_it):
            ld_hi, ld_lo = _split(ld_c[i])
            cum.append(_dot(tri_incl, ld_hi) + _dot(tri_incl, ld_lo))
        cum_end = [x[c - 1:c, :] for x in cum]
        at = [a_c[i] * jnp.exp(cum[i] - ld_c[i]) for i in range(n_it)]
        rt = [r_c[i] * jnp.exp(cum[i]) for i in range(n_it)]
        v_bf = [_bf(x) for x in v_c]
        aa = []
        for i in range(n_it):
            e_neg = jnp.exp(-cum[i])
            lhs = _bf(jnp.concatenate([jnp.where(head0, at[i], 0.0), jnp.where(head0, 0.0, at[i]),
                                       jnp.where(head0, rt[i], 0.0), jnp.where(head0, 0.0, rt[i])], axis=0))
            rhs = _bf(jnp.concatenate([b_c[i] * e_neg, k_c[i] * e_neg], axis=0))
            aa.append(_dot_nt(lhs, rhs))
        t_both = [jnp.where(strict_w, x[0:2 * c, :], 0.0) for x in aa]
        q_both = [_bf(jnp.where(incl_w, x[2 * c:4 * c, :], 0.0)) for x in aa]
        akv = [_dot(_bf(jnp.where(second_w, t_both[i], 0.0)), jnp.concatenate([v_bf[i], v_bf[i]], axis=0))
               for i in range(n_it)]
        n_hd = 2 * n_it
        lower = [t_both[n // 2][(n % 2) * c:(n % 2 + 1) * c, 0:c] for n in range(n_hd)]
        inv = [eye_c + l for l in lower]
        lower_bf = [_bf(l) for l in lower]
        pw = [_bf(_dot(l, l)) for l in lower_bf]
        for _ in range(4):
            both = [_dot(jnp.concatenate([pw[n], _bf(inv[n])], axis=0), pw[n]) for n in range(n_hd)]
            inv = [inv[n] + both[n][c:, :] for n in range(n_hd)]
            pw = [_bf(both[n][0:c, :]) for n in range(n_hd)]
        inv = [inv[n] + _dot(_bf(inv[n]), pw[n]) for n in range(n_hd)]
        x_mat = [_bf(jnp.concatenate([at[i], jnp.where(head0, akv[i][0:c, :], akv[i][c:, :])], axis=1))
                 for i in range(n_it)]
        tx_both = [_dot(_bf(jnp.concatenate([inv[2 * i], inv[2 * i + 1]], axis=0)), x_mat[i]) for i in range(n_it)]
        w_mat = [jnp.concatenate([_bf(jnp.where(head0_2, tx_both[i][0:c, :], tx_both[i][c:, :])),
                                  jnp.concatenate([zeros_cl, v_bf[i]], axis=1)], axis=0) for i in range(n_it)]
        qw_both = [_dot(q_both[i], w_mat[i]) for i in range(n_it)]
        qw = [jnp.where(head0_2, x[0:c, :], x[c:, :]) for x in qw_both]
        mn = []
        for i in range(n_it):
            e_end = jnp.exp(cum_end[i] - cum[i])
            bk_t = _bf(jnp.concatenate([b_c[i] * e_end, k_c[i] * e_end], axis=0).T)
            mn.append(_dot(bk_t, w_mat[i]))
        g_col = [jnp.sum(jnp.where(diag_p, jnp.exp(cum_end[i]), 0.0), axis=1, keepdims=True) for i in range(n_it)]
        states = [st_ref[j] for j in range(n_pairs)]
        for i, (rows, lanes, j) in enumerate(items):
            st = states[j]
            st_bf = _bf(st)
            y_s[rows, lanes] = _dot(_bf(rt[i] + qw[i][:, 0:LANES]), st_bf) + qw[i][:, LANES:]
            m_low = _bf(jnp.where(same_head, mn[i][:, 0:LANES], 0.0))
            states[j] = g_col[i] * st + _dot(m_low, st_bf) + jnp.where(same_head, mn[i][:, LANES:], 0.0)
        for j in range(n_pairs):
            st_ref[j] = states[j]
        return carry

    lax.fori_loop(0, tt // (c * GROUP_CHUNKS), group_body, 0)

    y = y_s[...]
    inv_n = 1.0 / HEAD
    mean = _head_sums(y, bd) * inv_n
    d = y - mean
    var = _head_sums(d * d, bd) * inv_n
    y = d * lax.rsqrt(var + RWKV_GN_EPS) * lnw_ref[...] + lnb_ref[...]
    r = r_s[...]
    bonus = _head_sums(r * k_s[...] * rk_ref[...], bd) * v_s[...]
    o_ref[...] = ((y + bonus) * g).astype(o_ref.dtype)


def _rwkv(p_rwkv, vecs, lora_w, bd, batch, seq, dim):
    n, width = p_rwkv.shape
    tt = CHUNK * GROUP_CHUNKS
    n_t = seq // tt
    vec_spec = lambda a: pl.BlockSpec(a.shape, lambda b, t: (0, 0))
    act = pltpu.VMEM((tt, dim), F32)
    return pl.pallas_call(
        functools.partial(_rwkv_kernel, tt=tt, dim=dim),
        grid=(batch, n_t),
        in_specs=[pl.BlockSpec((tt, width), lambda b, t: (b * n_t + t, 0))]
                 + [vec_spec(a) for a in vecs] + [vec_spec(lora_w), vec_spec(bd)],
        out_specs=pl.BlockSpec((tt, dim), lambda b, t: (b * n_t + t, 0)),
        out_shape=jax.ShapeDtypeStruct((n, dim), BF16),
        scratch_shapes=[pltpu.VMEM((tt + 8, width), F32),
                        pltpu.VMEM((dim // LANES, LANES, LANES), F32),
                        act, act, act, act, act, act, act],
        compiler_params=_params("parallel", "arbitrary"),
        name="rwkv7",
    )(p_rwkv, *vecs, lora_w, bd)


def _conv_kernel(p_ref, w_ref, b_ref, lnw_ref, lnb_ref, shift_ref, o_ref, hbuf, *, tt, dim):
    sub = CONV_SUB
    ext = sub + 8

    @pl.when(pl.program_id(1) == 0)
    def _():
        hbuf[0:CONV_HALO, :] = jnp.zeros((CONV_HALO, dim), F32)

    hbuf[CONV_HALO:CONV_HALO + tt, :] = p_ref[:, 0:dim].astype(F32) * _sigmoid(p_ref[:, dim:2 * dim].astype(F32))
    for s in range(tt // sub):
        base = CONV_HALO + s * sub - 8
        wins = [hbuf[base - 8 * a:base - 8 * a + ext, :] for a in range(CONV_HALO // 8)]
        parts = []
        for r in range(8):
            lags = [8 * a + r for a in range(len(wins)) if 8 * a + r < CONV_WIDTH]
            terms = [wins[m // 8] * w_ref[CONV_WIDTH - 1 - m:CONV_WIDTH - m, :] for m in lags]
            parts.append(functools.reduce(jnp.add, terms))
        stacked = _bf(jnp.concatenate(parts, axis=0))
        acc = _dot(shift_ref[...], stacked) + b_ref[...]
        mu = jnp.mean(acc, axis=-1, keepdims=True)
        d = acc - mu
        var = jnp.mean(d * d, axis=-1, keepdims=True)
        y = d * lax.rsqrt(var + LN_EPS) * lnw_ref[...] + lnb_ref[...]
        o_ref[s * sub:(s + 1) * sub, :] = (y * _sigmoid(y)).astype(o_ref.dtype)
    hbuf[0:CONV_HALO, :] = hbuf[tt:tt + CONV_HALO, :]


def _conv(p_conv, conv_w, conv_b, ln_w, ln_b, batch, seq):
    n, width = p_conv.shape
    dim = width // 2
    tt = 512
    n_t = seq // tt
    ext = CONV_SUB + 8
    u_idx = jnp.arange(CONV_SUB)[:, None]
    col = jnp.arange(8 * ext)[None, :]
    shift = (col % ext == u_idx + 8 - col // ext).astype(BF16)
    vec = lambda a: pl.BlockSpec(a.shape, lambda b, t: (0, 0))
    return pl.pallas_call(
        functools.partial(_conv_kernel, tt=tt, dim=dim),
        grid=(batch, n_t),
        in_specs=[pl.BlockSpec((tt, width), lambda b, t: (b * n_t + t, 0)),
                  vec(conv_w), vec(conv_b), vec(ln_w), vec(ln_b), vec(shift)],
        out_specs=pl.BlockSpec((tt, dim), lambda b, t: (b * n_t + t, 0)),
        out_shape=jax.ShapeDtypeStruct((n, dim), BF16),
        scratch_shapes=[pltpu.VMEM((tt + CONV_HALO, dim), F32)],
        compiler_params=_params("parallel", "arbitrary"),
        name="conformer_conv",
    )(p_conv, conv_w, conv_b, ln_w, ln_b, shift)


def _mlaprep_kernel(p_ref, cos_ref, sin_ref, qn_ref, kvn_ref, wuq_ref, wuk_ref, wuv_ref,
                    q_ref, k_ref, v_ref, *, lora, n_heads, scale):
    cos = cos_ref[...]
    sin = sin_ref[...]
    qn = (_rms(p_ref[:, 0:lora].astype(F32)) * qn_ref[...]).astype(BF16)
    kvn = (_rms(p_ref[:, lora:2 * lora].astype(F32)) * kvn_ref[...]).astype(BF16)
    kpe = (p_ref[:, 2 * lora:2 * lora + LANES].astype(F32) * cos
           + p_ref[:, 2 * lora + LANES:2 * lora + 2 * LANES].astype(F32) * sin)
    lane = lax.broadcasted_iota(jnp.int32, cos.shape, 1)
    q_mul = jnp.where(lane < HEAD, 1.0, jnp.where(lane < HEAD + 32, cos, sin)) * scale
    for h in range(n_heads):
        cols = slice(h * LANES, (h + 1) * LANES)
        q_ref[:, cols] = (_dot(qn, wuq_ref[:, cols]) * q_mul).astype(BF16)
        k_ref[:, cols] = (_dot(kvn, wuk_ref[:, cols]) + kpe).astype(BF16)
        v_ref[:, cols] = jnp.where(lane < HEAD, _dot(kvn, wuv_ref[:, cols]), 1.0).astype(BF16)


def _mlaprep(p_mla, cos, sin, q_norm, kv_norm, wuq, wuk, wuv, lora, n_heads, scale):
    n, width = p_mla.shape
    tm = 512
    row = lambda w: pl.BlockSpec((tm, w), lambda i: (i, 0))
    full = lambda a: pl.BlockSpec(a.shape, lambda i: (0, 0))
    return pl.pallas_call(
        functools.partial(_mlaprep_kernel, lora=lora, n_heads=n_heads, scale=scale),
        grid=(n // tm,),
        in_specs=[row(width), row(LANES), row(LANES), full(q_norm), full(kv_norm), full(wuq), full(wuk), full(wuv)],
        out_specs=[row(n_heads * LANES)] * 3,
        out_shape=[jax.ShapeDtypeStruct((n, n_heads * LANES), BF16)] * 3,
        compiler_params=_params("parallel"),
        name="mla_prep",
    )(p_mla, cos, sin, q_norm, kv_norm, wuq, wuk, wuv)


def _flash_kernel(q_ref, k_ref, v_ref, o_ref, m_s, acc_s, s_buf, p_buf, al_buf):
    i = pl.program_id(2)
    t = FLASH_BLOCK
    m_s[...] = jnp.full(m_s.shape, -jnp.inf, F32)
    acc_s[...] = jnp.zeros(acc_s.shape, F32)
    p_buf[1] = jnp.zeros(p_buf.shape[1:], BF16)
    al_buf[1] = jnp.ones(al_buf.shape[1:], F32)
    causal = lax.broadcasted_iota(jnp.int32, (t, t), 0) >= lax.broadcasted_iota(jnp.int32, (t, t), 1)

    def keys_of(j):
        return pl.ds(pl.multiple_of(j * t, t), t)

    def scores(j, slot, u0):
        for h in range(2):
            cols = slice(h * LANES, (h + 1) * LANES)
            s_buf[slot, h, u0 * t:, :] = _dot_nt(q_ref[u0 * t:, cols], k_ref[keys_of(j), cols])

    def softmax(slot, u0, masked_sub):
        for u in range(u0, FLASH_SUBS):
            rows = slice(u * t, (u + 1) * t)
            for h in range(2):
                s = s_buf[slot, h, rows, :]
                if u == masked_sub:
                    s = jnp.where(causal, s, -jnp.inf)
                halves = [s[:, n * LANES:(n + 1) * LANES] for n in range(t // LANES)]
                m_prev = m_s[h, rows, :]
                m_new = jnp.maximum(m_prev, jnp.max(functools.reduce(jnp.maximum, halves), axis=-1, keepdims=True))
                m_s[h, rows, :] = m_new
                al_buf[slot, h, rows, :] = jnp.exp2(m_prev - m_new)
                p_buf[slot, h, rows, :] = jnp.concatenate(
                    [jnp.exp2((x - m_new).astype(BF16)) for x in halves], axis=1)

    def accumulate(j, slot, u0):
        for h in range(2):
            pv = _dot(p_buf[slot, h, u0 * t:, :], v_ref[keys_of(j), h * LANES:(h + 1) * LANES])
            acc_s[h, u0 * t:, :] = al_buf[slot, h, u0 * t:, :] * acc_s[h, u0 * t:, :] + pv

    n_full = FLASH_SUBS * i
    scores(0, 0, 0)

    def body(g, carry):
        for par in range(2):
            k = 2 * g + par
            scores(k + 1, 1 - par, 0)
            accumulate(jnp.maximum(k - 1, 0), 1 - par, 0)
            softmax(par, 0, None)
        return carry

    lax.fori_loop(0, n_full // 2, body, 0)
    accumulate(jnp.maximum(n_full - 1, 0), 1, 0)
    for d in range(FLASH_SUBS):
        if d + 1 < FLASH_SUBS:
            scores(n_full + d + 1, (d + 1) % 2, d + 1)
        softmax(d % 2, d, d)
        accumulate(n_full + d, d % 2, d)
    head0 = lax.broadcasted_iota(jnp.int32, (FLASH_SUBS * t, LANES), 1) < HEAD
    a0, a1 = acc_s[0], acc_s[1]
    out = jnp.where(head0, a0 / pltpu.roll(a0, HEAD, 1), pltpu.roll(a1, HEAD, 1) / a1)
    o_ref[...] = out.astype(o_ref.dtype)


def _flash(q, k, v, batch, seq, n_heads):
    n = q.shape[0]
    tq = FLASH_SUBS * FLASH_BLOCK
    n_q = seq // tq
    stat = pltpu.VMEM((2, tq, LANES), F32)
    return pl.pallas_call(
        _flash_kernel,
        grid=(batch, n_heads // 2, n_q),
        in_specs=[pl.BlockSpec((tq, 2 * LANES), lambda b, h, i: (b * n_q + i, h)),
                  pl.BlockSpec((seq, 2 * LANES), lambda b, h, i: (b, h)),
                  pl.BlockSpec((seq, 2 * LANES), lambda b, h, i: (b, h))],
        out_specs=pl.BlockSpec((tq, LANES), lambda b, h, i: (b * n_q + i, h)),
        out_shape=jax.ShapeDtypeStruct((n, n_heads * HEAD), BF16),
        scratch_shapes=[stat, stat,
                        pltpu.VMEM((2, 2, tq, FLASH_BLOCK), F32),
                        pltpu.VMEM((2, 2, tq, FLASH_BLOCK), BF16),
                        pltpu.VMEM((2, 2, tq, LANES), F32)],
        compiler_params=_params("parallel", "parallel", "arbitrary"),
        name="mla_flash",
    )(q, k, v)


def _outffn_kernel(ya_ref, yb_ref, yc_ref, x_ref, mod_ref, gpm_ref, gpf_ref, gqf_ref, wo_ref, w1_ref, w2_ref,
                   o_ref, x1_s, hid_s, *, da, db, tf):
    y = (_dot(ya_ref[...], wo_ref[0:da, :])
         + _dot(yb_ref[...], wo_ref[da:da + db, :])
         + _dot(yc_ref[...], wo_ref[da + db:, :]))
    x1 = x_ref[...] + mod_ref[0, 2:3, :] * (_rms(y) * gpm_ref[...])
    x1_s[...] = x1
    h = _rms(x1) * gpf_ref[...]
    h = (h * (1.0 + mod_ref[0, 4:5, :]) + mod_ref[0, 3:4, :]).astype(BF16)
    for f in range(hid_s.shape[1] // tf):
        cols = slice(f * tf, (f + 1) * tf)
        hid = jnp.maximum(_dot(h, w1_ref[:, cols]), 0.0)
        hid_s[:, cols] = (hid * hid).astype(BF16)
    y2 = _dot(hid_s[...], w2_ref[...])
    o_ref[...] = x1_s[...] + mod_ref[0, 5:6, :] * (_rms(y2) * gqf_ref[...])


def _outffn(ya, yb, yc, x, mod, g_post_mix, g_pre_ffn, g_post_ffn, w_out, w_ff1, w_ff2, seq):
    n, d = x.shape
    d_ff = w_ff1.shape[1]
    tm, tf = 512, 512
    per_b = seq // tm
    row = lambda w: pl.BlockSpec((tm, w), lambda i: (i, 0))
    vec = pl.BlockSpec((1, d), lambda i: (0, 0))
    resident = lambda a: pl.BlockSpec(a.shape, lambda i: (0, 0), pipeline_mode=pl.Buffered(1))
    return pl.pallas_call(
        functools.partial(_outffn_kernel, da=ya.shape[1], db=yb.shape[1], tf=tf),
        grid=(n // tm,),
        in_specs=[row(ya.shape[1]), row(yb.shape[1]), row(yc.shape[1]), row(d),
                  pl.BlockSpec((1, 6, d), lambda i: (i // per_b, 0, 0)),
                  vec, vec, vec, resident(w_out), resident(w_ff1), resident(w_ff2)],
        out_specs=row(d),
        out_shape=jax.ShapeDtypeStruct((n, d), F32),
        scratch_shapes=[pltpu.VMEM((tm, d), F32), pltpu.VMEM((tm, d_ff), BF16)],
        compiler_params=_params("parallel"),
        name="out_ffn",
    )(ya, yb, yc, x, mod, g_post_mix, g_pre_ffn, g_post_ffn, w_out, w_ff1, w_ff2)


def _rot_cols(w):
    half = w.shape[-1] // 2
    return jnp.concatenate([-w[..., half:], w[..., :half]], axis=-1)


def kernel(x, c, positions, g_pre_mix, g_post_mix, g_pre_ffn, g_post_ffn, w_ada, b_ada, w_in, w_out, rwkv_mu, rwkv_w0, rwkv_w2, rwkv_a0, rwkv_a2, rwkv_g2, rwkv_k_k, rwkv_k_a, rwkv_r_k, rwkv_ln_w, rwkv_ln_b, conv_w, conv_b, conv_ln_w, conv_ln_b, mla_q_norm, mla_w_uq, mla_kv_norm, mla_w_ukv, w_ff1, w_ff2):
    batch, seq, d = x.shape
    n = batch * seq
    n_layers = w_ada.shape[0]
    r_dim = rwkv_w0.shape[1]
    w_lora, a_lora, g_lora = rwkv_w2.shape[1], rwkv_a2.shape[1], rwkv_g2.shape[1]
    assert w_lora + a_lora + g_lora == LANES and r_dim % LANES == 0
    n_r = 3 * r_dim + LANES
    c_dim = conv_w.shape[2]
    n_c = 2 * c_dim
    lora = mla_q_norm.shape[1]
    n_heads = mla_w_ukv.shape[2] // (2 * HEAD)
    rope = mla_w_uq.shape[2] // n_heads - HEAD
    assert HEAD + 2 * rope == LANES
    n_m = 2 * lora + 2 * LANES
    scale = float(HEAD + rope) ** -0.5 * 1.4426950408889634

    xf = x.reshape(n, d)
    c8 = jnp.pad(c, ((0, 8 - batch), (0, 0)))
    mod_all = _ada(c8, w_ada, b_ada.reshape(n_layers, 1, 6 * d))[:, :batch].reshape(n_layers, batch, 6, d)

    inv_freq = 1.0 / (ROPE_THETA ** (jnp.arange(0, rope, 2, dtype=F32) / rope))
    invf = jnp.tile(inv_freq, LANES // inv_freq.shape[0]).reshape(1, LANES)
    cos, sin = _rope_tables(positions.reshape(n, 1), invf)

    head_id = jnp.arange(LANES) // HEAD
    bd = (head_id[:, None] == head_id[None, :]).astype(BF16)
    row1 = lambda a: a.reshape(1, -1)

    for l in range(n_layers):
        wi = w_in[l]
        k_pe = wi[:, n_r + n_c + 2 * lora:]
        z_half = jnp.zeros((d, HEAD), F32)
        w_ext = jnp.concatenate([wi[:, :n_r + n_c + 2 * lora],
                                 z_half, k_pe, k_pe,
                                 z_half, _rot_cols(k_pe), _rot_cols(k_pe)], axis=1).astype(BF16)
        wq = mla_w_uq[l].reshape(lora, n_heads, HEAD + rope)
        wuq = jnp.concatenate([wq, _rot_cols(wq[..., HEAD:])], axis=-1).reshape(lora, n_heads * LANES).astype(BF16)
        wkv = mla_w_ukv[l].reshape(lora, n_heads, 2 * HEAD)
        wuk = jnp.concatenate([wkv[..., :HEAD], jnp.zeros_like(wkv[..., :HEAD])], axis=-1)
        wuk = wuk.reshape(lora, n_heads * LANES).astype(BF16)
        wuv = jnp.concatenate([wkv[..., HEAD:], jnp.zeros_like(wkv[..., HEAD:])], axis=-1)
        wuv = wuv.reshape(lora, n_heads * LANES).astype(BF16)
        lora_w = jnp.zeros((LANES, 3 * r_dim), F32)
        lora_w = lora_w.at[0:w_lora, 0:r_dim].set(rwkv_w2[l])
        lora_w = lora_w.at[w_lora:w_lora + a_lora, r_dim:2 * r_dim].set(rwkv_a2[l])
        lora_w = lora_w.at[w_lora + a_lora:, 2 * r_dim:].set(rwkv_g2[l]).astype(BF16)

        mod = mod_all[l]
        p_rwkv, p_conv, p_mla = _inproj(xf, mod, row1(g_pre_mix[l]), w_ext, seq, n_r, n_c, n_m)
        vecs = [row1(rwkv_mu[l]), row1(rwkv_w0[l]), row1(rwkv_a0[l]), row1(rwkv_k_k[l]), row1(rwkv_k_a[l]),
                row1(rwkv_r_k[l]), row1(rwkv_ln_w[l]), row1(rwkv_ln_b[l])]
        y_a = _rwkv(p_rwkv, vecs, lora_w, bd, batch, seq, r_dim)
        y_b = _conv(p_conv, conv_w[l], row1(conv_b[l]), row1(conv_ln_w[l]), row1(conv_ln_b[l]), batch, seq)
        q, k, v = _mlaprep(p_mla, cos, sin, row1(mla_q_norm[l]), row1(mla_kv_norm[l]), wuq, wuk, wuv,
                           lora, n_heads, scale)
        y_c = _flash(q, k, v, batch, seq, n_heads)
        xf = _outffn(y_a, y_b, y_c, xf, mod, row1(g_post_mix[l]), row1(g_pre_ffn[l]), row1(g_post_ffn[l]),
                     w_out[l].astype(BF16), w_ff1[l].astype(BF16), w_ff2[l].astype(BF16), seq)
    return xf.reshape(batch, seq, d)
```

```python
import functools

import jax
import jax.numpy as jnp
from jax import lax
from jax.experimental import pallas as pl
from jax.experimental.pallas import tpu as pltpu

F32 = jnp.float32
BF16 = jnp.bfloat16

LANES = 128
HEAD = 64
NORM_EPS = 1e-6
LN_EPS = 1e-5
RWKV_GN_EPS = 64e-5
ROPE_THETA = 10000.0
CONV_WIDTH = 31
CONV_HALO = 32
CONV_SUB = 128
CHUNK = 64
FLASH_BLOCK = 256
FLASH_SUBS = 2
GROUP_CHUNKS = 8
VMEM_LIMIT = 56 * 1024 * 1024


def _dot(a, b, precision=None):
    return jnp.dot(a, b, preferred_element_type=F32, precision=precision)


def _dot_nt(a, b, precision=None):
    return lax.dot_general(a, b, (((1,), (1,)), ((), ())), preferred_element_type=F32, precision=precision)


def _bf(x):
    return x.astype(BF16)


def _split(x):
    hi = x.astype(BF16)
    return hi, (x - hi.astype(F32)).astype(BF16)


def _head_sums(x, ones_bd):
    groups = [_dot(_bf(x[:, g * LANES:(g + 1) * LANES]), ones_bd) for g in range(x.shape[1] // LANES)]
    return jnp.concatenate(groups, axis=1)


def _rms(x):
    return x * lax.rsqrt(jnp.mean(x * x, axis=-1, keepdims=True) + NORM_EPS)


def _sigmoid(x):
    return 1.0 / (1.0 + jnp.exp(-x))


def _params(*semantics):
    return pltpu.CompilerParams(dimension_semantics=semantics, vmem_limit_bytes=VMEM_LIMIT)


def _ada_kernel(c_ref, w_ref, b_ref, o_ref):
    c = c_ref[...]
    cs = c * _sigmoid(c)
    cs_hi, cs_lo = _split(cs)
    w_hi, w_lo = _split(w_ref[0])
    o_ref[0] = _dot(cs_hi, w_hi) + _dot(cs_hi, w_lo) + _dot(cs_lo, w_hi) + b_ref[0]


def _ada(c8, w_ada, b_ada):
    n_layers, d, d6 = w_ada.shape
    tn = 1536
    return pl.pallas_call(
        _ada_kernel,
        grid=(n_layers, d6 // tn),
        in_specs=[pl.BlockSpec((8, d), lambda l, j: (0, 0)),
                  pl.BlockSpec((1, d, tn), lambda l, j: (l, 0, j)),
                  pl.BlockSpec((1, 1, tn), lambda l, j: (l, 0, j))],
        out_specs=pl.BlockSpec((1, 8, tn), lambda l, j: (l, 0, j)),
        out_shape=jax.ShapeDtypeStruct((n_layers, 8, d6), F32),
        compiler_params=_params("parallel", "parallel"),
        name="ada_mod",
    )(c8, w_ada, b_ada)


def _rope_kernel(pos_ref, invf_ref, cos_ref, sin_ref):
    ang = pos_ref[...].astype(F32) * invf_ref[...]
    cos_ref[...] = jnp.cos(ang)
    sin_ref[...] = jnp.sin(ang)


def _rope_tables(pos_col, invf):
    n = pos_col.shape[0]
    tm = 2048
    spec = pl.BlockSpec((tm, LANES), lambda i: (i, 0))
    return pl.pallas_call(
        _rope_kernel,
        grid=(n // tm,),
        in_specs=[pl.BlockSpec((tm, 1), lambda i: (i, 0)), pl.BlockSpec((1, LANES), lambda i: (0, 0))],
        out_specs=[spec, spec],
        out_shape=[jax.ShapeDtypeStruct((n, LANES), F32)] * 2,
        compiler_params=_params("parallel"),
        name="rope_tables",
    )(pos_col, invf)


def _inproj_kernel(x_ref, mod_ref, g_ref, w_ref, wx_ref, pr_ref, pc_ref, pm_ref, *, n_r, n_c, n_lat):
    h = _rms(x_ref[...]) * g_ref[...]
    h = (h * (1.0 + mod_ref[0, 1:2, :]) + mod_ref[0, 0:1, :]).astype(BF16)
    pr_ref[...] = _dot(h, w_ref[:, :n_r]).astype(pr_ref.dtype)
    pc_ref[...] = _dot(h, w_ref[:, n_r:n_r + n_c]).astype(pc_ref.dtype)
    pm_ref[:, :n_lat] = _dot(h, w_ref[:, n_r + n_c:n_r + n_c + n_lat]).astype(pm_ref.dtype)
    pm_ref[:, n_lat:] = _dot(h, wx_ref[...]).astype(pm_ref.dtype)


def _inproj(x, mod, g, w_in_all, layer, w_extra, seq, n_r, n_c, n_m):
    n, d = x.shape
    tm = 512
    per_b = seq // tm
    return pl.pallas_call(
        functools.partial(_inproj_kernel, n_r=n_r, n_c=n_c, n_lat=n_m - w_extra.shape[1]),
        grid=(n // tm,),
        in_specs=[pl.BlockSpec((tm, d), lambda i: (i, 0)),
                  pl.BlockSpec((1, 6, d), lambda i: (i // per_b, 0, 0)),
                  pl.BlockSpec((1, d), lambda i: (0, 0)),
                  pl.BlockSpec((None,) + w_in_all.shape[1:], lambda i: (layer, 0, 0)),
                  pl.BlockSpec(w_extra.shape, lambda i: (0, 0))],
        out_specs=[pl.BlockSpec((tm, n_r), lambda i: (i, 0)),
                   pl.BlockSpec((tm, n_c), lambda i: (i, 0)),
                   pl.BlockSpec((tm, n_m), lambda i: (i, 0))],
        out_shape=[jax.ShapeDtypeStruct((n, n_r), BF16),
                   jax.ShapeDtypeStruct((n, n_c), BF16),
                   jax.ShapeDtypeStruct((n, n_m), BF16)],
        compiler_params=_params("parallel"),
        name="in_proj",
    )(x, mod, g, w_in_all, w_extra)


def _rwkv_kernel(p_ref, mu_ref, w0_ref, a0_ref, kk_ref, ka_ref, rk_ref, lnw_ref, lnb_ref, lora_ref, bd_ref,
                 o_ref, zbuf, st_ref, r_s, k_s, v_s, a_s, b_s, ld_s, y_s, *, tt, dim):
    n_pairs = dim // LANES
    c = CHUNK

    @pl.when(pl.program_id(1) == 0)
    def _():
        zbuf[0:8, :] = jnp.zeros((8, zbuf.shape[1]), F32)
        st_ref[...] = jnp.zeros(st_ref.shape, F32)

    p = p_ref[...].astype(F32)
    zbuf[8:8 + tt, :] = p
    prev = zbuf[7:7 + tt, :]
    zbuf[7:8, :] = zbuf[tt + 7:tt + 8, :]
    z = p + (prev - p) * mu_ref[...]

    r = z[:, 0:dim]
    k = z[:, dim:2 * dim]
    v = z[:, 2 * dim:3 * dim]
    lo = z[:, 3 * dim:3 * dim + LANES]
    lane = lax.broadcasted_iota(jnp.int32, lo.shape, 1)
    act = jnp.where(lane < 32, jnp.tanh(lo), jnp.where(lane < 64, lo, _sigmoid(lo)))
    lora = _dot(_bf(act), lora_ref[...])
    x_w = w0_ref[...] + lora[:, 0:dim]
    w_log = -(jnp.maximum(-x_w, 0.0) + jnp.log(1.0 + jnp.exp(-jnp.abs(x_w)))) - 0.5
    a = _sigmoid(a0_ref[...] + lora[:, dim:2 * dim])
    g = lora[:, 2 * dim:3 * dim]
    bd = bd_ref[...]
    kk = k * kk_ref[...]
    kk = kk * lax.rsqrt(jnp.maximum(_head_sums(kk * kk, bd), 1e-24))
    k = k * (1.0 + (a - 1.0) * ka_ref[...])
    r_s[...] = r
    k_s[...] = k
    v_s[...] = v
    a_s[...] = -kk
    b_s[...] = kk * a
    ld_s[...] = -jnp.exp(w_log)

    row_c = lax.broadcasted_iota(jnp.int32, (c, c), 0)
    col_c = lax.broadcasted_iota(jnp.int32, (c, c), 1)
    tri_incl = (row_c >= col_c).astype(BF16)
    eye_c = (row_c == col_c).astype(F32)
    row_w = lax.broadcasted_iota(jnp.int32, (2 * c, 2 * c), 0)
    col_w = lax.broadcasted_iota(jnp.int32, (2 * c, 2 * c), 1)
    row_in = jnp.where(row_w >= c, row_w - c, row_w)
    col_in = jnp.where(col_w >= c, col_w - c, col_w)
    strict_w = row_in > col_in
    incl_w = row_in >= col_in
    second_w = col_w >= c
    lane_c = lax.broadcasted_iota(jnp.int32, (c, LANES), 1)
    head0 = lane_c < HEAD
    lane_2 = lax.broadcasted_iota(jnp.int32, (c, 2 * LANES), 1)
    head0_2 = jnp.where(lane_2 >= LANES, lane_2 - LANES, lane_2) < HEAD
    row_p = lax.broadcasted_iota(jnp.int32, (LANES, LANES), 0)
    col_p = lax.broadcasted_iota(jnp.int32, (LANES, LANES), 1)
    same_head = (row_p < HEAD) == (col_p < HEAD)
    diag_p = row_p == col_p
    zeros_cl = jnp.zeros((c, LANES), BF16)

    def group_body(gi, carry):
        base = gi * (GROUP_CHUNKS * c)
        items = [(pl.ds(pl.multiple_of(base + q * c, c), c), slice(j * LANES, (j + 1) * LANES), j)
                 for q in range(GROUP_CHUNKS) for j in range(n_pairs)]
        n_it = len(items)
        loaded = [[ref[rows, lanes] for ref in (r_s, k_s, v_s, a_s, b_s, ld_s)] for rows, lanes, _ in items]
        r_c, k_c, v_c, a_c, b_c, ld_c = [list(t) for t in zip(*loaded)]
        cum = []
        for i in range(n_it):
            ld_hi, ld_lo = _split(ld_c[i])
            cum.append(_dot(tri_incl, ld_hi) + _dot(tri_incl, ld_lo))
        cum_end = [x[c - 1:c, :] for x in cum]
        at = [a_c[i] * jnp.exp(cum[i] - ld_c[i]) for i in range(n_it)]
        rt = [r_c[i] * jnp.exp(cum[i]) for i in range(n_it)]
        v_bf = [_bf(x) for x in v_c]
        aa = []
        for i in range(n_it):
            e_neg = jnp.exp(-cum[i])
            lhs = _bf(jnp.concatenate([jnp.where(head0, at[i], 0.0), jnp.where(head0, 0.0, at[i]),
                                       jnp.where(head0, rt[i], 0.0), jnp.where(head0, 0.0, rt[i])], axis=0))
            rhs = _bf(jnp.concatenate([b_c[i] * e_neg, k_c[i] * e_neg], axis=0))
            aa.append(_dot_nt(lhs, rhs))
        t_both = [jnp.where(strict_w, x[0:2 * c, :], 0.0) for x in aa]
        q_both = [_bf(jnp.where(incl_w, x[2 * c:4 * c, :], 0.0)) for x in aa]
        akv = [_dot(_bf(jnp.where(second_w, t_both[i], 0.0)), jnp.concatenate([v_bf[i], v_bf[i]], axis=0))
               for i in range(n_it)]
        n_hd = 2 * n_it
        lower = [t_both[n // 2][(n % 2) * c:(n % 2 + 1) * c, 0:c] for n in range(n_hd)]
        inv = [eye_c + l for l in lower]
        lower_bf = [_bf(l) for l in lower]
        pw = [_bf(_dot(l, l)) for l in lower_bf]
        for _ in range(4):
            both = [_dot(jnp.concatenate([pw[n], _bf(inv[n])], axis=0), pw[n]) for n in range(n_hd)]
            inv = [inv[n] + both[n][c:, :] for n in range(n_hd)]
            pw = [_bf(both[n][0:c, :]) for n in range(n_hd)]
        inv = [inv[n] + _dot(_bf(inv[n]), pw[n]) for n in range(n_hd)]
        x_mat = [_bf(jnp.concatenate([at[i], jnp.where(head0, akv[i][0:c, :], akv[i][c:, :])], axis=1))
                 for i in range(n_it)]
        tx_both = [_dot(_bf(jnp.concatenate([inv[2 * i], inv[2 * i + 1]], axis=0)), x_mat[i]) for i in range(n_it)]
        w_mat = [jnp.concatenate([_bf(jnp.where(head0_2, tx_both[i][0:c, :], tx_both[i][c:, :])),
                                  jnp.concatenate([zeros_cl, v_bf[i]], axis=1)], axis=0) for i in range(n_it)]
        qw_both = [_dot(q_both[i], w_mat[i]) for i in range(n_it)]
        qw = [jnp.where(head0_2, x[0:c, :], x[c:, :]) for x in qw_both]
        mn = []
        for i in range(n_it):
            e_end = jnp.exp(cum_end[i] - cum[i])
            bk_t = _bf(jnp.concatenate([b_c[i] * e_end, k_c[i] * e_end], axis=0).T)
            mn.append(_dot(bk_t, w_mat[i]))
        g_col = [jnp.sum(jnp.where(diag_p, jnp.exp(cum_end[i]), 0.0), axis=1, keepdims=True) for i in range(n_it)]
        states = [st_ref[j] for j in range(n_pairs)]
        for i, (rows, lanes, j) in enumerate(items):
            st = states[j]
            st_bf = _bf(st)
            y_s[rows, lanes] = _dot(_bf(rt[i] + qw[i][:, 0:LANES]), st_bf) + qw[i][:, LANES:]
            m_low = _bf(jnp.where(same_head, mn[i][:, 0:LANES], 0.0))
            states[j] = g_col[i] * st + _dot(m_low, st_bf) + jnp.where(same_head, mn[i][:, LANES:], 0.0)
        for j in range(n_pairs):
            st_ref[j] = states[j]
        return carry

    lax.fori_loop(0, tt // (c * GROUP_CHUNKS), group_body, 0)

    y = y_s[...]
    inv_n = 1.0 / HEAD
    mean = _head_sums(y, bd) * inv_n
    d = y - mean
    var = _head_sums(d * d, bd) * inv_n
    y = d * lax.rsqrt(var + RWKV_GN_EPS) * lnw_ref[...] + lnb_ref[...]
    r = r_s[...]
    bonus = _head_sums(r * k_s[...] * rk_ref[...], bd) * v_s[...]
    o_ref[...] = ((y + bonus) * g).astype(o_ref.dtype)


def _rwkv(p_rwkv, vecs, lora_w, bd, batch, seq, dim):
    n, width = p_rwkv.shape
    tt = CHUNK * GROUP_CHUNKS
    n_t = seq // tt
    vec_spec = lambda a: pl.BlockSpec(a.shape, lambda b, t: (0, 0))
    act = pltpu.VMEM((tt, dim), F32)
    return pl.pallas_call(
        functools.partial(_rwkv_kernel, tt=tt, dim=dim),
        grid=(batch, n_t),
        in_specs=[pl.BlockSpec((tt, width), lambda b, t: (b * n_t + t, 0))]
                 + [vec_spec(a) for a in vecs] + [vec_spec(lora_w), vec_spec(bd)],
        out_specs=pl.BlockSpec((tt, dim), lambda b, t: (b * n_t + t, 0)),
        out_shape=jax.ShapeDtypeStruct((n, dim), BF16),
        scratch_shapes=[pltpu.VMEM((tt + 8, width), F32),
                        pltpu.VMEM((dim // LANES, LANES, LANES), F32),
                        act, act, act, act, act, act, act],
        compiler_params=_params("parallel", "arbitrary"),
        name="rwkv7",
    )(p_rwkv, *vecs, lora_w, bd)


def _conv_kernel(p_ref, w_ref, b_ref, lnw_ref, lnb_ref, shift_ref, o_ref, hbuf, *, tt, dim):
    sub = CONV_SUB
    ext = sub + 8

    @pl.when(pl.program_id(1) == 0)
    def _():
        hbuf[0:CONV_HALO, :] = jnp.zeros((CONV_HALO, dim), F32)

    hbuf[CONV_HALO:CONV_HALO + tt, :] = p_ref[:, 0:dim].astype(F32) * _sigmoid(p_ref[:, dim:2 * dim].astype(F32))
    for s in range(tt // sub):
        base = CONV_HALO + s * sub - 8
        wins = [hbuf[base - 8 * a:base - 8 * a + ext, :] for a in range(CONV_HALO // 8)]
        parts = []
        for r in range(8):
            lags = [8 * a + r for a in range(len(wins)) if 8 * a + r < CONV_WIDTH]
            terms = [wins[m // 8] * w_ref[CONV_WIDTH - 1 - m:CONV_WIDTH - m, :] for m in lags]
            parts.append(functools.reduce(jnp.add, terms))
        stacked = _bf(jnp.concatenate(parts, axis=0))
        acc = _dot(shift_ref[...], stacked) + b_ref[...]
        mu = jnp.mean(acc, axis=-1, keepdims=True)
        d = acc - mu
        var = jnp.mean(d * d, axis=-1, keepdims=True)
        y = d * lax.rsqrt(var + LN_EPS) * lnw_ref[...] + lnb_ref[...]
        o_ref[s * sub:(s + 1) * sub, :] = (y * _sigmoid(y)).astype(o_ref.dtype)
    hbuf[0:CONV_HALO, :] = hbuf[tt:tt + CONV_HALO, :]


def _conv(p_conv, conv_w, conv_b, ln_w, ln_b, batch, seq):
    n, width = p_conv.shape
    dim = width // 2
    tt = 512
    n_t = seq // tt
    ext = CONV_SUB + 8
    u_idx = jnp.arange(CONV_SUB)[:, None]
    col = jnp.arange(8 * ext)[None, :]
    shift = (col % ext == u_idx + 8 - col // ext).astype(BF16)
    vec = lambda a: pl.BlockSpec(a.shape, lambda b, t: (0, 0))
    return pl.pallas_call(
        functools.partial(_conv_kernel, tt=tt, dim=dim),
        grid=(batch, n_t),
        in_specs=[pl.BlockSpec((tt, width), lambda b, t: (b * n_t + t, 0)),
                  vec(conv_w), vec(conv_b), vec(ln_w), vec(ln_b), vec(shift)],
        out_specs=pl.BlockSpec((tt, dim), lambda b, t: (b * n_t + t, 0)),
        out_shape=jax.ShapeDtypeStruct((n, dim), BF16),
        scratch_shapes=[pltpu.VMEM((tt + CONV_HALO, dim), F32)],
        compiler_params=_params("parallel", "arbitrary"),
        name="conformer_conv",
    )(p_conv, conv_w, conv_b, ln_w, ln_b, shift)


def _mlaprep_kernel(p_ref, cos_ref, sin_ref, qn_ref, kvn_ref, wuq_ref, wuk_ref, wuv_ref,
                    q_ref, k_ref, v_ref, *, lora, n_heads, scale):
    cos = cos_ref[...]
    sin = sin_ref[...]
    qn = (_rms(p_ref[:, 0:lora].astype(F32)) * qn_ref[...]).astype(BF16)
    kvn = (_rms(p_ref[:, lora:2 * lora].astype(F32)) * kvn_ref[...]).astype(BF16)
    kpe = (p_ref[:, 2 * lora:2 * lora + LANES].astype(F32) * cos
           + p_ref[:, 2 * lora + LANES:2 * lora + 2 * LANES].astype(F32) * sin)
    lane = lax.broadcasted_iota(jnp.int32, cos.shape, 1)
    q_mul = jnp.where(lane < HEAD, 1.0, jnp.where(lane < HEAD + 32, cos, sin)) * scale
    for h in range(n_heads):
        cols = slice(h * LANES, (h + 1) * LANES)
        q_ref[:, cols] = (_dot(qn, wuq_ref[:, cols]) * q_mul).astype(BF16)
        k_ref[:, cols] = (_dot(kvn, wuk_ref[:, cols]) + kpe).astype(BF16)
        v_ref[:, cols] = jnp.where(lane < HEAD, _dot(kvn, wuv_ref[:, cols]), 1.0).astype(BF16)


def _mlaprep(p_mla, cos, sin, q_norm, kv_norm, wuq, wuk, wuv, lora, n_heads, scale):
    n, width = p_mla.shape
    tm = 512
    row = lambda w: pl.BlockSpec((tm, w), lambda i: (i, 0))
    full = lambda a: pl.BlockSpec(a.shape, lambda i: (0, 0))
    return pl.pallas_call(
        functools.partial(_mlaprep_kernel, lora=lora, n_heads=n_heads, scale=scale),
        grid=(n // tm,),
        in_specs=[row(width), row(LANES), row(LANES), full(q_norm), full(kv_norm), full(wuq), full(wuk), full(wuv)],
        out_specs=[row(n_heads * LANES)] * 3,
        out_shape=[jax.ShapeDtypeStruct((n, n_heads * LANES), BF16)] * 3,
        compiler_params=_params("parallel"),
        name="mla_prep",
    )(p_mla, cos, sin, q_norm, kv_norm, wuq, wuk, wuv)


def _flash_kernel(q_ref, k_ref, v_ref, o_ref, m_s, acc_s, s_buf, p_buf, al_buf):
    i = pl.program_id(2)
    t = FLASH_BLOCK
    m_s[...] = jnp.full(m_s.shape, -jnp.inf, F32)
    acc_s[...] = jnp.zeros(acc_s.shape, F32)
    p_buf[1] = jnp.zeros(p_buf.shape[1:], BF16)
    al_buf[1] = jnp.ones(al_buf.shape[1:], F32)
    causal = lax.broadcasted_iota(jnp.int32, (t, t), 0) >= lax.broadcasted_iota(jnp.int32, (t, t), 1)

    def keys_of(j):
        return pl.ds(pl.multiple_of(j * t, t), t)

    def scores(j, slot, u0):
        for h in range(2):
            cols = slice(h * LANES, (h + 1) * LANES)
            s_buf[slot, h, u0 * t:, :] = _dot_nt(q_ref[u0 * t:, cols], k_ref[keys_of(j), cols])

    def softmax(slot, u0, masked_sub):
        for u in range(u0, FLASH_SUBS):
            rows = slice(u * t, (u + 1) * t)
            for h in range(2):
                s = s_buf[slot, h, rows, :]
                if u == masked_sub:
                    s = jnp.where(causal, s, -jnp.inf)
                halves = [s[:, n * LANES:(n + 1) * LANES] for n in range(t // LANES)]
                m_prev = m_s[h, rows, :]
                m_new = jnp.maximum(m_prev, jnp.max(functools.reduce(jnp.maximum, halves), axis=-1, keepdims=True))
                m_s[h, rows, :] = m_new
                al_buf[slot, h, rows, :] = jnp.exp2(m_prev - m_new)
                p_buf[slot, h, rows, :] = jnp.concatenate(
                    [jnp.exp2((x - m_new).astype(BF16)) for x in halves], axis=1)

    def accumulate(j, slot, u0):
        for h in range(2):
            pv = _dot(p_buf[slot, h, u0 * t:, :], v_ref[keys_of(j), h * LANES:(h + 1) * LANES])
            acc_s[h, u0 * t:, :] = al_buf[slot, h, u0 * t:, :] * acc_s[h, u0 * t:, :] + pv

    n_full = FLASH_SUBS * i
    scores(0, 0, 0)

    def body(g, carry):
        for par in range(2):
            k = 2 * g + par
            scores(k + 1, 1 - par, 0)
            accumulate(jnp.maximum(k - 1, 0), 1 - par, 0)
            softmax(par, 0, None)
        return carry

    lax.fori_loop(0, n_full // 2, body, 0)
    accumulate(jnp.maximum(n_full - 1, 0), 1, 0)
    for d in range(FLASH_SUBS):
        if d + 1 < FLASH_SUBS:
            scores(n_full + d + 1, (d + 1) % 2, d + 1)
        softmax(d % 2, d, d)
        accumulate(n_full + d, d % 2, d)
    head0 = lax.broadcasted_iota(jnp.int32, (FLASH_SUBS * t, LANES), 1) < HEAD
    a0, a1 = acc_s[0], acc_s[1]
    out = jnp.where(head0, a0 / pltpu.roll(a0, HEAD, 1), pltpu.roll(a1, HEAD, 1) / a1)
    o_ref[...] = out.astype(o_ref.dtype)


def _flash(q, k, v, batch, seq, n_heads):
    n = q.shape[0]
    tq = FLASH_SUBS * FLASH_BLOCK
    n_q = seq // tq
    stat = pltpu.VMEM((2, tq, LANES), F32)
    return pl.pallas_call(
        _flash_kernel,
        grid=(batch, n_heads // 2, n_q),
        in_specs=[pl.BlockSpec((tq, 2 * LANES), lambda b, h, i: (b * n_q + i, h)),
                  pl.BlockSpec((seq, 2 * LANES), lambda b, h, i: (b, h)),
                  pl.BlockSpec((seq, 2 * LANES), lambda b, h, i: (b, h))],
        out_specs=pl.BlockSpec((tq, LANES), lambda b, h, i: (b * n_q + i, h)),
        out_shape=jax.ShapeDtypeStruct((n, n_heads * HEAD), BF16),
        scratch_shapes=[stat, stat,
                        pltpu.VMEM((2, 2, tq, FLASH_BLOCK), F32),
                        pltpu.VMEM((2, 2, tq, FLASH_BLOCK), BF16),
                        pltpu.VMEM((2, 2, tq, LANES), F32)],
        compiler_params=_params("parallel", "parallel", "arbitrary"),
        name="mla_flash",
    )(q, k, v)


def _outffn_kernel(ya_ref, yb_ref, yc_ref, x_ref, mod_ref, gpm_ref, gpf_ref, gqf_ref, wo_ref, w1_ref, w2_ref,
                   o_ref, x1_s, hid_s, *, da, db, tf):
    y = (_dot(ya_ref[...], wo_ref[0:da, :])
         + _dot(yb_ref[...], wo_ref[da:da + db, :])
         + _dot(yc_ref[...], wo_ref[da + db:, :]))
    x1 = x_ref[...] + mod_ref[0, 2:3, :] * (_rms(y) * gpm_ref[...])
    x1_s[...] = x1
    h = _rms(x1) * gpf_ref[...]
    h = (h * (1.0 + mod_ref[0, 4:5, :]) + mod_ref[0, 3:4, :]).astype(BF16)
    for f in range(hid_s.shape[1] // tf):
        cols = slice(f * tf, (f + 1) * tf)
        hid = jnp.maximum(_dot(h, w1_ref[:, cols]), 0.0)
        hid_s[:, cols] = (hid * hid).astype(BF16)
    y2 = _dot(hid_s[...], w2_ref[...])
    o_ref[...] = x1_s[...] + mod_ref[0, 5:6, :] * (_rms(y2) * gqf_ref[...])


def _outffn(ya, yb, yc, x, mod, g_post_mix, g_pre_ffn, g_post_ffn, w_out, w_ff1, w_ff2, layer, seq):
    n, d = x.shape
    d_ff = w_ff1.shape[2]
    tm, tf = 512, 512
    per_b = seq // tm
    row = lambda w: pl.BlockSpec((tm, w), lambda i: (i, 0))
    vec = pl.BlockSpec((1, d), lambda i: (0, 0))
    resident = lambda a: pl.BlockSpec((None,) + a.shape[1:], lambda i: (layer, 0, 0), pipeline_mode=pl.Buffered(1))
    return pl.pallas_call(
        functools.partial(_outffn_kernel, da=ya.shape[1], db=yb.shape[1], tf=tf),
        grid=(n // tm,),
        in_specs=[row(ya.shape[1]), row(yb.shape[1]), row(yc.shape[1]), row(d),
                  pl.BlockSpec((1, 6, d), lambda i: (i // per_b, 0, 0)),
                  vec, vec, vec, resident(w_out), resident(w_ff1), resident(w_ff2)],
        out_specs=row(d),
        out_shape=jax.ShapeDtypeStruct((n, d), F32),
        scratch_shapes=[pltpu.VMEM((tm, d), F32), pltpu.VMEM((tm, d_ff), BF16)],
        compiler_params=_params("parallel"),
        name="out_ffn",
    )(ya, yb, yc, x, mod, g_post_mix, g_pre_ffn, g_post_ffn, w_out, w_ff1, w_ff2)


def _rot_cols(w):
    half = w.shape[-1] // 2
    return jnp.concatenate([-w[..., half:], w[..., :half]], axis=-1)


def kernel(x, c, positions, g_pre_mix, g_post_mix, g_pre_ffn, g_post_ffn, w_ada, b_ada, w_in, w_out, rwkv_mu, rwkv_w0, rwkv_w2, rwkv_a0, rwkv_a2, rwkv_g2, rwkv_k_k, rwkv_k_a, rwkv_r_k, rwkv_ln_w, rwkv_ln_b, conv_w, conv_b, conv_ln_w, conv_ln_b, mla_q_norm, mla_w_uq, mla_kv_norm, mla_w_ukv, w_ff1, w_ff2):
    batch, seq, d = x.shape
    n = batch * seq
    n_layers = w_ada.shape[0]
    r_dim = rwkv_w0.shape[1]
    w_lora, a_lora, g_lora = rwkv_w2.shape[1], rwkv_a2.shape[1], rwkv_g2.shape[1]
    assert w_lora + a_lora + g_lora == LANES and r_dim % LANES == 0
    n_r = 3 * r_dim + LANES
    c_dim = conv_w.shape[2]
    n_c = 2 * c_dim
    lora = mla_q_norm.shape[1]
    n_heads = mla_w_ukv.shape[2] // (2 * HEAD)
    rope = mla_w_uq.shape[2] // n_heads - HEAD
    assert HEAD + 2 * rope == LANES
    n_m = 2 * lora + 2 * LANES
    scale = float(HEAD + rope) ** -0.5 * 1.4426950408889634

    xf = x.reshape(n, d)
    c8 = jnp.pad(c, ((0, 8 - batch), (0, 0)))
    mod_all = _ada(c8, w_ada, b_ada.reshape(n_layers, 1, 6 * d))[:, :batch].reshape(n_layers, batch, 6, d)

    inv_freq = 1.0 / (ROPE_THETA ** (jnp.arange(0, rope, 2, dtype=F32) / rope))
    invf = jnp.tile(inv_freq, LANES // inv_freq.shape[0]).reshape(1, LANES)
    cos, sin = _rope_tables(positions.reshape(n, 1), invf)

    head_id = jnp.arange(LANES) // HEAD
    bd = (head_id[:, None] == head_id[None, :]).astype(BF16)
    row1 = lambda a: a.reshape(1, -1)

    w_in_bf, w_out_bf, w_ff1_bf, w_ff2_bf = [w.astype(BF16) for w in (w_in, w_out, w_ff1, w_ff2)]
    for l in range(n_layers):
        k_pe = w_in[l, :, n_r + n_c + 2 * lora:]
        z_half = jnp.zeros((d, HEAD), F32)
        w_extra = jnp.concatenate([z_half, k_pe, k_pe, z_half, _rot_cols(k_pe), _rot_cols(k_pe)], axis=1).astype(BF16)
        wq = mla_w_uq[l].reshape(lora, n_heads, HEAD + rope)
        wuq = jnp.concatenate([wq, _rot_cols(wq[..., HEAD:])], axis=-1).reshape(lora, n_heads * LANES).astype(BF16)
        wkv = mla_w_ukv[l].reshape(lora, n_heads, 2 * HEAD)
        wuk = jnp.concatenate([wkv[..., :HEAD], jnp.zeros_like(wkv[..., :HEAD])], axis=-1)
        wuk = wuk.reshape(lora, n_heads * LANES).astype(BF16)
        wuv = jnp.concatenate([wkv[..., HEAD:], jnp.zeros_like(wkv[..., HEAD:])], axis=-1)
        wuv = wuv.reshape(lora, n_heads * LANES).astype(BF16)
        lora_w = jnp.zeros((LANES, 3 * r_dim), F32)
        lora_w = lora_w.at[0:w_lora, 0:r_dim].set(rwkv_w2[l])
        lora_w = lora_w.at[w_lora:w_lora + a_lora, r_dim:2 * r_dim].set(rwkv_a2[l])
        lora_w = lora_w.at[w_lora + a_lora:, 2 * r_dim:].set(rwkv_g2[l]).astype(BF16)

        mod = mod_all[l]
        p_rwkv, p_conv, p_mla = _inproj(xf, mod, row1(g_pre_mix[l]), w_in_bf, l, w_extra, seq, n_r, n_c, n_m)
        vecs = [row1(rwkv_mu[l]), row1(rwkv_w0[l]), row1(rwkv_a0[l]), row1(rwkv_k_k[l]), row1(rwkv_k_a[l]),
                row1(rwkv_r_k[l]), row1(rwkv_ln_w[l]), row1(rwkv_ln_b[l])]
        y_a = _rwkv(p_rwkv, vecs, lora_w, bd, batch, seq, r_dim)
        y_b = _conv(p_conv, conv_w[l], row1(conv_b[l]), row1(conv_ln_w[l]), row1(conv_ln_b[l]), batch, seq)
        q, k, v = _mlaprep(p_mla, cos, sin, row1(mla_q_norm[l]), row1(mla_kv_norm[l]), wuq, wuk, wuv,
                           lora, n_heads, scale)
        y_c = _flash(q, k, v, batch, seq, n_heads)
        xf = _outffn(y_a, y_b, y_c, xf, mod, row1(g_post_mix[l]), row1(g_pre_ffn[l]), row1(g_post_ffn[l]),
                     w_out_bf, w_ff1_bf, w_ff2_bf, l, seq)
    return xf.reshape(batch, seq, d)
```

```python
import functools

import jax
import jax.numpy as jnp
from jax import lax
from jax.experimental import pallas as pl
from jax.experimental.pallas import tpu as pltpu

F32 = jnp.float32
BF16 = jnp.bfloat16

LANES = 128
HEAD = 64
NORM_EPS = 1e-6
LN_EPS = 1e-5
RWKV_GN_EPS = 64e-5
ROPE_THETA = 10000.0
CONV_WIDTH = 31
CONV_HALO = 32
CONV_SUB = 128
CHUNK = 64
FLASH_BLOCK = 256
FLASH_SUBS = 4
GROUP_CHUNKS = 8
VMEM_LIMIT = 56 * 1024 * 1024


def _dot(a, b, precision=None):
    return jnp.dot(a, b, preferred_element_type=F32, precision=precision)


def _dot_nt(a, b, precision=None):
    return lax.dot_general(a, b, (((1,), (1,)), ((), ())), preferred_element_type=F32, precision=precision)


def _bf(x):
    return x.astype(BF16)


def _split(x):
    hi = x.astype(BF16)
    return hi, (x - hi.astype(F32)).astype(BF16)


def _head_sums(x, ones_bd):
    groups = [_dot(_bf(x[:, g * LANES:(g + 1) * LANES]), ones_bd) for g in range(x.shape[1] // LANES)]
    return jnp.concatenate(groups, axis=1)


def _rms(x):
    return x * lax.rsqrt(jnp.mean(x * x, axis=-1, keepdims=True) + NORM_EPS)


def _sigmoid(x):
    return 1.0 / (1.0 + jnp.exp(-x))


def _params(*semantics):
    return pltpu.CompilerParams(dimension_semantics=semantics, vmem_limit_bytes=VMEM_LIMIT)


def _ada_kernel(c_ref, w_ref, b_ref, o_ref):
    c = c_ref[...]
    cs = c * _sigmoid(c)
    cs_hi, cs_lo = _split(cs)
    w_hi, w_lo = _split(w_ref[0])
    o_ref[0] = _dot(cs_hi, w_hi) + _dot(cs_hi, w_lo) + _dot(cs_lo, w_hi) + b_ref[0]


def _ada(c8, w_ada, b_ada):
    n_layers, d, d6 = w_ada.shape
    tn = 1536
    return pl.pallas_call(
        _ada_kernel,
        grid=(n_layers, d6 // tn),
        in_specs=[pl.BlockSpec((8, d), lambda l, j: (0, 0)),
                  pl.BlockSpec((1, d, tn), lambda l, j: (l, 0, j)),
                  pl.BlockSpec((1, 1, tn), lambda l, j: (l, 0, j))],
        out_specs=pl.BlockSpec((1, 8, tn), lambda l, j: (l, 0, j)),
        out_shape=jax.ShapeDtypeStruct((n_layers, 8, d6), F32),
        compiler_params=_params("parallel", "parallel"),
        name="ada_mod",
    )(c8, w_ada, b_ada)


def _rope_kernel(pos_ref, invf_ref, cos_ref, sin_ref):
    ang = pos_ref[...].astype(F32) * invf_ref[...]
    cos_ref[...] = jnp.cos(ang)
    sin_ref[...] = jnp.sin(ang)


def _rope_tables(pos_col, invf):
    n = pos_col.shape[0]
    tm = 2048
    spec = pl.BlockSpec((tm, LANES), lambda i: (i, 0))
    return pl.pallas_call(
        _rope_kernel,
        grid=(n // tm,),
        in_specs=[pl.BlockSpec((tm, 1), lambda i: (i, 0)), pl.BlockSpec((1, LANES), lambda i: (0, 0))],
        out_specs=[spec, spec],
        out_shape=[jax.ShapeDtypeStruct((n, LANES), F32)] * 2,
        compiler_params=_params("parallel"),
        name="rope_tables",
    )(pos_col, invf)


def _inproj_kernel(x_ref, mod_ref, g_ref, w_ref, wx_ref, pr_ref, pc_ref, pm_ref, *, n_r, n_c, n_lat):
    h = _rms(x_ref[...]) * g_ref[...]
    h = (h * (1.0 + mod_ref[0, 1:2, :]) + mod_ref[0, 0:1, :]).astype(BF16)
    pr_ref[...] = _dot(h, w_ref[:, :n_r]).astype(pr_ref.dtype)
    pc_ref[...] = _dot(h, w_ref[:, n_r:n_r + n_c]).astype(pc_ref.dtype)
    pm_ref[:, :n_lat] = _dot(h, w_ref[:, n_r + n_c:n_r + n_c + n_lat]).astype(pm_ref.dtype)
    pm_ref[:, n_lat:] = _dot(h, wx_ref[...]).astype(pm_ref.dtype)


def _inproj(x, mod, g, w_in_all, layer, w_extra, seq, n_r, n_c, n_m):
    n, d = x.shape
    tm = 512
    per_b = seq // tm
    return pl.pallas_call(
        functools.partial(_inproj_kernel, n_r=n_r, n_c=n_c, n_lat=n_m - w_extra.shape[1]),
        grid=(n // tm,),
        in_specs=[pl.BlockSpec((tm, d), lambda i: (i, 0)),
                  pl.BlockSpec((1, 6, d), lambda i: (i // per_b, 0, 0)),
                  pl.BlockSpec((1, d), lambda i: (0, 0)),
                  pl.BlockSpec((None,) + w_in_all.shape[1:], lambda i: (layer, 0, 0)),
                  pl.BlockSpec(w_extra.shape, lambda i: (0, 0))],
        out_specs=[pl.BlockSpec((tm, n_r), lambda i: (i, 0)),
                   pl.BlockSpec((tm, n_c), lambda i: (i, 0)),
                   pl.BlockSpec((tm, n_m), lambda i: (i, 0))],
        out_shape=[jax.ShapeDtypeStruct((n, n_r), BF16),
                   jax.ShapeDtypeStruct((n, n_c), BF16),
                   jax.ShapeDtypeStruct((n, n_m), BF16)],
        compiler_params=_params("parallel"),
        name="in_proj",
    )(x, mod, g, w_in_all, w_extra)


def _rwkv_kernel(p_ref, mu_ref, w0_ref, a0_ref, kk_ref, ka_ref, rk_ref, lnw_ref, lnb_ref, lora_ref, bd_ref,
                 o_ref, zbuf, st_ref, r_s, k_s, v_s, a_s, b_s, ld_s, y_s, *, tt, dim):
    n_pairs = dim // LANES
    c = CHUNK

    @pl.when(pl.program_id(1) == 0)
    def _():
        zbuf[0:8, :] = jnp.zeros((8, zbuf.shape[1]), F32)
        st_ref[...] = jnp.zeros(st_ref.shape, F32)

    p = p_ref[...].astype(F32)
    zbuf[8:8 + tt, :] = p
    prev = zbuf[7:7 + tt, :]
    zbuf[7:8, :] = zbuf[tt + 7:tt + 8, :]
    z = p + (prev - p) * mu_ref[...]

    r = z[:, 0:dim]
    k = z[:, dim:2 * dim]
    v = z[:, 2 * dim:3 * dim]
    lo = z[:, 3 * dim:3 * dim + LANES]
    lane = lax.broadcasted_iota(jnp.int32, lo.shape, 1)
    act = jnp.where(lane < 32, jnp.tanh(lo), jnp.where(lane < 64, lo, _sigmoid(lo)))
    lora = _dot(_bf(act), lora_ref[...])
    x_w = w0_ref[...] + lora[:, 0:dim]
    w_log = -(jnp.maximum(-x_w, 0.0) + jnp.log(1.0 + jnp.exp(-jnp.abs(x_w)))) - 0.5
    a = _sigmoid(a0_ref[...] + lora[:, dim:2 * dim])
    g = lora[:, 2 * dim:3 * dim]
    bd = bd_ref[...]
    kk = k * kk_ref[...]
    kk = kk * lax.rsqrt(jnp.maximum(_head_sums(kk * kk, bd), 1e-24))
    k = k * (1.0 + (a - 1.0) * ka_ref[...])
    r_s[...] = r
    k_s[...] = k
    v_s[...] = v
    a_s[...] = -kk
    b_s[...] = kk * a
    ld_s[...] = -jnp.exp(w_log)

    row_c = lax.broadcasted_iota(jnp.int32, (c, c), 0)
    col_c = lax.broadcasted_iota(jnp.int32, (c, c), 1)
    tri_incl = (row_c >= col_c).astype(BF16)
    eye_c = (row_c == col_c).astype(F32)
    row_w = lax.broadcasted_iota(jnp.int32, (2 * c, 2 * c), 0)
    col_w = lax.broadcasted_iota(jnp.int32, (2 * c, 2 * c), 1)
    row_in = jnp.where(row_w >= c, row_w - c, row_w)
    col_in = jnp.where(col_w >= c, col_w - c, col_w)
    strict_w = row_in > col_in
    incl_w = row_in >= col_in
    second_w = col_w >= c
    lane_c = lax.broadcasted_iota(jnp.int32, (c, LANES), 1)
    head0 = lane_c < HEAD
    lane_2 = lax.broadcasted_iota(jnp.int32, (c, 2 * LANES), 1)
    head0_2 = jnp.where(lane_2 >= LANES, lane_2 - LANES, lane_2) < HEAD
    row_p = lax.broadcasted_iota(jnp.int32, (LANES, LANES), 0)
    col_p = lax.broadcasted_iota(jnp.int32, (LANES, LANES), 1)
    same_head = (row_p < HEAD) == (col_p < HEAD)
    diag_p = row_p == col_p
    zeros_cl = jnp.zeros((c, LANES), BF16)

    def group_body(gi, carry):
        base = gi * (GROUP_CHUNKS * c)
        items = [(pl.ds(pl.multiple_of(base + q * c, c), c), slice(j * LANES, (j + 1) * LANES), j)
                 for q in range(GROUP_CHUNKS) for j in range(n_pairs)]
        n_it = len(items)
        loaded = [[ref[rows, lanes] for ref in (r_s, k_s, v_s, a_s, b_s, ld_s)] for rows, lanes, _ in items]
        r_c, k_c, v_c, a_c, b_c, ld_c = [list(t) for t in zip(*loaded)]
        cum = []
        for i in range(n_it):
            ld_hi, ld_lo = _split(ld_c[i])
            cum.append(_dot(tri_incl, ld_hi) + _dot(tri_incl, ld_lo))
        cum_end = [x[c - 1:c, :] for x in cum]
        at = [a_c[i] * jnp.exp(cum[i] - ld_c[i]) for i in range(n_it)]
        rt = [r_c[i] * jnp.exp(cum[i]) for i in range(n_it)]
        v_bf = [_bf(x) for x in v_c]
        aa = []
        for i in range(n_it):
            e_neg = jnp.exp(-cum[i])
            lhs = _bf(jnp.concatenate([jnp.where(head0, at[i], 0.0), jnp.where(head0, 0.0, at[i]),
                                       jnp.where(head0, rt[i], 0.0), jnp.where(head0, 0.0, rt[i])], axis=0))
            rhs = _bf(jnp.concatenate([b_c[i] * e_neg, k_c[i] * e_neg], axis=0))
            aa.append(_dot_nt(lhs, rhs))
        t_both = [jnp.where(strict_w, x[0:2 * c, :], 0.0) for x in aa]
        q_both = [_bf(jnp.where(incl_w, x[2 * c:4 * c, :], 0.0)) for x in aa]
        akv = [_dot(_bf(jnp.where(second_w, t_both[i], 0.0)), jnp.concatenate([v_bf[i], v_bf[i]], axis=0))
               for i in range(n_it)]
        n_hd = 2 * n_it
        lower = [t_both[n // 2][(n % 2) * c:(n % 2 + 1) * c, 0:c] for n in range(n_hd)]
        inv = [eye_c + l for l in lower]
        lower_bf = [_bf(l) for l in lower]
        pw = [_bf(_dot(l, l)) for l in lower_bf]
        for _ in range(4):
            both = [_dot(jnp.concatenate([pw[n], _bf(inv[n])], axis=0), pw[n]) for n in range(n_hd)]
            inv = [inv[n] + both[n][c:, :] for n in range(n_hd)]
            pw = [_bf(both[n][0:c, :]) for n in range(n_hd)]
        inv = [inv[n] + _dot(_bf(inv[n]), pw[n]) for n in range(n_hd)]
        x_mat = [_bf(jnp.concatenate([at[i], jnp.where(head0, akv[i][0:c, :], akv[i][c:, :])], axis=1))
                 for i in range(n_it)]
        tx_both = [_dot(_bf(jnp.concatenate([inv[2 * i], inv[2 * i + 1]], axis=0)), x_mat[i]) for i in range(n_it)]
        w_mat = [jnp.concatenate([_bf(jnp.where(head0_2, tx_both[i][0:c, :], tx_both[i][c:, :])),
                                  jnp.concatenate([zeros_cl, v_bf[i]], axis=1)], axis=0) for i in range(n_it)]
        qw_both = [_dot(q_both[i], w_mat[i]) for i in range(n_it)]
        qw = [jnp.where(head0_2, x[0:c, :], x[c:, :]) for x in qw_both]
        mn = []
        for i in range(n_it):
            e_end = jnp.exp(cum_end[i] - cum[i])
            bk_t = _bf(jnp.concatenate([b_c[i] * e_end, k_c[i] * e_end], axis=0).T)
            mn.append(_dot(bk_t, w_mat[i]))
        g_col = [jnp.sum(jnp.where(diag_p, jnp.exp(cum_end[i]), 0.0), axis=1, keepdims=True) for i in range(n_it)]
        states = [st_ref[j] for j in range(n_pairs)]
        for i, (rows, lanes, j) in enumerate(items):
            st = states[j]
            st_bf = _bf(st)
            y_s[rows, lanes] = _dot(_bf(rt[i] + qw[i][:, 0:LANES]), st_bf) + qw[i][:, LANES:]
            m_low = _bf(jnp.where(same_head, mn[i][:, 0:LANES], 0.0))
            states[j] = g_col[i] * st + _dot(m_low, st_bf) + jnp.where(same_head, mn[i][:, LANES:], 0.0)
        for j in range(n_pairs):
            st_ref[j] = states[j]
        return carry

    lax.fori_loop(0, tt // (c * GROUP_CHUNKS), group_body, 0)

    y = y_s[...]
    inv_n = 1.0 / HEAD
    mean = _head_sums(y, bd) * inv_n
    d = y - mean
    var = _head_sums(d * d, bd) * inv_n
    y = d * lax.rsqrt(var + RWKV_GN_EPS) * lnw_ref[...] + lnb_ref[...]
    r = r_s[...]
    bonus = _head_sums(r * k_s[...] * rk_ref[...], bd) * v_s[...]
    o_ref[...] = ((y + bonus) * g).astype(o_ref.dtype)


def _rwkv(p_rwkv, vecs, lora_w, bd, batch, seq, dim):
    n, width = p_rwkv.shape
    tt = CHUNK * GROUP_CHUNKS
    n_t = seq // tt
    vec_spec = lambda a: pl.BlockSpec(a.shape, lambda b, t: (0, 0))
    act = pltpu.VMEM((tt, dim), F32)
    return pl.pallas_call(
        functools.partial(_rwkv_kernel, tt=tt, dim=dim),
        grid=(batch, n_t),
        in_specs=[pl.BlockSpec((tt, width), lambda b, t: (b * n_t + t, 0))]
                 + [vec_spec(a) for a in vecs] + [vec_spec(lora_w), vec_spec(bd)],
        out_specs=pl.BlockSpec((tt, dim), lambda b, t: (b * n_t + t, 0)),
        out_shape=jax.ShapeDtypeStruct((n, dim), BF16),
        scratch_shapes=[pltpu.VMEM((tt + 8, width), F32),
                        pltpu.VMEM((dim // LANES, LANES, LANES), F32),
                        act, act, act, act, act, act, act],
        compiler_params=_params("parallel", "arbitrary"),
        name="rwkv7",
    )(p_rwkv, *vecs, lora_w, bd)


def _conv_kernel(p_ref, w_ref, b_ref, lnw_ref, lnb_ref, shift_ref, o_ref, hbuf, *, tt, dim):
    sub = CONV_SUB
    ext = sub + 8

    @pl.when(pl.program_id(1) == 0)
    def _():
        hbuf[0:CONV_HALO, :] = jnp.zeros((CONV_HALO, dim), F32)

    hbuf[CONV_HALO:CONV_HALO + tt, :] = p_ref[:, 0:dim].astype(F32) * _sigmoid(p_ref[:, dim:2 * dim].astype(F32))
    for s in range(tt // sub):
        base = CONV_HALO + s * sub - 8
        wins = [hbuf[base - 8 * a:base - 8 * a + ext, :] for a in range(CONV_HALO // 8)]
        parts = []
        for r in range(8):
            lags = [8 * a + r for a in range(len(wins)) if 8 * a + r < CONV_WIDTH]
            terms = [wins[m // 8] * w_ref[CONV_WIDTH - 1 - m:CONV_WIDTH - m, :] for m in lags]
            parts.append(functools.reduce(jnp.add, terms))
        stacked = _bf(jnp.concatenate(parts, axis=0))
        acc = _dot(shift_ref[...], stacked) + b_ref[...]
        mu = jnp.mean(acc, axis=-1, keepdims=True)
        d = acc - mu
        var = jnp.mean(d * d, axis=-1, keepdims=True)
        y = d * lax.rsqrt(var + LN_EPS) * lnw_ref[...] + lnb_ref[...]
        o_ref[s * sub:(s + 1) * sub, :] = (y * _sigmoid(y)).astype(o_ref.dtype)
    hbuf[0:CONV_HALO, :] = hbuf[tt:tt + CONV_HALO, :]


def _conv(p_conv, conv_w, conv_b, ln_w, ln_b, batch, seq):
    n, width = p_conv.shape
    dim = width // 2
    tt = 512
    n_t = seq // tt
    ext = CONV_SUB + 8
    u_idx = jnp.arange(CONV_SUB)[:, None]
    col = jnp.arange(8 * ext)[None, :]
    shift = (col % ext == u_idx + 8 - col // ext).astype(BF16)
    vec = lambda a: pl.BlockSpec(a.shape, lambda b, t: (0, 0))
    return pl.pallas_call(
        functools.partial(_conv_kernel, tt=tt, dim=dim),
        grid=(batch, n_t),
        in_specs=[pl.BlockSpec((tt, width), lambda b, t: (b * n_t + t, 0)),
                  vec(conv_w), vec(conv_b), vec(ln_w), vec(ln_b), vec(shift)],
        out_specs=pl.BlockSpec((tt, dim), lambda b, t: (b * n_t + t, 0)),
        out_shape=jax.ShapeDtypeStruct((n, dim), BF16),
        scratch_shapes=[pltpu.VMEM((tt + CONV_HALO, dim), F32)],
        compiler_params=_params("parallel", "arbitrary"),
        name="conformer_conv",
    )(p_conv, conv_w, conv_b, ln_w, ln_b, shift)


def _mlaprep_kernel(p_ref, cos_ref, sin_ref, qn_ref, kvn_ref, wuq_ref, wuk_ref, wuv_ref,
                    q_ref, k_ref, v_ref, *, lora, n_heads, scale):
    cos = cos_ref[...]
    sin = sin_ref[...]
    qn = (_rms(p_ref[:, 0:lora].astype(F32)) * qn_ref[...]).astype(BF16)
    kvn = (_rms(p_ref[:, lora:2 * lora].astype(F32)) * kvn_ref[...]).astype(BF16)
    kpe = (p_ref[:, 2 * lora:2 * lora + LANES].astype(F32) * cos
           + p_ref[:, 2 * lora + LANES:2 * lora + 2 * LANES].astype(F32) * sin)
    lane = lax.broadcasted_iota(jnp.int32, cos.shape, 1)
    q_mul = jnp.where(lane < HEAD, 1.0, jnp.where(lane < HEAD + 32, cos, sin)) * scale
    for h in range(n_heads):
        cols = slice(h * LANES, (h + 1) * LANES)
        q_ref[:, cols] = (_dot(qn, wuq_ref[:, cols]) * q_mul).astype(BF16)
        k_ref[:, cols] = (_dot(kvn, wuk_ref[:, cols]) + kpe).astype(BF16)
        v_ref[:, cols] = jnp.where(lane < HEAD, _dot(kvn, wuv_ref[:, cols]), 1.0).astype(BF16)


def _mlaprep(p_mla, cos, sin, q_norm, kv_norm, wuq, wuk, wuv, lora, n_heads, scale):
    n, width = p_mla.shape
    tm = 512
    row = lambda w: pl.BlockSpec((tm, w), lambda i: (i, 0))
    full = lambda a: pl.BlockSpec(a.shape, lambda i: (0, 0))
    return pl.pallas_call(
        functools.partial(_mlaprep_kernel, lora=lora, n_heads=n_heads, scale=scale),
        grid=(n // tm,),
        in_specs=[row(width), row(LANES), row(LANES), full(q_norm), full(kv_norm), full(wuq), full(wuk), full(wuv)],
        out_specs=[row(n_heads * LANES)] * 3,
        out_shape=[jax.ShapeDtypeStruct((n, n_heads * LANES), BF16)] * 3,
        compiler_params=_params("parallel"),
        name="mla_prep",
    )(p_mla, cos, sin, q_norm, kv_norm, wuq, wuk, wuv)


def _flash_kernel(q_ref, k_ref, v_ref, o_ref, m_s, acc_s, s_buf, p_buf, al_buf):
    i = pl.program_id(2)
    t = FLASH_BLOCK
    m_s[...] = jnp.full(m_s.shape, -jnp.inf, F32)
    acc_s[...] = jnp.zeros(acc_s.shape, F32)
    p_buf[1] = jnp.zeros(p_buf.shape[1:], BF16)
    al_buf[1] = jnp.ones(al_buf.shape[1:], F32)
    causal = lax.broadcasted_iota(jnp.int32, (t, t), 0) >= lax.broadcasted_iota(jnp.int32, (t, t), 1)

    def keys_of(j):
        return pl.ds(pl.multiple_of(j * t, t), t)

    def scores(j, slot, u0):
        for h in range(2):
            cols = slice(h * LANES, (h + 1) * LANES)
            s_buf[slot, h, u0 * t:, :] = _dot_nt(q_ref[u0 * t:, cols], k_ref[keys_of(j), cols])

    def softmax(slot, u0, masked_sub):
        for u in range(u0, FLASH_SUBS):
            rows = slice(u * t, (u + 1) * t)
            for h in range(2):
                s = s_buf[slot, h, rows, :]
                if u == masked_sub:
                    s = jnp.where(causal, s, -jnp.inf)
                halves = [s[:, n * LANES:(n + 1) * LANES] for n in range(t // LANES)]
                m_prev = m_s[h, rows, :]
                m_new = jnp.maximum(m_prev, jnp.max(functools.reduce(jnp.maximum, halves), axis=-1, keepdims=True))
                m_s[h, rows, :] = m_new
                al_buf[slot, h, rows, :] = jnp.exp2(m_prev - m_new)
                p_buf[slot, h, rows, :] = jnp.concatenate(
                    [jnp.exp2((x - m_new).astype(BF16)) for x in halves], axis=1)

    def accumulate(j, slot, u0):
        for h in range(2):
            pv = _dot(p_buf[slot, h, u0 * t:, :], v_ref[keys_of(j), h * LANES:(h + 1) * LANES])
            acc_s[h, u0 * t:, :] = al_buf[slot, h, u0 * t:, :] * acc_s[h, u0 * t:, :] + pv

    n_full = FLASH_SUBS * i
    scores(0, 0, 0)

    def body(g, carry):
        for par in range(2):
            k = 2 * g + par
            scores(k + 1, 1 - par, 0)
            accumulate(jnp.maximum(k - 1, 0), 1 - par, 0)
            softmax(par, 0, None)
        return carry

    lax.fori_loop(0, n_full // 2, body, 0)
    accumulate(jnp.maximum(n_full - 1, 0), 1, 0)
    for d in range(FLASH_SUBS):
        if d + 1 < FLASH_SUBS:
            scores(n_full + d + 1, (d + 1) % 2, d + 1)
        softmax(d % 2, d, d)
        accumulate(n_full + d, d % 2, d)
    head0 = lax.broadcasted_iota(jnp.int32, (FLASH_SUBS * t, LANES), 1) < HEAD
    a0, a1 = acc_s[0], acc_s[1]
    out = jnp.where(head0, a0 / pltpu.roll(a0, HEAD, 1), pltpu.roll(a1, HEAD, 1) / a1)
    o_ref[...] = out.astype(o_ref.dtype)


def _flash(q, k, v, batch, seq, n_heads):
    n = q.shape[0]
    tq = FLASH_SUBS * FLASH_BLOCK
    n_q = seq // tq
    stat = pltpu.VMEM((2, tq, LANES), F32)
    return pl.pallas_call(
        _flash_kernel,
        grid=(batch, n_heads // 2, n_q),
        in_specs=[pl.BlockSpec((tq, 2 * LANES), lambda b, h, i: (b * n_q + i, h)),
                  pl.BlockSpec((seq, 2 * LANES), lambda b, h, i: (b, h)),
                  pl.BlockSpec((seq, 2 * LANES), lambda b, h, i: (b, h))],
        out_specs=pl.BlockSpec((tq, LANES), lambda b, h, i: (b * n_q + i, h)),
        out_shape=jax.ShapeDtypeStruct((n, n_heads * HEAD), BF16),
        scratch_shapes=[stat, stat,
                        pltpu.VMEM((2, 2, tq, FLASH_BLOCK), F32),
                        pltpu.VMEM((2, 2, tq, FLASH_BLOCK), BF16),
                        pltpu.VMEM((2, 2, tq, LANES), F32)],
        compiler_params=_params("parallel", "parallel", "arbitrary"),
        name="mla_flash",
    )(q, k, v)


def _outffn_kernel(ya_ref, yb_ref, yc_ref, x_ref, mod_ref, gpm_ref, gpf_ref, gqf_ref, wo_ref, w1_ref, w2_ref,
                   o_ref, x1_s, hid_s, *, da, db, tf):
    y = (_dot(ya_ref[...], wo_ref[0:da, :])
         + _dot(yb_ref[...], wo_ref[da:da + db, :])
         + _dot(yc_ref[...], wo_ref[da + db:, :]))
    x1 = x_ref[...] + mod_ref[0, 2:3, :] * (_rms(y) * gpm_ref[...])
    x1_s[...] = x1
    h = _rms(x1) * gpf_ref[...]
    h = (h * (1.0 + mod_ref[0, 4:5, :]) + mod_ref[0, 3:4, :]).astype(BF16)
    for f in range(hid_s.shape[1] // tf):
        cols = slice(f * tf, (f + 1) * tf)
        hid = jnp.maximum(_dot(h, w1_ref[:, cols]), 0.0)
        hid_s[:, cols] = (hid * hid).astype(BF16)
    y2 = _dot(hid_s[...], w2_ref[...])
    o_ref[...] = x1_s[...] + mod_ref[0, 5:6, :] * (_rms(y2) * gqf_ref[...])


def _outffn(ya, yb, yc, x, mod, g_post_mix, g_pre_ffn, g_post_ffn, w_out, w_ff1, w_ff2, layer, seq):
    n, d = x.shape
    d_ff = w_ff1.shape[2]
    tm, tf = 512, 512
    per_b = seq // tm
    row = lambda w: pl.BlockSpec((tm, w), lambda i: (i, 0))
    vec = pl.BlockSpec((1, d), lambda i: (0, 0))
    resident = lambda a: pl.BlockSpec((None,) + a.shape[1:], lambda i: (layer, 0, 0), pipeline_mode=pl.Buffered(1))
    return pl.pallas_call(
        functools.partial(_outffn_kernel, da=ya.shape[1], db=yb.shape[1], tf=tf),
        grid=(n // tm,),
        in_specs=[row(ya.shape[1]), row(yb.shape[1]), row(yc.shape[1]), row(d),
                  pl.BlockSpec((1, 6, d), lambda i: (i // per_b, 0, 0)),
                  vec, vec, vec, resident(w_out), resident(w_ff1), resident(w_ff2)],
        out_specs=row(d),
        out_shape=jax.ShapeDtypeStruct((n, d), F32),
        scratch_shapes=[pltpu.VMEM((tm, d), F32), pltpu.VMEM((tm, d_ff), BF16)],
        compiler_params=_params("parallel"),
        name="out_ffn",
    )(ya, yb, yc, x, mod, g_post_mix, g_pre_ffn, g_post_ffn, w_out, w_ff1, w_ff2)


def _rot_cols(w):
    half = w.shape[-1] // 2
    return jnp.concatenate([-w[..., half:], w[..., :half]], axis=-1)


def kernel(x, c, positions, g_pre_mix, g_post_mix, g_pre_ffn, g_post_ffn, w_ada, b_ada, w_in, w_out, rwkv_mu, rwkv_w0, rwkv_w2, rwkv_a0, rwkv_a2, rwkv_g2, rwkv_k_k, rwkv_k_a, rwkv_r_k, rwkv_ln_w, rwkv_ln_b, conv_w, conv_b, conv_ln_w, conv_ln_b, mla_q_norm, mla_w_uq, mla_kv_norm, mla_w_ukv, w_ff1, w_ff2):
    batch, seq, d = x.shape
    n = batch * seq
    n_layers = w_ada.shape[0]
    r_dim = rwkv_w0.shape[1]
    w_lora, a_lora, g_lora = rwkv_w2.shape[1], rwkv_a2.shape[1], rwkv_g2.shape[1]
    assert w_lora + a_lora + g_lora == LANES and r_dim % LANES == 0
    n_r = 3 * r_dim + LANES
    c_dim = conv_w.shape[2]
    n_c = 2 * c_dim
    lora = mla_q_norm.shape[1]
    n_heads = mla_w_ukv.shape[2] // (2 * HEAD)
    rope = mla_w_uq.shape[2] // n_heads - HEAD
    assert HEAD + 2 * rope == LANES
    n_m = 2 * lora + 2 * LANES
    scale = float(HEAD + rope) ** -0.5 * 1.4426950408889634

    xf = x.reshape(n, d)
    c8 = jnp.pad(c, ((0, 8 - batch), (0, 0)))
    mod_all = _ada(c8, w_ada, b_ada.reshape(n_layers, 1, 6 * d))[:, :batch].reshape(n_layers, batch, 6, d)

    inv_freq = 1.0 / (ROPE_THETA ** (jnp.arange(0, rope, 2, dtype=F32) / rope))
    invf = jnp.tile(inv_freq, LANES // inv_freq.shape[0]).reshape(1, LANES)
    cos, sin = _rope_tables(positions.reshape(n, 1), invf)

    head_id = jnp.arange(LANES) // HEAD
    bd = (head_id[:, None] == head_id[None, :]).astype(BF16)
    row1 = lambda a: a.reshape(1, -1)

    w_in_bf, w_out_bf, w_ff1_bf, w_ff2_bf = [w.astype(BF16) for w in (w_in, w_out, w_ff1, w_ff2)]
    for l in range(n_layers):
        k_pe = w_in_bf[l, :, n_r + n_c + 2 * lora:]
        z_half = jnp.zeros((d, HEAD), BF16)
        w_extra = jnp.concatenate([z_half, k_pe, k_pe, z_half, _rot_cols(k_pe), _rot_cols(k_pe)], axis=1)
        wq = mla_w_uq[l].reshape(lora, n_heads, HEAD + rope)
        wuq = jnp.concatenate([wq, _rot_cols(wq[..., HEAD:])], axis=-1).reshape(lora, n_heads * LANES).astype(BF16)
        wkv = mla_w_ukv[l].reshape(lora, n_heads, 2 * HEAD)
        wuk = jnp.concatenate([wkv[..., :HEAD], jnp.zeros_like(wkv[..., :HEAD])], axis=-1)
        wuk = wuk.reshape(lora, n_heads * LANES).astype(BF16)
        wuv = jnp.concatenate([wkv[..., HEAD:], jnp.zeros_like(wkv[..., HEAD:])], axis=-1)
        wuv = wuv.reshape(lora, n_heads * LANES).astype(BF16)
        lora_w = jnp.zeros((LANES, 3 * r_dim), F32)
        lora_w = lora_w.at[0:w_lora, 0:r_dim].set(rwkv_w2[l])
        lora_w = lora_w.at[w_lora:w_lora + a_lora, r_dim:2 * r_dim].set(rwkv_a2[l])
        lora_w = lora_w.at[w_lora + a_lora:, 2 * r_dim:].set(rwkv_g2[l]).astype(BF16)

        mod = mod_all[l]
        p_rwkv, p_conv, p_mla = _inproj(xf, mod, row1(g_pre_mix[l]), w_in_bf, l, w_extra, seq, n_r, n_c, n_m)
        vecs = [row1(rwkv_mu[l]), row1(rwkv_w0[l]), row1(rwkv_a0[l]), row1(rwkv_k_k[l]), row1(rwkv_k_a[l]),
                row1(rwkv_r_k[l]), row1(rwkv_ln_w[l]), row1(rwkv_ln_b[l])]
        y_a = _rwkv(p_rwkv, vecs, lora_w, bd, batch, seq, r_dim)
        y_b = _conv(p_conv, conv_w[l], row1(conv_b[l]), row1(conv_ln_w[l]), row1(conv_ln_b[l]), batch, seq)
        q, k, v = _mlaprep(p_mla, cos, sin, row1(mla_q_norm[l]), row1(mla_kv_norm[l]), wuq, wuk, wuv,
                           lora, n_heads, scale)
        y_c = _flash(q, k, v, batch, seq, n_heads)
        xf = _outffn(y_a, y_b, y_c, xf, mod, row1(g_post_mix[l]), row1(g_pre_ffn[l]), row1(g_post_ffn[l]),
                     w_out_bf, w_ff1_bf, w_ff2_bf, l, seq)
    return xf.reshape(batch, seq, d)
```

```python
import functools

import jax
import jax.numpy as jnp
from jax import lax
from jax.experimental import pallas as pl
from jax.experimental.pallas import tpu as pltpu

F32 = jnp.float32
BF16 = jnp.bfloat16

LANES = 128
HEAD = 64
NORM_EPS = 1e-6
LN_EPS = 1e-5
RWKV_GN_EPS = 64e-5
ROPE_THETA = 10000.0
CONV_WIDTH = 31
CONV_HALO = 32
CONV_SUB = 128
CHUNK = 64
FLASH_BLOCK = 256
FLASH_SUBS = 8
GROUP_CHUNKS = 8
VMEM_LIMIT = 56 * 1024 * 1024


def _dot(a, b, precision=None):
    return jnp.dot(a, b, preferred_element_type=F32, precision=precision)


def _dot_nt(a, b, precision=None):
    return lax.dot_general(a, b, (((1,), (1,)), ((), ())), preferred_element_type=F32, precision=precision)


def _bf(x):
    return x.astype(BF16)


def _split(x):
    hi = x.astype(BF16)
    return hi, (x - hi.astype(F32)).astype(BF16)


def _head_sums(x, ones_bd):
    groups = [_dot(_bf(x[:, g * LANES:(g + 1) * LANES]), ones_bd) for g in range(x.shape[1] // LANES)]
    return jnp.concatenate(groups, axis=1)


def _rms(x):
    return x * lax.rsqrt(jnp.mean(x * x, axis=-1, keepdims=True) + NORM_EPS)


def _sigmoid(x):
    return 1.0 / (1.0 + jnp.exp(-x))


def _params(*semantics):
    return pltpu.CompilerParams(dimension_semantics=semantics, vmem_limit_bytes=VMEM_LIMIT)


def _ada_kernel(c_ref, w_ref, b_ref, o_ref):
    c = c_ref[...]
    cs = c * _sigmoid(c)
    cs_hi, cs_lo = _split(cs)
    w_hi, w_lo = _split(w_ref[0])
    o_ref[0] = _dot(cs_hi, w_hi) + _dot(cs_hi, w_lo) + _dot(cs_lo, w_hi) + b_ref[0]


def _ada(c8, w_ada, b_ada):
    n_layers, d, d6 = w_ada.shape
    tn = 1536
    return pl.pallas_call(
        _ada_kernel,
        grid=(n_layers, d6 // tn),
        in_specs=[pl.BlockSpec((8, d), lambda l, j: (0, 0)),
                  pl.BlockSpec((1, d, tn), lambda l, j: (l, 0, j)),
                  pl.BlockSpec((1, 1, tn), lambda l, j: (l, 0, j))],
        out_specs=pl.BlockSpec((1, 8, tn), lambda l, j: (l, 0, j)),
        out_shape=jax.ShapeDtypeStruct((n_layers, 8, d6), F32),
        compiler_params=_params("parallel", "parallel"),
        name="ada_mod",
    )(c8, w_ada, b_ada)


def _rope_kernel(pos_ref, invf_ref, cos_ref, sin_ref):
    ang = pos_ref[...].astype(F32) * invf_ref[...]
    cos_ref[...] = jnp.cos(ang)
    sin_ref[...] = jnp.sin(ang)


def _rope_tables(pos_col, invf):
    n = pos_col.shape[0]
    tm = 2048
    spec = pl.BlockSpec((tm, LANES), lambda i: (i, 0))
    return pl.pallas_call(
        _rope_kernel,
        grid=(n // tm,),
        in_specs=[pl.BlockSpec((tm, 1), lambda i: (i, 0)), pl.BlockSpec((1, LANES), lambda i: (0, 0))],
        out_specs=[spec, spec],
        out_shape=[jax.ShapeDtypeStruct((n, LANES), F32)] * 2,
        compiler_params=_params("parallel"),
        name="rope_tables",
    )(pos_col, invf)


def _inproj_kernel(x_ref, mod_ref, g_ref, w_ref, wx_ref, pr_ref, pc_ref, pm_ref, *, n_r, n_c, n_lat):
    h = _rms(x_ref[...]) * g_ref[...]
    h = (h * (1.0 + mod_ref[0, 1:2, :]) + mod_ref[0, 0:1, :]).astype(BF16)
    pr_ref[...] = _dot(h, w_ref[:, :n_r]).astype(pr_ref.dtype)
    pc_ref[...] = _dot(h, w_ref[:, n_r:n_r + n_c]).astype(pc_ref.dtype)
    pm_ref[:, :n_lat] = _dot(h, w_ref[:, n_r + n_c:n_r + n_c + n_lat]).astype(pm_ref.dtype)
    pm_ref[:, n_lat:] = _dot(h, wx_ref[...]).astype(pm_ref.dtype)


def _inproj(x, mod, g, w_in_all, layer, w_extra, seq, n_r, n_c, n_m):
    n, d = x.shape
    tm = 512
    per_b = seq // tm
    return pl.pallas_call(
        functools.partial(_inproj_kernel, n_r=n_r, n_c=n_c, n_lat=n_m - w_extra.shape[1]),
        grid=(n // tm,),
        in_specs=[pl.BlockSpec((tm, d), lambda i: (i, 0)),
                  pl.BlockSpec((1, 6, d), lambda i: (i // per_b, 0, 0)),
                  pl.BlockSpec((1, d), lambda i: (0, 0)),
                  pl.BlockSpec((None,) + w_in_all.shape[1:], lambda i: (layer, 0, 0)),
                  pl.BlockSpec(w_extra.shape, lambda i: (0, 0))],
        out_specs=[pl.BlockSpec((tm, n_r), lambda i: (i, 0)),
                   pl.BlockSpec((tm, n_c), lambda i: (i, 0)),
                   pl.BlockSpec((tm, n_m), lambda i: (i, 0))],
        out_shape=[jax.ShapeDtypeStruct((n, n_r), BF16),
                   jax.ShapeDtypeStruct((n, n_c), BF16),
                   jax.ShapeDtypeStruct((n, n_m), BF16)],
        compiler_params=_params("parallel"),
        name="in_proj",
    )(x, mod, g, w_in_all, w_extra)


def _rwkv_kernel(p_ref, mu_ref, w0_ref, a0_ref, kk_ref, ka_ref, rk_ref, lnw_ref, lnb_ref, lora_ref, bd_ref,
                 o_ref, zbuf, st_ref, r_s, k_s, v_s, a_s, b_s, ld_s, y_s, *, tt, dim):
    n_pairs = dim // LANES
    c = CHUNK

    @pl.when(pl.program_id(1) == 0)
    def _():
        zbuf[0:8, :] = jnp.zeros((8, zbuf.shape[1]), F32)
        st_ref[...] = jnp.zeros(st_ref.shape, F32)

    p = p_ref[...].astype(F32)
    zbuf[8:8 + tt, :] = p
    prev = zbuf[7:7 + tt, :]
    zbuf[7:8, :] = zbuf[tt + 7:tt + 8, :]
    z = p + (prev - p) * mu_ref[...]

    r = z[:, 0:dim]
    k = z[:, dim:2 * dim]
    v = z[:, 2 * dim:3 * dim]
    lo = z[:, 3 * dim:3 * dim + LANES]
    lane = lax.broadcasted_iota(jnp.int32, lo.shape, 1)
    act = jnp.where(lane < 32, jnp.tanh(lo), jnp.where(lane < 64, lo, _sigmoid(lo)))
    lora = _dot(_bf(act), lora_ref[...])
    x_w = w0_ref[...] + lora[:, 0:dim]
    w_log = -(jnp.maximum(-x_w, 0.0) + jnp.log(1.0 + jnp.exp(-jnp.abs(x_w)))) - 0.5
    a = _sigmoid(a0_ref[...] + lora[:, dim:2 * dim])
    g = lora[:, 2 * dim:3 * dim]
    bd = bd_ref[...]
    kk = k * kk_ref[...]
    kk = kk * lax.rsqrt(jnp.maximum(_head_sums(kk * kk, bd), 1e-24))
    k = k * (1.0 + (a - 1.0) * ka_ref[...])
    r_s[...] = r
    k_s[...] = k
    v_s[...] = v
    a_s[...] = -kk
    b_s[...] = kk * a
    ld_s[...] = -jnp.exp(w_log)

    row_c = lax.broadcasted_iota(jnp.int32, (c, c), 0)
    col_c = lax.broadcasted_iota(jnp.int32, (c, c), 1)
    tri_incl = (row_c >= col_c).astype(BF16)
    eye_c = (row_c == col_c).astype(F32)
    row_w = lax.broadcasted_iota(jnp.int32, (2 * c, 2 * c), 0)
    col_w = lax.broadcasted_iota(jnp.int32, (2 * c, 2 * c), 1)
    row_in = jnp.where(row_w >= c, row_w - c, row_w)
    col_in = jnp.where(col_w >= c, col_w - c, col_w)
    strict_w = row_in > col_in
    incl_w = row_in >= col_in
    second_w = col_w >= c
    lane_c = lax.broadcasted_iota(jnp.int32, (c, LANES), 1)
    head0 = lane_c < HEAD
    lane_2 = lax.broadcasted_iota(jnp.int32, (c, 2 * LANES), 1)
    head0_2 = jnp.where(lane_2 >= LANES, lane_2 - LANES, lane_2) < HEAD
    row_p = lax.broadcasted_iota(jnp.int32, (LANES, LANES), 0)
    col_p = lax.broadcasted_iota(jnp.int32, (LANES, LANES), 1)
    same_head = (row_p < HEAD) == (col_p < HEAD)
    diag_p = row_p == col_p
    zeros_cl = jnp.zeros((c, LANES), BF16)

    def group_body(gi, carry):
        base = gi * (GROUP_CHUNKS * c)
        items = [(pl.ds(pl.multiple_of(base + q * c, c), c), slice(j * LANES, (j + 1) * LANES), j)
                 for q in range(GROUP_CHUNKS) for j in range(n_pairs)]
        n_it = len(items)
        loaded = [[ref[rows, lanes] for ref in (r_s, k_s, v_s, a_s, b_s, ld_s)] for rows, lanes, _ in items]
        r_c, k_c, v_c, a_c, b_c, ld_c = [list(t) for t in zip(*loaded)]
        cum = []
        for i in range(n_it):
            ld_hi, ld_lo = _split(ld_c[i])
            cum.append(_dot(tri_incl, ld_hi) + _dot(tri_incl, ld_lo))
        cum_end = [x[c - 1:c, :] for x in cum]
        at = [a_c[i] * jnp.exp(cum[i] - ld_c[i]) for i in range(n_it)]
        rt = [r_c[i] * jnp.exp(cum[i]) for i in range(n_it)]
        v_bf = [_bf(x) for x in v_c]
        aa = []
        for i in range(n_it):
            e_neg = jnp.exp(-cum[i])
            lhs = _bf(jnp.concatenate([jnp.where(head0, at[i], 0.0), jnp.where(head0, 0.0, at[i]),
                                       jnp.where(head0, rt[i], 0.0), jnp.where(head0, 0.0, rt[i])], axis=0))
            rhs = _bf(jnp.concatenate([b_c[i] * e_neg, k_c[i] * e_neg], axis=0))
            aa.append(_dot_nt(lhs, rhs))
        t_both = [jnp.where(strict_w, x[0:2 * c, :], 0.0) for x in aa]
        q_both = [_bf(jnp.where(incl_w, x[2 * c:4 * c, :], 0.0)) for x in aa]
        akv = [_dot(_bf(jnp.where(second_w, t_both[i], 0.0)), jnp.concatenate([v_bf[i], v_bf[i]], axis=0))
               for i in range(n_it)]
        n_hd = 2 * n_it
        lower = [t_both[n // 2][(n % 2) * c:(n % 2 + 1) * c, 0:c] for n in range(n_hd)]
        inv = [eye_c + l for l in lower]
        lower_bf = [_bf(l) for l in lower]
        pw = [_bf(_dot(l, l)) for l in lower_bf]
        for _ in range(4):
            both = [_dot(jnp.concatenate([pw[n], _bf(inv[n])], axis=0), pw[n]) for n in range(n_hd)]
            inv = [inv[n] + both[n][c:, :] for n in range(n_hd)]
            pw = [_bf(both[n][0:c, :]) for n in range(n_hd)]
        inv = [inv[n] + _dot(_bf(inv[n]), pw[n]) for n in range(n_hd)]
        x_mat = [_bf(jnp.concatenate([at[i], jnp.where(head0, akv[i][0:c, :], akv[i][c:, :])], axis=1))
                 for i in range(n_it)]
        tx_both = [_dot(_bf(jnp.concatenate([inv[2 * i], inv[2 * i + 1]], axis=0)), x_mat[i]) for i in range(n_it)]
        w_mat = [jnp.concatenate([_bf(jnp.where(head0_2, tx_both[i][0:c, :], tx_both[i][c:, :])),
                                  jnp.concatenate([zeros_cl, v_bf[i]], axis=1)], axis=0) for i in range(n_it)]
        qw_both = [_dot(q_both[i], w_mat[i]) for i in range(n_it)]
        qw = [jnp.where(head0_2, x[0:c, :], x[c:, :]) for x in qw_both]
        mn = []
        for i in range(n_it):
            e_end = jnp.exp(cum_end[i] - cum[i])
            bk_t = _bf(jnp.concatenate([b_c[i] * e_end, k_c[i] * e_end], axis=0).T)
            mn.append(_dot(bk_t, w_mat[i]))
        g_col = [jnp.sum(jnp.where(diag_p, jnp.exp(cum_end[i]), 0.0), axis=1, keepdims=True) for i in range(n_it)]
        states = [st_ref[j] for j in range(n_pairs)]
        for i, (rows, lanes, j) in enumerate(items):
            st = states[j]
            st_bf = _bf(st)
            y_s[rows, lanes] = _dot(_bf(rt[i] + qw[i][:, 0:LANES]), st_bf) + qw[i][:, LANES:]
            m_low = _bf(jnp.where(same_head, mn[i][:, 0:LANES], 0.0))
            states[j] = g_col[i] * st + _dot(m_low, st_bf) + jnp.where(same_head, mn[i][:, LANES:], 0.0)
        for j in range(n_pairs):
            st_ref[j] = states[j]
        return carry

    lax.fori_loop(0, tt // (c * GROUP_CHUNKS), group_body, 0)

    y = y_s[...]
    inv_n = 1.0 / HEAD
    mean = _head_sums(y, bd) * inv_n
    d = y - mean
    var = _head_sums(d * d, bd) * inv_n
    y = d * lax.rsqrt(var + RWKV_GN_EPS) * lnw_ref[...] + lnb_ref[...]
    r = r_s[...]
    bonus = _head_sums(r * k_s[...] * rk_ref[...], bd) * v_s[...]
    o_ref[...] = ((y + bonus) * g).astype(o_ref.dtype)


def _rwkv(p_rwkv, vecs, lora_w, bd, batch, seq, dim):
    n, width = p_rwkv.shape
    tt = CHUNK * GROUP_CHUNKS
    n_t = seq // tt
    vec_spec = lambda a: pl.BlockSpec(a.shape, lambda b, t: (0, 0))
    act = pltpu.VMEM((tt, dim), F32)
    return pl.pallas_call(
        functools.partial(_rwkv_kernel, tt=tt, dim=dim),
        grid=(batch, n_t),
        in_specs=[pl.BlockSpec((tt, width), lambda b, t: (b * n_t + t, 0))]
                 + [vec_spec(a) for a in vecs] + [vec_spec(lora_w), vec_spec(bd)],
        out_specs=pl.BlockSpec((tt, dim), lambda b, t: (b * n_t + t, 0)),
        out_shape=jax.ShapeDtypeStruct((n, dim), BF16),
        scratch_shapes=[pltpu.VMEM((tt + 8, width), F32),
                        pltpu.VMEM((dim // LANES, LANES, LANES), F32),
                        act, act, act, act, act, act, act],
        compiler_params=_params("parallel", "arbitrary"),
        name="rwkv7",
    )(p_rwkv, *vecs, lora_w, bd)


def _conv_kernel(p_ref, w_ref, b_ref, lnw_ref, lnb_ref, shift_ref, o_ref, hbuf, *, tt, dim):
    sub = CONV_SUB
    ext = sub + 8

    @pl.when(pl.program_id(1) == 0)
    def _():
        hbuf[0:CONV_HALO, :] = jnp.zeros((CONV_HALO, dim), F32)

    hbuf[CONV_HALO:CONV_HALO + tt, :] = p_ref[:, 0:dim].astype(F32) * _sigmoid(p_ref[:, dim:2 * dim].astype(F32))
    for s in range(tt // sub):
        base = CONV_HALO + s * sub - 8
        wins = [hbuf[base - 8 * a:base - 8 * a + ext, :] for a in range(CONV_HALO // 8)]
        parts = []
        for r in range(8):
            lags = [8 * a + r for a in range(len(wins)) if 8 * a + r < CONV_WIDTH]
            terms = [wins[m // 8] * w_ref[CONV_WIDTH - 1 - m:CONV_WIDTH - m, :] for m in lags]
            parts.append(functools.reduce(jnp.add, terms))
        stacked = _bf(jnp.concatenate(parts, axis=0))
        acc = _dot(shift_ref[...], stacked) + b_ref[...]
        mu = jnp.mean(acc, axis=-1, keepdims=True)
        d = acc - mu
        var = jnp.mean(d * d, axis=-1, keepdims=True)
        y = d * lax.rsqrt(var + LN_EPS) * lnw_ref[...] + lnb_ref[...]
        o_ref[s * sub:(s + 1) * sub, :] = (y * _sigmoid(y)).astype(o_ref.dtype)
    hbuf[0:CONV_HALO, :] = hbuf[tt:tt + CONV_HALO, :]


def _conv(p_conv, conv_w, conv_b, ln_w, ln_b, batch, seq):
    n, width = p_conv.shape
    dim = width // 2
    tt = 512
    n_t = seq // tt
    ext = CONV_SUB + 8
    u_idx = jnp.arange(CONV_SUB)[:, None]
    col = jnp.arange(8 * ext)[None, :]
    shift = (col % ext == u_idx + 8 - col // ext).astype(BF16)
    vec = lambda a: pl.BlockSpec(a.shape, lambda b, t: (0, 0))
    return pl.pallas_call(
        functools.partial(_conv_kernel, tt=tt, dim=dim),
        grid=(batch, n_t),
        in_specs=[pl.BlockSpec((tt, width), lambda b, t: (b * n_t + t, 0)),
                  vec(conv_w), vec(conv_b), vec(ln_w), vec(ln_b), vec(shift)],
        out_specs=pl.BlockSpec((tt, dim), lambda b, t: (b * n_t + t, 0)),
        out_shape=jax.ShapeDtypeStruct((n, dim), BF16),
        scratch_shapes=[pltpu.VMEM((tt + CONV_HALO, dim), F32)],
        compiler_params=_params("parallel", "arbitrary"),
        name="conformer_conv",
    )(p_conv, conv_w, conv_b, ln_w, ln_b, shift)


def _mlaprep_kernel(p_ref, cos_ref, sin_ref, qn_ref, kvn_ref, wuq_ref, wuk_ref, wuv_ref,
                    q_ref, k_ref, v_ref, *, lora, n_heads, scale):
    cos = cos_ref[...]
    sin = sin_ref[...]
    qn = (_rms(p_ref[:, 0:lora].astype(F32)) * qn_ref[...]).astype(BF16)
    kvn = (_rms(p_ref[:, lora:2 * lora].astype(F32)) * kvn_ref[...]).astype(BF16)
    kpe = (p_ref[:, 2 * lora:2 * lora + LANES].astype(F32) * cos
           + p_ref[:, 2 * lora + LANES:2 * lora + 2 * LANES].astype(F32) * sin)
    lane = lax.broadcasted_iota(jnp.int32, cos.shape, 1)
    q_mul = jnp.where(lane < HEAD, 1.0, jnp.where(lane < HEAD + 32, cos, sin)) * scale
    for h in range(n_heads):
        cols = slice(h * LANES, (h + 1) * LANES)
        q_ref[:, cols] = (_dot(qn, wuq_ref[:, cols]) * q_mul).astype(BF16)
        k_ref[:, cols] = (_dot(kvn, wuk_ref[:, cols]) + kpe).astype(BF16)
        v_ref[:, cols] = jnp.where(lane < HEAD, _dot(kvn, wuv_ref[:, cols]), 1.0).astype(BF16)


def _mlaprep(p_mla, cos, sin, q_norm, kv_norm, wuq, wuk, wuv, lora, n_heads, scale):
    n, width = p_mla.shape
    tm = 512
    row = lambda w: pl.BlockSpec((tm, w), lambda i: (i, 0))
    full = lambda a: pl.BlockSpec(a.shape, lambda i: (0, 0))
    return pl.pallas_call(
        functools.partial(_mlaprep_kernel, lora=lora, n_heads=n_heads, scale=scale),
        grid=(n // tm,),
        in_specs=[row(width), row(LANES), row(LANES), full(q_norm), full(kv_norm), full(wuq), full(wuk), full(wuv)],
        out_specs=[row(n_heads * LANES)] * 3,
        out_shape=[jax.ShapeDtypeStruct((n, n_heads * LANES), BF16)] * 3,
        compiler_params=_params("parallel"),
        name="mla_prep",
    )(p_mla, cos, sin, q_norm, kv_norm, wuq, wuk, wuv)


def _flash_kernel(q_ref, k_ref, v_ref, o_ref, m_s, acc_s, s_buf, p_buf, al_buf):
    i = pl.program_id(2)
    t = FLASH_BLOCK
    m_s[...] = jnp.full(m_s.shape, -jnp.inf, F32)
    acc_s[...] = jnp.zeros(acc_s.shape, F32)
    p_buf[1] = jnp.zeros(p_buf.shape[1:], BF16)
    al_buf[1] = jnp.ones(al_buf.shape[1:], F32)
    causal = lax.broadcasted_iota(jnp.int32, (t, t), 0) >= lax.broadcasted_iota(jnp.int32, (t, t), 1)

    def keys_of(j):
        return pl.ds(pl.multiple_of(j * t, t), t)

    def scores(j, slot, u0):
        for h in range(2):
            cols = slice(h * LANES, (h + 1) * LANES)
            s_buf[slot, h, u0 * t:, :] = _dot_nt(q_ref[u0 * t:, cols], k_ref[keys_of(j), cols])

    def softmax(slot, u0, masked_sub):
        for u in range(u0, FLASH_SUBS):
            rows = slice(u * t, (u + 1) * t)
            for h in range(2):
                s = s_buf[slot, h, rows, :]
                if u == masked_sub:
                    s = jnp.where(causal, s, -jnp.inf)
                halves = [s[:, n * LANES:(n + 1) * LANES] for n in range(t // LANES)]
                m_prev = m_s[h, rows, :]
                m_new = jnp.maximum(m_prev, jnp.max(functools.reduce(jnp.maximum, halves), axis=-1, keepdims=True))
                m_s[h, rows, :] = m_new
                al_buf[slot, h, rows, :] = jnp.exp2(m_prev - m_new)
                p_buf[slot, h, rows, :] = jnp.concatenate(
                    [jnp.exp2((x - m_new).astype(BF16)) for x in halves], axis=1)

    def accumulate(j, slot, u0):
        for h in range(2):
            pv = _dot(p_buf[slot, h, u0 * t:, :], v_ref[keys_of(j), h * LANES:(h + 1) * LANES])
            acc_s[h, u0 * t:, :] = al_buf[slot, h, u0 * t:, :] * acc_s[h, u0 * t:, :] + pv

    n_full = FLASH_SUBS * i
    scores(0, 0, 0)

    def body(g, carry):
        for par in range(2):
            k = 2 * g + par
            scores(k + 1, 1 - par, 0)
            accumulate(jnp.maximum(k - 1, 0), 1 - par, 0)
            softmax(par, 0, None)
        return carry

    lax.fori_loop(0, n_full // 2, body, 0)
    accumulate(jnp.maximum(n_full - 1, 0), 1, 0)
    for d in range(FLASH_SUBS):
        if d + 1 < FLASH_SUBS:
            scores(n_full + d + 1, (d + 1) % 2, d + 1)
        softmax(d % 2, d, d)
        accumulate(n_full + d, d % 2, d)
    head0 = lax.broadcasted_iota(jnp.int32, (FLASH_SUBS * t, LANES), 1) < HEAD
    a0, a1 = acc_s[0], acc_s[1]
    out = jnp.where(head0, a0 / pltpu.roll(a0, HEAD, 1), pltpu.roll(a1, HEAD, 1) / a1)
    o_ref[...] = out.astype(o_ref.dtype)


def _flash(q, k, v, batch, seq, n_heads):
    n = q.shape[0]
    tq = FLASH_SUBS * FLASH_BLOCK
    n_q = seq // tq
    stat = pltpu.VMEM((2, tq, LANES), F32)
    return pl.pallas_call(
        _flash_kernel,
        grid=(batch, n_heads // 2, n_q),
        in_specs=[pl.BlockSpec((tq, 2 * LANES), lambda b, h, i: (b * n_q + i, h)),
                  pl.BlockSpec((seq, 2 * LANES), lambda b, h, i: (b, h)),
                  pl.BlockSpec((seq, 2 * LANES), lambda b, h, i: (b, h))],
        out_specs=pl.BlockSpec((tq, LANES), lambda b, h, i: (b * n_q + i, h)),
        out_shape=jax.ShapeDtypeStruct((n, n_heads * HEAD), BF16),
        scratch_shapes=[stat, stat,
                        pltpu.VMEM((2, 2, tq, FLASH_BLOCK), F32),
                        pltpu.VMEM((2, 2, tq, FLASH_BLOCK), BF16),
                        pltpu.VMEM((2, 2, tq, LANES), F32)],
        compiler_params=_params("parallel", "parallel", "arbitrary"),
        name="mla_flash",
    )(q, k, v)


def _outffn_kernel(ya_ref, yb_ref, yc_ref, x_ref, mod_ref, gpm_ref, gpf_ref, gqf_ref, wo_ref, w1_ref, w2_ref,
                   o_ref, x1_s, hid_s, *, da, db, tf):
    y = (_dot(ya_ref[...], wo_ref[0:da, :])
         + _dot(yb_ref[...], wo_ref[da:da + db, :])
         + _dot(yc_ref[...], wo_ref[da + db:, :]))
    x1 = x_ref[...] + mod_ref[0, 2:3, :] * (_rms(y) * gpm_ref[...])
    x1_s[...] = x1
    h = _rms(x1) * gpf_ref[...]
    h = (h * (1.0 + mod_ref[0, 4:5, :]) + mod_ref[0, 3:4, :]).astype(BF16)
    for f in range(hid_s.shape[1] // tf):
        cols = slice(f * tf, (f + 1) * tf)
        hid = jnp.maximum(_dot(h, w1_ref[:, cols]), 0.0)
        hid_s[:, cols] = (hid * hid).astype(BF16)
    y2 = _dot(hid_s[...], w2_ref[...])
    o_ref[...] = x1_s[...] + mod_ref[0, 5:6, :] * (_rms(y2) * gqf_ref[...])


def _outffn(ya, yb, yc, x, mod, g_post_mix, g_pre_ffn, g_post_ffn, w_out, w_ff1, w_ff2, layer, seq):
    n, d = x.shape
    d_ff = w_ff1.shape[2]
    tm, tf = 512, 512
    per_b = seq // tm
    row = lambda w: pl.BlockSpec((tm, w), lambda i: (i, 0))
    vec = pl.BlockSpec((1, d), lambda i: (0, 0))
    resident = lambda a: pl.BlockSpec((None,) + a.shape[1:], lambda i: (layer, 0, 0), pipeline_mode=pl.Buffered(1))
    return pl.pallas_call(
        functools.partial(_outffn_kernel, da=ya.shape[1], db=yb.shape[1], tf=tf),
        grid=(n // tm,),
        in_specs=[row(ya.shape[1]), row(yb.shape[1]), row(yc.shape[1]), row(d),
                  pl.BlockSpec((1, 6, d), lambda i: (i // per_b, 0, 0)),
                  vec, vec, vec, resident(w_out), resident(w_ff1), resident(w_ff2)],
        out_specs=row(d),
        out_shape=jax.ShapeDtypeStruct((n, d), F32),
        scratch_shapes=[pltpu.VMEM((tm, d), F32), pltpu.VMEM((tm, d_ff), BF16)],
        compiler_params=_params("parallel"),
        name="out_ffn",
    )(ya, yb, yc, x, mod, g_post_mix, g_pre_ffn, g_post_ffn, w_out, w_ff1, w_ff2)


def _rot_cols(w):
    half = w.shape[-1] // 2
    return jnp.concatenate([-w[..., half:], w[..., :half]], axis=-1)


def kernel(x, c, positions, g_pre_mix, g_post_mix, g_pre_ffn, g_post_ffn, w_ada, b_ada, w_in, w_out, rwkv_mu, rwkv_w0, rwkv_w2, rwkv_a0, rwkv_a2, rwkv_g2, rwkv_k_k, rwkv_k_a, rwkv_r_k, rwkv_ln_w, rwkv_ln_b, conv_w, conv_b, conv_ln_w, conv_ln_b, mla_q_norm, mla_w_uq, mla_kv_norm, mla_w_ukv, w_ff1, w_ff2):
    batch, seq, d = x.shape
    n = batch * seq
    n_layers = w_ada.shape[0]
    r_dim = rwkv_w0.shape[1]
    w_lora, a_lora, g_lora = rwkv_w2.shape[1], rwkv_a2.shape[1], rwkv_g2.shape[1]
    assert w_lora + a_lora + g_lora == LANES and r_dim % LANES == 0
    n_r = 3 * r_dim + LANES
    c_dim = conv_w.shape[2]
    n_c = 2 * c_dim
    lora = mla_q_norm.shape[1]
    n_heads = mla_w_ukv.shape[2] // (2 * HEAD)
    rope = mla_w_uq.shape[2] // n_heads - HEAD
    assert HEAD + 2 * rope == LANES
    n_m = 2 * lora + 2 * LANES
    scale = float(HEAD + rope) ** -0.5 * 1.4426950408889634

    xf = x.reshape(n, d)
    c8 = jnp.pad(c, ((0, 8 - batch), (0, 0)))
    mod_all = _ada(c8, w_ada, b_ada.reshape(n_layers, 1, 6 * d))[:, :batch].reshape(n_layers, batch, 6, d)

    inv_freq = 1.0 / (ROPE_THETA ** (jnp.arange(0, rope, 2, dtype=F32) / rope))
    invf = jnp.tile(inv_freq, LANES // inv_freq.shape[0]).reshape(1, LANES)
    cos, sin = _rope_tables(positions.reshape(n, 1), invf)

    head_id = jnp.arange(LANES) // HEAD
    bd = (head_id[:, None] == head_id[None, :]).astype(BF16)
    row1 = lambda a: a.reshape(1, -1)

    w_in_bf, w_out_bf, w_ff1_bf, w_ff2_bf = [w.astype(BF16) for w in (w_in, w_out, w_ff1, w_ff2)]
    for l in range(n_layers):
        k_pe = w_in_bf[l, :, n_r + n_c + 2 * lora:]
        z_half = jnp.zeros((d, HEAD), BF16)
        w_extra = jnp.concatenate([z_half, k_pe, k_pe, z_half, _rot_cols(k_pe), _rot_cols(k_pe)], axis=1)
        wq = mla_w_uq[l].reshape(lora, n_heads, HEAD + rope)
        wuq = jnp.concatenate([wq, _rot_cols(wq[..., HEAD:])], axis=-1).reshape(lora, n_heads * LANES).astype(BF16)
        wkv = mla_w_ukv[l].reshape(lora, n_heads, 2 * HEAD)
        wuk = jnp.concatenate([wkv[..., :HEAD], jnp.zeros_like(wkv[..., :HEAD])], axis=-1)
        wuk = wuk.reshape(lora, n_heads * LANES).astype(BF16)
        wuv = jnp.concatenate([wkv[..., HEAD:], jnp.zeros_like(wkv[..., HEAD:])], axis=-1)
        wuv = wuv.reshape(lora, n_heads * LANES).astype(BF16)
        lora_w = jnp.zeros((LANES, 3 * r_dim), F32)
        lora_w = lora_w.at[0:w_lora, 0:r_dim].set(rwkv_w2[l])
        lora_w = lora_w.at[w_lora:w_lora + a_lora, r_dim:2 * r_dim].set(rwkv_a2[l])
        lora_w = lora_w.at[w_lora + a_lora:, 2 * r_dim:].set(rwkv_g2[l]).astype(BF16)

        mod = mod_all[l]
        p_rwkv, p_conv, p_mla = _inproj(xf, mod, row1(g_pre_mix[l]), w_in_bf, l, w_extra, seq, n_r, n_c, n_m)
        vecs = [row1(rwkv_mu[l]), row1(rwkv_w0[l]), row1(rwkv_a0[l]), row1(rwkv_k_k[l]), row1(rwkv_k_a[l]),
                row1(rwkv_r_k[l]), row1(rwkv_ln_w[l]), row1(rwkv_ln_b[l])]
        y_a = _rwkv(p_rwkv, vecs, lora_w, bd, batch, seq, r_dim)
        y_b = _conv(p_conv, conv_w[l], row1(conv_b[l]), row1(conv_ln_w[l]), row1(conv_ln_b[l]), batch, seq)
        q, k, v = _mlaprep(p_mla, cos, sin, row1(mla_q_norm[l]), row1(mla_kv_norm[l]), wuq, wuk, wuv,
                           lora, n_heads, scale)
        y_c = _flash(q, k, v, batch, seq, n_heads)
        xf = _outffn(y_a, y_b, y_c, xf, mod, row1(g_post_mix[l]), row1(g_pre_ffn[l]), row1(g_post_ffn[l]),
                     w_out_bf, w_ff1_bf, w_ff2_bf, l, seq)
    return xf.reshape(batch, seq, d)
```

```python
import functools

import jax
import jax.numpy as jnp
from jax import lax
from jax.experimental import pallas as pl
from jax.experimental.pallas import tpu as pltpu

F32 = jnp.float32
BF16 = jnp.bfloat16

LANES = 128
HEAD = 64
NORM_EPS = 1e-6
LN_EPS = 1e-5
RWKV_GN_EPS = 64e-5
ROPE_THETA = 10000.0
CONV_WIDTH = 31
CONV_HALO = 32
CONV_SUB = 128
CHUNK = 64
FLASH_BLOCK = 256
FLASH_SUBS = 8
GROUP_CHUNKS = 8
VMEM_LIMIT = 56 * 1024 * 1024


def _dot(a, b, precision=None):
    return jnp.dot(a, b, preferred_element_type=F32, precision=precision)


def _dot_nt(a, b, precision=None):
    return lax.dot_general(a, b, (((1,), (1,)), ((), ())), preferred_element_type=F32, precision=precision)


def _bf(x):
    return x.astype(BF16)


def _split(x):
    hi = x.astype(BF16)
    return hi, (x - hi.astype(F32)).astype(BF16)


def _head_sums(x, ones_bd):
    groups = [_dot(_bf(x[:, g * LANES:(g + 1) * LANES]), ones_bd) for g in range(x.shape[1] // LANES)]
    return jnp.concatenate(groups, axis=1)


def _rms(x):
    return x * lax.rsqrt(jnp.mean(x * x, axis=-1, keepdims=True) + NORM_EPS)


def _sigmoid(x):
    return 1.0 / (1.0 + jnp.exp(-x))


def _params(*semantics):
    return pltpu.CompilerParams(dimension_semantics=semantics, vmem_limit_bytes=VMEM_LIMIT)


def _ada_kernel(c_ref, w_ref, b_ref, o_ref):
    c = c_ref[...]
    cs = c * _sigmoid(c)
    cs_hi, cs_lo = _split(cs)
    w_hi, w_lo = _split(w_ref[0])
    o_ref[0] = _dot(cs_hi, w_hi) + _dot(cs_hi, w_lo) + _dot(cs_lo, w_hi) + b_ref[0]


def _ada(c8, w_ada, b_ada):
    n_layers, d, d6 = w_ada.shape
    tn = 1536
    return pl.pallas_call(
        _ada_kernel,
        grid=(n_layers, d6 // tn),
        in_specs=[pl.BlockSpec((8, d), lambda l, j: (0, 0)),
                  pl.BlockSpec((1, d, tn), lambda l, j: (l, 0, j)),
                  pl.BlockSpec((1, 1, tn), lambda l, j: (l, 0, j))],
        out_specs=pl.BlockSpec((1, 8, tn), lambda l, j: (l, 0, j)),
        out_shape=jax.ShapeDtypeStruct((n_layers, 8, d6), F32),
        compiler_params=_params("parallel", "parallel"),
        name="ada_mod",
    )(c8, w_ada, b_ada)


def _rope_kernel(pos_ref, invf_ref, cos_ref, sin_ref):
    ang = pos_ref[...].astype(F32) * invf_ref[...]
    cos_ref[...] = jnp.cos(ang)
    sin_ref[...] = jnp.sin(ang)


def _rope_tables(pos_col, invf):
    n = pos_col.shape[0]
    tm = 2048
    spec = pl.BlockSpec((tm, LANES), lambda i: (i, 0))
    return pl.pallas_call(
        _rope_kernel,
        grid=(n // tm,),
        in_specs=[pl.BlockSpec((tm, 1), lambda i: (i, 0)), pl.BlockSpec((1, LANES), lambda i: (0, 0))],
        out_specs=[spec, spec],
        out_shape=[jax.ShapeDtypeStruct((n, LANES), F32)] * 2,
        compiler_params=_params("parallel"),
        name="rope_tables",
    )(pos_col, invf)


def _inproj_kernel(x_ref, mod_ref, g_ref, w_ref, wx_ref, pr_ref, pc_ref, pm_ref, *, n_r, n_c, n_lat):
    h = _rms(x_ref[...]) * g_ref[...]
    h = (h * (1.0 + mod_ref[0, 1:2, :]) + mod_ref[0, 0:1, :]).astype(BF16)
    pr_ref[...] = _dot(h, w_ref[:, :n_r]).astype(pr_ref.dtype)
    pc_ref[...] = _dot(h, w_ref[:, n_r:n_r + n_c]).astype(pc_ref.dtype)
    pm_ref[:, :n_lat] = _dot(h, w_ref[:, n_r + n_c:n_r + n_c + n_lat]).astype(pm_ref.dtype)
    pm_ref[:, n_lat:] = _dot(h, wx_ref[...]).astype(pm_ref.dtype)


def _inproj(x, mod, g, w_in_all, layer, w_extra, seq, n_r, n_c, n_m):
    n, d = x.shape
    tm = 512
    per_b = seq // tm
    return pl.pallas_call(
        functools.partial(_inproj_kernel, n_r=n_r, n_c=n_c, n_lat=n_m - w_extra.shape[1]),
        grid=(n // tm,),
        in_specs=[pl.BlockSpec((tm, d), lambda i: (i, 0)),
                  pl.BlockSpec((1, 6, d), lambda i: (i // per_b, 0, 0)),
                  pl.BlockSpec((1, d), lambda i: (0, 0)),
                  pl.BlockSpec((None,) + w_in_all.shape[1:], lambda i: (layer, 0, 0)),
                  pl.BlockSpec(w_extra.shape, lambda i: (0, 0))],
        out_specs=[pl.BlockSpec((tm, n_r), lambda i: (i, 0)),
                   pl.BlockSpec((tm, n_c), lambda i: (i, 0)),
                   pl.BlockSpec((tm, n_m), lambda i: (i, 0))],
        out_shape=[jax.ShapeDtypeStruct((n, n_r), BF16),
                   jax.ShapeDtypeStruct((n, n_c), BF16),
                   jax.ShapeDtypeStruct((n, n_m), BF16)],
        compiler_params=_params("parallel"),
        name="in_proj",
    )(x, mod, g, w_in_all, w_extra)


def _rwkv_kernel(p_ref, mu_ref, w0_ref, a0_ref, kk_ref, ka_ref, rk_ref, lnw_ref, lnb_ref, lora_ref, bd_ref,
                 o_ref, zbuf, st_ref, r_s, k_s, v_s, a_s, b_s, ld_s, y_s, *, tt, dim):
    n_pairs = dim // LANES
    c = CHUNK

    @pl.when(pl.program_id(1) == 0)
    def _():
        zbuf[0:8, :] = jnp.zeros((8, zbuf.shape[1]), F32)
        st_ref[...] = jnp.zeros(st_ref.shape, F32)

    p = p_ref[...].astype(F32)
    zbuf[8:8 + tt, :] = p
    prev = zbuf[7:7 + tt, :]
    zbuf[7:8, :] = zbuf[tt + 7:tt + 8, :]
    z = p + (prev - p) * mu_ref[...]

    r = z[:, 0:dim]
    k = z[:, dim:2 * dim]
    v = z[:, 2 * dim:3 * dim]
    lo = z[:, 3 * dim:3 * dim + LANES]
    lane = lax.broadcasted_iota(jnp.int32, lo.shape, 1)
    act = jnp.where(lane < 32, jnp.tanh(lo), jnp.where(lane < 64, lo, _sigmoid(lo)))
    lora = _dot(_bf(act), lora_ref[...])
    x_w = w0_ref[...] + lora[:, 0:dim]
    w_log = -(jnp.maximum(-x_w, 0.0) + jnp.log(1.0 + jnp.exp(-jnp.abs(x_w)))) - 0.5
    a = _sigmoid(a0_ref[...] + lora[:, dim:2 * dim])
    g = lora[:, 2 * dim:3 * dim]
    bd = bd_ref[...]
    kk = k * kk_ref[...]
    kk = kk * lax.rsqrt(jnp.maximum(_head_sums(kk * kk, bd), 1e-24))
    k = k * (1.0 + (a - 1.0) * ka_ref[...])
    r_s[...] = r
    k_s[...] = k
    v_s[...] = v
    a_s[...] = -kk
    b_s[...] = kk * a
    ld_s[...] = -jnp.exp(w_log)

    row_c = lax.broadcasted_iota(jnp.int32, (c, c), 0)
    col_c = lax.broadcasted_iota(jnp.int32, (c, c), 1)
    tri_incl = (row_c >= col_c).astype(BF16)
    eye_c = (row_c == col_c).astype(F32)
    row_w = lax.broadcasted_iota(jnp.int32, (2 * c, 2 * c), 0)
    col_w = lax.broadcasted_iota(jnp.int32, (2 * c, 2 * c), 1)
    row_in = jnp.where(row_w >= c, row_w - c, row_w)
    col_in = jnp.where(col_w >= c, col_w - c, col_w)
    strict_w = row_in > col_in
    incl_w = row_in >= col_in
    second_w = col_w >= c
    lane_c = lax.broadcasted_iota(jnp.int32, (c, LANES), 1)
    head0 = lane_c < HEAD
    lane_2 = lax.broadcasted_iota(jnp.int32, (c, 2 * LANES), 1)
    head0_2 = jnp.where(lane_2 >= LANES, lane_2 - LANES, lane_2) < HEAD
    row_p = lax.broadcasted_iota(jnp.int32, (LANES, LANES), 0)
    col_p = lax.broadcasted_iota(jnp.int32, (LANES, LANES), 1)
    same_head = (row_p < HEAD) == (col_p < HEAD)
    diag_p = row_p == col_p
    zeros_cl = jnp.zeros((c, LANES), BF16)

    def group_body(gi, carry):
        base = gi * (GROUP_CHUNKS * c)
        items = [(pl.ds(pl.multiple_of(base + q * c, c), c), slice(j * LANES, (j + 1) * LANES), j)
                 for q in range(GROUP_CHUNKS) for j in range(n_pairs)]
        n_it = len(items)
        loaded = [[ref[rows, lanes] for ref in (r_s, k_s, v_s, a_s, b_s, ld_s)] for rows, lanes, _ in items]
        r_c, k_c, v_c, a_c, b_c, ld_c = [list(t) for t in zip(*loaded)]
        cum = []
        for i in range(n_it):
            ld_hi, ld_lo = _split(ld_c[i])
            cum.append(_dot(tri_incl, ld_hi) + _dot(tri_incl, ld_lo))
        cum_end = [x[c - 1:c, :] for x in cum]
        at = [a_c[i] * jnp.exp(cum[i] - ld_c[i]) for i in range(n_it)]
        rt = [r_c[i] * jnp.exp(cum[i]) for i in range(n_it)]
        v_bf = [_bf(x) for x in v_c]
        aa = []
        for i in range(n_it):
            e_neg = jnp.exp(-cum[i])
            lhs = _bf(jnp.concatenate([jnp.where(head0, at[i], 0.0), jnp.where(head0, 0.0, at[i]),
                                       jnp.where(head0, rt[i], 0.0), jnp.where(head0, 0.0, rt[i])], axis=0))
            rhs = _bf(jnp.concatenate([b_c[i] * e_neg, k_c[i] * e_neg], axis=0))
            aa.append(_dot_nt(lhs, rhs))
        t_both = [jnp.where(strict_w, x[0:2 * c, :], 0.0) for x in aa]
        q_both = [_bf(jnp.where(incl_w, x[2 * c:4 * c, :], 0.0)) for x in aa]
        akv = [_dot(_bf(jnp.where(second_w, t_both[i], 0.0)), jnp.concatenate([v_bf[i], v_bf[i]], axis=0))
               for i in range(n_it)]
        n_hd = 2 * n_it
        lower = [t_both[n // 2][(n % 2) * c:(n % 2 + 1) * c, 0:c] for n in range(n_hd)]
        inv = [eye_c + l for l in lower]
        lower_bf = [_bf(l) for l in lower]
        pw = [_bf(_dot(l, l)) for l in lower_bf]
        for _ in range(4):
            both = [_dot(jnp.concatenate([pw[n], _bf(inv[n])], axis=0), pw[n]) for n in range(n_hd)]
            inv = [inv[n] + both[n][c:, :] for n in range(n_hd)]
            pw = [_bf(both[n][0:c, :]) for n in range(n_hd)]
        inv = [inv[n] + _dot(_bf(inv[n]), pw[n]) for n in range(n_hd)]
        x_mat = [_bf(jnp.concatenate([at[i], jnp.where(head0, akv[i][0:c, :], akv[i][c:, :])], axis=1))
                 for i in range(n_it)]
        tx_both = [_dot(_bf(jnp.concatenate([inv[2 * i], inv[2 * i + 1]], axis=0)), x_mat[i]) for i in range(n_it)]
        w_mat = [jnp.concatenate([_bf(jnp.where(head0_2, tx_both[i][0:c, :], tx_both[i][c:, :])),
                                  jnp.concatenate([zeros_cl, v_bf[i]], axis=1)], axis=0) for i in range(n_it)]
        qw_both = [_dot(q_both[i], w_mat[i]) for i in range(n_it)]
        qw = [jnp.where(head0_2, x[0:c, :], x[c:, :]) for x in qw_both]
        mn = []
        for i in range(n_it):
            e_end = jnp.exp(cum_end[i] - cum[i])
            bk_t = _bf(jnp.concatenate([b_c[i] * e_end, k_c[i] * e_end], axis=0).T)
            mn.append(_dot(bk_t, w_mat[i]))
        g_col = [jnp.sum(jnp.where(diag_p, jnp.exp(cum_end[i]), 0.0), axis=1, keepdims=True) for i in range(n_it)]
        states = [st_ref[j] for j in range(n_pairs)]
        for i, (rows, lanes, j) in enumerate(items):
            st = states[j]
            st_bf = _bf(st)
            y_s[rows, lanes] = _dot(_bf(rt[i] + qw[i][:, 0:LANES]), st_bf) + qw[i][:, LANES:]
            m_low = _bf(jnp.where(same_head, mn[i][:, 0:LANES], 0.0))
            states[j] = g_col[i] * st + _dot(m_low, st_bf) + jnp.where(same_head, mn[i][:, LANES:], 0.0)
        for j in range(n_pairs):
            st_ref[j] = states[j]
        return carry

    lax.fori_loop(0, tt // (c * GROUP_CHUNKS), group_body, 0)

    y = y_s[...]
    inv_n = 1.0 / HEAD
    mean = _head_sums(y, bd) * inv_n
    d = y - mean
    var = _head_sums(d * d, bd) * inv_n
    y = d * lax.rsqrt(var + RWKV_GN_EPS) * lnw_ref[...] + lnb_ref[...]
    r = r_s[...]
    bonus = _head_sums(r * k_s[...] * rk_ref[...], bd) * v_s[...]
    o_ref[...] = ((y + bonus) * g).astype(o_ref.dtype)


def _rwkv(p_rwkv, vecs, lora_w, bd, batch, seq, dim):
    n, width = p_rwkv.shape
    tt = CHUNK * GROUP_CHUNKS
    n_t = seq // tt
    vec_spec = lambda a: pl.BlockSpec(a.shape, lambda b, t: (0, 0))
    act = pltpu.VMEM((tt, dim), F32)
    return pl.pallas_call(
        functools.partial(_rwkv_kernel, tt=tt, dim=dim),
        grid=(batch, n_t),
        in_specs=[pl.BlockSpec((tt, width), lambda b, t: (b * n_t + t, 0))]
                 + [vec_spec(a) for a in vecs] + [vec_spec(lora_w), vec_spec(bd)],
        out_specs=pl.BlockSpec((tt, dim), lambda b, t: (b * n_t + t, 0)),
        out_shape=jax.ShapeDtypeStruct((n, dim), BF16),
        scratch_shapes=[pltpu.VMEM((tt + 8, width), F32),
                        pltpu.VMEM((dim // LANES, LANES, LANES), F32),
                        act, act, act, act, act, act, act],
        compiler_params=_params("parallel", "arbitrary"),
        name="rwkv7",
    )(p_rwkv, *vecs, lora_w, bd)


def _conv_kernel(p_ref, w_ref, b_ref, lnw_ref, lnb_ref, shift_ref, o_ref, hbuf, *, tt, dim):
    sub = CONV_SUB
    ext = sub + 8

    @pl.when(pl.program_id(1) == 0)
    def _():
        hbuf[0:CONV_HALO, :] = jnp.zeros((CONV_HALO, dim), F32)

    hbuf[CONV_HALO:CONV_HALO + tt, :] = p_ref[:, 0:dim].astype(F32) * _sigmoid(p_ref[:, dim:2 * dim].astype(F32))
    for s in range(tt // sub):
        base = CONV_HALO + s * sub - 8
        wins = [hbuf[base - 8 * a:base - 8 * a + ext, :] for a in range(CONV_HALO // 8)]
        parts = []
        for r in range(8):
            lags = [8 * a + r for a in range(len(wins)) if 8 * a + r < CONV_WIDTH]
            terms = [wins[m // 8] * w_ref[CONV_WIDTH - 1 - m:CONV_WIDTH - m, :] for m in lags]
            parts.append(functools.reduce(jnp.add, terms))
        stacked = _bf(jnp.concatenate(parts, axis=0))
        acc = _dot(shift_ref[...], stacked) + b_ref[...]
        mu = jnp.mean(acc, axis=-1, keepdims=True)
        d = acc - mu
        var = jnp.mean(d * d, axis=-1, keepdims=True)
        y = d * lax.rsqrt(var + LN_EPS) * lnw_ref[...] + lnb_ref[...]
        o_ref[s * sub:(s + 1) * sub, :] = (y * _sigmoid(y)).astype(o_ref.dtype)
    hbuf[0:CONV_HALO, :] = hbuf[tt:tt + CONV_HALO, :]


def _conv(p_conv, conv_w, conv_b, ln_w, ln_b, batch, seq):
    n, width = p_conv.shape
    dim = width // 2
    tt = 512
    n_t = seq // tt
    ext = CONV_SUB + 8
    u_idx = jnp.arange(CONV_SUB)[:, None]
    col = jnp.arange(8 * ext)[None, :]
    shift = (col % ext == u_idx + 8 - col // ext).astype(BF16)
    vec = lambda a: pl.BlockSpec(a.shape, lambda b, t: (0, 0))
    return pl.pallas_call(
        functools.partial(_conv_kernel, tt=tt, dim=dim),
        grid=(batch, n_t),
        in_specs=[pl.BlockSpec((tt, width), lambda b, t: (b * n_t + t, 0)),
                  vec(conv_w), vec(conv_b), vec(ln_w), vec(ln_b), vec(shift)],
        out_specs=pl.BlockSpec((tt, dim), lambda b, t: (b * n_t + t, 0)),
        out_shape=jax.ShapeDtypeStruct((n, dim), BF16),
        scratch_shapes=[pltpu.VMEM((tt + CONV_HALO, dim), F32)],
        compiler_params=_params("parallel", "arbitrary"),
        name="conformer_conv",
    )(p_conv, conv_w, conv_b, ln_w, ln_b, shift)


def _mlaprep_kernel(p_ref, cos_ref, sin_ref, qn_ref, kvn_ref, wuq_ref, wuk_ref, wuv_ref,
                    q_ref, k_ref, v_ref, *, lora, n_heads, scale):
    cos = cos_ref[...]
    sin = sin_ref[...]
    qn = (_rms(p_ref[:, 0:lora].astype(F32)) * qn_ref[...]).astype(BF16)
    kvn = (_rms(p_ref[:, lora:2 * lora].astype(F32)) * kvn_ref[...]).astype(BF16)
    kpe = (p_ref[:, 2 * lora:2 * lora + LANES].astype(F32) * cos
           + p_ref[:, 2 * lora + LANES:2 * lora + 2 * LANES].astype(F32) * sin)
    lane = lax.broadcasted_iota(jnp.int32, cos.shape, 1)
    q_mul = jnp.where(lane < HEAD, 1.0, jnp.where(lane < HEAD + 32, cos, sin)) * scale
    for h in range(n_heads):
        cols = slice(h * LANES, (h + 1) * LANES)
        q_ref[:, cols] = (_dot(qn, wuq_ref[:, cols]) * q_mul).astype(BF16)
        k_ref[:, cols] = (_dot(kvn, wuk_ref[:, cols]) + kpe).astype(BF16)
        v_ref[:, cols] = jnp.where(lane < HEAD, _dot(kvn, wuv_ref[:, cols]), 1.0).astype(BF16)


def _mlaprep(p_mla, cos, sin, q_norm, kv_norm, wuq, wuk, wuv, lora, n_heads, scale):
    n, width = p_mla.shape
    tm = 512
    row = lambda w: pl.BlockSpec((tm, w), lambda i: (i, 0))
    full = lambda a: pl.BlockSpec(a.shape, lambda i: (0, 0))
    return pl.pallas_call(
        functools.partial(_mlaprep_kernel, lora=lora, n_heads=n_heads, scale=scale),
        grid=(n // tm,),
        in_specs=[row(width), row(LANES), row(LANES), full(q_norm), full(kv_norm), full(wuq), full(wuk), full(wuv)],
        out_specs=[row(n_heads * LANES)] * 3,
        out_shape=[jax.ShapeDtypeStruct((n, n_heads * LANES), BF16)] * 3,
        compiler_params=_params("parallel"),
        name="mla_prep",
    )(p_mla, cos, sin, q_norm, kv_norm, wuq, wuk, wuv)


def _flash_kernel(q_ref, k_ref, v_ref, o_ref, m_s, acc_s, s_buf, p_buf, al_buf):
    i = pl.program_id(2)
    t = FLASH_BLOCK
    m_s[...] = jnp.full(m_s.shape, -jnp.inf, F32)
    acc_s[...] = jnp.zeros(acc_s.shape, F32)
    p_buf[1] = jnp.zeros(p_buf.shape[1:], BF16)
    al_buf[1] = jnp.ones(al_buf.shape[1:], F32)
    causal = lax.broadcasted_iota(jnp.int32, (t, t), 0) <= lax.broadcasted_iota(jnp.int32, (t, t), 1)

    def keys_of(j):
        return pl.ds(pl.multiple_of(j * t, t), t)

    def scores(j, slot, u0):
        for h in range(2):
            cols = slice(h * LANES, (h + 1) * LANES)
            s_buf[slot, h, :, u0 * t:] = _dot_nt(k_ref[keys_of(j), cols], q_ref[u0 * t:, cols])

    def softmax(slot, u0, masked_sub):
        for u in range(u0, FLASH_SUBS):
            qs = slice(u * t, (u + 1) * t)
            for h in range(2):
                s = s_buf[slot, h, :, qs]
                if u == masked_sub:
                    s = jnp.where(causal, s, -jnp.inf)
                m_prev = m_s[h, :, qs]
                m_new = jnp.maximum(m_prev, jnp.max(s, axis=0, keepdims=True))
                m_s[h, :, qs] = m_new
                al_buf[slot, h, :, qs] = jnp.exp2(m_prev - m_new)
                p_buf[slot, h, :, qs] = jnp.exp2((s - m_new).astype(BF16))

    def accumulate(j, slot, u0):
        for h in range(2):
            v_blk = v_ref[keys_of(j), h * LANES:(h + 1) * LANES]
            pv = lax.dot_general(v_blk, p_buf[slot, h, :, u0 * t:], (((0,), (0,)), ((), ())),
                                 preferred_element_type=F32)
            acc_s[h, :, u0 * t:] = al_buf[slot, h, :, u0 * t:] * acc_s[h, :, u0 * t:] + pv

    n_full = FLASH_SUBS * i
    scores(0, 0, 0)

    def body(g, carry):
        for par in range(2):
            k = 2 * g + par
            scores(k + 1, 1 - par, 0)
            accumulate(jnp.maximum(k - 1, 0), 1 - par, 0)
            softmax(par, 0, None)
        return carry

    lax.fori_loop(0, n_full // 2, body, 0)
    accumulate(jnp.maximum(n_full - 1, 0), 1, 0)
    for d in range(FLASH_SUBS):
        if d + 1 < FLASH_SUBS:
            scores(n_full + d + 1, (d + 1) % 2, d + 1)
        softmax(d % 2, d, d)
        accumulate(n_full + d, d % 2, d)
    out_t = jnp.concatenate([acc_s[h, 0:HEAD, :] / acc_s[h, HEAD:HEAD + 1, :] for h in range(2)], axis=0)
    o_ref[...] = out_t.T.astype(o_ref.dtype)


def _flash(q, k, v, batch, seq, n_heads):
    n = q.shape[0]
    tq = FLASH_SUBS * FLASH_BLOCK
    n_q = seq // tq
    return pl.pallas_call(
        _flash_kernel,
        grid=(batch, n_heads // 2, n_q),
        in_specs=[pl.BlockSpec((tq, 2 * LANES), lambda b, h, i: (b * n_q + i, h)),
                  pl.BlockSpec((seq, 2 * LANES), lambda b, h, i: (b, h)),
                  pl.BlockSpec((seq, 2 * LANES), lambda b, h, i: (b, h))],
        out_specs=pl.BlockSpec((tq, LANES), lambda b, h, i: (b * n_q + i, h)),
        out_shape=jax.ShapeDtypeStruct((n, n_heads * HEAD), BF16),
        scratch_shapes=[pltpu.VMEM((2, 1, tq), F32),
                        pltpu.VMEM((2, LANES, tq), F32),
                        pltpu.VMEM((2, 2, FLASH_BLOCK, tq), F32),
                        pltpu.VMEM((2, 2, FLASH_BLOCK, tq), BF16),
                        pltpu.VMEM((2, 2, 1, tq), F32)],
        compiler_params=_params("parallel", "parallel", "arbitrary"),
        name="mla_flash",
    )(q, k, v)


def _outffn_kernel(ya_ref, yb_ref, yc_ref, x_ref, mod_ref, gpm_ref, gpf_ref, gqf_ref, wo_ref, w1_ref, w2_ref,
                   o_ref, x1_s, hid_s, *, da, db, tf):
    y = (_dot(ya_ref[...], wo_ref[0:da, :])
         + _dot(yb_ref[...], wo_ref[da:da + db, :])
         + _dot(yc_ref[...], wo_ref[da + db:, :]))
    x1 = x_ref[...] + mod_ref[0, 2:3, :] * (_rms(y) * gpm_ref[...])
    x1_s[...] = x1
    h = _rms(x1) * gpf_ref[...]
    h = (h * (1.0 + mod_ref[0, 4:5, :]) + mod_ref[0, 3:4, :]).astype(BF16)
    for f in range(hid_s.shape[1] // tf):
        cols = slice(f * tf, (f + 1) * tf)
        hid = jnp.maximum(_dot(h, w1_ref[:, cols]), 0.0)
        hid_s[:, cols] = (hid * hid).astype(BF16)
    y2 = _dot(hid_s[...], w2_ref[...])
    o_ref[...] = x1_s[...] + mod_ref[0, 5:6, :] * (_rms(y2) * gqf_ref[...])


def _outffn(ya, yb, yc, x, mod, g_post_mix, g_pre_ffn, g_post_ffn, w_out, w_ff1, w_ff2, layer, seq):
    n, d = x.shape
    d_ff = w_ff1.shape[2]
    tm, tf = 512, 512
    per_b = seq // tm
    row = lambda w: pl.BlockSpec((tm, w), lambda i: (i, 0))
    vec = pl.BlockSpec((1, d), lambda i: (0, 0))
    resident = lambda a: pl.BlockSpec((None,) + a.shape[1:], lambda i: (layer, 0, 0), pipeline_mode=pl.Buffered(1))
    return pl.pallas_call(
        functools.partial(_outffn_kernel, da=ya.shape[1], db=yb.shape[1], tf=tf),
        grid=(n // tm,),
        in_specs=[row(ya.shape[1]), row(yb.shape[1]), row(yc.shape[1]), row(d),
                  pl.BlockSpec((1, 6, d), lambda i: (i // per_b, 0, 0)),
                  vec, vec, vec, resident(w_out), resident(w_ff1), resident(w_ff2)],
        out_specs=row(d),
        out_shape=jax.ShapeDtypeStruct((n, d), F32),
        scratch_shapes=[pltpu.VMEM((tm, d), F32), pltpu.VMEM((tm, d_ff), BF16)],
        compiler_params=_params("parallel"),
        name="out_ffn",
    )(ya, yb, yc, x, mod, g_post_mix, g_pre_ffn, g_post_ffn, w_out, w_ff1, w_ff2)


def _rot_cols(w):
    half = w.shape[-1] // 2
    return jnp.concatenate([-w[..., half:], w[..., :half]], axis=-1)


def kernel(x, c, positions, g_pre_mix, g_post_mix, g_pre_ffn, g_post_ffn, w_ada, b_ada, w_in, w_out, rwkv_mu, rwkv_w0, rwkv_w2, rwkv_a0, rwkv_a2, rwkv_g2, rwkv_k_k, rwkv_k_a, rwkv_r_k, rwkv_ln_w, rwkv_ln_b, conv_w, conv_b, conv_ln_w, conv_ln_b, mla_q_norm, mla_w_uq, mla_kv_norm, mla_w_ukv, w_ff1, w_ff2):
    batch, seq, d = x.shape
    n = batch * seq
    n_layers = w_ada.shape[0]
    r_dim = rwkv_w0.shape[1]
    w_lora, a_lora, g_lora = rwkv_w2.shape[1], rwkv_a2.shape[1], rwkv_g2.shape[1]
    assert w_lora + a_lora + g_lora == LANES and r_dim % LANES == 0
    n_r = 3 * r_dim + LANES
    c_dim = conv_w.shape[2]
    n_c = 2 * c_dim
    lora = mla_q_norm.shape[1]
    n_heads = mla_w_ukv.shape[2] // (2 * HEAD)
    rope = mla_w_uq.shape[2] // n_heads - HEAD
    assert HEAD + 2 * rope == LANES
    n_m = 2 * lora + 2 * LANES
    scale = float(HEAD + rope) ** -0.5 * 1.4426950408889634

    xf = x.reshape(n, d)
    c8 = jnp.pad(c, ((0, 8 - batch), (0, 0)))
    mod_all = _ada(c8, w_ada, b_ada.reshape(n_layers, 1, 6 * d))[:, :batch].reshape(n_layers, batch, 6, d)

    inv_freq = 1.0 / (ROPE_THETA ** (jnp.arange(0, rope, 2, dtype=F32) / rope))
    invf = jnp.tile(inv_freq, LANES // inv_freq.shape[0]).reshape(1, LANES)
    cos, sin = _rope_tables(positions.reshape(n, 1), invf)

    head_id = jnp.arange(LANES) // HEAD
    bd = (head_id[:, None] == head_id[None, :]).astype(BF16)
    row1 = lambda a: a.reshape(1, -1)

    w_in_bf, w_out_bf, w_ff1_bf, w_ff2_bf = [w.astype(BF16) for w in (w_in, w_out, w_ff1, w_ff2)]
    for l in range(n_layers):
        k_pe = w_in_bf[l, :, n_r + n_c + 2 * lora:]
        z_half = jnp.zeros((d, HEAD), BF16)
        w_extra = jnp.concatenate([z_half, k_pe, k_pe, z_half, _rot_cols(k_pe), _rot_cols(k_pe)], axis=1)
        wq = mla_w_uq[l].reshape(lora, n_heads, HEAD + rope)
        wuq = jnp.concatenate([wq, _rot_cols(wq[..., HEAD:])], axis=-1).reshape(lora, n_heads * LANES).astype(BF16)
        wkv = mla_w_ukv[l].reshape(lora, n_heads, 2 * HEAD)
        wuk = jnp.concatenate([wkv[..., :HEAD], jnp.zeros_like(wkv[..., :HEAD])], axis=-1)
        wuk = wuk.reshape(lora, n_heads * LANES).astype(BF16)
        wuv = jnp.concatenate([wkv[..., HEAD:], jnp.zeros_like(wkv[..., HEAD:])], axis=-1)
        wuv = wuv.reshape(lora, n_heads * LANES).astype(BF16)
        lora_w = jnp.zeros((LANES, 3 * r_dim), F32)
        lora_w = lora_w.at[0:w_lora, 0:r_dim].set(rwkv_w2[l])
        lora_w = lora_w.at[w_lora:w_lora + a_lora, r_dim:2 * r_dim].set(rwkv_a2[l])
        lora_w = lora_w.at[w_lora + a_lora:, 2 * r_dim:].set(rwkv_g2[l]).astype(BF16)

        mod = mod_all[l]
        p_rwkv, p_conv, p_mla = _inproj(xf, mod, row1(g_pre_mix[l]), w_in_bf, l, w_extra, seq, n_r, n_c, n_m)
        vecs = [row1(rwkv_mu[l]), row1(rwkv_w0[l]), row1(rwkv_a0[l]), row1(rwkv_k_k[l]), row1(rwkv_k_a[l]),
                row1(rwkv_r_k[l]), row1(rwkv_ln_w[l]), row1(rwkv_ln_b[l])]
        y_a = _rwkv(p_rwkv, vecs, lora_w, bd, batch, seq, r_dim)
        y_b = _conv(p_conv, conv_w[l], row1(conv_b[l]), row1(conv_ln_w[l]), row1(conv_ln_b[l]), batch, seq)
        q, k, v = _mlaprep(p_mla, cos, sin, row1(mla_q_norm[l]), row1(mla_kv_norm[l]), wuq, wuk, wuv,
                           lora, n_heads, scale)
        y_c = _flash(q, k, v, batch, seq, n_heads)
        xf = _outffn(y_a, y_b, y_c, xf, mod, row1(g_post_mix[l]), row1(g_pre_ffn[l]), row1(g_post_ffn[l]),
                     w_out_bf, w_ff1_bf, w_ff2_bf, l, seq)
    return xf.reshape(batch, seq, d)
```

```python
import functools

import jax
import jax.numpy as jnp
from jax import lax
from jax.experimental import pallas as pl
from jax.experimental.pallas import tpu as pltpu

F32 = jnp.float32
BF16 = jnp.bfloat16

LANES = 128
HEAD = 64
NORM_EPS = 1e-6
LN_EPS = 1e-5
RWKV_GN_EPS = 64e-5
ROPE_THETA = 10000.0
CONV_WIDTH = 31
CONV_HALO = 32
CONV_SUB = 128
CHUNK = 64
FLASH_BLOCK = 256
FLASH_SUBS = 8
GROUP_CHUNKS = 8
VMEM_LIMIT = 56 * 1024 * 1024


def _dot(a, b, precision=None):
    return jnp.dot(a, b, preferred_element_type=F32, precision=precision)


def _dot_nt(a, b, precision=None):
    return lax.dot_general(a, b, (((1,), (1,)), ((), ())), preferred_element_type=F32, precision=precision)


def _bf(x):
    return x.astype(BF16)


def _split(x):
    hi = x.astype(BF16)
    return hi, (x - hi.astype(F32)).astype(BF16)


def _head_sums(x, ones_bd):
    groups = [_dot(_bf(x[:, g * LANES:(g + 1) * LANES]), ones_bd) for g in range(x.shape[1] // LANES)]
    return jnp.concatenate(groups, axis=1)


def _rms(x):
    return x * lax.rsqrt(jnp.mean(x * x, axis=-1, keepdims=True) + NORM_EPS)


def _sigmoid(x):
    return 1.0 / (1.0 + jnp.exp(-x))


def _params(*semantics):
    return pltpu.CompilerParams(dimension_semantics=semantics, vmem_limit_bytes=VMEM_LIMIT)


def _ada_kernel(c_ref, w_ref, b_ref, o_ref):
    c = c_ref[...]
    cs = c * _sigmoid(c)
    cs_hi, cs_lo = _split(cs)
    w_hi, w_lo = _split(w_ref[0])
    o_ref[0] = _dot(cs_hi, w_hi) + _dot(cs_hi, w_lo) + _dot(cs_lo, w_hi) + b_ref[0]


def _ada(c8, w_ada, b_ada):
    n_layers, d, d6 = w_ada.shape
    tn = 1536
    return pl.pallas_call(
        _ada_kernel,
        grid=(n_layers, d6 // tn),
        in_specs=[pl.BlockSpec((8, d), lambda l, j: (0, 0)),
                  pl.BlockSpec((1, d, tn), lambda l, j: (l, 0, j)),
                  pl.BlockSpec((1, 1, tn), lambda l, j: (l, 0, j))],
        out_specs=pl.BlockSpec((1, 8, tn), lambda l, j: (l, 0, j)),
        out_shape=jax.ShapeDtypeStruct((n_layers, 8, d6), F32),
        compiler_params=_params("parallel", "parallel"),
        name="ada_mod",
    )(c8, w_ada, b_ada)


def _rope_kernel(pos_ref, invf_ref, cos_ref, sin_ref):
    ang = pos_ref[...].astype(F32) * invf_ref[...]
    cos_ref[...] = jnp.cos(ang)
    sin_ref[...] = jnp.sin(ang)


def _rope_tables(pos_col, invf):
    n = pos_col.shape[0]
    tm = 2048
    spec = pl.BlockSpec((tm, LANES), lambda i: (i, 0))
    return pl.pallas_call(
        _rope_kernel,
        grid=(n // tm,),
        in_specs=[pl.BlockSpec((tm, 1), lambda i: (i, 0)), pl.BlockSpec((1, LANES), lambda i: (0, 0))],
        out_specs=[spec, spec],
        out_shape=[jax.ShapeDtypeStruct((n, LANES), F32)] * 2,
        compiler_params=_params("parallel"),
        name="rope_tables",
    )(pos_col, invf)


def _inproj_kernel(x_ref, mod_ref, g_ref, w_ref, wx_ref, pr_ref, pc_ref, pm_ref, *, n_r, n_c, n_lat):
    h = _rms(x_ref[...]) * g_ref[...]
    h = (h * (1.0 + mod_ref[0, 1:2, :]) + mod_ref[0, 0:1, :]).astype(BF16)
    pr_ref[...] = _dot(h, w_ref[:, :n_r]).astype(pr_ref.dtype)
    pc_ref[...] = _dot(h, w_ref[:, n_r:n_r + n_c]).astype(pc_ref.dtype)
    pm_ref[:, :n_lat] = _dot(h, w_ref[:, n_r + n_c:n_r + n_c + n_lat]).astype(pm_ref.dtype)
    pm_ref[:, n_lat:] = _dot(h, wx_ref[...]).astype(pm_ref.dtype)


def _inproj(x, mod, g, w_in_all, layer, w_extra, seq, n_r, n_c, n_m):
    n, d = x.shape
    tm = 512
    per_b = seq // tm
    return pl.pallas_call(
        functools.partial(_inproj_kernel, n_r=n_r, n_c=n_c, n_lat=n_m - w_extra.shape[1]),
        grid=(n // tm,),
        in_specs=[pl.BlockSpec((tm, d), lambda i: (i, 0)),
                  pl.BlockSpec((1, 6, d), lambda i: (i // per_b, 0, 0)),
                  pl.BlockSpec((1, d), lambda i: (0, 0)),
                  pl.BlockSpec((None,) + w_in_all.shape[1:], lambda i: (layer, 0, 0)),
                  pl.BlockSpec(w_extra.shape, lambda i: (0, 0))],
        out_specs=[pl.BlockSpec((tm, n_r), lambda i: (i, 0)),
                   pl.BlockSpec((tm, n_c), lambda i: (i, 0)),
                   pl.BlockSpec((tm, n_m), lambda i: (i, 0))],
        out_shape=[jax.ShapeDtypeStruct((n, n_r), BF16),
                   jax.ShapeDtypeStruct((n, n_c), BF16),
                   jax.ShapeDtypeStruct((n, n_m), BF16)],
        compiler_params=_params("parallel"),
        name="in_proj",
    )(x, mod, g, w_in_all, w_extra)


def _rwkv_kernel(p_ref, mu_ref, w0_ref, a0_ref, kk_ref, ka_ref, rk_ref, lnw_ref, lnb_ref, lora_ref, bd_ref,
                 o_ref, zbuf, st_ref, r_s, k_s, v_s, a_s, b_s, ld_s, y_s, *, tt, dim):
    n_pairs = dim // LANES
    c = CHUNK

    @pl.when(pl.program_id(1) == 0)
    def _():
        zbuf[0:8, :] = jnp.zeros((8, zbuf.shape[1]), F32)
        st_ref[...] = jnp.zeros(st_ref.shape, F32)

    p = p_ref[...].astype(F32)
    zbuf[8:8 + tt, :] = p
    prev = zbuf[7:7 + tt, :]
    zbuf[7:8, :] = zbuf[tt + 7:tt + 8, :]
    z = p + (prev - p) * mu_ref[...]

    r = z[:, 0:dim]
    k = z[:, dim:2 * dim]
    v = z[:, 2 * dim:3 * dim]
    lo = z[:, 3 * dim:3 * dim + LANES]
    lane = lax.broadcasted_iota(jnp.int32, lo.shape, 1)
    act = jnp.where(lane < 32, jnp.tanh(lo), jnp.where(lane < 64, lo, _sigmoid(lo)))
    lora = _dot(_bf(act), lora_ref[...])
    x_w = w0_ref[...] + lora[:, 0:dim]
    w_log = -(jnp.maximum(-x_w, 0.0) + jnp.log(1.0 + jnp.exp(-jnp.abs(x_w)))) - 0.5
    a = _sigmoid(a0_ref[...] + lora[:, dim:2 * dim])
    g = lora[:, 2 * dim:3 * dim]
    bd = bd_ref[...]
    kk = k * kk_ref[...]
    kk = kk * lax.rsqrt(jnp.maximum(_head_sums(kk * kk, bd), 1e-24))
    k = k * (1.0 + (a - 1.0) * ka_ref[...])
    r_s[...] = r
    k_s[...] = k
    v_s[...] = v
    a_s[...] = -kk
    b_s[...] = kk * a
    ld_s[...] = -jnp.exp(w_log)

    row_c = lax.broadcasted_iota(jnp.int32, (c, c), 0)
    col_c = lax.broadcasted_iota(jnp.int32, (c, c), 1)
    tri_incl = (row_c >= col_c).astype(BF16)
    head0 = lax.broadcasted_iota(jnp.int32, (c, LANES), 1) < HEAD
    lane_2 = lax.broadcasted_iota(jnp.int32, (c, 2 * LANES), 1)
    head0_2 = jnp.where(lane_2 >= LANES, lane_2 - LANES, lane_2) < HEAD
    row_4 = lax.broadcasted_iota(jnp.int32, (c, 4 * c), 0)
    col_4 = jnp.bitwise_and(lax.broadcasted_iota(jnp.int32, (c, 4 * c), 1), c - 1)
    strict4 = row_4 > col_4
    incl4 = row_4 >= col_4
    eye4 = (row_4 == col_4).astype(F32)
    bd4 = jnp.bitwise_xor(lax.broadcasted_iota(jnp.int32, (4 * c, 4 * c), 0),
                          lax.broadcasted_iota(jnp.int32, (4 * c, 4 * c), 1)) < c
    row_p = lax.broadcasted_iota(jnp.int32, (LANES, LANES), 0)
    col_p = lax.broadcasted_iota(jnp.int32, (LANES, LANES), 1)
    same_head = (row_p < HEAD) == (col_p < HEAD)
    diag_p = row_p == col_p
    zeros_cl = jnp.zeros((c, LANES), BF16)

    def group_body(gi, carry):
        base = gi * (GROUP_CHUNKS * c)
        items = [(pl.ds(pl.multiple_of(base + q * c, c), c), slice(j * LANES, (j + 1) * LANES), j)
                 for q in range(GROUP_CHUNKS) for j in range(n_pairs)]
        n_it = len(items)
        loaded = [[ref[rows, lanes] for ref in (r_s, k_s, v_s, a_s, b_s, ld_s)] for rows, lanes, _ in items]
        r_c, k_c, v_c, a_c, b_c, ld_c = [list(t) for t in zip(*loaded)]
        cum = []
        for i in range(n_it):
            ld_hi, ld_lo = _split(ld_c[i])
            cum.append(_dot(tri_incl, ld_hi) + _dot(tri_incl, ld_lo))
        cum_end = [x[c - 1:c, :] for x in cum]
        at = [a_c[i] * jnp.exp(cum[i] - ld_c[i]) for i in range(n_it)]
        rt = [r_c[i] * jnp.exp(cum[i]) for i in range(n_it)]
        v_bf = [_bf(x) for x in v_c]
        def by_head(x, mask):
            return jnp.concatenate([jnp.where(mask, x, 0), jnp.where(mask, 0, x)], axis=0)

        aa = []
        for i in range(n_it):
            e_neg = jnp.exp(-cum[i])
            rhs = jnp.concatenate([by_head(_bf(b_c[i] * e_neg), head0), by_head(_bf(k_c[i] * e_neg), head0)], axis=0)
            aa.append(_dot_nt(_bf(jnp.concatenate([at[i], rt[i]], axis=0)), rhs))
        t_cat = [jnp.where(strict4, x[0:c, :], 0.0) for x in aa]
        q_cat = [_bf(jnp.where(incl4, x[c:2 * c, :], 0.0)) for x in aa]
        akv = [_dot(_bf(t_cat[i][:, LANES:]), by_head(v_bf[i], head0)) for i in range(n_it)]
        n_grp = n_it // 2

        def bdiag(x):
            return jnp.where(bd4, jnp.concatenate([x, x, x, x], axis=0), 0)

        p4 = [jnp.concatenate([t_cat[2 * g][:, 0:LANES], t_cat[2 * g + 1][:, 0:LANES]], axis=1) for g in range(n_grp)]
        inv4 = [eye4 + x for x in p4]
        p4_bf = [_bf(x) for x in p4]
        pw = [_bf(_dot(x, bdiag(x))) for x in p4_bf]
        for _ in range(4):
            both = [_dot(jnp.concatenate([pw[g], _bf(inv4[g])], axis=0), bdiag(pw[g])) for g in range(n_grp)]
            inv4 = [inv4[g] + both[g][c:, :] for g in range(n_grp)]
            pw = [_bf(both[g][0:c, :]) for g in range(n_grp)]
        inv4 = [_bf(inv4[g] + _dot(_bf(inv4[g]), bdiag(pw[g]))) for g in range(n_grp)]
        inv_cat = [inv4[i // 2][:, (i % 2) * LANES:(i % 2 + 1) * LANES] for i in range(n_it)]
        x_mat = [_bf(jnp.concatenate([at[i], akv[i]], axis=1)) for i in range(n_it)]
        tx = [_dot(inv_cat[i], by_head(x_mat[i], head0_2)) for i in range(n_it)]
        tx_bf = [_bf(x) for x in tx]
        v_pad = [jnp.concatenate([zeros_cl, v_bf[i]], axis=1) for i in range(n_it)]
        w_mat = [jnp.concatenate([tx_bf[i], v_pad[i]], axis=0) for i in range(n_it)]
        qw = [_dot(q_cat[i], jnp.concatenate([by_head(tx_bf[i], head0_2), by_head(v_pad[i], head0_2)], axis=0))
              for i in range(n_it)]
        mn = []
        for i in range(n_it):
            e_end = jnp.exp(cum_end[i] - cum[i])
            bk_t = _bf(jnp.concatenate([b_c[i] * e_end, k_c[i] * e_end], axis=0).T)
            mn.append(_dot(bk_t, w_mat[i]))
        g_col = [jnp.sum(jnp.where(diag_p, jnp.exp(cum_end[i]), 0.0), axis=1, keepdims=True) for i in range(n_it)]
        states = [st_ref[j] for j in range(n_pairs)]
        for i, (rows, lanes, j) in enumerate(items):
            st = states[j]
            st_bf = _bf(st)
            y_s[rows, lanes] = _dot(_bf(rt[i] + qw[i][:, 0:LANES]), st_bf) + qw[i][:, LANES:]
            m_low = _bf(jnp.where(same_head, mn[i][:, 0:LANES], 0.0))
            states[j] = g_col[i] * st + _dot(m_low, st_bf) + jnp.where(same_head, mn[i][:, LANES:], 0.0)
        for j in range(n_pairs):
            st_ref[j] = states[j]
        return carry

    lax.fori_loop(0, tt // (c * GROUP_CHUNKS), group_body, 0)

    y = y_s[...]
    inv_n = 1.0 / HEAD
    mean = _head_sums(y, bd) * inv_n
    d = y - mean
    var = _head_sums(d * d, bd) * inv_n
    y = d * lax.rsqrt(var + RWKV_GN_EPS) * lnw_ref[...] + lnb_ref[...]
    r = r_s[...]
    bonus = _head_sums(r * k_s[...] * rk_ref[...], bd) * v_s[...]
    o_ref[...] = ((y + bonus) * g).astype(o_ref.dtype)


def _rwkv(p_rwkv, vecs, lora_w, bd, batch, seq, dim):
    n, width = p_rwkv.shape
    tt = CHUNK * GROUP_CHUNKS
    n_t = seq // tt
    vec_spec = lambda a: pl.BlockSpec(a.shape, lambda b, t: (0, 0))
    act = pltpu.VMEM((tt, dim), F32)
    return pl.pallas_call(
        functools.partial(_rwkv_kernel, tt=tt, dim=dim),
        grid=(batch, n_t),
        in_specs=[pl.BlockSpec((tt, width), lambda b, t: (b * n_t + t, 0))]
                 + [vec_spec(a) for a in vecs] + [vec_spec(lora_w), vec_spec(bd)],
        out_specs=pl.BlockSpec((tt, dim), lambda b, t: (b * n_t + t, 0)),
        out_shape=jax.ShapeDtypeStruct((n, dim), BF16),
        scratch_shapes=[pltpu.VMEM((tt + 8, width), F32),
                        pltpu.VMEM((dim // LANES, LANES, LANES), F32),
                        act, act, act, act, act, act, act],
        compiler_params=_params("parallel", "arbitrary"),
        name="rwkv7",
    )(p_rwkv, *vecs, lora_w, bd)


def _conv_kernel(p_ref, w_ref, b_ref, lnw_ref, lnb_ref, shift_ref, o_ref, hbuf, *, tt, dim):
    sub = CONV_SUB
    ext = sub + 8

    @pl.when(pl.program_id(1) == 0)
    def _():
        hbuf[0:CONV_HALO, :] = jnp.zeros((CONV_HALO, dim), F32)

    hbuf[CONV_HALO:CONV_HALO + tt, :] = p_ref[:, 0:dim].astype(F32) * _sigmoid(p_ref[:, dim:2 * dim].astype(F32))
    for s in range(tt // sub):
        base = CONV_HALO + s * sub - 8
        wins = [hbuf[base - 8 * a:base - 8 * a + ext, :] for a in range(CONV_HALO // 8)]
        parts = []
        for r in range(8):
            lags = [8 * a + r for a in range(len(wins)) if 8 * a + r < CONV_WIDTH]
            terms = [wins[m // 8] * w_ref[CONV_WIDTH - 1 - m:CONV_WIDTH - m, :] for m in lags]
            parts.append(functools.reduce(jnp.add, terms))
        stacked = _bf(jnp.concatenate(parts, axis=0))
        acc = _dot(shift_ref[...], stacked) + b_ref[...]
        mu = jnp.mean(acc, axis=-1, keepdims=True)
        d = acc - mu
        var = jnp.mean(d * d, axis=-1, keepdims=True)
        y = d * lax.rsqrt(var + LN_EPS) * lnw_ref[...] + lnb_ref[...]
        o_ref[s * sub:(s + 1) * sub, :] = (y * _sigmoid(y)).astype(o_ref.dtype)
    hbuf[0:CONV_HALO, :] = hbuf[tt:tt + CONV_HALO, :]


def _conv(p_conv, conv_w, conv_b, ln_w, ln_b, batch, seq):
    n, width = p_conv.shape
    dim = width // 2
    tt = 512
    n_t = seq // tt
    ext = CONV_SUB + 8
    u_idx = jnp.arange(CONV_SUB)[:, None]
    col = jnp.arange(8 * ext)[None, :]
    shift = (col % ext == u_idx + 8 - col // ext).astype(BF16)
    vec = lambda a: pl.BlockSpec(a.shape, lambda b, t: (0, 0))
    return pl.pallas_call(
        functools.partial(_conv_kernel, tt=tt, dim=dim),
        grid=(batch, n_t),
        in_specs=[pl.BlockSpec((tt, width), lambda b, t: (b * n_t + t, 0)),
                  vec(conv_w), vec(conv_b), vec(ln_w), vec(ln_b), vec(shift)],
        out_specs=pl.BlockSpec((tt, dim), lambda b, t: (b * n_t + t, 0)),
        out_shape=jax.ShapeDtypeStruct((n, dim), BF16),
        scratch_shapes=[pltpu.VMEM((tt + CONV_HALO, dim), F32)],
        compiler_params=_params("parallel", "arbitrary"),
        name="conformer_conv",
    )(p_conv, conv_w, conv_b, ln_w, ln_b, shift)


def _mlaprep_kernel(p_ref, cos_ref, sin_ref, qn_ref, kvn_ref, wuq_ref, wuk_ref, wuv_ref,
                    q_ref, k_ref, v_ref, *, lora, n_heads, scale):
    cos = cos_ref[...]
    sin = sin_ref[...]
    qn = (_rms(p_ref[:, 0:lora].astype(F32)) * qn_ref[...]).astype(BF16)
    kvn = (_rms(p_ref[:, lora:2 * lora].astype(F32)) * kvn_ref[...]).astype(BF16)
    kpe = (p_ref[:, 2 * lora:2 * lora + LANES].astype(F32) * cos
           + p_ref[:, 2 * lora + LANES:2 * lora + 2 * LANES].astype(F32) * sin)
    lane = lax.broadcasted_iota(jnp.int32, cos.shape, 1)
    q_mul = jnp.where(lane < HEAD, 1.0, jnp.where(lane < HEAD + 32, cos, sin)) * scale
    for h in range(n_heads):
        cols = slice(h * LANES, (h + 1) * LANES)
        q_ref[:, cols] = (_dot(qn, wuq_ref[:, cols]) * q_mul).astype(BF16)
        k_ref[:, cols] = (_dot(kvn, wuk_ref[:, cols]) + kpe).astype(BF16)
        v_ref[:, cols] = jnp.where(lane < HEAD, _dot(kvn, wuv_ref[:, cols]), 1.0).astype(BF16)


def _mlaprep(p_mla, cos, sin, q_norm, kv_norm, wuq, wuk, wuv, lora, n_heads, scale):
    n, width = p_mla.shape
    tm = 512
    row = lambda w: pl.BlockSpec((tm, w), lambda i: (i, 0))
    full = lambda a: pl.BlockSpec(a.shape, lambda i: (0, 0))
    return pl.pallas_call(
        functools.partial(_mlaprep_kernel, lora=lora, n_heads=n_heads, scale=scale),
        grid=(n // tm,),
        in_specs=[row(width), row(LANES), row(LANES), full(q_norm), full(kv_norm), full(wuq), full(wuk), full(wuv)],
        out_specs=[row(n_heads * LANES)] * 3,
        out_shape=[jax.ShapeDtypeStruct((n, n_heads * LANES), BF16)] * 3,
        compiler_params=_params("parallel"),
        name="mla_prep",
    )(p_mla, cos, sin, q_norm, kv_norm, wuq, wuk, wuv)


def _flash_kernel(q_ref, k_ref, v_ref, o_ref, m_s, acc_s, s_buf, p_buf, al_buf):
    i = pl.program_id(2)
    t = FLASH_BLOCK
    m_s[...] = jnp.full(m_s.shape, -jnp.inf, F32)
    acc_s[...] = jnp.zeros(acc_s.shape, F32)
    p_buf[1] = jnp.zeros(p_buf.shape[1:], BF16)
    al_buf[1] = jnp.ones(al_buf.shape[1:], F32)
    causal = lax.broadcasted_iota(jnp.int32, (t, t), 0) <= lax.broadcasted_iota(jnp.int32, (t, t), 1)

    def keys_of(j):
        return pl.ds(pl.multiple_of(j * t, t), t)

    def scores(j, slot, u0):
        for h in range(2):
            cols = slice(h * LANES, (h + 1) * LANES)
            s_buf[slot, h, :, u0 * t:] = _dot_nt(k_ref[keys_of(j), cols], q_ref[u0 * t:, cols])

    def softmax(slot, u0, masked_sub):
        for u in range(u0, FLASH_SUBS):
            qs = slice(u * t, (u + 1) * t)
            for h in range(2):
                s = s_buf[slot, h, :, qs]
                if u == masked_sub:
                    s = jnp.where(causal, s, -jnp.inf)
                m_prev = m_s[h, :, qs]
                m_new = jnp.maximum(m_prev, jnp.max(s, axis=0, keepdims=True))
                m_s[h, :, qs] = m_new
                al_buf[slot, h, :, qs] = jnp.exp2(m_prev - m_new)
                p_buf[slot, h, :, qs] = jnp.exp2((s - m_new).astype(BF16))

    def accumulate(j, slot, u0):
        for h in range(2):
            v_blk = v_ref[keys_of(j), h * LANES:(h + 1) * LANES]
            pv = lax.dot_general(v_blk, p_buf[slot, h, :, u0 * t:], (((0,), (0,)), ((), ())),
                                 preferred_element_type=F32)
            acc_s[h, :, u0 * t:] = al_buf[slot, h, :, u0 * t:] * acc_s[h, :, u0 * t:] + pv

    n_full = FLASH_SUBS * i
    scores(0, 0, 0)

    def body(g, carry):
        for par in range(2):
            k = 2 * g + par
            scores(k + 1, 1 - par, 0)
            accumulate(jnp.maximum(k - 1, 0), 1 - par, 0)
            softmax(par, 0, None)
        return carry

    lax.fori_loop(0, n_full // 2, body, 0)
    accumulate(jnp.maximum(n_full - 1, 0), 1, 0)
    for d in range(FLASH_SUBS):
        if d + 1 < FLASH_SUBS:
            scores(n_full + d + 1, (d + 1) % 2, d + 1)
        softmax(d % 2, d, d)
        accumulate(n_full + d, d % 2, d)
    out_t = jnp.concatenate([acc_s[h, 0:HEAD, :] / acc_s[h, HEAD:HEAD + 1, :] for h in range(2)], axis=0)
    o_ref[...] = out_t.T.astype(o_ref.dtype)


def _flash(q, k, v, batch, seq, n_heads):
    n = q.shape[0]
    tq = FLASH_SUBS * FLASH_BLOCK
    n_q = seq // tq
    return pl.pallas_call(
        _flash_kernel,
        grid=(batch, n_heads // 2, n_q),
        in_specs=[pl.BlockSpec((tq, 2 * LANES), lambda b, h, i: (b * n_q + i, h)),
                  pl.BlockSpec((seq, 2 * LANES), lambda b, h, i: (b, h)),
                  pl.BlockSpec((seq, 2 * LANES), lambda b, h, i: (b, h))],
        out_specs=pl.BlockSpec((tq, LANES), lambda b, h, i: (b * n_q + i, h)),
        out_shape=jax.ShapeDtypeStruct((n, n_heads * HEAD), BF16),
        scratch_shapes=[pltpu.VMEM((2, 1, tq), F32),
                        pltpu.VMEM((2, LANES, tq), F32),
                        pltpu.VMEM((2, 2, FLASH_BLOCK, tq), F32),
                        pltpu.VMEM((2, 2, FLASH_BLOCK, tq), BF16),
                        pltpu.VMEM((2, 2, 1, tq), F32)],
        compiler_params=_params("parallel", "parallel", "arbitrary"),
        name="mla_flash",
    )(q, k, v)


def _outffn_kernel(ya_ref, yb_ref, yc_ref, x_ref, mod_ref, gpm_ref, gpf_ref, gqf_ref, wo_ref, w1_ref, w2_ref,
                   o_ref, x1_s, hid_s, *, da, db, tf):
    y = (_dot(ya_ref[...], wo_ref[0:da, :])
         + _dot(yb_ref[...], wo_ref[da:da + db, :])
         + _dot(yc_ref[...], wo_ref[da + db:, :]))
    x1 = x_ref[...] + mod_ref[0, 2:3, :] * (_rms(y) * gpm_ref[...])
    x1_s[...] = x1
    h = _rms(x1) * gpf_ref[...]
    h = (h * (1.0 + mod_ref[0, 4:5, :]) + mod_ref[0, 3:4, :]).astype(BF16)
    for f in range(hid_s.shape[1] // tf):
        cols = slice(f * tf, (f + 1) * tf)
        hid = jnp.maximum(_dot(h, w1_ref[:, cols]), 0.0)
        hid_s[:, cols] = (hid * hid).astype(BF16)
    y2 = _dot(hid_s[...], w2_ref[...])
    o_ref[...] = x1_s[...] + mod_ref[0, 5:6, :] * (_rms(y2) * gqf_ref[...])


def _outffn(ya, yb, yc, x, mod, g_post_mix, g_pre_ffn, g_post_ffn, w_out, w_ff1, w_ff2, layer, seq):
    n, d = x.shape
    d_ff = w_ff1.shape[2]
    tm, tf = 512, 512
    per_b = seq // tm
    row = lambda w: pl.BlockSpec((tm, w), lambda i: (i, 0))
    vec = pl.BlockSpec((1, d), lambda i: (0, 0))
    resident = lambda a: pl.BlockSpec((None,) + a.shape[1:], lambda i: (layer, 0, 0), pipeline_mode=pl.Buffered(1))
    return pl.pallas_call(
        functools.partial(_outffn_kernel, da=ya.shape[1], db=yb.shape[1], tf=tf),
        grid=(n // tm,),
        in_specs=[row(ya.shape[1]), row(yb.shape[1]), row(yc.shape[1]), row(d),
                  pl.BlockSpec((1, 6, d), lambda i: (i // per_b, 0, 0)),
                  vec, vec, vec, resident(w_out), resident(w_ff1), resident(w_ff2)],
        out_specs=row(d),
        out_shape=jax.ShapeDtypeStruct((n, d), F32),
        scratch_shapes=[pltpu.VMEM((tm, d), F32), pltpu.VMEM((tm, d_ff), BF16)],
        compiler_params=_params("parallel"),
        name="out_ffn",
    )(ya, yb, yc, x, mod, g_post_mix, g_pre_ffn, g_post_ffn, w_out, w_ff1, w_ff2)


def _rot_cols(w):
    half = w.shape[-1] // 2
    return jnp.concatenate([-w[..., half:], w[..., :half]], axis=-1)


def kernel(x, c, positions, g_pre_mix, g_post_mix, g_pre_ffn, g_post_ffn, w_ada, b_ada, w_in, w_out, rwkv_mu, rwkv_w0, rwkv_w2, rwkv_a0, rwkv_a2, rwkv_g2, rwkv_k_k, rwkv_k_a, rwkv_r_k, rwkv_ln_w, rwkv_ln_b, conv_w, conv_b, conv_ln_w, conv_ln_b, mla_q_norm, mla_w_uq, mla_kv_norm, mla_w_ukv, w_ff1, w_ff2):
    batch, seq, d = x.shape
    n = batch * seq
    n_layers = w_ada.shape[0]
    r_dim = rwkv_w0.shape[1]
    w_lora, a_lora, g_lora = rwkv_w2.shape[1], rwkv_a2.shape[1], rwkv_g2.shape[1]
    assert w_lora + a_lora + g_lora == LANES and r_dim % LANES == 0
    n_r = 3 * r_dim + LANES
    c_dim = conv_w.shape[2]
    n_c = 2 * c_dim
    lora = mla_q_norm.shape[1]
    n_heads = mla_w_ukv.shape[2] // (2 * HEAD)
    rope = mla_w_uq.shape[2] // n_heads - HEAD
    assert HEAD + 2 * rope == LANES
    n_m = 2 * lora + 2 * LANES
    scale = float(HEAD + rope) ** -0.5 * 1.4426950408889634

    xf = x.reshape(n, d)
    c8 = jnp.pad(c, ((0, 8 - batch), (0, 0)))
    mod_all = _ada(c8, w_ada, b_ada.reshape(n_layers, 1, 6 * d))[:, :batch].reshape(n_layers, batch, 6, d)

    inv_freq = 1.0 / (ROPE_THETA ** (jnp.arange(0, rope, 2, dtype=F32) / rope))
    invf = jnp.tile(inv_freq, LANES // inv_freq.shape[0]).reshape(1, LANES)
    cos, sin = _rope_tables(positions.reshape(n, 1), invf)

    head_id = jnp.arange(LANES) // HEAD
    bd = (head_id[:, None] == head_id[None, :]).astype(BF16)
    row1 = lambda a: a.reshape(1, -1)

    w_in_bf, w_out_bf, w_ff1_bf, w_ff2_bf = [w.astype(BF16) for w in (w_in, w_out, w_ff1, w_ff2)]
    for l in range(n_layers):
        k_pe = w_in_bf[l, :, n_r + n_c + 2 * lora:]
        z_half = jnp.zeros((d, HEAD), BF16)
        w_extra = jnp.concatenate([z_half, k_pe, k_pe, z_half, _rot_cols(k_pe), _rot_cols(k_pe)], axis=1)
        wq = mla_w_uq[l].reshape(lora, n_heads, HEAD + rope)
        wuq = jnp.concatenate([wq, _rot_cols(wq[..., HEAD:])], axis=-1).reshape(lora, n_heads * LANES).astype(BF16)
        wkv = mla_w_ukv[l].reshape(lora, n_heads, 2 * HEAD)
        wuk = jnp.concatenate([wkv[..., :HEAD], jnp.zeros_like(wkv[..., :HEAD])], axis=-1)
        wuk = wuk.reshape(lora, n_heads * LANES).astype(BF16)
        wuv = jnp.concatenate([wkv[..., HEAD:], jnp.zeros_like(wkv[..., HEAD:])], axis=-1)
        wuv = wuv.reshape(lora, n_heads * LANES).astype(BF16)
        lora_w = jnp.zeros((LANES, 3 * r_dim), F32)
        lora_w = lora_w.at[0:w_lora, 0:r_dim].set(rwkv_w2[l])
        lora_w = lora_w.at[w_lora:w_lora + a_lora, r_dim:2 * r_dim].set(rwkv_a2[l])
        lora_w = lora_w.at[w_lora + a_lora:, 2 * r_dim:].set(rwkv_g2[l]).astype(BF16)

        mod = mod_all[l]
        p_rwkv, p_conv, p_mla = _inproj(xf, mod, row1(g_pre_mix[l]), w_in_bf, l, w_extra, seq, n_r, n_c, n_m)
        vecs = [row1(rwkv_mu[l]), row1(rwkv_w0[l]), row1(rwkv_a0[l]), row1(rwkv_k_k[l]), row1(rwkv_k_a[l]),
                row1(rwkv_r_k[l]), row1(rwkv_ln_w[l]), row1(rwkv_ln_b[l])]
        y_a = _rwkv(p_rwkv, vecs, lora_w, bd, batch, seq, r_dim)
        y_b = _conv(p_conv, conv_w[l], row1(conv_b[l]), row1(conv_ln_w[l]), row1(conv_ln_b[l]), batch, seq)
        q, k, v = _mlaprep(p_mla, cos, sin, row1(mla_q_norm[l]), row1(mla_kv_norm[l]), wuq, wuk, wuv,
                           lora, n_heads, scale)
        y_c = _flash(q, k, v, batch, seq, n_heads)
        xf = _outffn(y_a, y_b, y_c, xf, mod, row1(g_post_mix[l]), row1(g_pre_ffn[l]), row1(g_post_ffn[l]),
                     w_out_bf, w_ff1_bf, w_ff2_bf, l, seq)
    return xf.reshape(batch, seq, d)
```

```python
import functools

import jax
import jax.numpy as jnp
from jax import lax
from jax.experimental import pallas as pl
from jax.experimental.pallas import tpu as pltpu

F32 = jnp.float32
BF16 = jnp.bfloat16

LANES = 128
HEAD = 64
NORM_EPS = 1e-6
LN_EPS = 1e-5
RWKV_GN_EPS = 64e-5
ROPE_THETA = 10000.0
CONV_WIDTH = 31
CONV_HALO = 32
CONV_SUB = 128
CHUNK = 64
FLASH_BLOCK = 256
FLASH_VROWS = HEAD + 16
FLASH_SUBS = 8
GROUP_CHUNKS = 8
VMEM_LIMIT = 56 * 1024 * 1024


def _dot(a, b, precision=None):
    return jnp.dot(a, b, preferred_element_type=F32, precision=precision)


def _dot_nt(a, b, precision=None):
    return lax.dot_general(a, b, (((1,), (1,)), ((), ())), preferred_element_type=F32, precision=precision)


def _bf(x):
    return x.astype(BF16)


def _split(x):
    hi = x.astype(BF16)
    return hi, (x - hi.astype(F32)).astype(BF16)


def _head_sums(x, ones_bd):
    groups = [_dot(_bf(x[:, g * LANES:(g + 1) * LANES]), ones_bd) for g in range(x.shape[1] // LANES)]
    return jnp.concatenate(groups, axis=1)


def _rms(x):
    return x * lax.rsqrt(jnp.mean(x * x, axis=-1, keepdims=True) + NORM_EPS)


def _sigmoid(x):
    return 1.0 / (1.0 + jnp.exp(-x))


def _params(*semantics):
    return pltpu.CompilerParams(dimension_semantics=semantics, vmem_limit_bytes=VMEM_LIMIT)


def _ada_kernel(c_ref, w_ref, b_ref, o_ref):
    c = c_ref[...]
    cs = c * _sigmoid(c)
    cs_hi, cs_lo = _split(cs)
    w_hi, w_lo = _split(w_ref[0])
    o_ref[0] = _dot(cs_hi, w_hi) + _dot(cs_hi, w_lo) + _dot(cs_lo, w_hi) + b_ref[0]


def _ada(c8, w_ada, b_ada):
    n_layers, d, d6 = w_ada.shape
    tn = 1536
    return pl.pallas_call(
        _ada_kernel,
        grid=(n_layers, d6 // tn),
        in_specs=[pl.BlockSpec((8, d), lambda l, j: (0, 0)),
                  pl.BlockSpec((1, d, tn), lambda l, j: (l, 0, j)),
                  pl.BlockSpec((1, 1, tn), lambda l, j: (l, 0, j))],
        out_specs=pl.BlockSpec((1, 8, tn), lambda l, j: (l, 0, j)),
        out_shape=jax.ShapeDtypeStruct((n_layers, 8, d6), F32),
        compiler_params=_params("parallel", "parallel"),
        name="ada_mod",
    )(c8, w_ada, b_ada)


def _rope_kernel(pos_ref, invf_ref, cos_ref, sin_ref):
    ang = pos_ref[...].astype(F32) * invf_ref[...]
    cos_ref[...] = jnp.cos(ang)
    sin_ref[...] = jnp.sin(ang)


def _rope_tables(pos_col, invf):
    n = pos_col.shape[0]
    tm = 2048
    spec = pl.BlockSpec((tm, LANES), lambda i: (i, 0))
    return pl.pallas_call(
        _rope_kernel,
        grid=(n // tm,),
        in_specs=[pl.BlockSpec((tm, 1), lambda i: (i, 0)), pl.BlockSpec((1, LANES), lambda i: (0, 0))],
        out_specs=[spec, spec],
        out_shape=[jax.ShapeDtypeStruct((n, LANES), F32)] * 2,
        compiler_params=_params("parallel"),
        name="rope_tables",
    )(pos_col, invf)


def _inproj_kernel(x_ref, mod_ref, g_ref, w_ref, wx_ref, pr_ref, pc_ref, pm_ref, *, n_r, n_c, n_lat):
    h = _rms(x_ref[...]) * g_ref[...]
    h = (h * (1.0 + mod_ref[0, 1:2, :]) + mod_ref[0, 0:1, :]).astype(BF16)
    pr_ref[...] = _dot(h, w_ref[:, :n_r]).astype(pr_ref.dtype)
    pc_ref[...] = _dot(h, w_ref[:, n_r:n_r + n_c]).astype(pc_ref.dtype)
    pm_ref[:, :n_lat] = _dot(h, w_ref[:, n_r + n_c:n_r + n_c + n_lat]).astype(pm_ref.dtype)
    pm_ref[:, n_lat:] = _dot(h, wx_ref[...]).astype(pm_ref.dtype)


def _inproj(x, mod, g, w_in_all, layer, w_extra, seq, n_r, n_c, n_m):
    n, d = x.shape
    tm = 512
    per_b = seq // tm
    return pl.pallas_call(
        functools.partial(_inproj_kernel, n_r=n_r, n_c=n_c, n_lat=n_m - w_extra.shape[1]),
        grid=(n // tm,),
        in_specs=[pl.BlockSpec((tm, d), lambda i: (i, 0)),
                  pl.BlockSpec((1, 6, d), lambda i: (i // per_b, 0, 0)),
                  pl.BlockSpec((1, d), lambda i: (0, 0)),
                  pl.BlockSpec((None,) + w_in_all.shape[1:], lambda i: (layer, 0, 0)),
                  pl.BlockSpec(w_extra.shape, lambda i: (0, 0))],
        out_specs=[pl.BlockSpec((tm, n_r), lambda i: (i, 0)),
                   pl.BlockSpec((tm, n_c), lambda i: (i, 0)),
                   pl.BlockSpec((tm, n_m), lambda i: (i, 0))],
        out_shape=[jax.ShapeDtypeStruct((n, n_r), BF16),
                   jax.ShapeDtypeStruct((n, n_c), BF16),
                   jax.ShapeDtypeStruct((n, n_m), BF16)],
        compiler_params=_params("parallel"),
        name="in_proj",
    )(x, mod, g, w_in_all, w_extra)


def _rwkv_kernel(p_ref, mu_ref, w0_ref, a0_ref, kk_ref, ka_ref, rk_ref, lnw_ref, lnb_ref, lora_ref, bd_ref,
                 o_ref, zbuf, st_ref, r_s, k_s, v_s, a_s, b_s, ld_s, y_s, *, tt, dim):
    n_pairs = dim // LANES
    c = CHUNK

    @pl.when(pl.program_id(1) == 0)
    def _():
        zbuf[0:8, :] = jnp.zeros((8, zbuf.shape[1]), F32)
        st_ref[...] = jnp.zeros(st_ref.shape, F32)

    p = p_ref[...].astype(F32)
    zbuf[8:8 + tt, :] = p
    prev = zbuf[7:7 + tt, :]
    zbuf[7:8, :] = zbuf[tt + 7:tt + 8, :]
    z = p + (prev - p) * mu_ref[...]

    r = z[:, 0:dim]
    k = z[:, dim:2 * dim]
    v = z[:, 2 * dim:3 * dim]
    lo = z[:, 3 * dim:3 * dim + LANES]
    lane = lax.broadcasted_iota(jnp.int32, lo.shape, 1)
    act = jnp.where(lane < 32, jnp.tanh(lo), jnp.where(lane < 64, lo, _sigmoid(lo)))
    lora = _dot(_bf(act), lora_ref[...])
    x_w = w0_ref[...] + lora[:, 0:dim]
    w_log = -(jnp.maximum(-x_w, 0.0) + jnp.log(1.0 + jnp.exp(-jnp.abs(x_w)))) - 0.5
    a = _sigmoid(a0_ref[...] + lora[:, dim:2 * dim])
    g = lora[:, 2 * dim:3 * dim]
    bd = bd_ref[...]
    kk = k * kk_ref[...]
    kk = kk * lax.rsqrt(jnp.maximum(_head_sums(kk * kk, bd), 1e-24))
    k = k * (1.0 + (a - 1.0) * ka_ref[...])
    r_s[...] = r
    k_s[...] = k
    v_s[...] = v
    a_s[...] = -kk
    b_s[...] = kk * a
    ld_s[...] = -jnp.exp(w_log)

    row_c = lax.broadcasted_iota(jnp.int32, (c, c), 0)
    col_c = lax.broadcasted_iota(jnp.int32, (c, c), 1)
    tri_incl = (row_c >= col_c).astype(BF16)
    head0 = lax.broadcasted_iota(jnp.int32, (c, LANES), 1) < HEAD
    lane_2 = lax.broadcasted_iota(jnp.int32, (c, 2 * LANES), 1)
    head0_2 = jnp.where(lane_2 >= LANES, lane_2 - LANES, lane_2) < HEAD
    row_4 = lax.broadcasted_iota(jnp.int32, (c, 4 * c), 0)
    col_4 = jnp.bitwise_and(lax.broadcasted_iota(jnp.int32, (c, 4 * c), 1), c - 1)
    strict4 = row_4 > col_4
    incl4 = row_4 >= col_4
    eye4 = (row_4 == col_4).astype(F32)
    bd4 = jnp.bitwise_xor(lax.broadcasted_iota(jnp.int32, (4 * c, 4 * c), 0),
                          lax.broadcasted_iota(jnp.int32, (4 * c, 4 * c), 1)) < c
    row_p = lax.broadcasted_iota(jnp.int32, (LANES, LANES), 0)
    col_p = lax.broadcasted_iota(jnp.int32, (LANES, LANES), 1)
    same_head = (row_p < HEAD) == (col_p < HEAD)
    diag_p = row_p == col_p
    zeros_cl = jnp.zeros((c, LANES), BF16)

    def group_body(gi, carry):
        base = gi * (GROUP_CHUNKS * c)
        items = [(pl.ds(pl.multiple_of(base + q * c, c), c), slice(j * LANES, (j + 1) * LANES), j)
                 for q in range(GROUP_CHUNKS) for j in range(n_pairs)]
        n_it = len(items)
        loaded = [[ref[rows, lanes] for ref in (r_s, k_s, v_s, a_s, b_s, ld_s)] for rows, lanes, _ in items]
        r_c, k_c, v_c, a_c, b_c, ld_c = [list(t) for t in zip(*loaded)]
        cum = []
        for i in range(n_it):
            ld_hi, ld_lo = _split(ld_c[i])
            cum.append(_dot(tri_incl, ld_hi) + _dot(tri_incl, ld_lo))
        cum_end = [x[c - 1:c, :] for x in cum]
        at = [a_c[i] * jnp.exp(cum[i] - ld_c[i]) for i in range(n_it)]
        rt = [r_c[i] * jnp.exp(cum[i]) for i in range(n_it)]
        v_bf = [_bf(x) for x in v_c]
        def by_head(x, mask):
            return jnp.concatenate([jnp.where(mask, x, 0), jnp.where(mask, 0, x)], axis=0)

        aa = []
        for i in range(n_it):
            e_neg = jnp.exp(-cum[i])
            rhs = jnp.concatenate([by_head(_bf(b_c[i] * e_neg), head0), by_head(_bf(k_c[i] * e_neg), head0)], axis=0)
            aa.append(_dot_nt(_bf(jnp.concatenate([at[i], rt[i]], axis=0)), rhs))
        t_cat = [jnp.where(strict4, x[0:c, :], 0.0) for x in aa]
        q_cat = [_bf(jnp.where(incl4, x[c:2 * c, :], 0.0)) for x in aa]
        akv = [_dot(_bf(t_cat[i][:, LANES:]), by_head(v_bf[i], head0)) for i in range(n_it)]
        n_grp = n_it // 2

        def bdiag(x):
            return jnp.where(bd4, jnp.concatenate([x, x, x, x], axis=0), 0)

        p4 = [jnp.concatenate([t_cat[2 * g][:, 0:LANES], t_cat[2 * g + 1][:, 0:LANES]], axis=1) for g in range(n_grp)]
        inv4 = [eye4 + x for x in p4]
        p4_bf = [_bf(x) for x in p4]
        pw = [_bf(_dot(x, bdiag(x))) for x in p4_bf]
        for _ in range(4):
            both = [_dot(jnp.concatenate([pw[g], _bf(inv4[g])], axis=0), bdiag(pw[g])) for g in range(n_grp)]
            inv4 = [inv4[g] + both[g][c:, :] for g in range(n_grp)]
            pw = [_bf(both[g][0:c, :]) for g in range(n_grp)]
        inv4 = [_bf(inv4[g] + _dot(_bf(inv4[g]), bdiag(pw[g]))) for g in range(n_grp)]
        inv_cat = [inv4[i // 2][:, (i % 2) * LANES:(i % 2 + 1) * LANES] for i in range(n_it)]
        x_mat = [_bf(jnp.concatenate([at[i], akv[i]], axis=1)) for i in range(n_it)]
        tx = [_dot(inv_cat[i], by_head(x_mat[i], head0_2)) for i in range(n_it)]
        tx_bf = [_bf(x) for x in tx]
        v_pad = [jnp.concatenate([zeros_cl, v_bf[i]], axis=1) for i in range(n_it)]
        w_mat = [jnp.concatenate([tx_bf[i], v_pad[i]], axis=0) for i in range(n_it)]
        qw = [_dot(q_cat[i], jnp.concatenate([by_head(tx_bf[i], head0_2), by_head(v_pad[i], head0_2)], axis=0))
              for i in range(n_it)]
        mn = []
        for i in range(n_it):
            e_end = jnp.exp(cum_end[i] - cum[i])
            bk_t = _bf(jnp.concatenate([b_c[i] * e_end, k_c[i] * e_end], axis=0).T)
            mn.append(_dot(bk_t, w_mat[i]))
        g_col = [jnp.sum(jnp.where(diag_p, jnp.exp(cum_end[i]), 0.0), axis=1, keepdims=True) for i in range(n_it)]
        states = [st_ref[j] for j in range(n_pairs)]
        for i, (rows, lanes, j) in enumerate(items):
            st = states[j]
            st_bf = _bf(st)
            y_s[rows, lanes] = _dot(_bf(rt[i] + qw[i][:, 0:LANES]), st_bf) + qw[i][:, LANES:]
            m_low = _bf(jnp.where(same_head, mn[i][:, 0:LANES], 0.0))
            states[j] = g_col[i] * st + _dot(m_low, st_bf) + jnp.where(same_head, mn[i][:, LANES:], 0.0)
        for j in range(n_pairs):
            st_ref[j] = states[j]
        return carry

    lax.fori_loop(0, tt // (c * GROUP_CHUNKS), group_body, 0)

    y = y_s[...]
    inv_n = 1.0 / HEAD
    mean = _head_sums(y, bd) * inv_n
    d = y - mean
    var = _head_sums(d * d, bd) * inv_n
    y = d * lax.rsqrt(var + RWKV_GN_EPS) * lnw_ref[...] + lnb_ref[...]
    r = r_s[...]
    bonus = _head_sums(r * k_s[...] * rk_ref[...], bd) * v_s[...]
    o_ref[...] = ((y + bonus) * g).astype(o_ref.dtype)


def _rwkv(p_rwkv, vecs, lora_w, bd, batch, seq, dim):
    n, width = p_rwkv.shape
    tt = CHUNK * GROUP_CHUNKS
    n_t = seq // tt
    vec_spec = lambda a: pl.BlockSpec(a.shape, lambda b, t: (0, 0))
    act = pltpu.VMEM((tt, dim), F32)
    return pl.pallas_call(
        functools.partial(_rwkv_kernel, tt=tt, dim=dim),
        grid=(batch, n_t),
        in_specs=[pl.BlockSpec((tt, width), lambda b, t: (b * n_t + t, 0))]
                 + [vec_spec(a) for a in vecs] + [vec_spec(lora_w), vec_spec(bd)],
        out_specs=pl.BlockSpec((tt, dim), lambda b, t: (b * n_t + t, 0)),
        out_shape=jax.ShapeDtypeStruct((n, dim), BF16),
        scratch_shapes=[pltpu.VMEM((tt + 8, width), F32),
                        pltpu.VMEM((dim // LANES, LANES, LANES), F32),
                        act, act, act, act, act, act, act],
        compiler_params=_params("parallel", "arbitrary"),
        name="rwkv7",
    )(p_rwkv, *vecs, lora_w, bd)


def _conv_kernel(p_ref, w_ref, b_ref, lnw_ref, lnb_ref, shift_ref, o_ref, hbuf, *, tt, dim):
    sub = CONV_SUB
    ext = sub + 8

    @pl.when(pl.program_id(1) == 0)
    def _():
        hbuf[0:CONV_HALO, :] = jnp.zeros((CONV_HALO, dim), F32)

    hbuf[CONV_HALO:CONV_HALO + tt, :] = p_ref[:, 0:dim].astype(F32) * _sigmoid(p_ref[:, dim:2 * dim].astype(F32))
    for s in range(tt // sub):
        base = CONV_HALO + s * sub - 8
        wins = [hbuf[base - 8 * a:base - 8 * a + ext, :] for a in range(CONV_HALO // 8)]
        parts = []
        for r in range(8):
            lags = [8 * a + r for a in range(len(wins)) if 8 * a + r < CONV_WIDTH]
            terms = [wins[m // 8] * w_ref[CONV_WIDTH - 1 - m:CONV_WIDTH - m, :] for m in lags]
            parts.append(functools.reduce(jnp.add, terms))
        stacked = _bf(jnp.concatenate(parts, axis=0))
        acc = _dot(shift_ref[...], stacked) + b_ref[...]
        mu = jnp.mean(acc, axis=-1, keepdims=True)
        d = acc - mu
        var = jnp.mean(d * d, axis=-1, keepdims=True)
        y = d * lax.rsqrt(var + LN_EPS) * lnw_ref[...] + lnb_ref[...]
        o_ref[s * sub:(s + 1) * sub, :] = (y * _sigmoid(y)).astype(o_ref.dtype)
    hbuf[0:CONV_HALO, :] = hbuf[tt:tt + CONV_HALO, :]


def _conv(p_conv, conv_w, conv_b, ln_w, ln_b, batch, seq):
    n, width = p_conv.shape
    dim = width // 2
    tt = 512
    n_t = seq // tt
    ext = CONV_SUB + 8
    u_idx = jnp.arange(CONV_SUB)[:, None]
    col = jnp.arange(8 * ext)[None, :]
    shift = (col % ext == u_idx + 8 - col // ext).astype(BF16)
    vec = lambda a: pl.BlockSpec(a.shape, lambda b, t: (0, 0))
    return pl.pallas_call(
        functools.partial(_conv_kernel, tt=tt, dim=dim),
        grid=(batch, n_t),
        in_specs=[pl.BlockSpec((tt, width), lambda b, t: (b * n_t + t, 0)),
                  vec(conv_w), vec(conv_b), vec(ln_w), vec(ln_b), vec(shift)],
        out_specs=pl.BlockSpec((tt, dim), lambda b, t: (b * n_t + t, 0)),
        out_shape=jax.ShapeDtypeStruct((n, dim), BF16),
        scratch_shapes=[pltpu.VMEM((tt + CONV_HALO, dim), F32)],
        compiler_params=_params("parallel", "arbitrary"),
        name="conformer_conv",
    )(p_conv, conv_w, conv_b, ln_w, ln_b, shift)


def _mlaprep_kernel(p_ref, cos_ref, sin_ref, qn_ref, kvn_ref, wuq_ref, wuk_ref, wuv_ref,
                    q_ref, k_ref, v_ref, *, lora, n_heads, scale):
    cos = cos_ref[...]
    sin = sin_ref[...]
    qn = (_rms(p_ref[:, 0:lora].astype(F32)) * qn_ref[...]).astype(BF16)
    kvn = (_rms(p_ref[:, lora:2 * lora].astype(F32)) * kvn_ref[...]).astype(BF16)
    kpe = (p_ref[:, 2 * lora:2 * lora + LANES].astype(F32) * cos
           + p_ref[:, 2 * lora + LANES:2 * lora + 2 * LANES].astype(F32) * sin)
    lane = lax.broadcasted_iota(jnp.int32, cos.shape, 1)
    q_mul = jnp.where(lane < HEAD, 1.0, jnp.where(lane < HEAD + 32, cos, sin)) * scale
    for h in range(n_heads):
        cols = slice(h * LANES, (h + 1) * LANES)
        q_ref[:, cols] = (_dot(qn, wuq_ref[:, cols]) * q_mul).astype(BF16)
        k_ref[:, cols] = (_dot(kvn, wuk_ref[:, cols]) + kpe).astype(BF16)
        v_ref[:, cols] = jnp.where(lane < HEAD, _dot(kvn, wuv_ref[:, cols]), 1.0).astype(BF16)


def _mlaprep(p_mla, cos, sin, q_norm, kv_norm, wuq, wuk, wuv, lora, n_heads, scale):
    n, width = p_mla.shape
    tm = 512
    row = lambda w: pl.BlockSpec((tm, w), lambda i: (i, 0))
    full = lambda a: pl.BlockSpec(a.shape, lambda i: (0, 0))
    return pl.pallas_call(
        functools.partial(_mlaprep_kernel, lora=lora, n_heads=n_heads, scale=scale),
        grid=(n // tm,),
        in_specs=[row(width), row(LANES), row(LANES), full(q_norm), full(kv_norm), full(wuq), full(wuk), full(wuv)],
        out_specs=[row(n_heads * LANES)] * 3,
        out_shape=[jax.ShapeDtypeStruct((n, n_heads * LANES), BF16)] * 3,
        compiler_params=_params("parallel"),
        name="mla_prep",
    )(p_mla, cos, sin, q_norm, kv_norm, wuq, wuk, wuv)


def _flash_kernel(q_ref, k_ref, v_ref, o_ref, m_s, acc_s, s_buf, p_buf, al_buf):
    i = pl.program_id(2)
    t = FLASH_BLOCK
    m_s[...] = jnp.full(m_s.shape, -jnp.inf, F32)
    acc_s[...] = jnp.zeros(acc_s.shape, F32)
    p_buf[1] = jnp.zeros(p_buf.shape[1:], BF16)
    al_buf[1] = jnp.ones(al_buf.shape[1:], F32)
    causal = lax.broadcasted_iota(jnp.int32, (t, t), 0) <= lax.broadcasted_iota(jnp.int32, (t, t), 1)

    def keys_of(j):
        return pl.ds(pl.multiple_of(j * t, t), t)

    def scores(j, slot, u0):
        for h in range(2):
            cols = slice(h * LANES, (h + 1) * LANES)
            s_buf[slot, h, :, u0 * t:] = _dot_nt(k_ref[keys_of(j), cols], q_ref[u0 * t:, cols])

    def softmax(slot, u0, masked_sub):
        for u in range(u0, FLASH_SUBS):
            qs = slice(u * t, (u + 1) * t)
            for h in range(2):
                s = s_buf[slot, h, :, qs]
                if u == masked_sub:
                    s = jnp.where(causal, s, -jnp.inf)
                m_prev = m_s[h, :, qs]
                m_new = jnp.maximum(m_prev, jnp.max(s, axis=0, keepdims=True))
                m_s[h, :, qs] = m_new
                al_buf[slot, h, :, qs] = jnp.exp2(m_prev - m_new)
                p_buf[slot, h, :, qs] = jnp.exp2((s - m_new).astype(BF16))

    def accumulate(j, slot, u0):
        for h in range(2):
            v_blk = v_ref[keys_of(j), h * LANES:h * LANES + FLASH_VROWS]
            pv = lax.dot_general(v_blk, p_buf[slot, h, :, u0 * t:], (((0,), (0,)), ((), ())),
                                 preferred_element_type=F32)
            acc_s[h, :, u0 * t:] = al_buf[slot, h, :, u0 * t:] * acc_s[h, :, u0 * t:] + pv

    n_full = FLASH_SUBS * i
    scores(0, 0, 0)

    def body(g, carry):
        for par in range(2):
            k = 2 * g + par
            scores(k + 1, 1 - par, 0)
            accumulate(jnp.maximum(k - 1, 0), 1 - par, 0)
            softmax(par, 0, None)
        return carry

    lax.fori_loop(0, n_full // 2, body, 0)
    accumulate(jnp.maximum(n_full - 1, 0), 1, 0)
    for d in range(FLASH_SUBS):
        if d + 1 < FLASH_SUBS:
            scores(n_full + d + 1, (d + 1) % 2, d + 1)
        softmax(d % 2, d, d)
        accumulate(n_full + d, d % 2, d)
    out_t = jnp.concatenate([acc_s[h, 0:HEAD, :] / acc_s[h, HEAD:HEAD + 1, :] for h in range(2)], axis=0)
    o_ref[...] = out_t.T.astype(o_ref.dtype)


def _flash(q, k, v, batch, seq, n_heads):
    n = q.shape[0]
    tq = FLASH_SUBS * FLASH_BLOCK
    n_q = seq // tq
    return pl.pallas_call(
        _flash_kernel,
        grid=(batch, n_heads // 2, n_q),
        in_specs=[pl.BlockSpec((tq, 2 * LANES), lambda b, h, i: (b * n_q + i, h)),
                  pl.BlockSpec((seq, 2 * LANES), lambda b, h, i: (b, h)),
                  pl.BlockSpec((seq, 2 * LANES), lambda b, h, i: (b, h))],
        out_specs=pl.BlockSpec((tq, LANES), lambda b, h, i: (b * n_q + i, h)),
        out_shape=jax.ShapeDtypeStruct((n, n_heads * HEAD), BF16),
        scratch_shapes=[pltpu.VMEM((2, 1, tq), F32),
                        pltpu.VMEM((2, FLASH_VROWS, tq), F32),
                        pltpu.VMEM((2, 2, FLASH_BLOCK, tq), F32),
                        pltpu.VMEM((2, 2, FLASH_BLOCK, tq), BF16),
                        pltpu.VMEM((2, 2, 1, tq), F32)],
        compiler_params=_params("parallel", "parallel", "arbitrary"),
        name="mla_flash",
    )(q, k, v)


def _outffn_kernel(ya_ref, yb_ref, yc_ref, x_ref, mod_ref, gpm_ref, gpf_ref, gqf_ref, wo_ref, w1_ref, w2_ref,
                   o_ref, x1_s, hid_s, *, da, db, tf):
    half = x_ref.shape[0] // 2
    halves = [slice(0, half), slice(half, 2 * half)]
    ys = [_dot(ya_ref[r, :], wo_ref[0:da, :]) + _dot(yb_ref[r, :], wo_ref[da:da + db, :])
          + _dot(yc_ref[r, :], wo_ref[da + db:, :]) for r in halves]
    hs = []
    for r, y in zip(halves, ys):
        x1 = x_ref[r, :] + mod_ref[0, 2:3, :] * (_rms(y) * gpm_ref[...])
        x1_s[r, :] = x1
        h = _rms(x1) * gpf_ref[...]
        hs.append((h * (1.0 + mod_ref[0, 4:5, :]) + mod_ref[0, 3:4, :]).astype(BF16))
    for f in range(hid_s.shape[1] // tf):
        cols = slice(f * tf, (f + 1) * tf)
        for r, h in zip(halves, hs):
            hid = jnp.maximum(_dot(h, w1_ref[:, cols]), 0.0)
            hid_s[r, cols] = (hid * hid).astype(BF16)
    for r in halves:
        y2 = _dot(hid_s[r, :], w2_ref[...])
        o_ref[r, :] = x1_s[r, :] + mod_ref[0, 5:6, :] * (_rms(y2) * gqf_ref[...])


def _outffn(ya, yb, yc, x, mod, g_post_mix, g_pre_ffn, g_post_ffn, w_out, w_ff1, w_ff2, layer, seq):
    n, d = x.shape
    d_ff = w_ff1.shape[2]
    tm, tf = 512, 512
    per_b = seq // tm
    row = lambda w: pl.BlockSpec((tm, w), lambda i: (i, 0))
    vec = pl.BlockSpec((1, d), lambda i: (0, 0))
    resident = lambda a: pl.BlockSpec((None,) + a.shape[1:], lambda i: (layer, 0, 0), pipeline_mode=pl.Buffered(1))
    return pl.pallas_call(
        functools.partial(_outffn_kernel, da=ya.shape[1], db=yb.shape[1], tf=tf),
        grid=(n // tm,),
        in_specs=[row(ya.shape[1]), row(yb.shape[1]), row(yc.shape[1]), row(d),
                  pl.BlockSpec((1, 6, d), lambda i: (i // per_b, 0, 0)),
                  vec, vec, vec, resident(w_out), resident(w_ff1), resident(w_ff2)],
        out_specs=row(d),
        out_shape=jax.ShapeDtypeStruct((n, d), F32),
        scratch_shapes=[pltpu.VMEM((tm, d), F32), pltpu.VMEM((tm, d_ff), BF16)],
        compiler_params=_params("parallel"),
        name="out_ffn",
    )(ya, yb, yc, x, mod, g_post_mix, g_pre_ffn, g_post_ffn, w_out, w_ff1, w_ff2)


def _rot_cols(w):
    half = w.shape[-1] // 2
    return jnp.concatenate([-w[..., half:], w[..., :half]], axis=-1)


def kernel(x, c, positions, g_pre_mix, g_post_mix, g_pre_ffn, g_post_ffn, w_ada, b_ada, w_in, w_out, rwkv_mu, rwkv_w0, rwkv_w2, rwkv_a0, rwkv_a2, rwkv_g2, rwkv_k_k, rwkv_k_a, rwkv_r_k, rwkv_ln_w, rwkv_ln_b, conv_w, conv_b, conv_ln_w, conv_ln_b, mla_q_norm, mla_w_uq, mla_kv_norm, mla_w_ukv, w_ff1, w_ff2):
    batch, seq, d = x.shape
    n = batch * seq
    n_layers = w_ada.shape[0]
    r_dim = rwkv_w0.shape[1]
    w_lora, a_lora, g_lora = rwkv_w2.shape[1], rwkv_a2.shape[1], rwkv_g2.shape[1]
    assert w_lora + a_lora + g_lora == LANES and r_dim % LANES == 0
    n_r = 3 * r_dim + LANES
    c_dim = conv_w.shape[2]
    n_c = 2 * c_dim
    lora = mla_q_norm.shape[1]
    n_heads = mla_w_ukv.shape[2] // (2 * HEAD)
    rope = mla_w_uq.shape[2] // n_heads - HEAD
    assert HEAD + 2 * rope == LANES
    n_m = 2 * lora + 2 * LANES
    scale = float(HEAD + rope) ** -0.5 * 1.4426950408889634

    xf = x.reshape(n, d)
    c8 = jnp.pad(c, ((0, 8 - batch), (0, 0)))
    mod_all = _ada(c8, w_ada, b_ada.reshape(n_layers, 1, 6 * d))[:, :batch].reshape(n_layers, batch, 6, d)

    inv_freq = 1.0 / (ROPE_THETA ** (jnp.arange(0, rope, 2, dtype=F32) / rope))
    invf = jnp.tile(inv_freq, LANES // inv_freq.shape[0]).reshape(1, LANES)
    cos, sin = _rope_tables(positions.reshape(n, 1), invf)

    head_id = jnp.arange(LANES) // HEAD
    bd = (head_id[:, None] == head_id[None, :]).astype(BF16)
    row1 = lambda a: a.reshape(1, -1)

    w_in_bf, w_out_bf, w_ff1_bf, w_ff2_bf = [w.astype(BF16) for w in (w_in, w_out, w_ff1, w_ff2)]
    for l in range(n_layers):
        k_pe = w_in_bf[l, :, n_r + n_c + 2 * lora:]
        z_half = jnp.zeros((d, HEAD), BF16)
        w_extra = jnp.concatenate([z_half, k_pe, k_pe, z_half, _rot_cols(k_pe), _rot_cols(k_pe)], axis=1)
        wq = mla_w_uq[l].reshape(lora, n_heads, HEAD + rope)
        wuq = jnp.concatenate([wq, _rot_cols(wq[..., HEAD:])], axis=-1).reshape(lora, n_heads * LANES).astype(BF16)
        wkv = mla_w_ukv[l].reshape(lora, n_heads, 2 * HEAD)
        wuk = jnp.concatenate([wkv[..., :HEAD], jnp.zeros_like(wkv[..., :HEAD])], axis=-1)
        wuk = wuk.reshape(lora, n_heads * LANES).astype(BF16)
        wuv = jnp.concatenate([wkv[..., HEAD:], jnp.zeros_like(wkv[..., HEAD:])], axis=-1)
        wuv = wuv.reshape(lora, n_heads * LANES).astype(BF16)
        lora_w = jnp.zeros((LANES, 3 * r_dim), F32)
        lora_w = lora_w.at[0:w_lora, 0:r_dim].set(rwkv_w2[l])
        lora_w = lora_w.at[w_lora:w_lora + a_lora, r_dim:2 * r_dim].set(rwkv_a2[l])
        lora_w = lora_w.at[w_lora + a_lora:, 2 * r_dim:].set(rwkv_g2[l]).astype(BF16)

        mod = mod_all[l]
        p_rwkv, p_conv, p_mla = _inproj(xf, mod, row1(g_pre_mix[l]), w_in_bf, l, w_extra, seq, n_r, n_c, n_m)
        vecs = [row1(rwkv_mu[l]), row1(rwkv_w0[l]), row1(rwkv_a0[l]), row1(rwkv_k_k[l]), row1(rwkv_k_a[l]),
                row1(rwkv_r_k[l]), row1(rwkv_ln_w[l]), row1(rwkv_ln_b[l])]
        y_a = _rwkv(p_rwkv, vecs, lora_w, bd, batch, seq, r_dim)
        y_b = _conv(p_conv, conv_w[l], row1(conv_b[l]), row1(conv_ln_w[l]), row1(conv_ln_b[l]), batch, seq)
        q, k, v = _mlaprep(p_mla, cos, sin, row1(mla_q_norm[l]), row1(mla_kv_norm[l]), wuq, wuk, wuv,
                           lora, n_heads, scale)
        y_c = _flash(q, k, v, batch, seq, n_heads)
        xf = _outffn(y_a, y_b, y_c, xf, mod, row1(g_post_mix[l]), row1(g_pre_ffn[l]), row1(g_post_ffn[l]),
                     w_out_bf, w_ff1_bf, w_ff2_bf, l, seq)
    return xf.reshape(batch, seq, d)
```

```python
import functools

import jax
import jax.numpy as jnp
from jax import lax
from jax.experimental import pallas as pl
from jax.experimental.pallas import tpu as pltpu

F32 = jnp.float32
BF16 = jnp.bfloat16

LANES = 128
HEAD = 64
NORM_EPS = 1e-6
LN_EPS = 1e-5
RWKV_GN_EPS = 64e-5
ROPE_THETA = 10000.0
CONV_WIDTH = 31
CONV_HALO = 32
CONV_SUB = 128
CHUNK = 64
FLASH_BLOCK = 256
FLASH_VROWS = HEAD + 16
FLASH_SUBS = 8
GROUP_CHUNKS = 8
VMEM_LIMIT = 56 * 1024 * 1024


def _dot(a, b, precision=None):
    return jnp.dot(a, b, preferred_element_type=F32, precision=precision)


def _dot_nt(a, b, precision=None):
    return lax.dot_general(a, b, (((1,), (1,)), ((), ())), preferred_element_type=F32, precision=precision)


def _bf(x):
    return x.astype(BF16)


def _split(x):
    hi = x.astype(BF16)
    return hi, (x - hi.astype(F32)).astype(BF16)


def _head_sums(x, ones_bd):
    width = x.shape[1]
    groups = [_dot(_bf(x[:, lo:lo + 2 * LANES]), ones_bd) for lo in range(0, width - 2 * LANES + 1, 2 * LANES)]
    if width % (2 * LANES):
        groups.append(_dot(_bf(x[:, width - LANES:]), ones_bd[0:LANES, 0:LANES]))
    return jnp.concatenate(groups, axis=1)


def _rms(x):
    return x * lax.rsqrt(jnp.mean(x * x, axis=-1, keepdims=True) + NORM_EPS)


def _sigmoid(x):
    return 1.0 / (1.0 + jnp.exp(-x))


def _params(*semantics):
    return pltpu.CompilerParams(dimension_semantics=semantics, vmem_limit_bytes=VMEM_LIMIT)


def _ada_kernel(c_ref, w_ref, b_ref, o_ref):
    c = c_ref[...]
    cs = c * _sigmoid(c)
    cs_hi, cs_lo = _split(cs)
    w_hi, w_lo = _split(w_ref[0])
    o_ref[0] = _dot(cs_hi, w_hi) + _dot(cs_hi, w_lo) + _dot(cs_lo, w_hi) + b_ref[0]


def _ada(c8, w_ada, b_ada):
    n_layers, d, d6 = w_ada.shape
    tn = 1536
    return pl.pallas_call(
        _ada_kernel,
        grid=(n_layers, d6 // tn),
        in_specs=[pl.BlockSpec((8, d), lambda l, j: (0, 0)),
                  pl.BlockSpec((1, d, tn), lambda l, j: (l, 0, j)),
                  pl.BlockSpec((1, 1, tn), lambda l, j: (l, 0, j))],
        out_specs=pl.BlockSpec((1, 8, tn), lambda l, j: (l, 0, j)),
        out_shape=jax.ShapeDtypeStruct((n_layers, 8, d6), F32),
        compiler_params=_params("parallel", "parallel"),
        name="ada_mod",
    )(c8, w_ada, b_ada)


def _rope_kernel(pos_ref, invf_ref, cos_ref, sin_ref):
    ang = pos_ref[...].astype(F32) * invf_ref[...]
    cos_ref[...] = jnp.cos(ang)
    sin_ref[...] = jnp.sin(ang)


def _rope_tables(pos_col, invf):
    n = pos_col.shape[0]
    tm = 2048
    spec = pl.BlockSpec((tm, LANES), lambda i: (i, 0))
    return pl.pallas_call(
        _rope_kernel,
        grid=(n // tm,),
        in_specs=[pl.BlockSpec((tm, 1), lambda i: (i, 0)), pl.BlockSpec((1, LANES), lambda i: (0, 0))],
        out_specs=[spec, spec],
        out_shape=[jax.ShapeDtypeStruct((n, LANES), F32)] * 2,
        compiler_params=_params("parallel"),
        name="rope_tables",
    )(pos_col, invf)


def _inproj_kernel(x_ref, mod_ref, g_ref, w_ref, wx_ref, pr_ref, pc_ref, pm_ref, *, n_r, n_c, n_lat):
    h = _rms(x_ref[...]) * g_ref[...]
    h = (h * (1.0 + mod_ref[0, 1:2, :]) + mod_ref[0, 0:1, :]).astype(BF16)
    pr_ref[...] = _dot(h, w_ref[:, :n_r]).astype(pr_ref.dtype)
    pc_ref[...] = _dot(h, w_ref[:, n_r:n_r + n_c]).astype(pc_ref.dtype)
    pm_ref[:, :n_lat] = _dot(h, w_ref[:, n_r + n_c:n_r + n_c + n_lat]).astype(pm_ref.dtype)
    pm_ref[:, n_lat:] = _dot(h, wx_ref[...]).astype(pm_ref.dtype)


def _inproj(x, mod, g, w_in_all, layer, w_extra, seq, n_r, n_c, n_m):
    n, d = x.shape
    tm = 512
    per_b = seq // tm
    return pl.pallas_call(
        functools.partial(_inproj_kernel, n_r=n_r, n_c=n_c, n_lat=n_m - w_extra.shape[1]),
        grid=(n // tm,),
        in_specs=[pl.BlockSpec((tm, d), lambda i: (i, 0)),
                  pl.BlockSpec((1, 6, d), lambda i: (i // per_b, 0, 0)),
                  pl.BlockSpec((1, d), lambda i: (0, 0)),
                  pl.BlockSpec((None,) + w_in_all.shape[1:], lambda i: (layer, 0, 0)),
                  pl.BlockSpec(w_extra.shape, lambda i: (0, 0))],
        out_specs=[pl.BlockSpec((tm, n_r), lambda i: (i, 0)),
                   pl.BlockSpec((tm, n_c), lambda i: (i, 0)),
                   pl.BlockSpec((tm, n_m), lambda i: (i, 0))],
        out_shape=[jax.ShapeDtypeStruct((n, n_r), BF16),
                   jax.ShapeDtypeStruct((n, n_c), BF16),
                   jax.ShapeDtypeStruct((n, n_m), BF16)],
        compiler_params=_params("parallel"),
        name="in_proj",
    )(x, mod, g, w_in_all, w_extra)


def _rwkv_kernel(p_ref, mu_ref, w0_ref, a0_ref, kk_ref, ka_ref, rk_ref, lnw_ref, lnb_ref, lora_ref, bd_ref,
                 o_ref, zbuf, st_ref, r_s, k_s, v_s, a_s, b_s, ld_s, y_s, *, tt, dim):
    n_pairs = dim // LANES
    c = CHUNK

    @pl.when(pl.program_id(1) == 0)
    def _():
        zbuf[0:8, :] = jnp.zeros((8, zbuf.shape[1]), F32)
        st_ref[...] = jnp.zeros(st_ref.shape, F32)

    p = p_ref[...].astype(F32)
    zbuf[8:8 + tt, :] = p
    prev = zbuf[7:7 + tt, :]
    zbuf[7:8, :] = zbuf[tt + 7:tt + 8, :]
    z = p + (prev - p) * mu_ref[...]

    r = z[:, 0:dim]
    k = z[:, dim:2 * dim]
    v = z[:, 2 * dim:3 * dim]
    lo = z[:, 3 * dim:3 * dim + LANES]
    lane = lax.broadcasted_iota(jnp.int32, lo.shape, 1)
    act = jnp.where(lane < 32, jnp.tanh(lo), jnp.where(lane < 64, lo, _sigmoid(lo)))
    lora = _dot(_bf(act), lora_ref[...])
    x_w = w0_ref[...] + lora[:, 0:dim]
    w_log = -(jnp.maximum(-x_w, 0.0) + jnp.log(1.0 + jnp.exp(-jnp.abs(x_w)))) - 0.5
    a = _sigmoid(a0_ref[...] + lora[:, dim:2 * dim])
    g = lora[:, 2 * dim:3 * dim]
    bd = bd_ref[...]
    kk = k * kk_ref[...]
    kk = kk * lax.rsqrt(jnp.maximum(_head_sums(kk * kk, bd), 1e-24))
    k = k * (1.0 + (a - 1.0) * ka_ref[...])
    r_s[...] = r
    k_s[...] = k
    v_s[...] = v
    a_s[...] = -kk
    b_s[...] = kk * a
    ld_s[...] = -jnp.exp(w_log)

    row_c = lax.broadcasted_iota(jnp.int32, (c, c), 0)
    col_c = lax.broadcasted_iota(jnp.int32, (c, c), 1)
    tri_incl = (row_c >= col_c).astype(BF16)
    head0 = lax.broadcasted_iota(jnp.int32, (c, LANES), 1) < HEAD
    lane_2 = lax.broadcasted_iota(jnp.int32, (c, 2 * LANES), 1)
    head0_2 = jnp.where(lane_2 >= LANES, lane_2 - LANES, lane_2) < HEAD
    row_4 = lax.broadcasted_iota(jnp.int32, (c, 4 * c), 0)
    col_4 = jnp.bitwise_and(lax.broadcasted_iota(jnp.int32, (c, 4 * c), 1), c - 1)
    strict4 = row_4 > col_4
    incl4 = row_4 >= col_4
    eye4 = (row_4 == col_4).astype(F32)
    bd4 = jnp.bitwise_xor(lax.broadcasted_iota(jnp.int32, (4 * c, 4 * c), 0),
                          lax.broadcasted_iota(jnp.int32, (4 * c, 4 * c), 1)) < c
    row_p = lax.broadcasted_iota(jnp.int32, (LANES, LANES), 0)
    col_p = lax.broadcasted_iota(jnp.int32, (LANES, LANES), 1)
    same_head = (row_p < HEAD) == (col_p < HEAD)
    diag_p = row_p == col_p
    zeros_cl = jnp.zeros((c, LANES), BF16)

    def group_body(gi, carry):
        base = gi * (GROUP_CHUNKS * c)
        items = [(pl.ds(pl.multiple_of(base + q * c, c), c), slice(j * LANES, (j + 1) * LANES), j)
                 for q in range(GROUP_CHUNKS) for j in range(n_pairs)]
        n_it = len(items)
        loaded = [[ref[rows, lanes] for ref in (r_s, k_s, v_s, a_s, b_s, ld_s)] for rows, lanes, _ in items]
        r_c, k_c, v_c, a_c, b_c, ld_c = [list(t) for t in zip(*loaded)]
        cum = []
        for i in range(n_it):
            ld_hi, ld_lo = _split(ld_c[i])
            both = _dot(tri_incl, jnp.concatenate([ld_hi, ld_lo], axis=1))
            cum.append(both[:, 0:LANES] + both[:, LANES:])
        cum_end = [x[c - 1:c, :] for x in cum]
        at = [a_c[i] * jnp.exp(cum[i] - ld_c[i]) for i in range(n_it)]
        rt = [r_c[i] * jnp.exp(cum[i]) for i in range(n_it)]
        v_bf = [_bf(x) for x in v_c]
        def by_head(x, mask):
            return jnp.concatenate([jnp.where(mask, x, 0), jnp.where(mask, 0, x)], axis=0)

        aa = []
        for i in range(n_it):
            e_neg = jnp.exp(-cum[i])
            rhs = jnp.concatenate([by_head(_bf(b_c[i] * e_neg), head0), by_head(_bf(k_c[i] * e_neg), head0)], axis=0)
            aa.append(_dot_nt(_bf(jnp.concatenate([at[i], rt[i]], axis=0)), rhs))
        t_cat = [jnp.where(strict4, x[0:c, :], 0.0) for x in aa]
        q_cat = [_bf(jnp.where(incl4, x[c:2 * c, :], 0.0)) for x in aa]
        akv = [_dot(_bf(t_cat[i][:, LANES:]), by_head(v_bf[i], head0)) for i in range(n_it)]
        n_grp = n_it // 2

        def bdiag(x):
            return jnp.where(bd4, jnp.concatenate([x, x, x, x], axis=0), 0)

        p4 = [jnp.concatenate([t_cat[2 * g][:, 0:LANES], t_cat[2 * g + 1][:, 0:LANES]], axis=1) for g in range(n_grp)]
        inv4 = [eye4 + x for x in p4]
        p4_bf = [_bf(x) for x in p4]
        pw = [_bf(_dot(x, bdiag(x))) for x in p4_bf]
        for _ in range(4):
            both = [_dot(jnp.concatenate([pw[g], _bf(inv4[g])], axis=0), bdiag(pw[g])) for g in range(n_grp)]
            inv4 = [inv4[g] + both[g][c:, :] for g in range(n_grp)]
            pw = [_bf(both[g][0:c, :]) for g in range(n_grp)]
        inv4 = [_bf(inv4[g] + _dot(_bf(inv4[g]), bdiag(pw[g]))) for g in range(n_grp)]
        inv_cat = [inv4[i // 2][:, (i % 2) * LANES:(i % 2 + 1) * LANES] for i in range(n_it)]
        x_mat = [_bf(jnp.concatenate([at[i], akv[i]], axis=1)) for i in range(n_it)]
        tx = [_dot(inv_cat[i], by_head(x_mat[i], head0_2)) for i in range(n_it)]
        tx_bf = [_bf(x) for x in tx]
        v_pad = [jnp.concatenate([zeros_cl, v_bf[i]], axis=1) for i in range(n_it)]
        w_mat = [jnp.concatenate([tx_bf[i], v_pad[i]], axis=0) for i in range(n_it)]
        qw = [_dot(q_cat[i], jnp.concatenate([by_head(tx_bf[i], head0_2), by_head(v_pad[i], head0_2)], axis=0))
              for i in range(n_it)]
        mn = []
        for i in range(n_it):
            e_end = jnp.exp(cum_end[i] - cum[i])
            bk_t = _bf(jnp.concatenate([b_c[i] * e_end, k_c[i] * e_end], axis=0).T)
            mn.append(_dot(bk_t, w_mat[i]))
        g_col = [jnp.sum(jnp.where(diag_p, jnp.exp(cum_end[i]), 0.0), axis=1, keepdims=True) for i in range(n_it)]
        states = [st_ref[j] for j in range(n_pairs)]
        for i, (rows, lanes, j) in enumerate(items):
            st = states[j]
            st_bf = _bf(st)
            y_s[rows, lanes] = _dot(_bf(rt[i] + qw[i][:, 0:LANES]), st_bf) + qw[i][:, LANES:]
            m_low = _bf(jnp.where(same_head, mn[i][:, 0:LANES], 0.0))
            states[j] = g_col[i] * st + _dot(m_low, st_bf) + jnp.where(same_head, mn[i][:, LANES:], 0.0)
        for j in range(n_pairs):
            st_ref[j] = states[j]
        return carry

    lax.fori_loop(0, tt // (c * GROUP_CHUNKS), group_body, 0)

    y = y_s[...]
    inv_n = 1.0 / HEAD
    mean = _head_sums(y, bd) * inv_n
    d = y - mean
    var = _head_sums(d * d, bd) * inv_n
    y = d * lax.rsqrt(var + RWKV_GN_EPS) * lnw_ref[...] + lnb_ref[...]
    r = r_s[...]
    bonus = _head_sums(r * k_s[...] * rk_ref[...], bd) * v_s[...]
    o_ref[...] = ((y + bonus) * g).astype(o_ref.dtype)


def _rwkv(p_rwkv, vecs, lora_w, bd, batch, seq, dim):
    n, width = p_rwkv.shape
    tt = CHUNK * GROUP_CHUNKS
    n_t = seq // tt
    vec_spec = lambda a: pl.BlockSpec(a.shape, lambda b, t: (0, 0))
    act = pltpu.VMEM((tt, dim), F32)
    return pl.pallas_call(
        functools.partial(_rwkv_kernel, tt=tt, dim=dim),
        grid=(batch, n_t),
        in_specs=[pl.BlockSpec((tt, width), lambda b, t: (b * n_t + t, 0))]
                 + [vec_spec(a) for a in vecs] + [vec_spec(lora_w), vec_spec(bd)],
        out_specs=pl.BlockSpec((tt, dim), lambda b, t: (b * n_t + t, 0)),
        out_shape=jax.ShapeDtypeStruct((n, dim), BF16),
        scratch_shapes=[pltpu.VMEM((tt + 8, width), F32),
                        pltpu.VMEM((dim // LANES, LANES, LANES), F32),
                        act, act, act, act, act, act, act],
        compiler_params=_params("parallel", "arbitrary"),
        name="rwkv7",
    )(p_rwkv, *vecs, lora_w, bd)


def _conv_kernel(p_ref, w_ref, b_ref, lnw_ref, lnb_ref, shift_ref, o_ref, hbuf, *, tt, dim):
    sub = CONV_SUB
    ext = sub + 8

    @pl.when(pl.program_id(1) == 0)
    def _():
        hbuf[0:CONV_HALO, :] = jnp.zeros((CONV_HALO, dim), F32)

    hbuf[CONV_HALO:CONV_HALO + tt, :] = p_ref[:, 0:dim].astype(F32) * _sigmoid(p_ref[:, dim:2 * dim].astype(F32))
    for s in range(tt // sub):
        base = CONV_HALO + s * sub - 8
        wins = [hbuf[base - 8 * a:base - 8 * a + ext, :] for a in range(CONV_HALO // 8)]
        parts = []
        for r in range(8):
            lags = [8 * a + r for a in range(len(wins)) if 8 * a + r < CONV_WIDTH]
            terms = [wins[m // 8] * w_ref[CONV_WIDTH - 1 - m:CONV_WIDTH - m, :] for m in lags]
            parts.append(functools.reduce(jnp.add, terms))
        stacked = _bf(jnp.concatenate(parts, axis=0))
        acc = _dot(shift_ref[...], stacked) + b_ref[...]
        mu = jnp.mean(acc, axis=-1, keepdims=True)
        d = acc - mu
        var = jnp.mean(d * d, axis=-1, keepdims=True)
        y = d * lax.rsqrt(var + LN_EPS) * lnw_ref[...] + lnb_ref[...]
        o_ref[s * sub:(s + 1) * sub, :] = (y * _sigmoid(y)).astype(o_ref.dtype)
    hbuf[0:CONV_HALO, :] = hbuf[tt:tt + CONV_HALO, :]


def _conv(p_conv, conv_w, conv_b, ln_w, ln_b, batch, seq):
    n, width = p_conv.shape
    dim = width // 2
    tt = 512
    n_t = seq // tt
    ext = CONV_SUB + 8
    u_idx = jnp.arange(CONV_SUB)[:, None]
    col = jnp.arange(8 * ext)[None, :]
    shift = (col % ext == u_idx + 8 - col // ext).astype(BF16)
    vec = lambda a: pl.BlockSpec(a.shape, lambda b, t: (0, 0))
    return pl.pallas_call(
        functools.partial(_conv_kernel, tt=tt, dim=dim),
        grid=(batch, n_t),
        in_specs=[pl.BlockSpec((tt, width), lambda b, t: (b * n_t + t, 0)),
                  vec(conv_w), vec(conv_b), vec(ln_w), vec(ln_b), vec(shift)],
        out_specs=pl.BlockSpec((tt, dim), lambda b, t: (b * n_t + t, 0)),
        out_shape=jax.ShapeDtypeStruct((n, dim), BF16),
        scratch_shapes=[pltpu.VMEM((tt + CONV_HALO, dim), F32)],
        compiler_params=_params("parallel", "arbitrary"),
        name="conformer_conv",
    )(p_conv, conv_w, conv_b, ln_w, ln_b, shift)


def _mlaprep_kernel(p_ref, cos_ref, sin_ref, qn_ref, kvn_ref, wuq_ref, wuk_ref, wuv_ref,
                    q_ref, k_ref, v_ref, *, lora, n_heads, scale):
    cos = cos_ref[...]
    sin = sin_ref[...]
    qn = (_rms(p_ref[:, 0:lora].astype(F32)) * qn_ref[...]).astype(BF16)
    kvn = (_rms(p_ref[:, lora:2 * lora].astype(F32)) * kvn_ref[...]).astype(BF16)
    kpe = (p_ref[:, 2 * lora:2 * lora + LANES].astype(F32) * cos
           + p_ref[:, 2 * lora + LANES:2 * lora + 2 * LANES].astype(F32) * sin)
    lane = lax.broadcasted_iota(jnp.int32, cos.shape, 1)
    q_mul = jnp.where(lane < HEAD, 1.0, jnp.where(lane < HEAD + 32, cos, sin)) * scale
    q_mul2 = jnp.concatenate([q_mul, q_mul], axis=1)
    kpe2 = jnp.concatenate([kpe, kpe], axis=1)
    is_v = jnp.concatenate([lane < HEAD, lane < HEAD], axis=1)
    for j in range(n_heads // 2):
        cols = slice(2 * j * LANES, 2 * (j + 1) * LANES)
        q_ref[:, cols] = (_dot(qn, wuq_ref[:, cols]) * q_mul2).astype(BF16)
        k_ref[:, cols] = (_dot(kvn, wuk_ref[:, cols]) + kpe2).astype(BF16)
        v_ref[:, cols] = jnp.where(is_v, _dot(kvn, wuv_ref[:, cols]), 1.0).astype(BF16)


def _mlaprep(p_mla, cos, sin, q_norm, kv_norm, wuq, wuk, wuv, lora, n_heads, scale):
    n, width = p_mla.shape
    tm = 512
    row = lambda w: pl.BlockSpec((tm, w), lambda i: (i, 0))
    full = lambda a: pl.BlockSpec(a.shape, lambda i: (0, 0))
    return pl.pallas_call(
        functools.partial(_mlaprep_kernel, lora=lora, n_heads=n_heads, scale=scale),
        grid=(n // tm,),
        in_specs=[row(width), row(LANES), row(LANES), full(q_norm), full(kv_norm), full(wuq), full(wuk), full(wuv)],
        out_specs=[row(n_heads * LANES)] * 3,
        out_shape=[jax.ShapeDtypeStruct((n, n_heads * LANES), BF16)] * 3,
        compiler_params=_params("parallel"),
        name="mla_prep",
    )(p_mla, cos, sin, q_norm, kv_norm, wuq, wuk, wuv)


def _flash_kernel(q_ref, k_ref, v_ref, o_ref, m_s, acc_s, s_buf, p_buf, al_buf):
    i = pl.program_id(2)
    t = FLASH_BLOCK
    m_s[...] = jnp.full(m_s.shape, -jnp.inf, F32)
    acc_s[...] = jnp.zeros(acc_s.shape, F32)
    p_buf[1] = jnp.zeros(p_buf.shape[1:], BF16)
    al_buf[1] = jnp.ones(al_buf.shape[1:], F32)
    causal = lax.broadcasted_iota(jnp.int32, (t, t), 0) <= lax.broadcasted_iota(jnp.int32, (t, t), 1)

    def keys_of(j):
        return pl.ds(pl.multiple_of(j * t, t), t)

    def scores(j, slot, u0):
        for h in range(2):
            cols = slice(h * LANES, (h + 1) * LANES)
            s_buf[slot, h, :, u0 * t:] = _dot_nt(k_ref[keys_of(j), cols], q_ref[u0 * t:, cols])

    def softmax(slot, u0, masked_sub):
        for u in range(u0, FLASH_SUBS):
            qs = slice(u * t, (u + 1) * t)
            for h in range(2):
                s = s_buf[slot, h, :, qs]
                if u == masked_sub:
                    s = jnp.where(causal, s, -jnp.inf)
                m_prev = m_s[h, :, qs]
                m_new = jnp.maximum(m_prev, jnp.max(s, axis=0, keepdims=True))
                m_s[h, :, qs] = m_new
                al_buf[slot, h, :, qs] = jnp.exp2(m_prev - m_new)
                p_buf[slot, h, :, qs] = jnp.exp2((s - m_new).astype(BF16))

    def accumulate(j, slot, u0):
        for h in range(2):
            v_blk = v_ref[keys_of(j), h * LANES:h * LANES + FLASH_VROWS]
            pv = lax.dot_general(v_blk, p_buf[slot, h, :, u0 * t:], (((0,), (0,)), ((), ())),
                                 preferred_element_type=F32)
            acc_s[h, :, u0 * t:] = al_buf[slot, h, :, u0 * t:] * acc_s[h, :, u0 * t:] + pv

    n_full = FLASH_SUBS * i
    scores(0, 0, 0)

    def body(g, carry):
        for par in range(2):
            k = 2 * g + par
            scores(k + 1, 1 - par, 0)
            accumulate(jnp.maximum(k - 1, 0), 1 - par, 0)
            softmax(par, 0, None)
        return carry

    lax.fori_loop(0, n_full // 2, body, 0)
    accumulate(jnp.maximum(n_full - 1, 0), 1, 0)
    for d in range(FLASH_SUBS):
        if d + 1 < FLASH_SUBS:
            scores(n_full + d + 1, (d + 1) % 2, d + 1)
        softmax(d % 2, d, d)
        accumulate(n_full + d, d % 2, d)
    out_t = jnp.concatenate([acc_s[h, 0:HEAD, :] / acc_s[h, HEAD:HEAD + 1, :] for h in range(2)], axis=0)
    o_ref[...] = out_t.T.astype(o_ref.dtype)


def _flash(q, k, v, batch, seq, n_heads):
    n = q.shape[0]
    tq = FLASH_SUBS * FLASH_BLOCK
    n_q = seq // tq
    return pl.pallas_call(
        _flash_kernel,
        grid=(batch, n_heads // 2, n_q),
        in_specs=[pl.BlockSpec((tq, 2 * LANES), lambda b, h, i: (b * n_q + i, h)),
                  pl.BlockSpec((seq, 2 * LANES), lambda b, h, i: (b, h)),
                  pl.BlockSpec((seq, 2 * LANES), lambda b, h, i: (b, h))],
        out_specs=pl.BlockSpec((tq, LANES), lambda b, h, i: (b * n_q + i, h)),
        out_shape=jax.ShapeDtypeStruct((n, n_heads * HEAD), BF16),
        scratch_shapes=[pltpu.VMEM((2, 1, tq), F32),
                        pltpu.VMEM((2, FLASH_VROWS, tq), F32),
                        pltpu.VMEM((2, 2, FLASH_BLOCK, tq), F32),
                        pltpu.VMEM((2, 2, FLASH_BLOCK, tq), BF16),
                        pltpu.VMEM((2, 2, 1, tq), F32)],
        compiler_params=_params("parallel", "parallel", "arbitrary"),
        name="mla_flash",
    )(q, k, v)


def _outffn_kernel(ya_ref, yb_ref, yc_ref, x_ref, mod_ref, gpm_ref, gpf_ref, gqf_ref, wo_ref, w1_ref, w2_ref,
                   o_ref, x1_s, hid_s, *, da, db, tf):
    half = x_ref.shape[0] // 2
    halves = [slice(0, half), slice(half, 2 * half)]
    ys = [_dot(ya_ref[r, :], wo_ref[0:da, :]) + _dot(yb_ref[r, :], wo_ref[da:da + db, :])
          + _dot(yc_ref[r, :], wo_ref[da + db:, :]) for r in halves]
    hs = []
    for r, y in zip(halves, ys):
        x1 = x_ref[r, :] + mod_ref[0, 2:3, :] * (_rms(y) * gpm_ref[...])
        x1_s[r, :] = x1
        h = _rms(x1) * gpf_ref[...]
        hs.append((h * (1.0 + mod_ref[0, 4:5, :]) + mod_ref[0, 3:4, :]).astype(BF16))
    for f in range(hid_s.shape[1] // tf):
        cols = slice(f * tf, (f + 1) * tf)
        for r, h in zip(halves, hs):
            hid = jnp.maximum(_dot(h, w1_ref[:, cols]), 0.0)
            hid_s[r, cols] = (hid * hid).astype(BF16)
    for r in halves:
        y2 = _dot(hid_s[r, :], w2_ref[...])
        o_ref[r, :] = x1_s[r, :] + mod_ref[0, 5:6, :] * (_rms(y2) * gqf_ref[...])


def _outffn(ya, yb, yc, x, mod, g_post_mix, g_pre_ffn, g_post_ffn, w_out, w_ff1, w_ff2, layer, seq):
    n, d = x.shape
    d_ff = w_ff1.shape[2]
    tm, tf = 512, 512
    per_b = seq // tm
    row = lambda w: pl.BlockSpec((tm, w), lambda i: (i, 0))
    vec = pl.BlockSpec((1, d), lambda i: (0, 0))
    resident = lambda a: pl.BlockSpec((None,) + a.shape[1:], lambda i: (layer, 0, 0), pipeline_mode=pl.Buffered(1))
    return pl.pallas_call(
        functools.partial(_outffn_kernel, da=ya.shape[1], db=yb.shape[1], tf=tf),
        grid=(n // tm,),
        in_specs=[row(ya.shape[1]), row(yb.shape[1]), row(yc.shape[1]), row(d),
                  pl.BlockSpec((1, 6, d), lambda i: (i // per_b, 0, 0)),
                  vec, vec, vec, resident(w_out), resident(w_ff1), resident(w_ff2)],
        out_specs=row(d),
        out_shape=jax.ShapeDtypeStruct((n, d), F32),
        scratch_shapes=[pltpu.VMEM((tm, d), F32), pltpu.VMEM((tm, d_ff), BF16)],
        compiler_params=_params("parallel"),
        name="out_ffn",
    )(ya, yb, yc, x, mod, g_post_mix, g_pre_ffn, g_post_ffn, w_out, w_ff1, w_ff2)


def _rot_cols(w):
    half = w.shape[-1] // 2
    return jnp.concatenate([-w[..., half:], w[..., :half]], axis=-1)


def kernel(x, c, positions, g_pre_mix, g_post_mix, g_pre_ffn, g_post_ffn, w_ada, b_ada, w_in, w_out, rwkv_mu, rwkv_w0, rwkv_w2, rwkv_a0, rwkv_a2, rwkv_g2, rwkv_k_k, rwkv_k_a, rwkv_r_k, rwkv_ln_w, rwkv_ln_b, conv_w, conv_b, conv_ln_w, conv_ln_b, mla_q_norm, mla_w_uq, mla_kv_norm, mla_w_ukv, w_ff1, w_ff2):
    batch, seq, d = x.shape
    n = batch * seq
    n_layers = w_ada.shape[0]
    r_dim = rwkv_w0.shape[1]
    w_lora, a_lora, g_lora = rwkv_w2.shape[1], rwkv_a2.shape[1], rwkv_g2.shape[1]
    assert w_lora + a_lora + g_lora == LANES and r_dim % LANES == 0
    n_r = 3 * r_dim + LANES
    c_dim = conv_w.shape[2]
    n_c = 2 * c_dim
    lora = mla_q_norm.shape[1]
    n_heads = mla_w_ukv.shape[2] // (2 * HEAD)
    rope = mla_w_uq.shape[2] // n_heads - HEAD
    assert HEAD + 2 * rope == LANES
    n_m = 2 * lora + 2 * LANES
    scale = float(HEAD + rope) ** -0.5 * 1.4426950408889634

    xf = x.reshape(n, d)
    c8 = jnp.pad(c, ((0, 8 - batch), (0, 0)))
    mod_all = _ada(c8, w_ada, b_ada.reshape(n_layers, 1, 6 * d))[:, :batch].reshape(n_layers, batch, 6, d)

    inv_freq = 1.0 / (ROPE_THETA ** (jnp.arange(0, rope, 2, dtype=F32) / rope))
    invf = jnp.tile(inv_freq, LANES // inv_freq.shape[0]).reshape(1, LANES)
    cos, sin = _rope_tables(positions.reshape(n, 1), invf)

    head_id = jnp.arange(2 * LANES) // HEAD
    bd = (head_id[:, None] == head_id[None, :]).astype(BF16)
    row1 = lambda a: a.reshape(1, -1)

    w_in_bf, w_out_bf, w_ff1_bf, w_ff2_bf = [w.astype(BF16) for w in (w_in, w_out, w_ff1, w_ff2)]
    for l in range(n_layers):
        k_pe = w_in_bf[l, :, n_r + n_c + 2 * lora:]
        z_half = jnp.zeros((d, HEAD), BF16)
        w_extra = jnp.concatenate([z_half, k_pe, k_pe, z_half, _rot_cols(k_pe), _rot_cols(k_pe)], axis=1)
        wq = mla_w_uq[l].reshape(lora, n_heads, HEAD + rope)
        wuq = jnp.concatenate([wq, _rot_cols(wq[..., HEAD:])], axis=-1).reshape(lora, n_heads * LANES).astype(BF16)
        wkv = mla_w_ukv[l].reshape(lora, n_heads, 2 * HEAD)
        wuk = jnp.concatenate([wkv[..., :HEAD], jnp.zeros_like(wkv[..., :HEAD])], axis=-1)
        wuk = wuk.reshape(lora, n_heads * LANES).astype(BF16)
        wuv = jnp.concatenate([wkv[..., HEAD:], jnp.zeros_like(wkv[..., HEAD:])], axis=-1)
        wuv = wuv.reshape(lora, n_heads * LANES).astype(BF16)
        lora_w = jnp.zeros((LANES, 3 * r_dim), F32)
        lora_w = lora_w.at[0:w_lora, 0:r_dim].set(rwkv_w2[l])
        lora_w = lora_w.at[w_lora:w_lora + a_lora, r_dim:2 * r_dim].set(rwkv_a2[l])
        lora_w = lora_w.at[w_lora + a_lora:, 2 * r_dim:].set(rwkv_g2[l]).astype(BF16)

        mod = mod_all[l]
        p_rwkv, p_conv, p_mla = _inproj(xf, mod, row1(g_pre_mix[l]), w_in_bf, l, w_extra, seq, n_r, n_c, n_m)
        vecs = [row1(rwkv_mu[l]), row1(rwkv_w0[l]), row1(rwkv_a0[l]), row1(rwkv_k_k[l]), row1(rwkv_k_a[l]),
                row1(rwkv_r_k[l]), row1(rwkv_ln_w[l]), row1(rwkv_ln_b[l])]
        y_a = _rwkv(p_rwkv, vecs, lora_w, bd, batch, seq, r_dim)
        y_b = _conv(p_conv, conv_w[l], row1(conv_b[l]), row1(conv_ln_w[l]), row1(conv_ln_b[l]), batch, seq)
        q, k, v = _mlaprep(p_mla, cos, sin, row1(mla_q_norm[l]), row1(mla_kv_norm[l]), wuq, wuk, wuv,
                           lora, n_heads, scale)
        y_c = _flash(q, k, v, batch, seq, n_heads)
        xf = _outffn(y_a, y_b, y_c, xf, mod, row1(g_post_mix[l]), row1(g_pre_ffn[l]), row1(g_post_ffn[l]),
                     w_out_bf, w_ff1_bf, w_ff2_bf, l, seq)
    return xf.reshape(batch, seq, d)
```

```python
import functools

import jax
import jax.numpy as jnp
from jax import lax
from jax.experimental import pallas as pl
from jax.experimental.pallas import tpu as pltpu

F32 = jnp.float32
BF16 = jnp.bfloat16

LANES = 128
HEAD = 64
NORM_EPS = 1e-6
LN_EPS = 1e-5
RWKV_GN_EPS = 64e-5
ROPE_THETA = 10000.0
CONV_WIDTH = 31
CONV_HALO = 32
CONV_SUB = 128
CHUNK = 64
FLASH_BLOCK = 256
FLASH_VROWS = HEAD + 16
FLASH_SUBS = 8
GROUP_CHUNKS = 8
VMEM_LIMIT = 56 * 1024 * 1024
LOG2_E = 1.4426950408889634


def _dot(a, b):
    return jnp.dot(a, b, preferred_element_type=F32)


def _dot_nt(a, b):
    return lax.dot_general(a, b, (((1,), (1,)), ((), ())), preferred_element_type=F32)


def _bf(x):
    return x.astype(BF16)


def _split(x):
    hi = x.astype(BF16)
    return hi, (x - hi.astype(F32)).astype(BF16)


def _head_sums(x, ones_bd):
    width = x.shape[1]
    groups = [_dot(_bf(x[:, lo:lo + 2 * LANES]), ones_bd) for lo in range(0, width - 2 * LANES + 1, 2 * LANES)]
    if width % (2 * LANES):
        groups.append(_dot(_bf(x[:, width - LANES:]), ones_bd[0:LANES, 0:LANES]))
    return jnp.concatenate(groups, axis=1)


def _rms(x):
    return x * lax.rsqrt(jnp.mean(x * x, axis=-1, keepdims=True) + NORM_EPS)


def _sigmoid(x):
    return 1.0 / (1.0 + jnp.exp(-x))


def _params(*semantics):
    return pltpu.CompilerParams(dimension_semantics=semantics, vmem_limit_bytes=VMEM_LIMIT)


def _ada_kernel(c_ref, w_ref, b_ref, o_ref):
    c = c_ref[...]
    cs = c * _sigmoid(c)
    cs_hi, cs_lo = _split(cs)
    w_hi, w_lo = _split(w_ref[0])
    o_ref[0] = _dot(cs_hi, w_hi) + _dot(cs_hi, w_lo) + _dot(cs_lo, w_hi) + b_ref[0]


def _ada(c8, w_ada, b_ada):
    n_layers, d, d6 = w_ada.shape
    tn = 1536
    return pl.pallas_call(
        _ada_kernel,
        grid=(n_layers, d6 // tn),
        in_specs=[pl.BlockSpec((8, d), lambda l, j: (0, 0)),
                  pl.BlockSpec((1, d, tn), lambda l, j: (l, 0, j)),
                  pl.BlockSpec((1, 1, tn), lambda l, j: (l, 0, j))],
        out_specs=pl.BlockSpec((1, 8, tn), lambda l, j: (l, 0, j)),
        out_shape=jax.ShapeDtypeStruct((n_layers, 8, d6), F32),
        compiler_params=_params("parallel", "parallel"),
        name="ada_mod",
    )(c8, w_ada, b_ada)


def _rope_kernel(pos_ref, invf_ref, cos_ref, sin_ref):
    ang = pos_ref[...].astype(F32) * invf_ref[...]
    cos_ref[...] = jnp.cos(ang)
    sin_ref[...] = jnp.sin(ang)


def _rope_tables(pos_col, invf):
    n = pos_col.shape[0]
    tm = 2048
    spec = pl.BlockSpec((tm, LANES), lambda i: (i, 0))
    return pl.pallas_call(
        _rope_kernel,
        grid=(n // tm,),
        in_specs=[pl.BlockSpec((tm, 1), lambda i: (i, 0)), pl.BlockSpec((1, LANES), lambda i: (0, 0))],
        out_specs=[spec, spec],
        out_shape=[jax.ShapeDtypeStruct((n, LANES), F32)] * 2,
        compiler_params=_params("parallel"),
        name="rope_tables",
    )(pos_col, invf)


def _inproj_kernel(x_ref, mod_ref, g_ref, w_ref, wx_ref, cos_ref, sin_ref, qn_ref, kvn_ref, wuq_ref, wuk_ref, wuv_ref,
                   cw_ref, cb_ref, clw_ref, clb_ref, shift_ref,
                   pr_ref, yb_ref, q_ref, k_ref, v_ref, hbuf, *, n_r, n_c, lora, n_heads, rope, scale, per_b):
    @pl.when(pl.program_id(0) == 0)
    def _():
        hbuf[0:CONV_HALO, :] = jnp.zeros((CONV_HALO, hbuf.shape[1]), F32)

    h = _rms(x_ref[...]) * g_ref[...]
    h = (h * (1.0 + mod_ref[0, 1:2, :]) + mod_ref[0, 0:1, :]).astype(BF16)
    p_conv = _dot(h, w_ref[:, n_r:n_r + n_c])
    lat = _dot(h, w_ref[:, n_r + n_c:n_r + n_c + 2 * lora])
    k_rot = _dot(h, wx_ref[...])
    pr_ref[...] = _dot(h, w_ref[:, :n_r]).astype(pr_ref.dtype)
    cos = cos_ref[...]
    sin = sin_ref[...]
    qn = (_rms(lat[:, 0:lora]) * qn_ref[...]).astype(BF16)
    kvn = (_rms(lat[:, lora:]) * kvn_ref[...]).astype(BF16)
    kpe = k_rot[:, 0:LANES] * cos + k_rot[:, LANES:] * sin
    lane = lax.broadcasted_iota(jnp.int32, cos.shape, 1)
    q_mul = jnp.where(lane < HEAD, 1.0, jnp.where(lane < HEAD + rope, cos, sin)) * scale
    q_mul2 = jnp.concatenate([q_mul, q_mul], axis=1)
    kpe2 = jnp.concatenate([kpe, kpe], axis=1)
    is_v = jnp.concatenate([lane < HEAD, lane < HEAD], axis=1)
    for j in range(n_heads // 2):
        cols = slice(2 * j * LANES, 2 * (j + 1) * LANES)
        q_ref[:, cols] = (_dot(qn, wuq_ref[:, cols]) * q_mul2).astype(BF16)
        k_ref[:, cols] = (_dot(kvn, wuk_ref[:, cols]) + kpe2).astype(BF16)
        v_ref[:, cols] = jnp.where(is_v, _dot(kvn, wuv_ref[:, cols]), 1.0).astype(BF16)
    _conv_tile(p_conv, (pl.program_id(0) + 1) % per_b == 0, cw_ref, cb_ref, clw_ref, clb_ref, shift_ref, yb_ref, hbuf)


def _inproj(x, mod, g, w_in_all, layer, w_extra, cos, sin, q_norm, kv_norm, wuq, wuk, wuv, conv_params,
            seq, n_r, n_c, lora, n_heads, rope, scale):
    n, d = x.shape
    tm = 512
    per_b = seq // tm
    row = lambda w: pl.BlockSpec((tm, w), lambda i: (i, 0))
    full = lambda a: pl.BlockSpec(a.shape, lambda i: (0, 0))
    return pl.pallas_call(
        functools.partial(_inproj_kernel, n_r=n_r, n_c=n_c, lora=lora, n_heads=n_heads, rope=rope, scale=scale,
                          per_b=per_b),
        grid=(n // tm,),
        in_specs=[row(d),
                  pl.BlockSpec((1, 6, d), lambda i: (i // per_b, 0, 0)),
                  full(g),
                  pl.BlockSpec((None,) + w_in_all.shape[1:], lambda i: (layer, 0, 0)),
                  full(w_extra), row(LANES), row(LANES), full(q_norm), full(kv_norm), full(wuq), full(wuk), full(wuv)]
                 + [full(a) for a in conv_params],
        out_specs=[row(n_r), row(n_c // 2)] + [row(n_heads * LANES)] * 3,
        out_shape=[jax.ShapeDtypeStruct((n, n_r), BF16), jax.ShapeDtypeStruct((n, n_c // 2), BF16)]
                  + [jax.ShapeDtypeStruct((n, n_heads * LANES), BF16)] * 3,
        scratch_shapes=[pltpu.VMEM((tm + CONV_HALO, n_c // 2), F32)],
        compiler_params=_params("arbitrary"),
        name="in_proj",
    )(x, mod, g, w_in_all, w_extra, cos, sin, q_norm, kv_norm, wuq, wuk, wuv, *conv_params)


def _rwkv_kernel(p_ref, mu_ref, w0_ref, a0_ref, kk_ref, ka_ref, rk_ref, lnw_ref, lnb_ref, lora_ref, bd_ref,
                 o_ref, zbuf, st_ref, r_s, k_s, v_s, a_s, b_s, ld_s, y_s, *, tt, dim, w_lora, a_lora):
    n_pairs = dim // LANES
    c = CHUNK

    @pl.when(pl.program_id(1) == 0)
    def _():
        zbuf[0:8, :] = jnp.zeros((8, zbuf.shape[1]), F32)
        st_ref[...] = jnp.zeros(st_ref.shape, F32)

    p = p_ref[...].astype(F32)
    zbuf[8:8 + tt, :] = p
    prev = zbuf[7:7 + tt, :]
    zbuf[7:8, :] = zbuf[tt + 7:tt + 8, :]
    z = p + (prev - p) * mu_ref[...]

    r = z[:, 0:dim]
    k = z[:, dim:2 * dim]
    v = z[:, 2 * dim:3 * dim]
    lo = z[:, 3 * dim:3 * dim + LANES]
    lane = lax.broadcasted_iota(jnp.int32, lo.shape, 1)
    act = jnp.where(lane < w_lora, jnp.tanh(lo), jnp.where(lane < w_lora + a_lora, lo, _sigmoid(lo)))
    lora = _dot(_bf(act), lora_ref[...])
    x_w = w0_ref[...] + lora[:, 0:dim]
    w_log = -(jnp.maximum(-x_w, 0.0) + jnp.log(1.0 + jnp.exp(-jnp.abs(x_w)))) - 0.5
    a = _sigmoid(a0_ref[...] + lora[:, dim:2 * dim])
    g = lora[:, 2 * dim:3 * dim]
    bd = bd_ref[...]
    kk = k * kk_ref[...]
    kk = kk * lax.rsqrt(jnp.maximum(_head_sums(kk * kk, bd), 1e-24))
    k = k * (1.0 + (a - 1.0) * ka_ref[...])
    r_s[...] = r
    k_s[...] = k
    v_s[...] = v
    a_s[...] = -kk
    b_s[...] = kk * a
    ld_s[...] = -jnp.exp(w_log)

    row_c = lax.broadcasted_iota(jnp.int32, (c, c), 0)
    col_c = lax.broadcasted_iota(jnp.int32, (c, c), 1)
    tri_incl = (row_c >= col_c).astype(BF16)
    head0 = lax.broadcasted_iota(jnp.int32, (c, LANES), 1) < HEAD
    lane_2 = lax.broadcasted_iota(jnp.int32, (c, 2 * LANES), 1)
    head0_2 = jnp.where(lane_2 >= LANES, lane_2 - LANES, lane_2) < HEAD
    row_4 = lax.broadcasted_iota(jnp.int32, (c, 4 * c), 0)
    col_4 = jnp.bitwise_and(lax.broadcasted_iota(jnp.int32, (c, 4 * c), 1), c - 1)
    strict4 = row_4 > col_4
    incl4 = row_4 >= col_4
    eye4 = (row_4 == col_4).astype(F32)
    bd4 = jnp.bitwise_xor(lax.broadcasted_iota(jnp.int32, (4 * c, 4 * c), 0),
                          lax.broadcasted_iota(jnp.int32, (4 * c, 4 * c), 1)) < c
    row_p = lax.broadcasted_iota(jnp.int32, (LANES, LANES), 0)
    col_p = lax.broadcasted_iota(jnp.int32, (LANES, LANES), 1)
    same_head = (row_p < HEAD) == (col_p < HEAD)
    diag_p = row_p == col_p
    zeros_cl = jnp.zeros((c, LANES), BF16)

    def group_body(gi, carry):
        base = gi * (GROUP_CHUNKS * c)
        items = [(pl.ds(pl.multiple_of(base + q * c, c), c), slice(j * LANES, (j + 1) * LANES), j)
                 for q in range(GROUP_CHUNKS) for j in range(n_pairs)]
        n_it = len(items)
        loaded = [[ref[rows, lanes] for ref in (r_s, k_s, v_s, a_s, b_s, ld_s)] for rows, lanes, _ in items]
        r_c, k_c, v_c, a_c, b_c, ld_c = [list(t) for t in zip(*loaded)]
        cum = []
        for i in range(n_it):
            ld_hi, ld_lo = _split(ld_c[i])
            both = _dot(tri_incl, jnp.concatenate([ld_hi, ld_lo], axis=1))
            cum.append(both[:, 0:LANES] + both[:, LANES:])
        cum_end = [x[c - 1:c, :] for x in cum]
        at = [a_c[i] * jnp.exp(cum[i] - ld_c[i]) for i in range(n_it)]
        rt = [r_c[i] * jnp.exp(cum[i]) for i in range(n_it)]
        v_bf = [_bf(x) for x in v_c]
        def by_head(x, mask):
            return jnp.concatenate([jnp.where(mask, x, 0), jnp.where(mask, 0, x)], axis=0)

        aa = []
        for i in range(n_it):
            e_neg = jnp.exp(-cum[i])
            rhs = jnp.concatenate([by_head(_bf(b_c[i] * e_neg), head0), by_head(_bf(k_c[i] * e_neg), head0)], axis=0)
            aa.append(_dot_nt(_bf(jnp.concatenate([at[i], rt[i]], axis=0)), rhs))
        t_cat = [jnp.where(strict4, x[0:c, :], 0.0) for x in aa]
        q_cat = [_bf(jnp.where(incl4, x[c:2 * c, :], 0.0)) for x in aa]
        akv = [_dot(_bf(t_cat[i][:, LANES:]), by_head(v_bf[i], head0)) for i in range(n_it)]
        n_grp = n_it // 2

        def bdiag(x):
            return jnp.where(bd4, jnp.concatenate([x, x, x, x], axis=0), 0)

        p4 = [jnp.concatenate([t_cat[2 * g][:, 0:LANES], t_cat[2 * g + 1][:, 0:LANES]], axis=1) for g in range(n_grp)]
        inv4 = [eye4 + x for x in p4]
        p4_bf = [_bf(x) for x in p4]
        pw = [_bf(_dot(x, bdiag(x))) for x in p4_bf]
        for _ in range(4):
            both = [_dot(jnp.concatenate([pw[g], _bf(inv4[g])], axis=0), bdiag(pw[g])) for g in range(n_grp)]
            inv4 = [inv4[g] + both[g][c:, :] for g in range(n_grp)]
            pw = [_bf(both[g][0:c, :]) for g in range(n_grp)]
        inv4 = [_bf(inv4[g] + _dot(_bf(inv4[g]), bdiag(pw[g]))) for g in range(n_grp)]
        inv_cat = [inv4[i // 2][:, (i % 2) * LANES:(i % 2 + 1) * LANES] for i in range(n_it)]
        x_mat = [_bf(jnp.concatenate([at[i], akv[i]], axis=1)) for i in range(n_it)]
        tx = [_dot(inv_cat[i], by_head(x_mat[i], head0_2)) for i in range(n_it)]
        tx_bf = [_bf(x) for x in tx]
        v_pad = [jnp.concatenate([zeros_cl, v_bf[i]], axis=1) for i in range(n_it)]
        w_mat = [jnp.concatenate([tx_bf[i], v_pad[i]], axis=0) for i in range(n_it)]
        qw = [_dot(q_cat[i], jnp.concatenate([by_head(tx_bf[i], head0_2), by_head(v_pad[i], head0_2)], axis=0))
              for i in range(n_it)]
        mn = []
        for i in range(n_it):
            e_end = jnp.exp(cum_end[i] - cum[i])
            bk_t = _bf(jnp.concatenate([b_c[i] * e_end, k_c[i] * e_end], axis=0).T)
            mn.append(_dot(bk_t, w_mat[i]))
        g_col = [jnp.sum(jnp.where(diag_p, jnp.exp(cum_end[i]), 0.0), axis=1, keepdims=True) for i in range(n_it)]
        states = [st_ref[j] for j in range(n_pairs)]
        for i, (rows, lanes, j) in enumerate(items):
            st = states[j]
            st_bf = _bf(st)
            y_s[rows, lanes] = _dot(_bf(rt[i] + qw[i][:, 0:LANES]), st_bf) + qw[i][:, LANES:]
            m_low = _bf(jnp.where(same_head, mn[i][:, 0:LANES], 0.0))
            states[j] = g_col[i] * st + _dot(m_low, st_bf) + jnp.where(same_head, mn[i][:, LANES:], 0.0)
        for j in range(n_pairs):
            st_ref[j] = states[j]
        return carry

    lax.fori_loop(0, tt // (c * GROUP_CHUNKS), group_body, 0)

    y = y_s[...]
    inv_n = 1.0 / HEAD
    mean = _head_sums(y, bd) * inv_n
    d = y - mean
    var = _head_sums(d * d, bd) * inv_n
    y = d * lax.rsqrt(var + RWKV_GN_EPS) * lnw_ref[...] + lnb_ref[...]
    r = r_s[...]
    bonus = _head_sums(r * k_s[...] * rk_ref[...], bd) * v_s[...]
    o_ref[...] = ((y + bonus) * g).astype(o_ref.dtype)


def _rwkv(p_rwkv, vecs, lora_w, bd, batch, seq, dim, w_lora, a_lora):
    n, width = p_rwkv.shape
    tt = CHUNK * GROUP_CHUNKS
    n_t = seq // tt
    vec_spec = lambda a: pl.BlockSpec(a.shape, lambda b, t: (0, 0))
    act = pltpu.VMEM((tt, dim), F32)
    return pl.pallas_call(
        functools.partial(_rwkv_kernel, tt=tt, dim=dim, w_lora=w_lora, a_lora=a_lora),
        grid=(batch, n_t),
        in_specs=[pl.BlockSpec((tt, width), lambda b, t: (b * n_t + t, 0))]
                 + [vec_spec(a) for a in vecs] + [vec_spec(lora_w), vec_spec(bd)],
        out_specs=pl.BlockSpec((tt, dim), lambda b, t: (b * n_t + t, 0)),
        out_shape=jax.ShapeDtypeStruct((n, dim), BF16),
        scratch_shapes=[pltpu.VMEM((tt + 8, width), F32),
                        pltpu.VMEM((dim // LANES, LANES, LANES), F32),
                        act, act, act, act, act, act, act],
        compiler_params=_params("parallel", "arbitrary"),
        name="rwkv7",
    )(p_rwkv, *vecs, lora_w, bd)


def _conv_tile(p_conv, ends_sequence, w_ref, b_ref, lnw_ref, lnb_ref, shift_ref, o_ref, hbuf):
    tt, dim = o_ref.shape
    sub = CONV_SUB
    ext = sub + 8

    hbuf[CONV_HALO:CONV_HALO + tt, :] = p_conv[:, 0:dim] * _sigmoid(p_conv[:, dim:2 * dim])
    for s in range(tt // sub):
        base = CONV_HALO + s * sub - 8
        wins = [hbuf[base - 8 * a:base - 8 * a + ext, :] for a in range(CONV_HALO // 8)]
        parts = []
        for r in range(8):
            lags = [8 * a + r for a in range(len(wins)) if 8 * a + r < CONV_WIDTH]
            terms = [wins[m // 8] * w_ref[CONV_WIDTH - 1 - m:CONV_WIDTH - m, :] for m in lags]
            parts.append(functools.reduce(jnp.add, terms))
        stacked = _bf(jnp.concatenate(parts, axis=0))
        acc = _dot(shift_ref[...], stacked) + b_ref[...]
        mu = jnp.mean(acc, axis=-1, keepdims=True)
        d = acc - mu
        var = jnp.mean(d * d, axis=-1, keepdims=True)
        y = d * lax.rsqrt(var + LN_EPS) * lnw_ref[...] + lnb_ref[...]
        o_ref[s * sub:(s + 1) * sub, :] = (y * _sigmoid(y)).astype(o_ref.dtype)
    hbuf[0:CONV_HALO, :] = jnp.where(ends_sequence, 0.0, hbuf[tt:tt + CONV_HALO, :])


def _conv_shift_matrix():
    ext = CONV_SUB + 8
    u_idx = jnp.arange(CONV_SUB)[:, None]
    col = jnp.arange(8 * ext)[None, :]
    return (col % ext == u_idx + 8 - col // ext).astype(BF16)


def _flash_kernel(q_ref, k_ref, v_ref, o_ref, m_s, acc_s, s_buf, p_buf, al_buf):
    i = pl.program_id(2)
    t = FLASH_BLOCK
    m_s[...] = jnp.full(m_s.shape, -jnp.inf, F32)
    acc_s[...] = jnp.zeros(acc_s.shape, F32)
    p_buf[1] = jnp.zeros(p_buf.shape[1:], BF16)
    al_buf[1] = jnp.ones(al_buf.shape[1:], F32)
    causal = lax.broadcasted_iota(jnp.int32, (t, t), 0) <= lax.broadcasted_iota(jnp.int32, (t, t), 1)

    def keys_of(j):
        return pl.ds(pl.multiple_of(j * t, t), t)

    def scores(j, slot, u0):
        for h in range(2):
            cols = slice(h * LANES, (h + 1) * LANES)
            s_buf[slot, h, :, u0 * t:] = _dot_nt(k_ref[keys_of(j), cols], q_ref[u0 * t:, cols])

    def softmax(slot, u0, masked_sub):
        for u in range(u0, FLASH_SUBS):
            qs = slice(u * t, (u + 1) * t)
            for h in range(2):
                s = s_buf[slot, h, :, qs]
                if u == masked_sub:
                    s = jnp.where(causal, s, -jnp.inf)
                m_prev = m_s[h, :, qs]
                m_new = jnp.maximum(m_prev, jnp.max(s, axis=0, keepdims=True))
                m_s[h, :, qs] = m_new
                al_buf[slot, h, :, qs] = jnp.exp2(m_prev - m_new)
                p_buf[slot, h, :, qs] = jnp.exp2((s - m_new).astype(BF16))

    def accumulate(j, slot, u0):
        for h in range(2):
            v_blk = v_ref[keys_of(j), h * LANES:h * LANES + FLASH_VROWS]
            pv = lax.dot_general(v_blk, p_buf[slot, h, :, u0 * t:], (((0,), (0,)), ((), ())),
                                 preferred_element_type=F32)
            acc_s[h, :, u0 * t:] = al_buf[slot, h, :, u0 * t:] * acc_s[h, :, u0 * t:] + pv

    n_full = FLASH_SUBS * i
    scores(0, 0, 0)

    def body(g, carry):
        for par in range(2):
            k = 2 * g + par
            scores(k + 1, 1 - par, 0)
            accumulate(jnp.maximum(k - 1, 0), 1 - par, 0)
            softmax(par, 0, None)
        return carry

    lax.fori_loop(0, n_full // 2, body, 0)
    accumulate(jnp.maximum(n_full - 1, 0), 1, 0)
    for d in range(FLASH_SUBS):
        if d + 1 < FLASH_SUBS:
            scores(n_full + d + 1, (d + 1) % 2, d + 1)
        softmax(d % 2, d, d)
        accumulate(n_full + d, d % 2, d)
    out_t = jnp.concatenate([acc_s[h, 0:HEAD, :] / acc_s[h, HEAD:HEAD + 1, :] for h in range(2)], axis=0)
    o_ref[...] = out_t.T.astype(o_ref.dtype)


def _flash(q, k, v, batch, seq, n_heads):
    n = q.shape[0]
    tq = FLASH_SUBS * FLASH_BLOCK
    n_q = seq // tq
    return pl.pallas_call(
        _flash_kernel,
        grid=(batch, n_heads // 2, n_q),
        in_specs=[pl.BlockSpec((tq, 2 * LANES), lambda b, h, i: (b * n_q + i, h)),
                  pl.BlockSpec((seq, 2 * LANES), lambda b, h, i: (b, h)),
                  pl.BlockSpec((seq, 2 * LANES), lambda b, h, i: (b, h))],
        out_specs=pl.BlockSpec((tq, LANES), lambda b, h, i: (b * n_q + i, h)),
        out_shape=jax.ShapeDtypeStruct((n, n_heads * HEAD), BF16),
        scratch_shapes=[pltpu.VMEM((2, 1, tq), F32),
                        pltpu.VMEM((2, FLASH_VROWS, tq), F32),
                        pltpu.VMEM((2, 2, FLASH_BLOCK, tq), F32),
                        pltpu.VMEM((2, 2, FLASH_BLOCK, tq), BF16),
                        pltpu.VMEM((2, 2, 1, tq), F32)],
        compiler_params=_params("parallel", "parallel", "arbitrary"),
        name="mla_flash",
    )(q, k, v)


def _outffn_kernel(ya_ref, yb_ref, yc_ref, x_ref, mod_ref, gpm_ref, gpf_ref, gqf_ref, wo_ref, w1_ref, w2_ref,
                   o_ref, x1_s, hid_s, *, da, db, tf):
    half = x_ref.shape[0] // 2
    halves = [slice(0, half), slice(half, 2 * half)]
    ys = [_dot(ya_ref[r, :], wo_ref[0:da, :]) + _dot(yb_ref[r, :], wo_ref[da:da + db, :])
          + _dot(yc_ref[r, :], wo_ref[da + db:, :]) for r in halves]
    hs = []
    for r, y in zip(halves, ys):
        x1 = x_ref[r, :] + mod_ref[0, 2:3, :] * (_rms(y) * gpm_ref[...])
        x1_s[r, :] = x1
        h = _rms(x1) * gpf_ref[...]
        hs.append((h * (1.0 + mod_ref[0, 4:5, :]) + mod_ref[0, 3:4, :]).astype(BF16))
    for f in range(hid_s.shape[1] // tf):
        cols = slice(f * tf, (f + 1) * tf)
        for r, h in zip(halves, hs):
            hid = jnp.maximum(_dot(h, w1_ref[:, cols]), 0.0)
            hid_s[r, cols] = (hid * hid).astype(BF16)
    for r in halves:
        y2 = _dot(hid_s[r, :], w2_ref[...])
        o_ref[r, :] = x1_s[r, :] + mod_ref[0, 5:6, :] * (_rms(y2) * gqf_ref[...])


def _outffn(ya, yb, yc, x, mod, g_post_mix, g_pre_ffn, g_post_ffn, w_out, w_ff1, w_ff2, layer, seq):
    n, d = x.shape
    d_ff = w_ff1.shape[2]
    tm, tf = 512, 512
    per_b = seq // tm
    row = lambda w: pl.BlockSpec((tm, w), lambda i: (i, 0))
    vec = pl.BlockSpec((1, d), lambda i: (0, 0))
    resident = lambda a: pl.BlockSpec((None,) + a.shape[1:], lambda i: (layer, 0, 0), pipeline_mode=pl.Buffered(1))
    return pl.pallas_call(
        functools.partial(_outffn_kernel, da=ya.shape[1], db=yb.shape[1], tf=tf),
        grid=(n // tm,),
        in_specs=[row(ya.shape[1]), row(yb.shape[1]), row(yc.shape[1]), row(d),
                  pl.BlockSpec((1, 6, d), lambda i: (i // per_b, 0, 0)),
                  vec, vec, vec, resident(w_out), resident(w_ff1), resident(w_ff2)],
        out_specs=row(d),
        out_shape=jax.ShapeDtypeStruct((n, d), F32),
        scratch_shapes=[pltpu.VMEM((tm, d), F32), pltpu.VMEM((tm, d_ff), BF16)],
        compiler_params=_params("parallel"),
        name="out_ffn",
    )(ya, yb, yc, x, mod, g_post_mix, g_pre_ffn, g_post_ffn, w_out, w_ff1, w_ff2)


def _rot_cols(w):
    half = w.shape[-1] // 2
    return jnp.concatenate([-w[..., half:], w[..., :half]], axis=-1)


def kernel(x, c, positions, g_pre_mix, g_post_mix, g_pre_ffn, g_post_ffn, w_ada, b_ada, w_in, w_out, rwkv_mu, rwkv_w0, rwkv_w2, rwkv_a0, rwkv_a2, rwkv_g2, rwkv_k_k, rwkv_k_a, rwkv_r_k, rwkv_ln_w, rwkv_ln_b, conv_w, conv_b, conv_ln_w, conv_ln_b, mla_q_norm, mla_w_uq, mla_kv_norm, mla_w_ukv, w_ff1, w_ff2):
    batch, seq, d = x.shape
    n = batch * seq
    n_layers = w_ada.shape[0]
    r_dim = rwkv_w0.shape[1]
    w_lora, a_lora, g_lora = rwkv_w2.shape[1], rwkv_a2.shape[1], rwkv_g2.shape[1]
    assert w_lora + a_lora + g_lora == LANES and r_dim % LANES == 0
    n_r = 3 * r_dim + LANES
    c_dim = conv_w.shape[2]
    n_c = 2 * c_dim
    lora = mla_q_norm.shape[1]
    n_heads = mla_w_ukv.shape[2] // (2 * HEAD)
    rope = mla_w_uq.shape[2] // n_heads - HEAD
    assert HEAD + 2 * rope == LANES
    scale = float(HEAD + rope) ** -0.5 * LOG2_E

    xf = x.reshape(n, d)
    c8 = jnp.pad(c, ((0, 8 - batch), (0, 0)))
    mod_all = _ada(c8, w_ada, b_ada.reshape(n_layers, 1, 6 * d))[:, :batch].reshape(n_layers, batch, 6, d)

    inv_freq = 1.0 / (ROPE_THETA ** (jnp.arange(0, rope, 2, dtype=F32) / rope))
    invf = jnp.tile(inv_freq, LANES // inv_freq.shape[0]).reshape(1, LANES)
    cos, sin = _rope_tables(positions.reshape(n, 1), invf)

    head_id = jnp.arange(2 * LANES) // HEAD
    bd = (head_id[:, None] == head_id[None, :]).astype(BF16)
    row1 = lambda a: a.reshape(1, -1)
    conv_shift = _conv_shift_matrix()

    w_in_bf, w_out_bf, w_ff1_bf, w_ff2_bf = [w.astype(BF16) for w in (w_in, w_out, w_ff1, w_ff2)]
    for l in range(n_layers):
        k_pe = w_in_bf[l, :, n_r + n_c + 2 * lora:]
        z_half = jnp.zeros((d, HEAD), BF16)
        w_extra = jnp.concatenate([z_half, k_pe, k_pe, z_half, _rot_cols(k_pe), _rot_cols(k_pe)], axis=1)
        wq = mla_w_uq[l].reshape(lora, n_heads, HEAD + rope)
        wuq = jnp.concatenate([wq, _rot_cols(wq[..., HEAD:])], axis=-1).reshape(lora, n_heads * LANES).astype(BF16)
        wkv = mla_w_ukv[l].reshape(lora, n_heads, 2 * HEAD)
        wuk = jnp.concatenate([wkv[..., :HEAD], jnp.zeros_like(wkv[..., :HEAD])], axis=-1)
        wuk = wuk.reshape(lora, n_heads * LANES).astype(BF16)
        wuv = jnp.concatenate([wkv[..., HEAD:], jnp.zeros_like(wkv[..., HEAD:])], axis=-1)
        wuv = wuv.reshape(lora, n_heads * LANES).astype(BF16)
        lora_w = jnp.zeros((LANES, 3 * r_dim), F32)
        lora_w = lora_w.at[0:w_lora, 0:r_dim].set(rwkv_w2[l])
        lora_w = lora_w.at[w_lora:w_lora + a_lora, r_dim:2 * r_dim].set(rwkv_a2[l])
        lora_w = lora_w.at[w_lora + a_lora:, 2 * r_dim:].set(rwkv_g2[l]).astype(BF16)

        mod = mod_all[l]
        conv_params = [conv_w[l], row1(conv_b[l]), row1(conv_ln_w[l]), row1(conv_ln_b[l]), conv_shift]
        p_rwkv, y_b, q, k, v = _inproj(xf, mod, row1(g_pre_mix[l]), w_in_bf, l, w_extra, cos, sin,
                                       row1(mla_q_norm[l]), row1(mla_kv_norm[l]), wuq, wuk, wuv, conv_params,
                                       seq, n_r, n_c, lora, n_heads, rope, scale)
        vecs = [row1(rwkv_mu[l]), row1(rwkv_w0[l]), row1(rwkv_a0[l]), row1(rwkv_k_k[l]), row1(rwkv_k_a[l]),
                row1(rwkv_r_k[l]), row1(rwkv_ln_w[l]), row1(rwkv_ln_b[l])]
        y_a = _rwkv(p_rwkv, vecs, lora_w, bd, batch, seq, r_dim, w_lora, a_lora)
        y_c = _flash(q, k, v, batch, seq, n_heads)
        xf = _outffn(y_a, y_b, y_c, xf, mod, row1(g_post_mix[l]), row1(g_pre_ffn[l]), row1(g_post_ffn[l]),
                     w_out_bf, w_ff1_bf, w_ff2_bf, l, seq)
    return xf.reshape(batch, seq, d)
```

```python
import functools
import itertools

import jax
import jax.numpy as jnp
from jax import lax
from jax.experimental import pallas as pl
from jax.experimental.pallas import tpu as pltpu

F32 = jnp.float32
BF16 = jnp.bfloat16

LANES = 128
HEAD = 64
NORM_EPS = 1e-6
LN_EPS = 1e-5
RWKV_GN_EPS = 64e-5
ROPE_THETA = 10000.0
CONV_WIDTH = 31
CONV_HALO = 32
CONV_SUB = 128
CHUNK = 64
FLASH_BLOCK = 256
FLASH_VROWS = HEAD + 16
FLASH_SUBS = 8
GROUP_CHUNKS = 8
VMEM_LIMIT = 56 * 1024 * 1024
LOG2_E = 1.4426950408889634


def _dot(a, b):
    return jnp.dot(a, b, preferred_element_type=F32)


def _dot_nt(a, b):
    return lax.dot_general(a, b, (((1,), (1,)), ((), ())), preferred_element_type=F32)


def _bf(x):
    return x.astype(BF16)


def _split(x):
    hi = x.astype(BF16)
    return hi, (x - hi.astype(F32)).astype(BF16)


def _head_sums(x, ones_bd):
    width = x.shape[1]
    groups = [_dot(_bf(x[:, lo:lo + 2 * LANES]), ones_bd) for lo in range(0, width - 2 * LANES + 1, 2 * LANES)]
    if width % (2 * LANES):
        groups.append(_dot(_bf(x[:, width - LANES:]), ones_bd[0:LANES, 0:LANES]))
    return jnp.concatenate(groups, axis=1)


def _rms(x):
    return x * lax.rsqrt(jnp.mean(x * x, axis=-1, keepdims=True) + NORM_EPS)


def _sigmoid(x):
    return 1.0 / (1.0 + jnp.exp(-x))


def _params(*semantics):
    return pltpu.CompilerParams(dimension_semantics=semantics, vmem_limit_bytes=VMEM_LIMIT)


def _ada_kernel(c_ref, w_ref, b_ref, o_ref):
    c = c_ref[...]
    cs = c * _sigmoid(c)
    cs_hi, cs_lo = _split(cs)
    w_hi, w_lo = _split(w_ref[0])
    o_ref[0] = _dot(cs_hi, w_hi) + _dot(cs_hi, w_lo) + _dot(cs_lo, w_hi) + b_ref[0]


def _ada(c8, w_ada, b_ada):
    n_layers, d, d6 = w_ada.shape
    tn = 1536
    return pl.pallas_call(
        _ada_kernel,
        grid=(n_layers, d6 // tn),
        in_specs=[pl.BlockSpec((8, d), lambda l, j: (0, 0)),
                  pl.BlockSpec((1, d, tn), lambda l, j: (l, 0, j)),
                  pl.BlockSpec((1, 1, tn), lambda l, j: (l, 0, j))],
        out_specs=pl.BlockSpec((1, 8, tn), lambda l, j: (l, 0, j)),
        out_shape=jax.ShapeDtypeStruct((n_layers, 8, d6), F32),
        compiler_params=_params("parallel", "parallel"),
        name="ada_mod",
    )(c8, w_ada, b_ada)


def _rope_kernel(pos_ref, invf_ref, phase_ref, cs_ref):
    cs_ref[...] = jnp.cos(pos_ref[...].astype(F32) * invf_ref[...] + phase_ref[...])


def _rope_table(pos_col, invf, phase):
    n = pos_col.shape[0]
    tm = 2048
    vec = pl.BlockSpec((1, LANES), lambda i: (0, 0))
    return pl.pallas_call(
        _rope_kernel,
        grid=(n // tm,),
        in_specs=[pl.BlockSpec((tm, 1), lambda i: (i, 0)), vec, vec],
        out_specs=pl.BlockSpec((tm, LANES), lambda i: (i, 0)),
        out_shape=jax.ShapeDtypeStruct((n, LANES), F32),
        compiler_params=_params("parallel"),
        name="rope_table",
    )(pos_col, invf, phase)


def _inproj_kernel(x_ref, mod_ref, g_ref, w_ref, wx_ref, cs_ref, qn_ref, kvn_ref, wuq_ref, wuk_ref, wuv_ref,
                   cw_ref, cb_ref, clw_ref, clb_ref, shift_ref, *rest,
                   n_r, n_c, lora, n_heads, rope, scale, per_b, w_lora, a_lora):
    rwkv_in = rest[:10]
    ya_ref, yb_ref, q_ref, k_ref, v_ref, hbuf, zbuf, st_ref = rest[10:18]
    rwkv_scratch = rest[18:]

    @pl.when(pl.program_id(0) == 0)
    def _():
        hbuf[0:CONV_HALO, :] = jnp.zeros((CONV_HALO, hbuf.shape[1]), F32)
        zbuf[0:8, :] = jnp.zeros((8, zbuf.shape[1]), F32)
        st_ref[...] = jnp.zeros(st_ref.shape, F32)

    ends_sequence = (pl.program_id(0) + 1) % per_b == 0

    h = _rms(x_ref[...]) * g_ref[...]
    h = (h * (1.0 + mod_ref[0, 1:2, :]) + mod_ref[0, 0:1, :]).astype(BF16)
    p_rwkv = _dot(h, w_ref[:, :n_r])
    p_conv = _dot(h, w_ref[:, n_r:n_r + n_c])

    def mla_steps():
        lat = _dot(h, w_ref[:, n_r + n_c:n_r + n_c + 2 * lora])
        k_rot = _dot(h, wx_ref[...])
        cs = cs_ref[...]
        qn = (_rms(lat[:, 0:lora]) * qn_ref[...]).astype(BF16)
        kvn = (_rms(lat[:, lora:]) * kvn_ref[...]).astype(BF16)
        lane = lax.broadcasted_iota(jnp.int32, cs.shape, 1)
        t = k_rot * cs
        kpe = jnp.where(lane < HEAD, 0.0, t + pltpu.roll(t, rope, 1) + pltpu.roll(t, LANES - rope, 1))
        q_mul = jnp.where(lane < HEAD, 1.0, cs) * scale
        q_mul2 = jnp.concatenate([q_mul, q_mul], axis=1)
        kpe2 = jnp.concatenate([kpe, kpe], axis=1)
        is_v = jnp.concatenate([lane < HEAD, lane < HEAD], axis=1)
        yield
        for j in range(n_heads // 2):
            cols = slice(2 * j * LANES, 2 * (j + 1) * LANES)
            q_ref[:, cols] = (_dot(qn, wuq_ref[:, cols]) * q_mul2).astype(BF16)
            k_ref[:, cols] = (_dot(kvn, wuk_ref[:, cols]) + kpe2).astype(BF16)
            v_ref[:, cols] = jnp.where(is_v, _dot(kvn, wuv_ref[:, cols]), 1.0).astype(BF16)
            yield

    fill = itertools.chain(mla_steps(),
                           _conv_steps(p_conv, ends_sequence, cw_ref, cb_ref, clw_ref, clb_ref, shift_ref, yb_ref, hbuf))
    _rwkv_tile(p_rwkv, ends_sequence, *rwkv_in, ya_ref, zbuf, st_ref, *rwkv_scratch,
               w_lora=w_lora, a_lora=a_lora, fill=fill)
    for _ in fill:
        pass


def _inproj(x, mod, g, w_in_all, layer, w_extra, cs, q_norm, kv_norm, wuq, wuk, wuv, conv_params, rwkv_params,
            seq, n_r, n_c, lora, n_heads, rope, scale, r_dim, w_lora, a_lora):
    n, d = x.shape
    tm = CHUNK * GROUP_CHUNKS
    per_b = seq // tm
    row = lambda w: pl.BlockSpec((tm, w), lambda i: (i, 0))
    full = lambda a: pl.BlockSpec(a.shape, lambda i: (0, 0))
    act = pltpu.VMEM((tm, r_dim), F32)
    return pl.pallas_call(
        functools.partial(_inproj_kernel, n_r=n_r, n_c=n_c, lora=lora, n_heads=n_heads, rope=rope, scale=scale,
                          per_b=per_b, w_lora=w_lora, a_lora=a_lora),
        grid=(n // tm,),
        in_specs=[row(d),
                  pl.BlockSpec((1, 6, d), lambda i: (i // per_b, 0, 0)),
                  full(g),
                  pl.BlockSpec((None,) + w_in_all.shape[1:], lambda i: (layer, 0, 0)),
                  full(w_extra), row(LANES), full(q_norm), full(kv_norm), full(wuq), full(wuk), full(wuv)]
                 + [full(a) for a in conv_params] + [full(a) for a in rwkv_params],
        out_specs=[row(r_dim), row(n_c // 2)] + [row(n_heads * LANES)] * 3,
        out_shape=[jax.ShapeDtypeStruct((n, r_dim), BF16), jax.ShapeDtypeStruct((n, n_c // 2), BF16)]
                  + [jax.ShapeDtypeStruct((n, n_heads * LANES), BF16)] * 3,
        scratch_shapes=[pltpu.VMEM((tm + CONV_HALO, n_c // 2), F32),
                        pltpu.VMEM((tm + 8, n_r), F32),
                        pltpu.VMEM((r_dim // LANES, LANES, LANES), F32),
                        act, act, act, act, act, act, act],
        compiler_params=_params("arbitrary"),
        name="in_proj",
    )(x, mod, g, w_in_all, w_extra, cs, q_norm, kv_norm, wuq, wuk, wuv, *conv_params, *rwkv_params)


def _rwkv_tile(p, ends_sequence, mu_ref, w0_ref, a0_ref, kk_ref, ka_ref, rk_ref, lnw_ref, lnb_ref, lora_ref, bd_ref,
               o_ref, zbuf, st_ref, r_s, k_s, v_s, a_s, b_s, ld_s, y_s, *, w_lora, a_lora, fill):
    tt, dim = o_ref.shape
    assert tt == CHUNK * GROUP_CHUNKS

    def gap():
        next(fill, None)

    n_pairs = dim // LANES
    c = CHUNK

    zbuf[8:8 + tt, :] = p
    prev = zbuf[7:7 + tt, :]
    zbuf[7:8, :] = jnp.where(ends_sequence, 0.0, zbuf[tt + 7:tt + 8, :])
    z = p + (prev - p) * mu_ref[...]

    r = z[:, 0:dim]
    k = z[:, dim:2 * dim]
    v = z[:, 2 * dim:3 * dim]
    lo = z[:, 3 * dim:3 * dim + LANES]
    lane = lax.broadcasted_iota(jnp.int32, lo.shape, 1)
    act = jnp.where(lane < w_lora, jnp.tanh(lo), jnp.where(lane < w_lora + a_lora, lo, _sigmoid(lo)))
    lora = _dot(_bf(act), lora_ref[...])
    x_w = w0_ref[...] + lora[:, 0:dim]
    w_log = -(jnp.maximum(-x_w, 0.0) + jnp.log(1.0 + jnp.exp(-jnp.abs(x_w)))) - 0.5
    a = _sigmoid(a0_ref[...] + lora[:, dim:2 * dim])
    g = lora[:, 2 * dim:3 * dim]
    bd = bd_ref[...]
    kk = k * kk_ref[...]
    kk = kk * lax.rsqrt(jnp.maximum(_head_sums(kk * kk, bd), 1e-24))
    k = k * (1.0 + (a - 1.0) * ka_ref[...])
    r_s[...] = r
    k_s[...] = k
    v_s[...] = v
    a_s[...] = -kk
    b_s[...] = kk * a
    ld_s[...] = -jnp.exp(w_log)

    row_c = lax.broadcasted_iota(jnp.int32, (c, c), 0)
    col_c = lax.broadcasted_iota(jnp.int32, (c, c), 1)
    tri_incl = (row_c >= col_c).astype(BF16)
    head0 = lax.broadcasted_iota(jnp.int32, (c, LANES), 1) < HEAD
    lane_2 = lax.broadcasted_iota(jnp.int32, (c, 2 * LANES), 1)
    head0_2 = jnp.where(lane_2 >= LANES, lane_2 - LANES, lane_2) < HEAD
    row_4 = lax.broadcasted_iota(jnp.int32, (c, 4 * c), 0)
    col_4 = jnp.bitwise_and(lax.broadcasted_iota(jnp.int32, (c, 4 * c), 1), c - 1)
    strict4 = row_4 > col_4
    incl4 = row_4 >= col_4
    eye4 = (row_4 == col_4).astype(F32)
    bd4 = jnp.bitwise_xor(lax.broadcasted_iota(jnp.int32, (4 * c, 4 * c), 0),
                          lax.broadcasted_iota(jnp.int32, (4 * c, 4 * c), 1)) < c
    row_p = lax.broadcasted_iota(jnp.int32, (LANES, LANES), 0)
    col_p = lax.broadcasted_iota(jnp.int32, (LANES, LANES), 1)
    same_head = (row_p < HEAD) == (col_p < HEAD)
    diag_p = row_p == col_p
    zeros_cl = jnp.zeros((c, LANES), BF16)

    def group_body():
        items = [(slice(q * c, (q + 1) * c), slice(j * LANES, (j + 1) * LANES), j)
                 for q in range(GROUP_CHUNKS) for j in range(n_pairs)]
        n_it = len(items)
        loaded = [[ref[rows, lanes] for ref in (r_s, k_s, v_s, a_s, b_s, ld_s)] for rows, lanes, _ in items]
        r_c, k_c, v_c, a_c, b_c, ld_c = [list(t) for t in zip(*loaded)]
        cum = []
        for i in range(n_it):
            ld_hi, ld_lo = _split(ld_c[i])
            both = _dot(tri_incl, jnp.concatenate([ld_hi, ld_lo], axis=1))
            cum.append(both[:, 0:LANES] + both[:, LANES:])
        gap()
        cum_end = [x[c - 1:c, :] for x in cum]
        at = [a_c[i] * jnp.exp(cum[i] - ld_c[i]) for i in range(n_it)]
        rt = [r_c[i] * jnp.exp(cum[i]) for i in range(n_it)]
        v_bf = [_bf(x) for x in v_c]
        def by_head(x, mask):
            return jnp.concatenate([jnp.where(mask, x, 0), jnp.where(mask, 0, x)], axis=0)

        aa = []
        for i in range(n_it):
            e_neg = jnp.exp(-cum[i])
            rhs = jnp.concatenate([by_head(_bf(b_c[i] * e_neg), head0), by_head(_bf(k_c[i] * e_neg), head0)], axis=0)
            aa.append(_dot_nt(_bf(jnp.concatenate([at[i], rt[i]], axis=0)), rhs))
        gap()
        t_cat = [jnp.where(strict4, x[0:c, :], 0.0) for x in aa]
        q_cat = [_bf(jnp.where(incl4, x[c:2 * c, :], 0.0)) for x in aa]
        akv = [_dot(_bf(t_cat[i][:, LANES:]), by_head(v_bf[i], head0)) for i in range(n_it)]
        n_grp = n_it // 2

        def bdiag(x):
            return jnp.where(bd4, jnp.concatenate([x, x, x, x], axis=0), 0)

        p4 = [jnp.concatenate([t_cat[2 * g][:, 0:LANES], t_cat[2 * g + 1][:, 0:LANES]], axis=1) for g in range(n_grp)]
        inv4 = [eye4 + x for x in p4]
        p4_bf = [_bf(x) for x in p4]
        pw = [_bf(_dot(x, bdiag(x))) for x in p4_bf]
        gap()
        for _ in range(4):
            both = [_dot(jnp.concatenate([pw[g], _bf(inv4[g])], axis=0), bdiag(pw[g])) for g in range(n_grp)]
            inv4 = [inv4[g] + both[g][c:, :] for g in range(n_grp)]
            pw = [_bf(both[g][0:c, :]) for g in range(n_grp)]
            gap()
        inv4 = [_bf(inv4[g] + _dot(_bf(inv4[g]), bdiag(pw[g]))) for g in range(n_grp)]
        inv_cat = [inv4[i // 2][:, (i % 2) * LANES:(i % 2 + 1) * LANES] for i in range(n_it)]
        x_mat = [_bf(jnp.concatenate([at[i], akv[i]], axis=1)) for i in range(n_it)]
        tx = [_dot(inv_cat[i], by_head(x_mat[i], head0_2)) for i in range(n_it)]
        gap()
        tx_bf = [_bf(x) for x in tx]
        v_pad = [jnp.concatenate([zeros_cl, v_bf[i]], axis=1) for i in range(n_it)]
        w_mat = [jnp.concatenate([tx_bf[i], v_pad[i]], axis=0) for i in range(n_it)]
        qw = [_dot(q_cat[i], jnp.concatenate([by_head(tx_bf[i], head0_2), by_head(v_pad[i], head0_2)], axis=0))
              for i in range(n_it)]
        mn = []
        for i in range(n_it):
            e_end = jnp.exp(cum_end[i] - cum[i])
            bk_t = _bf(jnp.concatenate([b_c[i] * e_end, k_c[i] * e_end], axis=0).T)
            mn.append(_dot(bk_t, w_mat[i]))
        gap()
        g_col = [jnp.sum(jnp.where(diag_p, jnp.exp(cum_end[i]), 0.0), axis=1, keepdims=True) for i in range(n_it)]
        states = [st_ref[j] for j in range(n_pairs)]
        for i, (rows, lanes, j) in enumerate(items):
            st = states[j]
            st_bf = _bf(st)
            y_s[rows, lanes] = _dot(_bf(rt[i] + qw[i][:, 0:LANES]), st_bf) + qw[i][:, LANES:]
            m_low = _bf(jnp.where(same_head, mn[i][:, 0:LANES], 0.0))
            states[j] = g_col[i] * st + _dot(m_low, st_bf) + jnp.where(same_head, mn[i][:, LANES:], 0.0)
        for j in range(n_pairs):
            st_ref[j] = jnp.where(ends_sequence, 0.0, states[j])

    group_body()

    y = y_s[...]
    inv_n = 1.0 / HEAD
    mean = _head_sums(y, bd) * inv_n
    d = y - mean
    var = _head_sums(d * d, bd) * inv_n
    y = d * lax.rsqrt(var + RWKV_GN_EPS) * lnw_ref[...] + lnb_ref[...]
    r = r_s[...]
    bonus = _head_sums(r * k_s[...] * rk_ref[...], bd) * v_s[...]
    o_ref[...] = ((y + bonus) * g).astype(o_ref.dtype)


def _conv_steps(p_conv, ends_sequence, w_ref, b_ref, lnw_ref, lnb_ref, shift_ref, o_ref, hbuf):
    tt, dim = o_ref.shape
    sub = CONV_SUB
    ext = sub + 8

    hbuf[CONV_HALO:CONV_HALO + tt, :] = p_conv[:, 0:dim] * _sigmoid(p_conv[:, dim:2 * dim])
    yield
    for s in range(tt // sub):
        base = CONV_HALO + s * sub - 8
        wins = [hbuf[base - 8 * a:base - 8 * a + ext, :] for a in range(CONV_HALO // 8)]
        parts = []
        for r in range(8):
            lags = [8 * a + r for a in range(len(wins)) if 8 * a + r < CONV_WIDTH]
            terms = [wins[m // 8] * w_ref[CONV_WIDTH - 1 - m:CONV_WIDTH - m, :] for m in lags]
            parts.append(functools.reduce(jnp.add, terms))
        stacked = _bf(jnp.concatenate(parts, axis=0))
        acc = _dot(shift_ref[...], stacked) + b_ref[...]
        mu = jnp.mean(acc, axis=-1, keepdims=True)
        d = acc - mu
        var = jnp.mean(d * d, axis=-1, keepdims=True)
        y = d * lax.rsqrt(var + LN_EPS) * lnw_ref[...] + lnb_ref[...]
        o_ref[s * sub:(s + 1) * sub, :] = (y * _sigmoid(y)).astype(o_ref.dtype)
        if s + 1 < tt // sub:
            yield
    hbuf[0:CONV_HALO, :] = jnp.where(ends_sequence, 0.0, hbuf[tt:tt + CONV_HALO, :])


def _conv_shift_matrix():
    ext = CONV_SUB + 8
    u_idx = jnp.arange(CONV_SUB)[:, None]
    col = jnp.arange(8 * ext)[None, :]
    return (col % ext == u_idx + 8 - col // ext).astype(BF16)


def _flash_kernel(q_ref, k_ref, v_ref, o_ref, m_s, acc_s, s_buf, p_buf, al_buf):
    i = pl.program_id(2)
    t = FLASH_BLOCK
    m_s[...] = jnp.full(m_s.shape, -jnp.inf, F32)
    acc_s[...] = jnp.zeros(acc_s.shape, F32)
    p_buf[1] = jnp.zeros(p_buf.shape[1:], BF16)
    al_buf[1] = jnp.ones(al_buf.shape[1:], F32)
    causal = lax.broadcasted_iota(jnp.int32, (t, t), 0) <= lax.broadcasted_iota(jnp.int32, (t, t), 1)

    def keys_of(j):
        return pl.ds(pl.multiple_of(j * t, t), t)

    def scores(j, slot, u0):
        for h in range(2):
            cols = slice(h * LANES, (h + 1) * LANES)
            s_buf[slot, h, :, u0 * t:] = _dot_nt(k_ref[keys_of(j), cols], q_ref[u0 * t:, cols])

    def softmax(slot, u0, masked_sub):
        for u in range(u0, FLASH_SUBS):
            qs = slice(u * t, (u + 1) * t)
            for h in range(2):
                s = s_buf[slot, h, :, qs]
                if u == masked_sub:
                    s = jnp.where(causal, s, -jnp.inf)
                m_prev = m_s[h, :, qs]
                m_new = jnp.maximum(m_prev, jnp.max(s, axis=0, keepdims=True))
                m_s[h, :, qs] = m_new
                al_buf[slot, h, :, qs] = jnp.exp2(m_prev - m_new)
                p_buf[slot, h, :, qs] = jnp.exp2((s - m_new).astype(BF16))

    def accumulate(j, slot, u0):
        for h in range(2):
            v_blk = v_ref[keys_of(j), h * LANES:h * LANES + FLASH_VROWS]
            pv = lax.dot_general(v_blk, p_buf[slot, h, :, u0 * t:], (((0,), (0,)), ((), ())),
                                 preferred_element_type=F32)
            acc_s[h, :, u0 * t:] = al_buf[slot, h, :, u0 * t:] * acc_s[h, :, u0 * t:] + pv

    n_full = FLASH_SUBS * i
    scores(0, 0, 0)

    def body(g, carry):
        for par in range(2):
            k = 2 * g + par
            scores(k + 1, 1 - par, 0)
            accumulate(jnp.maximum(k - 1, 0), 1 - par, 0)
            softmax(par, 0, None)
        return carry

    lax.fori_loop(0, n_full // 2, body, 0)
    accumulate(jnp.maximum(n_full - 1, 0), 1, 0)
    for d in range(FLASH_SUBS):
        if d + 1 < FLASH_SUBS:
            scores(n_full + d + 1, (d + 1) % 2, d + 1)
        softmax(d % 2, d, d)
        accumulate(n_full + d, d % 2, d)
    out_t = jnp.concatenate([acc_s[h, 0:HEAD, :] / acc_s[h, HEAD:HEAD + 1, :] for h in range(2)], axis=0)
    o_ref[...] = out_t.T.astype(o_ref.dtype)


def _flash(q, k, v, batch, seq, n_heads):
    n = q.shape[0]
    tq = FLASH_SUBS * FLASH_BLOCK
    n_q = seq // tq
    return pl.pallas_call(
        _flash_kernel,
        grid=(batch, n_heads // 2, n_q),
        in_specs=[pl.BlockSpec((tq, 2 * LANES), lambda b, h, i: (b * n_q + i, h)),
                  pl.BlockSpec((seq, 2 * LANES), lambda b, h, i: (b, h)),
                  pl.BlockSpec((seq, 2 * LANES), lambda b, h, i: (b, h))],
        out_specs=pl.BlockSpec((tq, LANES), lambda b, h, i: (b * n_q + i, h)),
        out_shape=jax.ShapeDtypeStruct((n, n_heads * HEAD), BF16),
        scratch_shapes=[pltpu.VMEM((2, 1, tq), F32),
                        pltpu.VMEM((2, FLASH_VROWS, tq), F32),
                        pltpu.VMEM((2, 2, FLASH_BLOCK, tq), F32),
                        pltpu.VMEM((2, 2, FLASH_BLOCK, tq), BF16),
                        pltpu.VMEM((2, 2, 1, tq), F32)],
        compiler_params=_params("parallel", "parallel", "arbitrary"),
        name="mla_flash",
    )(q, k, v)


def _outffn_kernel(ya_ref, yb_ref, yc_ref, x_ref, mod_ref, gpm_ref, gpf_ref, gqf_ref, wo_ref, w1_ref, w2_ref,
                   o_ref, x1_s, hid_s, *, da, db, tf):
    half = x_ref.shape[0] // 2
    halves = [slice(0, half), slice(half, 2 * half)]
    ys = [_dot(ya_ref[r, :], wo_ref[0:da, :]) + _dot(yb_ref[r, :], wo_ref[da:da + db, :])
          + _dot(yc_ref[r, :], wo_ref[da + db:, :]) for r in halves]
    hs = []
    for r, y in zip(halves, ys):
        x1 = x_ref[r, :] + mod_ref[0, 2:3, :] * (_rms(y) * gpm_ref[...])
        x1_s[r, :] = x1
        h = _rms(x1) * gpf_ref[...]
        hs.append((h * (1.0 + mod_ref[0, 4:5, :]) + mod_ref[0, 3:4, :]).astype(BF16))
    for f in range(hid_s.shape[1] // tf):
        cols = slice(f * tf, (f + 1) * tf)
        for r, h in zip(halves, hs):
            hid = jnp.maximum(_dot(h, w1_ref[:, cols]), 0.0)
            hid_s[r, cols] = (hid * hid).astype(BF16)
    for r in halves:
        y2 = _dot(hid_s[r, :], w2_ref[...])
        o_ref[r, :] = x1_s[r, :] + mod_ref[0, 5:6, :] * (_rms(y2) * gqf_ref[...])


def _outffn(ya, yb, yc, x, mod, g_post_mix, g_pre_ffn, g_post_ffn, w_out, w_ff1, w_ff2, layer, seq):
    n, d = x.shape
    d_ff = w_ff1.shape[2]
    tm, tf = 512, 512
    per_b = seq // tm
    row = lambda w: pl.BlockSpec((tm, w), lambda i: (i, 0))
    vec = pl.BlockSpec((1, d), lambda i: (0, 0))
    resident = lambda a: pl.BlockSpec((None,) + a.shape[1:], lambda i: (layer, 0, 0), pipeline_mode=pl.Buffered(1))
    return pl.pallas_call(
        functools.partial(_outffn_kernel, da=ya.shape[1], db=yb.shape[1], tf=tf),
        grid=(n // tm,),
        in_specs=[row(ya.shape[1]), row(yb.shape[1]), row(yc.shape[1]), row(d),
                  pl.BlockSpec((1, 6, d), lambda i: (i // per_b, 0, 0)),
                  vec, vec, vec, resident(w_out), resident(w_ff1), resident(w_ff2)],
        out_specs=row(d),
        out_shape=jax.ShapeDtypeStruct((n, d), F32),
        scratch_shapes=[pltpu.VMEM((tm, d), F32), pltpu.VMEM((tm, d_ff), BF16)],
        compiler_params=_params("parallel"),
        name="out_ffn",
    )(ya, yb, yc, x, mod, g_post_mix, g_pre_ffn, g_post_ffn, w_out, w_ff1, w_ff2)


def _rot_cols(w):
    half = w.shape[-1] // 2
    return jnp.concatenate([-w[..., half:], w[..., :half]], axis=-1)


def kernel(x, c, positions, g_pre_mix, g_post_mix, g_pre_ffn, g_post_ffn, w_ada, b_ada, w_in, w_out, rwkv_mu, rwkv_w0, rwkv_w2, rwkv_a0, rwkv_a2, rwkv_g2, rwkv_k_k, rwkv_k_a, rwkv_r_k, rwkv_ln_w, rwkv_ln_b, conv_w, conv_b, conv_ln_w, conv_ln_b, mla_q_norm, mla_w_uq, mla_kv_norm, mla_w_ukv, w_ff1, w_ff2):
    batch, seq, d = x.shape
    n = batch * seq
    n_layers = w_ada.shape[0]
    r_dim = rwkv_w0.shape[1]
    w_lora, a_lora, g_lora = rwkv_w2.shape[1], rwkv_a2.shape[1], rwkv_g2.shape[1]
    assert w_lora + a_lora + g_lora == LANES and r_dim % LANES == 0
    n_r = 3 * r_dim + LANES
    c_dim = conv_w.shape[2]
    n_c = 2 * c_dim
    lora = mla_q_norm.shape[1]
    n_heads = mla_w_ukv.shape[2] // (2 * HEAD)
    rope = mla_w_uq.shape[2] // n_heads - HEAD
    assert HEAD + 2 * rope == LANES
    scale = float(HEAD + rope) ** -0.5 * LOG2_E

    xf = x.reshape(n, d)
    c8 = jnp.pad(c, ((0, 8 - batch), (0, 0)))
    mod_all = _ada(c8, w_ada, b_ada.reshape(n_layers, 1, 6 * d))[:, :batch].reshape(n_layers, batch, 6, d)

    inv_freq = 1.0 / (ROPE_THETA ** (jnp.arange(0, rope, 2, dtype=F32) / rope))
    invf = jnp.tile(inv_freq, LANES // inv_freq.shape[0]).reshape(1, LANES)
    phase = jnp.where(jnp.arange(LANES) < LANES - rope, 0.0, -0.5 * jnp.pi).astype(F32).reshape(1, LANES)
    cs = _rope_table(positions.reshape(n, 1), invf, phase)

    head_id = jnp.arange(2 * LANES) // HEAD
    bd = (head_id[:, None] == head_id[None, :]).astype(BF16)
    row1 = lambda a: a.reshape(1, -1)
    conv_shift = _conv_shift_matrix()

    w_in_bf, w_out_bf, w_ff1_bf, w_ff2_bf = [w.astype(BF16) for w in (w_in, w_out, w_ff1, w_ff2)]
    for l in range(n_layers):
        k_pe = w_in_bf[l, :, n_r + n_c + 2 * lora:]
        z_half = jnp.zeros((d, HEAD), BF16)
        w_extra = jnp.concatenate([z_half, k_pe, _rot_cols(k_pe)], axis=1)
        wq = mla_w_uq[l].reshape(lora, n_heads, HEAD + rope)
        wuq = jnp.concatenate([wq, _rot_cols(wq[..., HEAD:])], axis=-1).reshape(lora, n_heads * LANES).astype(BF16)
        wkv = mla_w_ukv[l].reshape(lora, n_heads, 2 * HEAD)
        wuk = jnp.concatenate([wkv[..., :HEAD], jnp.zeros_like(wkv[..., :HEAD])], axis=-1)
        wuk = wuk.reshape(lora, n_heads * LANES).astype(BF16)
        wuv = jnp.concatenate([wkv[..., HEAD:], jnp.zeros_like(wkv[..., HEAD:])], axis=-1)
        wuv = wuv.reshape(lora, n_heads * LANES).astype(BF16)
        lora_w = jnp.zeros((LANES, 3 * r_dim), F32)
        lora_w = lora_w.at[0:w_lora, 0:r_dim].set(rwkv_w2[l])
        lora_w = lora_w.at[w_lora:w_lora + a_lora, r_dim:2 * r_dim].set(rwkv_a2[l])
        lora_w = lora_w.at[w_lora + a_lora:, 2 * r_dim:].set(rwkv_g2[l]).astype(BF16)

        mod = mod_all[l]
        conv_params = [conv_w[l], row1(conv_b[l]), row1(conv_ln_w[l]), row1(conv_ln_b[l]), conv_shift]
        rwkv_params = [row1(rwkv_mu[l]), row1(rwkv_w0[l]), row1(rwkv_a0[l]), row1(rwkv_k_k[l]), row1(rwkv_k_a[l]),
                       row1(rwkv_r_k[l]), row1(rwkv_ln_w[l]), row1(rwkv_ln_b[l]), lora_w, bd]
        y_a, y_b, q, k, v = _inproj(xf, mod, row1(g_pre_mix[l]), w_in_bf, l, w_extra, cs,
                                    row1(mla_q_norm[l]), row1(mla_kv_norm[l]), wuq, wuk, wuv, conv_params, rwkv_params,
                                    seq, n_r, n_c, lora, n_heads, rope, scale, r_dim, w_lora, a_lora)
        y_c = _flash(q, k, v, batch, seq, n_heads)
        xf = _outffn(y_a, y_b, y_c, xf, mod, row1(g_post_mix[l]), row1(g_pre_ffn[l]), row1(g_post_ffn[l]),
                     w_out_bf, w_ff1_bf, w_ff2_bf, l, seq)
    return xf.reshape(batch, seq, d)
```

```python
import functools
import itertools

import jax
import jax.numpy as jnp
from jax import lax
from jax.experimental import pallas as pl
from jax.experimental.pallas import tpu as pltpu

F32 = jnp.float32
BF16 = jnp.bfloat16

LANES = 128
HEAD = 64
NORM_EPS = 1e-6
LN_EPS = 1e-5
RWKV_GN_EPS = 64e-5
ROPE_THETA = 10000.0
CONV_WIDTH = 31
CONV_HALO = 32
CONV_SUB = 128
CHUNK = 64
FLASH_BLOCK = 256
FLASH_VROWS = HEAD + 16
FLASH_SUBS = 8
GROUP_CHUNKS = 8
VMEM_LIMIT = 56 * 1024 * 1024
LOG2_E = 1.4426950408889634


def _dot(a, b):
    return jnp.dot(a, b, preferred_element_type=F32)


def _dot_nt(a, b):
    return lax.dot_general(a, b, (((1,), (1,)), ((), ())), preferred_element_type=F32)


def _bf(x):
    return x.astype(BF16)


def _split(x):
    hi = x.astype(BF16)
    return hi, (x - hi.astype(F32)).astype(BF16)


def _head_sums(x, ones_bd):
    width = x.shape[1]
    groups = [_dot(_bf(x[:, lo:lo + 2 * LANES]), ones_bd) for lo in range(0, width - 2 * LANES + 1, 2 * LANES)]
    if width % (2 * LANES):
        groups.append(_dot(_bf(x[:, width - LANES:]), ones_bd[0:LANES, 0:LANES]))
    return jnp.concatenate(groups, axis=1)


def _rms(x):
    return x * lax.rsqrt(jnp.mean(x * x, axis=-1, keepdims=True) + NORM_EPS)


def _sigmoid(x):
    return 1.0 / (1.0 + jnp.exp(-x))


def _params(*semantics):
    return pltpu.CompilerParams(dimension_semantics=semantics, vmem_limit_bytes=VMEM_LIMIT)


def _ada_kernel(c_ref, w_ref, b_ref, o_ref):
    c = c_ref[...]
    cs = c * _sigmoid(c)
    cs_hi, cs_lo = _split(cs)
    w_hi, w_lo = _split(w_ref[0])
    o_ref[0] = _dot(cs_hi, w_hi) + _dot(cs_hi, w_lo) + _dot(cs_lo, w_hi) + b_ref[0]


def _ada(c8, w_ada, b_ada):
    n_layers, d, d6 = w_ada.shape
    tn = 1536
    return pl.pallas_call(
        _ada_kernel,
        grid=(n_layers, d6 // tn),
        in_specs=[pl.BlockSpec((8, d), lambda l, j: (0, 0)),
                  pl.BlockSpec((1, d, tn), lambda l, j: (l, 0, j)),
                  pl.BlockSpec((1, 1, tn), lambda l, j: (l, 0, j))],
        out_specs=pl.BlockSpec((1, 8, tn), lambda l, j: (l, 0, j)),
        out_shape=jax.ShapeDtypeStruct((n_layers, 8, d6), F32),
        compiler_params=_params("parallel", "parallel"),
        name="ada_mod",
    )(c8, w_ada, b_ada)


def _rope_kernel(pos_ref, invf_ref, phase_ref, cs_ref):
    cs_ref[...] = jnp.cos(pos_ref[...].astype(F32) * invf_ref[...] + phase_ref[...])


def _rope_table(pos_col, invf, phase):
    n = pos_col.shape[0]
    tm = 2048
    vec = pl.BlockSpec((1, LANES), lambda i: (0, 0))
    return pl.pallas_call(
        _rope_kernel,
        grid=(n // tm,),
        in_specs=[pl.BlockSpec((tm, 1), lambda i: (i, 0)), vec, vec],
        out_specs=pl.BlockSpec((tm, LANES), lambda i: (i, 0)),
        out_shape=jax.ShapeDtypeStruct((n, LANES), F32),
        compiler_params=_params("parallel"),
        name="rope_table",
    )(pos_col, invf, phase)


def _inproj_kernel(x_ref, mod_ref, g_ref, w_ref, wx_ref, cs_ref, qn_ref, kvn_ref, wuq_ref, wuk_ref, wuv_ref,
                   cw_ref, cb_ref, clw_ref, clb_ref, shift_ref, *rest,
                   n_r, n_c, lora, n_heads, rope, scale, per_b, w_lora, a_lora, layer):
    g_ref, qn_ref, kvn_ref, cb_ref, clw_ref, clb_ref = _pick_row(
        (g_ref, qn_ref, kvn_ref, cb_ref, clw_ref, clb_ref), layer)
    rwkv_in = _pick_row(rest[:8], layer) + list(rest[8:10])
    ya_ref, yb_ref, q_ref, k_ref, v_ref, hbuf, zbuf, st_ref = rest[10:18]
    rwkv_scratch = rest[18:]

    @pl.when(pl.program_id(0) == 0)
    def _():
        hbuf[0:CONV_HALO, :] = jnp.zeros((CONV_HALO, hbuf.shape[1]), F32)
        zbuf[0:8, :] = jnp.zeros((8, zbuf.shape[1]), F32)
        st_ref[...] = jnp.zeros(st_ref.shape, F32)

    ends_sequence = (pl.program_id(0) + 1) % per_b == 0

    h = _rms(x_ref[...]) * g_ref[...]
    h = (h * (1.0 + mod_ref[0, 1:2, :]) + mod_ref[0, 0:1, :]).astype(BF16)
    p_rwkv = _dot(h, w_ref[:, :n_r])
    p_conv = _dot(h, w_ref[:, n_r:n_r + n_c])

    def mla_steps():
        lat = _dot(h, w_ref[:, n_r + n_c:n_r + n_c + 2 * lora])
        k_rot = _dot(h, wx_ref[...])
        cs = cs_ref[...]
        qn = (_rms(lat[:, 0:lora]) * qn_ref[...]).astype(BF16)
        kvn = (_rms(lat[:, lora:]) * kvn_ref[...]).astype(BF16)
        lane = lax.broadcasted_iota(jnp.int32, cs.shape, 1)
        t = k_rot * cs
        kpe = jnp.where(lane < HEAD, 0.0, t + pltpu.roll(t, rope, 1) + pltpu.roll(t, LANES - rope, 1))
        q_mul = jnp.where(lane < HEAD, 1.0, cs) * scale
        q_mul2 = jnp.concatenate([q_mul, q_mul], axis=1)
        kpe2 = jnp.concatenate([kpe, kpe], axis=1)
        is_v = jnp.concatenate([lane < HEAD, lane < HEAD], axis=1)
        yield
        for j in range(n_heads // 2):
            cols = slice(2 * j * LANES, 2 * (j + 1) * LANES)
            q_ref[:, cols] = (_dot(qn, wuq_ref[:, cols]) * q_mul2).astype(BF16)
            k_ref[:, cols] = (_dot(kvn, wuk_ref[:, cols]) + kpe2).astype(BF16)
            v_ref[:, cols] = jnp.where(is_v, _dot(kvn, wuv_ref[:, cols]), 1.0).astype(BF16)
            yield

    fill = itertools.chain(mla_steps(),
                           _conv_steps(p_conv, ends_sequence, cw_ref, cb_ref, clw_ref, clb_ref, shift_ref, yb_ref, hbuf))
    _rwkv_tile(p_rwkv, ends_sequence, *rwkv_in, ya_ref, zbuf, st_ref, *rwkv_scratch,
               w_lora=w_lora, a_lora=a_lora, fill=fill)
    for _ in fill:
        pass


def _layer_spec(a, layer):
    if a.ndim == 2:
        return pl.BlockSpec(a.shape, lambda i: (0, 0))
    return pl.BlockSpec((None,) + a.shape[1:], lambda i: (layer,) + (0,) * (a.ndim - 1))


def _pick_row(refs, layer):
    return [r.at[layer:layer + 1, :] for r in refs]


def _inproj(x, layer, mod_all, cs, layered, shared, seq, n_r, n_c, lora, n_heads, rope, scale, r_dim, w_lora, a_lora):
    n, d = x.shape
    tm = CHUNK * GROUP_CHUNKS
    per_b = seq // tm
    row = lambda w: pl.BlockSpec((tm, w), lambda i: (i, 0))
    full = lambda a: pl.BlockSpec(a.shape, lambda i: (0, 0))
    act = pltpu.VMEM((tm, r_dim), F32)
    g_w_wx, mla, conv, rwkv = layered
    conv_shift, bd = shared
    return pl.pallas_call(
        functools.partial(_inproj_kernel, n_r=n_r, n_c=n_c, lora=lora, n_heads=n_heads, rope=rope, scale=scale,
                          per_b=per_b, w_lora=w_lora, a_lora=a_lora, layer=layer),
        grid=(n // tm,),
        in_specs=[row(d), pl.BlockSpec((None, 1, 6, d), lambda i: (layer, i // per_b, 0, 0))]
                 + [_layer_spec(a, layer) for a in g_w_wx] + [row(LANES)] + [_layer_spec(a, layer) for a in mla]
                 + [_layer_spec(a, layer) for a in conv] + [full(conv_shift)]
                 + [_layer_spec(a, layer) for a in rwkv] + [full(bd)],
        out_specs=[row(r_dim), row(n_c // 2)] + [row(n_heads * LANES)] * 3,
        out_shape=[jax.ShapeDtypeStruct((n, r_dim), BF16), jax.ShapeDtypeStruct((n, n_c // 2), BF16)]
                  + [jax.ShapeDtypeStruct((n, n_heads * LANES), BF16)] * 3,
        scratch_shapes=[pltpu.VMEM((tm + CONV_HALO, n_c // 2), F32),
                        pltpu.VMEM((tm + 8, n_r), F32),
                        pltpu.VMEM((r_dim // LANES, LANES, LANES), F32),
                        act, act, act, act, act, act, act],
        compiler_params=_params("arbitrary"),
        name="in_proj",
    )(x, mod_all, *g_w_wx, cs, *mla, *conv, conv_shift, *rwkv, bd)


def _rwkv_tile(p, ends_sequence, mu_ref, w0_ref, a0_ref, kk_ref, ka_ref, rk_ref, lnw_ref, lnb_ref, lora_ref, bd_ref,
               o_ref, zbuf, st_ref, r_s, k_s, v_s, a_s, b_s, ld_s, y_s, *, w_lora, a_lora, fill):
    tt, dim = o_ref.shape
    assert tt == CHUNK * GROUP_CHUNKS

    def gap():
        next(fill, None)

    n_pairs = dim // LANES
    c = CHUNK

    zbuf[8:8 + tt, :] = p
    prev = zbuf[7:7 + tt, :]
    zbuf[7:8, :] = jnp.where(ends_sequence, 0.0, zbuf[tt + 7:tt + 8, :])
    z = p + (prev - p) * mu_ref[...]

    r = z[:, 0:dim]
    k = z[:, dim:2 * dim]
    v = z[:, 2 * dim:3 * dim]
    lo = z[:, 3 * dim:3 * dim + LANES]
    lane = lax.broadcasted_iota(jnp.int32, lo.shape, 1)
    act = jnp.where(lane < w_lora, jnp.tanh(lo), jnp.where(lane < w_lora + a_lora, lo, _sigmoid(lo)))
    lora = _dot(_bf(act), lora_ref[...])
    x_w = w0_ref[...] + lora[:, 0:dim]
    w_log = -(jnp.maximum(-x_w, 0.0) + jnp.log(1.0 + jnp.exp(-jnp.abs(x_w)))) - 0.5
    a = _sigmoid(a0_ref[...] + lora[:, dim:2 * dim])
    g = lora[:, 2 * dim:3 * dim]
    bd = bd_ref[...]
    kk = k * kk_ref[...]
    kk = kk * lax.rsqrt(jnp.maximum(_head_sums(kk * kk, bd), 1e-24))
    k = k * (1.0 + (a - 1.0) * ka_ref[...])
    r_s[...] = r
    k_s[...] = k
    v_s[...] = v
    a_s[...] = -kk
    b_s[...] = kk * a
    ld_s[...] = -jnp.exp(w_log)

    row_c = lax.broadcasted_iota(jnp.int32, (c, c), 0)
    col_c = lax.broadcasted_iota(jnp.int32, (c, c), 1)
    tri_incl = (row_c >= col_c).astype(BF16)
    head0 = lax.broadcasted_iota(jnp.int32, (c, LANES), 1) < HEAD
    lane_2 = lax.broadcasted_iota(jnp.int32, (c, 2 * LANES), 1)
    head0_2 = jnp.where(lane_2 >= LANES, lane_2 - LANES, lane_2) < HEAD
    row_4 = lax.broadcasted_iota(jnp.int32, (c, 4 * c), 0)
    col_4 = jnp.bitwise_and(lax.broadcasted_iota(jnp.int32, (c, 4 * c), 1), c - 1)
    strict4 = row_4 > col_4
    incl4 = row_4 >= col_4
    eye4 = (row_4 == col_4).astype(F32)
    bd4 = jnp.bitwise_xor(lax.broadcasted_iota(jnp.int32, (4 * c, 4 * c), 0),
                          lax.broadcasted_iota(jnp.int32, (4 * c, 4 * c), 1)) < c
    row_p = lax.broadcasted_iota(jnp.int32, (LANES, LANES), 0)
    col_p = lax.broadcasted_iota(jnp.int32, (LANES, LANES), 1)
    same_head = (row_p < HEAD) == (col_p < HEAD)
    diag_p = row_p == col_p
    zeros_cl = jnp.zeros((c, LANES), BF16)

    def group_body():
        items = [(slice(q * c, (q + 1) * c), slice(j * LANES, (j + 1) * LANES), j)
                 for q in range(GROUP_CHUNKS) for j in range(n_pairs)]
        n_it = len(items)
        loaded = [[ref[rows, lanes] for ref in (r_s, k_s, v_s, a_s, b_s, ld_s)] for rows, lanes, _ in items]
        r_c, k_c, v_c, a_c, b_c, ld_c = [list(t) for t in zip(*loaded)]
        cum = []
        for i in range(n_it):
            ld_hi, ld_lo = _split(ld_c[i])
            both = _dot(tri_incl, jnp.concatenate([ld_hi, ld_lo], axis=1))
            cum.append(both[:, 0:LANES] + both[:, LANES:])
        gap()
        cum_end = [x[c - 1:c, :] for x in cum]
        at = [a_c[i] * jnp.exp(cum[i] - ld_c[i]) for i in range(n_it)]
        rt = [r_c[i] * jnp.exp(cum[i]) for i in range(n_it)]
        v_bf = [_bf(x) for x in v_c]
        def by_head(x, mask):
            return jnp.concatenate([jnp.where(mask, x, 0), jnp.where(mask, 0, x)], axis=0)

        aa = []
        for i in range(n_it):
            e_neg = jnp.exp(-cum[i])
            rhs = jnp.concatenate([by_head(_bf(b_c[i] * e_neg), head0), by_head(_bf(k_c[i] * e_neg), head0)], axis=0)
            aa.append(_dot_nt(_bf(jnp.concatenate([at[i], rt[i]], axis=0)), rhs))
        gap()
        t_cat = [jnp.where(strict4, x[0:c, :], 0.0) for x in aa]
        q_cat = [_bf(jnp.where(incl4, x[c:2 * c, :], 0.0)) for x in aa]
        akv = [_dot(_bf(t_cat[i][:, LANES:]), by_head(v_bf[i], head0)) for i in range(n_it)]
        n_grp = n_it // 2

        def bdiag(x):
            return jnp.where(bd4, jnp.concatenate([x, x, x, x], axis=0), 0)

        p4 = [jnp.concatenate([t_cat[2 * g][:, 0:LANES], t_cat[2 * g + 1][:, 0:LANES]], axis=1) for g in range(n_grp)]
        inv4 = [eye4 + x for x in p4]
        p4_bf = [_bf(x) for x in p4]
        pw = [_bf(_dot(x, bdiag(x))) for x in p4_bf]
        gap()
        for _ in range(4):
            both = [_dot(jnp.concatenate([pw[g], _bf(inv4[g])], axis=0), bdiag(pw[g])) for g in range(n_grp)]
            inv4 = [inv4[g] + both[g][c:, :] for g in range(n_grp)]
            pw = [_bf(both[g][0:c, :]) for g in range(n_grp)]
            gap()
        inv4 = [_bf(inv4[g] + _dot(_bf(inv4[g]), bdiag(pw[g]))) for g in range(n_grp)]
        inv_cat = [inv4[i // 2][:, (i % 2) * LANES:(i % 2 + 1) * LANES] for i in range(n_it)]
        x_mat = [_bf(jnp.concatenate([at[i], akv[i]], axis=1)) for i in range(n_it)]
        tx = [_dot(inv_cat[i], by_head(x_mat[i], head0_2)) for i in range(n_it)]
        gap()
        tx_bf = [_bf(x) for x in tx]
        v_pad = [jnp.concatenate([zeros_cl, v_bf[i]], axis=1) for i in range(n_it)]
        w_mat = [jnp.concatenate([tx_bf[i], v_pad[i]], axis=0) for i in range(n_it)]
        qw = [_dot(q_cat[i], jnp.concatenate([by_head(tx_bf[i], head0_2), by_head(v_pad[i], head0_2)], axis=0))
              for i in range(n_it)]
        mn = []
        for i in range(n_it):
            e_end = jnp.exp(cum_end[i] - cum[i])
            bk_t = _bf(jnp.concatenate([b_c[i] * e_end, k_c[i] * e_end], axis=0).T)
            mn.append(_dot(bk_t, w_mat[i]))
        gap()
        g_col = [jnp.sum(jnp.where(diag_p, jnp.exp(cum_end[i]), 0.0), axis=1, keepdims=True) for i in range(n_it)]
        states = [st_ref[j] for j in range(n_pairs)]
        for i, (rows, lanes, j) in enumerate(items):
            st = states[j]
            st_bf = _bf(st)
            y_s[rows, lanes] = _dot(_bf(rt[i] + qw[i][:, 0:LANES]), st_bf) + qw[i][:, LANES:]
            m_low = _bf(jnp.where(same_head, mn[i][:, 0:LANES], 0.0))
            states[j] = g_col[i] * st + _dot(m_low, st_bf) + jnp.where(same_head, mn[i][:, LANES:], 0.0)
        for j in range(n_pairs):
            st_ref[j] = jnp.where(ends_sequence, 0.0, states[j])

    group_body()

    y = y_s[...]
    inv_n = 1.0 / HEAD
    mean = _head_sums(y, bd) * inv_n
    d = y - mean
    var = _head_sums(d * d, bd) * inv_n
    y = d * lax.rsqrt(var + RWKV_GN_EPS) * lnw_ref[...] + lnb_ref[...]
    r = r_s[...]
    bonus = _head_sums(r * k_s[...] * rk_ref[...], bd) * v_s[...]
    o_ref[...] = ((y + bonus) * g).astype(o_ref.dtype)


def _conv_steps(p_conv, ends_sequence, w_ref, b_ref, lnw_ref, lnb_ref, shift_ref, o_ref, hbuf):
    tt, dim = o_ref.shape
    sub = CONV_SUB
    ext = sub + 8

    hbuf[CONV_HALO:CONV_HALO + tt, :] = p_conv[:, 0:dim] * _sigmoid(p_conv[:, dim:2 * dim])
    yield
    for s in range(tt // sub):
        base = CONV_HALO + s * sub - 8
        wins = [hbuf[base - 8 * a:base - 8 * a + ext, :] for a in range(CONV_HALO // 8)]
        parts = []
        for r in range(8):
            lags = [8 * a + r for a in range(len(wins)) if 8 * a + r < CONV_WIDTH]
            terms = [wins[m // 8] * w_ref[CONV_WIDTH - 1 - m:CONV_WIDTH - m, :] for m in lags]
            parts.append(functools.reduce(jnp.add, terms))
        stacked = _bf(jnp.concatenate(parts, axis=0))
        acc = _dot(shift_ref[...], stacked) + b_ref[...]
        mu = jnp.mean(acc, axis=-1, keepdims=True)
        d = acc - mu
        var = jnp.mean(d * d, axis=-1, keepdims=True)
        y = d * lax.rsqrt(var + LN_EPS) * lnw_ref[...] + lnb_ref[...]
        o_ref[s * sub:(s + 1) * sub, :] = (y * _sigmoid(y)).astype(o_ref.dtype)
        if s + 1 < tt // sub:
            yield
    hbuf[0:CONV_HALO, :] = jnp.where(ends_sequence, 0.0, hbuf[tt:tt + CONV_HALO, :])


def _conv_shift_matrix():
    ext = CONV_SUB + 8
    u_idx = jnp.arange(CONV_SUB)[:, None]
    col = jnp.arange(8 * ext)[None, :]
    return (col % ext == u_idx + 8 - col // ext).astype(BF16)


def _flash_kernel(q_ref, k_ref, v_ref, o_ref, m_s, acc_s, p_buf, al_buf):
    i = pl.program_id(2)
    t = FLASH_BLOCK
    m_s[...] = jnp.full(m_s.shape, -jnp.inf, F32)
    acc_s[...] = jnp.zeros(acc_s.shape, F32)
    p_buf[1] = jnp.zeros(p_buf.shape[1:], BF16)
    al_buf[1] = jnp.ones(al_buf.shape[1:], F32)
    causal = lax.broadcasted_iota(jnp.int32, (t, t), 0) <= lax.broadcasted_iota(jnp.int32, (t, t), 1)

    def keys_of(j):
        return pl.ds(pl.multiple_of(j * t, t), t)

    def score_softmax(j, slot, u0, masked_sub):
        for u in range(u0, FLASH_SUBS):
            qs = slice(u * t, (u + 1) * t)
            for h in range(2):
                cols = slice(h * LANES, (h + 1) * LANES)
                s = _dot_nt(k_ref[keys_of(j), cols], q_ref[qs, cols])
                if u == masked_sub:
                    s = jnp.where(causal, s, -jnp.inf)
                m_prev = m_s[h, :, qs]
                m_new = jnp.maximum(m_prev, jnp.max(s, axis=0, keepdims=True))
                m_s[h, :, qs] = m_new
                al_buf[slot, h, :, qs] = jnp.exp2(m_prev - m_new)
                p_buf[slot, h, :, qs] = jnp.exp2((s - m_new).astype(BF16))

    def accumulate(j, slot, u0):
        for h in range(2):
            v_blk = v_ref[keys_of(j), h * LANES:h * LANES + FLASH_VROWS]
            pv = lax.dot_general(v_blk, p_buf[slot, h, :, u0 * t:], (((0,), (0,)), ((), ())),
                                 preferred_element_type=F32)
            acc_s[h, :, u0 * t:] = al_buf[slot, h, :, u0 * t:] * acc_s[h, :, u0 * t:] + pv

    n_full = FLASH_SUBS * i

    def body(g, carry):
        for par in range(2):
            k = 2 * g + par
            score_softmax(k, par, 0, None)
            accumulate(jnp.maximum(k - 1, 0), 1 - par, 0)
        return carry

    lax.fori_loop(0, n_full // 2, body, 0)
    pending = (jnp.maximum(n_full - 1, 0), 1, 0)
    for d in range(FLASH_SUBS):
        score_softmax(n_full + d, d % 2, d, d)
        accumulate(*pending)
        pending = (n_full + d, d % 2, d)
    accumulate(*pending)
    out_t = jnp.concatenate([acc_s[h, 0:HEAD, :] / acc_s[h, HEAD:HEAD + 1, :] for h in range(2)], axis=0)
    o_ref[...] = out_t.T.astype(o_ref.dtype)


def _flash(q, k, v, batch, seq, n_heads):
    n = q.shape[0]
    tq = FLASH_SUBS * FLASH_BLOCK
    n_q = seq // tq
    return pl.pallas_call(
        _flash_kernel,
        grid=(batch, n_heads // 2, n_q),
        in_specs=[pl.BlockSpec((tq, 2 * LANES), lambda b, h, i: (b * n_q + i, h)),
                  pl.BlockSpec((seq, 2 * LANES), lambda b, h, i: (b, h)),
                  pl.BlockSpec((seq, 2 * LANES), lambda b, h, i: (b, h))],
        out_specs=pl.BlockSpec((tq, LANES), lambda b, h, i: (b * n_q + i, h)),
        out_shape=jax.ShapeDtypeStruct((n, n_heads * HEAD), BF16),
        scratch_shapes=[pltpu.VMEM((2, 1, tq), F32),
                        pltpu.VMEM((2, FLASH_VROWS, tq), F32),
                        pltpu.VMEM((2, 2, FLASH_BLOCK, tq), BF16),
                        pltpu.VMEM((2, 2, 1, tq), F32)],
        compiler_params=_params("parallel", "parallel", "arbitrary"),
        name="mla_flash",
    )(q, k, v)


def _outffn_kernel(ya_ref, yb_ref, yc_ref, x_ref, mod_ref, gpm_ref, gpf_ref, gqf_ref, wo_ref, w1_ref, w2_ref,
                   o_ref, x1_s, hid_s, *, da, db, tf, layer):
    gpm_ref, gpf_ref, gqf_ref = _pick_row((gpm_ref, gpf_ref, gqf_ref), layer)
    half = x_ref.shape[0] // 2
    halves = [slice(0, half), slice(half, 2 * half)]
    ys = [_dot(ya_ref[r, :], wo_ref[0:da, :]) + _dot(yb_ref[r, :], wo_ref[da:da + db, :])
          + _dot(yc_ref[r, :], wo_ref[da + db:, :]) for r in halves]
    hs = []
    for r, y in zip(halves, ys):
        x1 = x_ref[r, :] + mod_ref[0, 2:3, :] * (_rms(y) * gpm_ref[...])
        x1_s[r, :] = x1
        h = _rms(x1) * gpf_ref[...]
        hs.append((h * (1.0 + mod_ref[0, 4:5, :]) + mod_ref[0, 3:4, :]).astype(BF16))
    for f in range(hid_s.shape[1] // tf):
        cols = slice(f * tf, (f + 1) * tf)
        for r, h in zip(halves, hs):
            hid = jnp.maximum(_dot(h, w1_ref[:, cols]), 0.0)
            hid_s[r, cols] = (hid * hid).astype(BF16)
    for r in halves:
        y2 = _dot(hid_s[r, :], w2_ref[...])
        o_ref[r, :] = x1_s[r, :] + mod_ref[0, 5:6, :] * (_rms(y2) * gqf_ref[...])


def _outffn(ya, yb, yc, x, layer, mod_all, gains, w_out, w_ff1, w_ff2, seq):
    n, d = x.shape
    d_ff = w_ff1.shape[2]
    tm, tf = 512, 512
    per_b = seq // tm
    row = lambda w: pl.BlockSpec((tm, w), lambda i: (i, 0))
    resident = lambda a: pl.BlockSpec((None,) + a.shape[1:], lambda i: (layer, 0, 0), pipeline_mode=pl.Buffered(1))
    return pl.pallas_call(
        functools.partial(_outffn_kernel, da=ya.shape[1], db=yb.shape[1], tf=tf, layer=layer),
        grid=(n // tm,),
        in_specs=[row(ya.shape[1]), row(yb.shape[1]), row(yc.shape[1]), row(d),
                  pl.BlockSpec((None, 1, 6, d), lambda i: (layer, i // per_b, 0, 0))]
                 + [_layer_spec(a, layer) for a in gains] + [resident(w_out), resident(w_ff1), resident(w_ff2)],
        out_specs=row(d),
        out_shape=jax.ShapeDtypeStruct((n, d), F32),
        scratch_shapes=[pltpu.VMEM((tm, d), F32), pltpu.VMEM((tm, d_ff), BF16)],
        compiler_params=_params("parallel"),
        name="out_ffn",
    )(ya, yb, yc, x, mod_all, *gains, w_out, w_ff1, w_ff2)


def _rot_cols(w):
    half = w.shape[-1] // 2
    return jnp.concatenate([-w[..., half:], w[..., :half]], axis=-1)


def kernel(x, c, positions, g_pre_mix, g_post_mix, g_pre_ffn, g_post_ffn, w_ada, b_ada, w_in, w_out, rwkv_mu, rwkv_w0, rwkv_w2, rwkv_a0, rwkv_a2, rwkv_g2, rwkv_k_k, rwkv_k_a, rwkv_r_k, rwkv_ln_w, rwkv_ln_b, conv_w, conv_b, conv_ln_w, conv_ln_b, mla_q_norm, mla_w_uq, mla_kv_norm, mla_w_ukv, w_ff1, w_ff2):
    batch, seq, d = x.shape
    n = batch * seq
    n_layers = w_ada.shape[0]
    r_dim = rwkv_w0.shape[1]
    w_lora, a_lora, g_lora = rwkv_w2.shape[1], rwkv_a2.shape[1], rwkv_g2.shape[1]
    assert w_lora + a_lora + g_lora == LANES and r_dim % LANES == 0
    n_r = 3 * r_dim + LANES
    c_dim = conv_w.shape[2]
    n_c = 2 * c_dim
    lora = mla_q_norm.shape[1]
    n_heads = mla_w_ukv.shape[2] // (2 * HEAD)
    rope = mla_w_uq.shape[2] // n_heads - HEAD
    assert HEAD + 2 * rope == LANES
    scale = float(HEAD + rope) ** -0.5 * LOG2_E

    xf = x.reshape(n, d)
    c8 = jnp.pad(c, ((0, 8 - batch), (0, 0)))
    mod_all = _ada(c8, w_ada, b_ada.reshape(n_layers, 1, 6 * d))[:, :batch].reshape(n_layers, batch, 6, d)

    inv_freq = 1.0 / (ROPE_THETA ** (jnp.arange(0, rope, 2, dtype=F32) / rope))
    invf = jnp.tile(inv_freq, LANES // inv_freq.shape[0]).reshape(1, LANES)
    phase = jnp.where(jnp.arange(LANES) < LANES - rope, 0.0, -0.5 * jnp.pi).astype(F32).reshape(1, LANES)
    cs = _rope_table(positions.reshape(n, 1), invf, phase)

    head_id = jnp.arange(2 * LANES) // HEAD
    bd = (head_id[:, None] == head_id[None, :]).astype(BF16)
    conv_shift = _conv_shift_matrix()

    vec = lambda a: a.reshape(n_layers, -1)
    w_in_bf, w_out_bf, w_ff1_bf, w_ff2_bf = [w.astype(BF16) for w in (w_in, w_out, w_ff1, w_ff2)]
    k_pe = w_in_bf[:, :, n_r + n_c + 2 * lora:]
    w_extra = jnp.concatenate([jnp.zeros((n_layers, d, HEAD), BF16), k_pe, _rot_cols(k_pe)], axis=-1)
    wq = mla_w_uq.reshape(n_layers, lora, n_heads, HEAD + rope)
    wuq = jnp.concatenate([wq, _rot_cols(wq[..., HEAD:])], axis=-1).reshape(n_layers, lora, n_heads * LANES)
    wkv = mla_w_ukv.reshape(n_layers, lora, n_heads, 2 * HEAD)
    pad = jnp.zeros_like(wkv[..., :HEAD])
    wuk = jnp.concatenate([wkv[..., :HEAD], pad], axis=-1).reshape(n_layers, lora, n_heads * LANES)
    wuv = jnp.concatenate([wkv[..., HEAD:], pad], axis=-1).reshape(n_layers, lora, n_heads * LANES)
    lora_w = jnp.zeros((n_layers, LANES, 3 * r_dim), F32)
    lora_w = lora_w.at[:, 0:w_lora, 0:r_dim].set(rwkv_w2)
    lora_w = lora_w.at[:, w_lora:w_lora + a_lora, r_dim:2 * r_dim].set(rwkv_a2)
    lora_w = lora_w.at[:, w_lora + a_lora:, 2 * r_dim:].set(rwkv_g2)
    layered = ([vec(g_pre_mix), w_in_bf, w_extra],
               [vec(mla_q_norm), vec(mla_kv_norm), wuq.astype(BF16), wuk.astype(BF16), wuv.astype(BF16)],
               [conv_w, vec(conv_b), vec(conv_ln_w), vec(conv_ln_b)],
               [vec(rwkv_mu), vec(rwkv_w0), vec(rwkv_a0), vec(rwkv_k_k), vec(rwkv_k_a), vec(rwkv_r_k),
                vec(rwkv_ln_w), vec(rwkv_ln_b), lora_w.astype(BF16)])
    gains = [vec(g_post_mix), vec(g_pre_ffn), vec(g_post_ffn)]
    for l in range(n_layers):
        y_a, y_b, q, k, v = _inproj(xf, l, mod_all, cs, layered, (conv_shift, bd),
                                    seq, n_r, n_c, lora, n_heads, rope, scale, r_dim, w_lora, a_lora)
        y_c = _flash(q, k, v, batch, seq, n_heads)
        xf = _outffn(y_a, y_b, y_c, xf, l, mod_all, gains, w_out_bf, w_ff1_bf, w_ff2_bf, seq)
    return xf.reshape(batch, seq, d)
```

```python
import functools
import itertools

import jax
import jax.numpy as jnp
from jax import lax
from jax.experimental import pallas as pl
from jax.experimental.pallas import tpu as pltpu

F32 = jnp.float32
BF16 = jnp.bfloat16

LANES = 128
HEAD = 64
NORM_EPS = 1e-6
LN_EPS = 1e-5
RWKV_GN_EPS = 64e-5
ROPE_THETA = 10000.0
CONV_WIDTH = 31
CONV_HALO = 32
CONV_SUB = 128
CHUNK = 64
FLASH_BLOCK = 256
FLASH_VROWS = HEAD + 16
FLASH_SUBS = 8
GROUP_CHUNKS = 8
VMEM_LIMIT = 56 * 1024 * 1024
LOG2_E = 1.4426950408889634


def _dot(a, b):
    return jnp.dot(a, b, preferred_element_type=F32)


def _dot_nt(a, b):
    return lax.dot_general(a, b, (((1,), (1,)), ((), ())), preferred_element_type=F32)


def _bf(x):
    return x.astype(BF16)


def _split(x):
    hi = x.astype(BF16)
    return hi, (x - hi.astype(F32)).astype(BF16)


def _head_sums(x, ones_bd):
    width = x.shape[1]
    groups = [_dot(_bf(x[:, lo:lo + 2 * LANES]), ones_bd) for lo in range(0, width - 2 * LANES + 1, 2 * LANES)]
    if width % (2 * LANES):
        groups.append(_dot(_bf(x[:, width - LANES:]), ones_bd[0:LANES, 0:LANES]))
    return jnp.concatenate(groups, axis=1)


def _rms(x):
    return x * lax.rsqrt(jnp.mean(x * x, axis=-1, keepdims=True) + NORM_EPS)


def _sigmoid(x):
    return 1.0 / (1.0 + jnp.exp(-x))


def _params(*semantics):
    return pltpu.CompilerParams(dimension_semantics=semantics, vmem_limit_bytes=VMEM_LIMIT)


def _ada_kernel(c_ref, w_ref, b_ref, o_ref):
    c = c_ref[...]
    cs = c * _sigmoid(c)
    cs_hi, cs_lo = _split(cs)
    w_hi, w_lo = _split(w_ref[0])
    o_ref[0] = _dot(cs_hi, w_hi) + _dot(cs_hi, w_lo) + _dot(cs_lo, w_hi) + b_ref[0]


def _ada(c8, w_ada, b_ada):
    n_layers, d, d6 = w_ada.shape
    tn = 1536
    return pl.pallas_call(
        _ada_kernel,
        grid=(n_layers, d6 // tn),
        in_specs=[pl.BlockSpec((8, d), lambda l, j: (0, 0)),
                  pl.BlockSpec((1, d, tn), lambda l, j: (l, 0, j)),
                  pl.BlockSpec((1, 1, tn), lambda l, j: (l, 0, j))],
        out_specs=pl.BlockSpec((1, 8, tn), lambda l, j: (l, 0, j)),
        out_shape=jax.ShapeDtypeStruct((n_layers, 8, d6), F32),
        compiler_params=_params("parallel", "parallel"),
        name="ada_mod",
    )(c8, w_ada, b_ada)


def _rope_kernel(pos_ref, invf_ref, phase_ref, cs_ref):
    cs_ref[...] = jnp.cos(pos_ref[...].astype(F32) * invf_ref[...] + phase_ref[...])


def _rope_table(pos_col, invf, phase):
    n = pos_col.shape[0]
    tm = 2048
    vec = pl.BlockSpec((1, LANES), lambda i: (0, 0))
    return pl.pallas_call(
        _rope_kernel,
        grid=(n // tm,),
        in_specs=[pl.BlockSpec((tm, 1), lambda i: (i, 0)), vec, vec],
        out_specs=pl.BlockSpec((tm, LANES), lambda i: (i, 0)),
        out_shape=jax.ShapeDtypeStruct((n, LANES), F32),
        compiler_params=_params("parallel"),
        name="rope_table",
    )(pos_col, invf, phase)


def _inproj_kernel(x_ref, mod_ref, g_ref, w_ref, wx_ref, cs_ref, qn_ref, kvn_ref, wuq_ref, wuk_ref, wuv_ref,
                   cw_ref, cb_ref, clw_ref, clb_ref, shift_ref, *rest,
                   n_r, n_c, lora, n_heads, rope, scale, per_b, w_lora, a_lora, layer):
    g_ref, qn_ref, kvn_ref, cb_ref, clw_ref, clb_ref = _pick_row(
        (g_ref, qn_ref, kvn_ref, cb_ref, clw_ref, clb_ref), layer)
    rwkv_in = _pick_row(rest[:8], layer) + list(rest[8:10])
    ya_ref, yb_ref, q_ref, k_ref, v_ref, hbuf, zbuf, st_ref = rest[10:18]
    rwkv_scratch = rest[18:]

    @pl.when(pl.program_id(0) == 0)
    def _():
        hbuf[0:CONV_HALO, :] = jnp.zeros((CONV_HALO, hbuf.shape[1]), F32)
        zbuf[0:8, :] = jnp.zeros((8, zbuf.shape[1]), F32)
        st_ref[...] = jnp.zeros(st_ref.shape, F32)

    ends_sequence = (pl.program_id(0) + 1) % per_b == 0

    h = _rms(x_ref[...]) * g_ref[...]
    h = (h * (1.0 + mod_ref[0, 1:2, :]) + mod_ref[0, 0:1, :]).astype(BF16)
    p_rwkv = _dot(h, w_ref[:, :n_r])
    p_conv = _dot(h, w_ref[:, n_r:n_r + n_c])

    def mla_steps():
        lat = _dot(h, w_ref[:, n_r + n_c:n_r + n_c + 2 * lora])
        k_rot = _dot(h, wx_ref[...])
        cs = cs_ref[...]
        qn = (_rms(lat[:, 0:lora]) * qn_ref[...]).astype(BF16)
        kvn = (_rms(lat[:, lora:]) * kvn_ref[...]).astype(BF16)
        lane = lax.broadcasted_iota(jnp.int32, cs.shape, 1)
        t = k_rot * cs
        kpe = jnp.where(lane < HEAD, 0.0, t + pltpu.roll(t, rope, 1) + pltpu.roll(t, LANES - rope, 1))
        q_mul = jnp.where(lane < HEAD, 1.0, cs) * scale
        q_mul2 = jnp.concatenate([q_mul, q_mul], axis=1)
        kpe2 = jnp.concatenate([kpe, kpe], axis=1)
        is_v = jnp.concatenate([lane < HEAD, lane < HEAD], axis=1)
        yield
        for j in range(n_heads // 2):
            cols = slice(2 * j * LANES, 2 * (j + 1) * LANES)
            q_ref[:, cols] = (_dot(qn, wuq_ref[:, cols]) * q_mul2).astype(BF16)
            k_ref[:, cols] = (_dot(kvn, wuk_ref[:, cols]) + kpe2).astype(BF16)
            v_ref[:, cols] = jnp.where(is_v, _dot(kvn, wuv_ref[:, cols]), 1.0).astype(BF16)
            yield

    fill = itertools.chain(mla_steps(),
                           _conv_steps(p_conv, ends_sequence, cw_ref, cb_ref, clw_ref, clb_ref, shift_ref, yb_ref, hbuf))
    _rwkv_tile(p_rwkv, ends_sequence, *rwkv_in, ya_ref, zbuf, st_ref, *rwkv_scratch,
               w_lora=w_lora, a_lora=a_lora, fill=fill)
    for _ in fill:
        pass


def _layer_spec(a, layer):
    if a.ndim == 2:
        return pl.BlockSpec(a.shape, lambda i: (0, 0))
    return pl.BlockSpec((None,) + a.shape[1:], lambda i: (layer,) + (0,) * (a.ndim - 1))


def _pick_row(refs, layer):
    return [r.at[layer:layer + 1, :] for r in refs]


def _inproj(x, layer, mod_all, cs, layered, shared, seq, n_r, n_c, lora, n_heads, rope, scale, r_dim, w_lora, a_lora):
    n, d = x.shape
    tm = CHUNK * GROUP_CHUNKS
    per_b = seq // tm
    row = lambda w: pl.BlockSpec((tm, w), lambda i: (i, 0))
    full = lambda a: pl.BlockSpec(a.shape, lambda i: (0, 0))
    act = pltpu.VMEM((tm, r_dim), F32)
    g_w_wx, mla, conv, rwkv = layered
    conv_shift, bd = shared
    return pl.pallas_call(
        functools.partial(_inproj_kernel, n_r=n_r, n_c=n_c, lora=lora, n_heads=n_heads, rope=rope, scale=scale,
                          per_b=per_b, w_lora=w_lora, a_lora=a_lora, layer=layer),
        grid=(n // tm,),
        in_specs=[row(d), pl.BlockSpec((None, 1, 6, d), lambda i: (layer, i // per_b, 0, 0))]
                 + [_layer_spec(a, layer) for a in g_w_wx] + [row(LANES)] + [_layer_spec(a, layer) for a in mla]
                 + [_layer_spec(a, layer) for a in conv] + [full(conv_shift)]
                 + [_layer_spec(a, layer) for a in rwkv] + [full(bd)],
        out_specs=[row(r_dim), row(n_c // 2)] + [row(n_heads * LANES)] * 3,
        out_shape=[jax.ShapeDtypeStruct((n, r_dim), BF16), jax.ShapeDtypeStruct((n, n_c // 2), BF16)]
                  + [jax.ShapeDtypeStruct((n, n_heads * LANES), BF16)] * 3,
        scratch_shapes=[pltpu.VMEM((tm + CONV_HALO, n_c // 2), F32),
                        pltpu.VMEM((tm + 8, n_r), F32),
                        pltpu.VMEM((r_dim // LANES, LANES, LANES), F32),
                        act, act, act, act, act, act, act],
        compiler_params=_params("arbitrary"),
        name="in_proj",
    )(x, mod_all, *g_w_wx, cs, *mla, *conv, conv_shift, *rwkv, bd)


def _rwkv_tile(p, ends_sequence, mu_ref, w0_ref, a0_ref, kk_ref, ka_ref, rk_ref, lnw_ref, lnb_ref, lora_ref, bd_ref,
               o_ref, zbuf, st_ref, r_s, k_s, v_s, a_s, b_s, ld_s, y_s, *, w_lora, a_lora, fill):
    tt, dim = o_ref.shape
    assert tt == CHUNK * GROUP_CHUNKS

    def gap():
        next(fill, None)

    n_pairs = dim // LANES
    c = CHUNK

    zbuf[8:8 + tt, :] = p
    prev = zbuf[7:7 + tt, :]
    zbuf[7:8, :] = jnp.where(ends_sequence, 0.0, zbuf[tt + 7:tt + 8, :])
    z = p + (prev - p) * mu_ref[...]

    r = z[:, 0:dim]
    k = z[:, dim:2 * dim]
    v = z[:, 2 * dim:3 * dim]
    lo = z[:, 3 * dim:3 * dim + LANES]
    lane = lax.broadcasted_iota(jnp.int32, lo.shape, 1)
    act = jnp.where(lane < w_lora, jnp.tanh(lo), jnp.where(lane < w_lora + a_lora, lo, _sigmoid(lo)))
    lora = _dot(_bf(act), lora_ref[...])
    x_w = w0_ref[...] + lora[:, 0:dim]
    w_log = -(jnp.maximum(-x_w, 0.0) + jnp.log(1.0 + jnp.exp(-jnp.abs(x_w)))) - 0.5
    a = _sigmoid(a0_ref[...] + lora[:, dim:2 * dim])
    g = lora[:, 2 * dim:3 * dim]
    bd = bd_ref[...]
    kk = k * kk_ref[...]
    kk = kk * lax.rsqrt(jnp.maximum(_head_sums(kk * kk, bd), 1e-24))
    k = k * (1.0 + (a - 1.0) * ka_ref[...])
    r_s[...] = r
    k_s[...] = k
    v_s[...] = v
    a_s[...] = -kk
    b_s[...] = kk * a
    ld_s[...] = -jnp.exp(w_log)

    row_c = lax.broadcasted_iota(jnp.int32, (c, c), 0)
    col_c = lax.broadcasted_iota(jnp.int32, (c, c), 1)
    tri_incl = (row_c >= col_c).astype(BF16)
    head0 = lax.broadcasted_iota(jnp.int32, (c, LANES), 1) < HEAD
    lane_2 = lax.broadcasted_iota(jnp.int32, (c, 2 * LANES), 1)
    head0_2 = jnp.where(lane_2 >= LANES, lane_2 - LANES, lane_2) < HEAD
    row_4 = lax.broadcasted_iota(jnp.int32, (c, 4 * c), 0)
    col_4 = jnp.bitwise_and(lax.broadcasted_iota(jnp.int32, (c, 4 * c), 1), c - 1)
    strict4 = row_4 > col_4
    incl4 = row_4 >= col_4
    eye4 = (row_4 == col_4).astype(F32)
    bd4 = jnp.bitwise_xor(lax.broadcasted_iota(jnp.int32, (4 * c, 4 * c), 0),
                          lax.broadcasted_iota(jnp.int32, (4 * c, 4 * c), 1)) < c
    row_p = lax.broadcasted_iota(jnp.int32, (LANES, LANES), 0)
    col_p = lax.broadcasted_iota(jnp.int32, (LANES, LANES), 1)
    same_head = (row_p < HEAD) == (col_p < HEAD)
    diag_p = row_p == col_p
    zeros_cl = jnp.zeros((c, LANES), BF16)

    def group_body():
        items = [(slice(q * c, (q + 1) * c), slice(j * LANES, (j + 1) * LANES), j)
                 for q in range(GROUP_CHUNKS) for j in range(n_pairs)]
        n_it = len(items)
        loaded = [[ref[rows, lanes] for ref in (r_s, k_s, v_s, a_s, b_s, ld_s)] for rows, lanes, _ in items]
        r_c, k_c, v_c, a_c, b_c, ld_c = [list(t) for t in zip(*loaded)]
        cum = []
        for i in range(n_it):
            ld_hi, ld_lo = _split(ld_c[i])
            both = _dot(tri_incl, jnp.concatenate([ld_hi, ld_lo], axis=1))
            cum.append(both[:, 0:LANES] + both[:, LANES:])
        gap()
        cum_end = [x[c - 1:c, :] for x in cum]
        at = [a_c[i] * jnp.exp(cum[i] - ld_c[i]) for i in range(n_it)]
        rt = [r_c[i] * jnp.exp(cum[i]) for i in range(n_it)]
        v_bf = [_bf(x) for x in v_c]
        def by_head(x, mask):
            return jnp.concatenate([jnp.where(mask, x, 0), jnp.where(mask, 0, x)], axis=0)

        aa = []
        for i in range(n_it):
            e_neg = jnp.exp(-cum[i])
            rhs = jnp.concatenate([by_head(_bf(b_c[i] * e_neg), head0), by_head(_bf(k_c[i] * e_neg), head0)], axis=0)
            aa.append(_dot_nt(_bf(jnp.concatenate([at[i], rt[i]], axis=0)), rhs))
        gap()
        t_cat = [jnp.where(strict4, x[0:c, :], 0.0) for x in aa]
        q_cat = [_bf(jnp.where(incl4, x[c:2 * c, :], 0.0)) for x in aa]
        akv = [_dot(_bf(t_cat[i][:, LANES:]), by_head(v_bf[i], head0)) for i in range(n_it)]
        n_grp = n_it // 2

        def bdiag(x):
            return jnp.where(bd4, jnp.concatenate([x, x, x, x], axis=0), 0)

        p4 = [jnp.concatenate([t_cat[2 * g][:, 0:LANES], t_cat[2 * g + 1][:, 0:LANES]], axis=1) for g in range(n_grp)]
        inv4 = [eye4 + x for x in p4]
        p4_bf = [_bf(x) for x in p4]
        pw = [_bf(_dot(x, bdiag(x))) for x in p4_bf]
        gap()
        for _ in range(4):
            both = [_dot(jnp.concatenate([pw[g], _bf(inv4[g])], axis=0), bdiag(pw[g])) for g in range(n_grp)]
            inv4 = [inv4[g] + both[g][c:, :] for g in range(n_grp)]
            pw = [_bf(both[g][0:c, :]) for g in range(n_grp)]
            gap()
        inv4 = [_bf(inv4[g] + _dot(_bf(inv4[g]), bdiag(pw[g]))) for g in range(n_grp)]
        inv_cat = [inv4[i // 2][:, (i % 2) * LANES:(i % 2 + 1) * LANES] for i in range(n_it)]
        x_mat = [_bf(jnp.concatenate([at[i], akv[i]], axis=1)) for i in range(n_it)]
        tx = [_dot(inv_cat[i], by_head(x_mat[i], head0_2)) for i in range(n_it)]
        gap()
        tx_bf = [_bf(x) for x in tx]
        v_pad = [jnp.concatenate([zeros_cl, v_bf[i]], axis=1) for i in range(n_it)]
        w_mat = [jnp.concatenate([tx_bf[i], v_pad[i]], axis=0) for i in range(n_it)]
        qw = [_dot(q_cat[i], jnp.concatenate([by_head(tx_bf[i], head0_2), by_head(v_pad[i], head0_2)], axis=0))
              for i in range(n_it)]
        mn = []
        for i in range(n_it):
            e_end = jnp.exp(cum_end[i] - cum[i])
            bk_t = _bf(jnp.concatenate([b_c[i] * e_end, k_c[i] * e_end], axis=0).T)
            mn.append(_dot(bk_t, w_mat[i]))
        gap()
        g_col = [jnp.sum(jnp.where(diag_p, jnp.exp(cum_end[i]), 0.0), axis=1, keepdims=True) for i in range(n_it)]
        states = [st_ref[j] for j in range(n_pairs)]
        for i, (rows, lanes, j) in enumerate(items):
            st = states[j]
            st_bf = _bf(st)
            y_s[rows, lanes] = _dot(_bf(rt[i] + qw[i][:, 0:LANES]), st_bf) + qw[i][:, LANES:]
            m_low = _bf(jnp.where(same_head, mn[i][:, 0:LANES], 0.0))
            states[j] = g_col[i] * st + _dot(m_low, st_bf) + jnp.where(same_head, mn[i][:, LANES:], 0.0)
        for j in range(n_pairs):
            st_ref[j] = jnp.where(ends_sequence, 0.0, states[j])

    group_body()

    y = y_s[...]
    inv_n = 1.0 / HEAD
    mean = _head_sums(y, bd) * inv_n
    d = y - mean
    var = _head_sums(d * d, bd) * inv_n
    y = d * lax.rsqrt(var + RWKV_GN_EPS) * lnw_ref[...] + lnb_ref[...]
    r = r_s[...]
    bonus = _head_sums(r * k_s[...] * rk_ref[...], bd) * v_s[...]
    o_ref[...] = ((y + bonus) * g).astype(o_ref.dtype)


def _conv_steps(p_conv, ends_sequence, w_ref, b_ref, lnw_ref, lnb_ref, shift_ref, o_ref, hbuf):
    tt, dim = o_ref.shape
    sub = CONV_SUB
    ext = sub + 8

    hbuf[CONV_HALO:CONV_HALO + tt, :] = p_conv[:, 0:dim] * _sigmoid(p_conv[:, dim:2 * dim])
    yield
    for s in range(tt // sub):
        base = CONV_HALO + s * sub - 8
        wins = [hbuf[base - 8 * a:base - 8 * a + ext, :] for a in range(CONV_HALO // 8)]
        parts = []
        for r in range(8):
            lags = [8 * a + r for a in range(len(wins)) if 8 * a + r < CONV_WIDTH]
            terms = [wins[m // 8] * w_ref[CONV_WIDTH - 1 - m:CONV_WIDTH - m, :] for m in lags]
            parts.append(functools.reduce(jnp.add, terms))
        stacked = _bf(jnp.concatenate(parts, axis=0))
        acc = _dot(shift_ref[...], stacked) + b_ref[...]
        mu = jnp.mean(acc, axis=-1, keepdims=True)
        d = acc - mu
        var = jnp.mean(d * d, axis=-1, keepdims=True)
        y = d * lax.rsqrt(var + LN_EPS) * lnw_ref[...] + lnb_ref[...]
        o_ref[s * sub:(s + 1) * sub, :] = (y * _sigmoid(y)).astype(o_ref.dtype)
        if s + 1 < tt // sub:
            yield
    hbuf[0:CONV_HALO, :] = jnp.where(ends_sequence, 0.0, hbuf[tt:tt + CONV_HALO, :])


def _conv_shift_matrix():
    ext = CONV_SUB + 8
    u_idx = jnp.arange(CONV_SUB)[:, None]
    col = jnp.arange(8 * ext)[None, :]
    return (col % ext == u_idx + 8 - col // ext).astype(BF16)


def _flash_kernel(q_ref, k_ref, v_ref, o_ref, m_s, acc_s, s_buf, p_buf, al_buf, qt_s):
    i = pl.program_id(2)
    t = FLASH_BLOCK
    m_s[...] = jnp.full(m_s.shape, -jnp.inf, F32)
    acc_s[...] = jnp.zeros(acc_s.shape, F32)
    p_buf[1] = jnp.zeros(p_buf.shape[1:], BF16)
    al_buf[1] = jnp.ones(al_buf.shape[1:], F32)
    for h in range(2):
        qt_s[h] = q_ref[:, h * LANES:(h + 1) * LANES].astype(F32).T.astype(BF16)
    causal = lax.broadcasted_iota(jnp.int32, (t, t), 0) <= lax.broadcasted_iota(jnp.int32, (t, t), 1)

    def keys_of(j):
        return pl.ds(pl.multiple_of(j * t, t), t)

    def scores(j, slot, u0):
        for h in range(2):
            cols = slice(h * LANES, (h + 1) * LANES)
            s_buf[slot, h, :, u0 * t:] = _dot(k_ref[keys_of(j), cols], qt_s[h, :, u0 * t:])

    def softmax(slot, u0, masked_sub):
        for u in range(u0, FLASH_SUBS):
            qs = slice(u * t, (u + 1) * t)
            for h in range(2):
                s = s_buf[slot, h, :, qs]
                if u == masked_sub:
                    s = jnp.where(causal, s, -jnp.inf)
                m_prev = m_s[h, :, qs]
                m_new = jnp.maximum(m_prev, jnp.max(s, axis=0, keepdims=True))
                m_s[h, :, qs] = m_new
                al_buf[slot, h, :, qs] = jnp.exp2(m_prev - m_new)
                p_buf[slot, h, :, qs] = jnp.exp2((s - m_new).astype(BF16))

    def accumulate(j, slot, u0):
        for h in range(2):
            v_blk = v_ref[keys_of(j), h * LANES:h * LANES + FLASH_VROWS]
            pv = lax.dot_general(v_blk, p_buf[slot, h, :, u0 * t:], (((0,), (0,)), ((), ())),
                                 preferred_element_type=F32)
            acc_s[h, :, u0 * t:] = al_buf[slot, h, :, u0 * t:] * acc_s[h, :, u0 * t:] + pv

    n_full = FLASH_SUBS * i
    scores(0, 0, 0)

    def body(g, carry):
        for par in range(2):
            k = 2 * g + par
            scores(k + 1, 1 - par, 0)
            accumulate(jnp.maximum(k - 1, 0), 1 - par, 0)
            softmax(par, 0, None)
        return carry

    lax.fori_loop(0, n_full // 2, body, 0)
    accumulate(jnp.maximum(n_full - 1, 0), 1, 0)
    for d in range(FLASH_SUBS):
        if d + 1 < FLASH_SUBS:
            scores(n_full + d + 1, (d + 1) % 2, d + 1)
        softmax(d % 2, d, d)
        accumulate(n_full + d, d % 2, d)
    out_t = jnp.concatenate([acc_s[h, 0:HEAD, :] / acc_s[h, HEAD:HEAD + 1, :] for h in range(2)], axis=0)
    o_ref[...] = out_t.T.astype(o_ref.dtype)


def _flash(q, k, v, batch, seq, n_heads):
    n = q.shape[0]
    tq = FLASH_SUBS * FLASH_BLOCK
    n_q = seq // tq
    return pl.pallas_call(
        _flash_kernel,
        grid=(batch, n_heads // 2, n_q),
        in_specs=[pl.BlockSpec((tq, 2 * LANES), lambda b, h, i: (b * n_q + i, h)),
                  pl.BlockSpec((seq, 2 * LANES), lambda b, h, i: (b, h)),
                  pl.BlockSpec((seq, 2 * LANES), lambda b, h, i: (b, h))],
        out_specs=pl.BlockSpec((tq, LANES), lambda b, h, i: (b * n_q + i, h)),
        out_shape=jax.ShapeDtypeStruct((n, n_heads * HEAD), BF16),
        scratch_shapes=[pltpu.VMEM((2, 1, tq), F32),
                        pltpu.VMEM((2, FLASH_VROWS, tq), F32),
                        pltpu.VMEM((2, 2, FLASH_BLOCK, tq), F32),
                        pltpu.VMEM((2, 2, FLASH_BLOCK, tq), BF16),
                        pltpu.VMEM((2, 2, 1, tq), F32),
                        pltpu.VMEM((2, LANES, tq), BF16)],
        compiler_params=_params("parallel", "parallel", "arbitrary"),
        name="mla_flash",
    )(q, k, v)


def _outffn_kernel(ya_ref, yb_ref, yc_ref, x_ref, mod_ref, gpm_ref, gpf_ref, gqf_ref, wo_ref, w1_ref, w2_ref,
                   o_ref, x1_s, hid_s, *, da, db, tf, layer):
    gpm_ref, gpf_ref, gqf_ref = _pick_row((gpm_ref, gpf_ref, gqf_ref), layer)
    half = x_ref.shape[0] // 2
    halves = [slice(0, half), slice(half, 2 * half)]
    ys = [_dot(ya_ref[r, :], wo_ref[0:da, :]) + _dot(yb_ref[r, :], wo_ref[da:da + db, :])
          + _dot(yc_ref[r, :], wo_ref[da + db:, :]) for r in halves]
    hs = []
    for r, y in zip(halves, ys):
        x1 = x_ref[r, :] + mod_ref[0, 2:3, :] * (_rms(y) * gpm_ref[...])
        x1_s[r, :] = x1
        h = _rms(x1) * gpf_ref[...]
        hs.append((h * (1.0 + mod_ref[0, 4:5, :]) + mod_ref[0, 3:4, :]).astype(BF16))
    for f in range(hid_s.shape[1] // tf):
        cols = slice(f * tf, (f + 1) * tf)
        for r, h in zip(halves, hs):
            hid = jnp.maximum(_dot(h, w1_ref[:, cols]), 0.0)
            hid_s[r, cols] = (hid * hid).astype(BF16)
    for r in halves:
        y2 = _dot(hid_s[r, :], w2_ref[...])
        o_ref[r, :] = x1_s[r, :] + mod_ref[0, 5:6, :] * (_rms(y2) * gqf_ref[...])


def _outffn(ya, yb, yc, x, layer, mod_all, gains, w_out, w_ff1, w_ff2, seq):
    n, d = x.shape
    d_ff = w_ff1.shape[2]
    tm, tf = 512, 512
    per_b = seq // tm
    row = lambda w: pl.BlockSpec((tm, w), lambda i: (i, 0))
    resident = lambda a: pl.BlockSpec((None,) + a.shape[1:], lambda i: (layer, 0, 0), pipeline_mode=pl.Buffered(1))
    return pl.pallas_call(
        functools.partial(_outffn_kernel, da=ya.shape[1], db=yb.shape[1], tf=tf, layer=layer),
        grid=(n // tm,),
        in_specs=[row(ya.shape[1]), row(yb.shape[1]), row(yc.shape[1]), row(d),
                  pl.BlockSpec((None, 1, 6, d), lambda i: (layer, i // per_b, 0, 0))]
                 + [_layer_spec(a, layer) for a in gains] + [resident(w_out), resident(w_ff1), resident(w_ff2)],
        out_specs=row(d),
        out_shape=jax.ShapeDtypeStruct((n, d), F32),
        scratch_shapes=[pltpu.VMEM((tm, d), F32), pltpu.VMEM((tm, d_ff), BF16)],
        compiler_params=_params("parallel"),
        name="out_ffn",
    )(ya, yb, yc, x, mod_all, *gains, w_out, w_ff1, w_ff2)


def _rot_cols(w):
    half = w.shape[-1] // 2
    return jnp.concatenate([-w[..., half:], w[..., :half]], axis=-1)


def kernel(x, c, positions, g_pre_mix, g_post_mix, g_pre_ffn, g_post_ffn, w_ada, b_ada, w_in, w_out, rwkv_mu, rwkv_w0, rwkv_w2, rwkv_a0, rwkv_a2, rwkv_g2, rwkv_k_k, rwkv_k_a, rwkv_r_k, rwkv_ln_w, rwkv_ln_b, conv_w, conv_b, conv_ln_w, conv_ln_b, mla_q_norm, mla_w_uq, mla_kv_norm, mla_w_ukv, w_ff1, w_ff2):
    batch, seq, d = x.shape
    n = batch * seq
    n_layers = w_ada.shape[0]
    r_dim = rwkv_w0.shape[1]
    w_lora, a_lora, g_lora = rwkv_w2.shape[1], rwkv_a2.shape[1], rwkv_g2.shape[1]
    assert w_lora + a_lora + g_lora == LANES and r_dim % LANES == 0
    n_r = 3 * r_dim + LANES
    c_dim = conv_w.shape[2]
    n_c = 2 * c_dim
    lora = mla_q_norm.shape[1]
    n_heads = mla_w_ukv.shape[2] // (2 * HEAD)
    rope = mla_w_uq.shape[2] // n_heads - HEAD
    assert HEAD + 2 * rope == LANES
    scale = float(HEAD + rope) ** -0.5 * LOG2_E

    xf = x.reshape(n, d)
    c8 = jnp.pad(c, ((0, 8 - batch), (0, 0)))
    mod_all = _ada(c8, w_ada, b_ada.reshape(n_layers, 1, 6 * d))[:, :batch].reshape(n_layers, batch, 6, d)

    inv_freq = 1.0 / (ROPE_THETA ** (jnp.arange(0, rope, 2, dtype=F32) / rope))
    invf = jnp.tile(inv_freq, LANES // inv_freq.shape[0]).reshape(1, LANES)
    phase = jnp.where(jnp.arange(LANES) < LANES - rope, 0.0, -0.5 * jnp.pi).astype(F32).reshape(1, LANES)
    cs = _rope_table(positions.reshape(n, 1), invf, phase)

    head_id = jnp.arange(2 * LANES) // HEAD
    bd = (head_id[:, None] == head_id[None, :]).astype(BF16)
    conv_shift = _conv_shift_matrix()

    vec = lambda a: a.reshape(n_layers, -1)
    w_in_bf, w_out_bf, w_ff1_bf, w_ff2_bf = [w.astype(BF16) for w in (w_in, w_out, w_ff1, w_ff2)]
    k_pe = w_in_bf[:, :, n_r + n_c + 2 * lora:]
    w_extra = jnp.concatenate([jnp.zeros((n_layers, d, HEAD), BF16), k_pe, _rot_cols(k_pe)], axis=-1)
    wq = mla_w_uq.reshape(n_layers, lora, n_heads, HEAD + rope)
    wuq = jnp.concatenate([wq, _rot_cols(wq[..., HEAD:])], axis=-1).reshape(n_layers, lora, n_heads * LANES)
    wkv = mla_w_ukv.reshape(n_layers, lora, n_heads, 2 * HEAD)
    pad = jnp.zeros_like(wkv[..., :HEAD])
    wuk = jnp.concatenate([wkv[..., :HEAD], pad], axis=-1).reshape(n_layers, lora, n_heads * LANES)
    wuv = jnp.concatenate([wkv[..., HEAD:], pad], axis=-1).reshape(n_layers, lora, n_heads * LANES)
    lora_w = jnp.zeros((n_layers, LANES, 3 * r_dim), F32)
    lora_w = lora_w.at[:, 0:w_lora, 0:r_dim].set(rwkv_w2)
    lora_w = lora_w.at[:, w_lora:w_lora + a_lora, r_dim:2 * r_dim].set(rwkv_a2)
    lora_w = lora_w.at[:, w_lora + a_lora:, 2 * r_dim:].set(rwkv_g2)
    layered = ([vec(g_pre_mix), w_in_bf, w_extra],
               [vec(mla_q_norm), vec(mla_kv_norm), wuq.astype(BF16), wuk.astype(BF16), wuv.astype(BF16)],
               [conv_w, vec(conv_b), vec(conv_ln_w), vec(conv_ln_b)],
               [vec(rwkv_mu), vec(rwkv_w0), vec(rwkv_a0), vec(rwkv_k_k), vec(rwkv_k_a), vec(rwkv_r_k),
                vec(rwkv_ln_w), vec(rwkv_ln_b), lora_w.astype(BF16)])
    gains = [vec(g_post_mix), vec(g_pre_ffn), vec(g_post_ffn)]
    for l in range(n_layers):
        y_a, y_b, q, k, v = _inproj(xf, l, mod_all, cs, layered, (conv_shift, bd),
                                    seq, n_r, n_c, lora, n_heads, rope, scale, r_dim, w_lora, a_lora)
        y_c = _flash(q, k, v, batch, seq, n_heads)
        xf = _outffn(y_a, y_b, y_c, xf, l, mod_all, gains, w_out_bf, w_ff1_bf, w_ff2_bf, seq)
    return xf.reshape(batch, seq, d)
```

```python
import functools
import itertools

import jax
import jax.numpy as jnp
from jax import lax
from jax.experimental import pallas as pl
from jax.experimental.pallas import tpu as pltpu

F32 = jnp.float32
BF16 = jnp.bfloat16

LANES = 128
HEAD = 64
NORM_EPS = 1e-6
LN_EPS = 1e-5
RWKV_GN_EPS = 64e-5
ROPE_THETA = 10000.0
CONV_WIDTH = 31
CONV_HALO = 32
CONV_SUB = 128
CHUNK = 64
FLASH_BLOCK = 256
FLASH_VROWS = HEAD + 16
FLASH_SUBS = 8
GROUP_CHUNKS = 8
VMEM_LIMIT = 56 * 1024 * 1024
LOG2_E = 1.4426950408889634


def _dot(a, b):
    return jnp.dot(a, b, preferred_element_type=F32)


def _dot_nt(a, b):
    return lax.dot_general(a, b, (((1,), (1,)), ((), ())), preferred_element_type=F32)


def _bf(x):
    return x.astype(BF16)


def _split(x):
    hi = x.astype(BF16)
    return hi, (x - hi.astype(F32)).astype(BF16)


def _head_sums(x, ones_bd):
    width = x.shape[1]
    groups = [_dot(_bf(x[:, lo:lo + 2 * LANES]), ones_bd) for lo in range(0, width - 2 * LANES + 1, 2 * LANES)]
    if width % (2 * LANES):
        groups.append(_dot(_bf(x[:, width - LANES:]), ones_bd[0:LANES, 0:LANES]))
    return jnp.concatenate(groups, axis=1)


def _rms(x):
    return x * lax.rsqrt(jnp.mean(x * x, axis=-1, keepdims=True) + NORM_EPS)


def _sigmoid(x):
    return 1.0 / (1.0 + jnp.exp(-x))


def _params(*semantics):
    return pltpu.CompilerParams(dimension_semantics=semantics, vmem_limit_bytes=VMEM_LIMIT)


def _ada_kernel(c_ref, w_ref, b_ref, o_ref):
    c = c_ref[...]
    cs = c * _sigmoid(c)
    cs_hi, cs_lo = _split(cs)
    w_hi, w_lo = _split(w_ref[0])
    o_ref[0] = _dot(cs_hi, w_hi) + _dot(cs_hi, w_lo) + _dot(cs_lo, w_hi) + b_ref[0]


def _ada(c8, w_ada, b_ada):
    n_layers, d, d6 = w_ada.shape
    tn = 1536
    return pl.pallas_call(
        _ada_kernel,
        grid=(n_layers, d6 // tn),
        in_specs=[pl.BlockSpec((8, d), lambda l, j: (0, 0)),
                  pl.BlockSpec((1, d, tn), lambda l, j: (l, 0, j)),
                  pl.BlockSpec((1, 1, tn), lambda l, j: (l, 0, j))],
        out_specs=pl.BlockSpec((1, 8, tn), lambda l, j: (l, 0, j)),
        out_shape=jax.ShapeDtypeStruct((n_layers, 8, d6), F32),
        compiler_params=_params("parallel", "parallel"),
        name="ada_mod",
    )(c8, w_ada, b_ada)


def _rope_kernel(pos_ref, invf_ref, phase_ref, cs_ref):
    cs_ref[...] = jnp.cos(invf_ref[...] * pos_ref[...].astype(F32) + phase_ref[...])


def _rope_table(pos_row, invf, phase):
    n = pos_row.shape[1]
    rows = invf.shape[0]
    tn = 2048
    col = pl.BlockSpec((rows, 1), lambda i: (0, 0))
    return pl.pallas_call(
        _rope_kernel,
        grid=(n // tn,),
        in_specs=[pl.BlockSpec((1, tn), lambda i: (0, i)), col, col],
        out_specs=pl.BlockSpec((rows, tn), lambda i: (0, i)),
        out_shape=jax.ShapeDtypeStruct((rows, n), F32),
        compiler_params=_params("parallel"),
        name="rope_table",
    )(pos_row, invf, phase)


def _inproj_kernel(x_ref, mod_ref, g_ref, w_ref, wx_ref, cs_ref, qn_ref, kvn_ref, wuq_ref, wuk_ref, wuv_ref,
                   cw_ref, cb_ref, clw_ref, clb_ref, shift_ref, *rest,
                   n_r, n_c, lora, n_heads, rope, scale, per_b, w_lora, a_lora, layer):
    g_ref, qn_ref, kvn_ref, cb_ref, clw_ref, clb_ref = _pick_row(
        (g_ref, qn_ref, kvn_ref, cb_ref, clw_ref, clb_ref), layer)
    rwkv_in = _pick_row(rest[:8], layer) + list(rest[8:10])
    ya_ref, yb_ref, q_ref, k_ref, v_ref, hbuf, zbuf, st_ref = rest[10:18]
    rwkv_scratch = rest[18:]

    @pl.when(pl.program_id(0) == 0)
    def _():
        hbuf[0:CONV_HALO, :] = jnp.zeros((CONV_HALO, hbuf.shape[1]), F32)
        zbuf[0:8, :] = jnp.zeros((8, zbuf.shape[1]), F32)
        st_ref[...] = jnp.zeros(st_ref.shape, F32)

    ends_sequence = (pl.program_id(0) + 1) % per_b == 0

    h = _rms(x_ref[...]) * g_ref[...]
    h = (h * (1.0 + mod_ref[0, 1:2, :]) + mod_ref[0, 0:1, :]).astype(BF16)
    p_rwkv = _dot(h, w_ref[:, :n_r])
    p_conv = _dot(h, w_ref[:, n_r:n_r + n_c])

    def mla_steps():
        lat = _dot(h, w_ref[:, n_r + n_c:n_r + n_c + 2 * lora])
        k_rot = _dot(h, wx_ref[...])
        cs_half = cs_ref[...].T
        cs = jnp.concatenate([cs_half, cs_half], axis=1)
        qn = (_rms(lat[:, 0:lora]) * qn_ref[...]).astype(BF16)
        kvn = (_rms(lat[:, lora:]) * kvn_ref[...]).astype(BF16)
        lane = lax.broadcasted_iota(jnp.int32, cs.shape, 1)
        t = k_rot * cs
        kpe = jnp.where(lane < HEAD, 0.0, t + pltpu.roll(t, rope, 1) + pltpu.roll(t, LANES - rope, 1))
        q_mul = jnp.where(lane < HEAD, 1.0, cs) * scale
        q_mul2 = jnp.concatenate([q_mul, q_mul], axis=1)
        kpe2 = jnp.concatenate([kpe, kpe], axis=1)
        is_v = jnp.concatenate([lane < HEAD, lane < HEAD], axis=1)
        yield
        for j in range(n_heads // 2):
            cols = slice(2 * j * LANES, 2 * (j + 1) * LANES)
            q_ref[:, cols] = (_dot(qn, wuq_ref[:, cols]) * q_mul2).astype(BF16)
            k_ref[:, cols] = (_dot(kvn, wuk_ref[:, cols]) + kpe2).astype(BF16)
            v_ref[:, cols] = jnp.where(is_v, _dot(kvn, wuv_ref[:, cols]), 1.0).astype(BF16)
            yield

    fill = itertools.chain(mla_steps(),
                           _conv_steps(p_conv, ends_sequence, cw_ref, cb_ref, clw_ref, clb_ref, shift_ref, yb_ref, hbuf))
    _rwkv_tile(p_rwkv, ends_sequence, *rwkv_in, ya_ref, zbuf, st_ref, *rwkv_scratch,
               w_lora=w_lora, a_lora=a_lora, fill=fill)
    for _ in fill:
        pass


def _layer_spec(a, layer):
    if a.ndim == 2:
        return pl.BlockSpec(a.shape, lambda i: (0, 0))
    return pl.BlockSpec((None,) + a.shape[1:], lambda i: (layer,) + (0,) * (a.ndim - 1))


def _pick_row(refs, layer):
    return [r.at[layer:layer + 1, :] for r in refs]


def _inproj(x, layer, mod_all, cs, layered, shared, seq, n_r, n_c, lora, n_heads, rope, scale, r_dim, w_lora, a_lora):
    n, d = x.shape
    tm = CHUNK * GROUP_CHUNKS
    per_b = seq // tm
    row = lambda w: pl.BlockSpec((tm, w), lambda i: (i, 0))
    full = lambda a: pl.BlockSpec(a.shape, lambda i: (0, 0))
    act = pltpu.VMEM((tm, r_dim), F32)
    g_w_wx, mla, conv, rwkv = layered
    conv_shift, bd = shared
    return pl.pallas_call(
        functools.partial(_inproj_kernel, n_r=n_r, n_c=n_c, lora=lora, n_heads=n_heads, rope=rope, scale=scale,
                          per_b=per_b, w_lora=w_lora, a_lora=a_lora, layer=layer),
        grid=(n // tm,),
        in_specs=[row(d), pl.BlockSpec((None, 1, 6, d), lambda i: (layer, i // per_b, 0, 0))]
                 + [_layer_spec(a, layer) for a in g_w_wx] + [pl.BlockSpec((cs.shape[0], tm), lambda i: (0, i))]
                 + [_layer_spec(a, layer) for a in mla]
                 + [_layer_spec(a, layer) for a in conv] + [full(conv_shift)]
                 + [_layer_spec(a, layer) for a in rwkv] + [full(bd)],
        out_specs=[row(r_dim), row(n_c // 2)] + [row(n_heads * LANES)] * 3,
        out_shape=[jax.ShapeDtypeStruct((n, r_dim), BF16), jax.ShapeDtypeStruct((n, n_c // 2), BF16)]
                  + [jax.ShapeDtypeStruct((n, n_heads * LANES), BF16)] * 3,
        scratch_shapes=[pltpu.VMEM((tm + CONV_HALO, n_c // 2), F32),
                        pltpu.VMEM((tm + 8, n_r), F32),
                        pltpu.VMEM((r_dim // LANES, LANES, LANES), F32),
                        act, act, act, act, act, act, act],
        compiler_params=_params("arbitrary"),
        name="in_proj",
    )(x, mod_all, *g_w_wx, cs, *mla, *conv, conv_shift, *rwkv, bd)


def _rwkv_tile(p, ends_sequence, mu_ref, w0_ref, a0_ref, kk_ref, ka_ref, rk_ref, lnw_ref, lnb_ref, lora_ref, bd_ref,
               o_ref, zbuf, st_ref, r_s, k_s, v_s, a_s, b_s, ld_s, y_s, *, w_lora, a_lora, fill):
    tt, dim = o_ref.shape
    assert tt == CHUNK * GROUP_CHUNKS

    def gap():
        next(fill, None)

    n_pairs = dim // LANES
    c = CHUNK

    zbuf[8:8 + tt, :] = p
    prev = zbuf[7:7 + tt, :]
    zbuf[7:8, :] = jnp.where(ends_sequence, 0.0, zbuf[tt + 7:tt + 8, :])
    z = p + (prev - p) * mu_ref[...]

    r = z[:, 0:dim]
    k = z[:, dim:2 * dim]
    v = z[:, 2 * dim:3 * dim]
    lo = z[:, 3 * dim:3 * dim + LANES]
    lane = lax.broadcasted_iota(jnp.int32, lo.shape, 1)
    act = jnp.where(lane < w_lora, jnp.tanh(lo), jnp.where(lane < w_lora + a_lora, lo, _sigmoid(lo)))
    lora = _dot(_bf(act), lora_ref[...])
    x_w = w0_ref[...] + lora[:, 0:dim]
    w_log = -(jnp.maximum(-x_w, 0.0) + jnp.log(1.0 + jnp.exp(-jnp.abs(x_w)))) - 0.5
    a = _sigmoid(a0_ref[...] + lora[:, dim:2 * dim])
    g = lora[:, 2 * dim:3 * dim]
    bd = bd_ref[...]
    kk = k * kk_ref[...]
    kk = kk * lax.rsqrt(jnp.maximum(_head_sums(kk * kk, bd), 1e-24))
    k = k * (1.0 + (a - 1.0) * ka_ref[...])
    r_s[...] = r
    k_s[...] = k
    v_s[...] = v
    a_s[...] = -kk
    b_s[...] = kk * a
    ld_s[...] = -jnp.exp(w_log)

    row_c = lax.broadcasted_iota(jnp.int32, (c, c), 0)
    col_c = lax.broadcasted_iota(jnp.int32, (c, c), 1)
    tri_incl = (row_c >= col_c).astype(BF16)
    head0 = lax.broadcasted_iota(jnp.int32, (c, LANES), 1) < HEAD
    lane_2 = lax.broadcasted_iota(jnp.int32, (c, 2 * LANES), 1)
    head0_2 = jnp.where(lane_2 >= LANES, lane_2 - LANES, lane_2) < HEAD
    row_4 = lax.broadcasted_iota(jnp.int32, (c, 4 * c), 0)
    col_4 = jnp.bitwise_and(lax.broadcasted_iota(jnp.int32, (c, 4 * c), 1), c - 1)
    strict4 = row_4 > col_4
    incl4 = row_4 >= col_4
    eye4 = (row_4 == col_4).astype(F32)
    bd4 = jnp.bitwise_xor(lax.broadcasted_iota(jnp.int32, (4 * c, 4 * c), 0),
                          lax.broadcasted_iota(jnp.int32, (4 * c, 4 * c), 1)) < c
    row_p = lax.broadcasted_iota(jnp.int32, (LANES, LANES), 0)
    col_p = lax.broadcasted_iota(jnp.int32, (LANES, LANES), 1)
    same_head = (row_p < HEAD) == (col_p < HEAD)
    diag_p = row_p == col_p
    zeros_cl = jnp.zeros((c, LANES), BF16)

    def group_body():
        items = [(slice(q * c, (q + 1) * c), slice(j * LANES, (j + 1) * LANES), j)
                 for q in range(GROUP_CHUNKS) for j in range(n_pairs)]
        n_it = len(items)
        loaded = [[ref[rows, lanes] for ref in (r_s, k_s, v_s, a_s, b_s, ld_s)] for rows, lanes, _ in items]
        r_c, k_c, v_c, a_c, b_c, ld_c = [list(t) for t in zip(*loaded)]
        cum = []
        for i in range(n_it):
            ld_hi, ld_lo = _split(ld_c[i])
            both = _dot(tri_incl, jnp.concatenate([ld_hi, ld_lo], axis=1))
            cum.append(both[:, 0:LANES] + both[:, LANES:])
        gap()
        cum_end = [x[c - 1:c, :] for x in cum]
        at = [a_c[i] * jnp.exp(cum[i] - ld_c[i]) for i in range(n_it)]
        rt = [r_c[i] * jnp.exp(cum[i]) for i in range(n_it)]
        v_bf = [_bf(x) for x in v_c]
        def by_head(x, mask):
            return jnp.concatenate([jnp.where(mask, x, 0), jnp.where(mask, 0, x)], axis=0)

        aa = []
        for i in range(n_it):
            e_neg = jnp.exp(-cum[i])
            rhs = jnp.concatenate([by_head(_bf(b_c[i] * e_neg), head0), by_head(_bf(k_c[i] * e_neg), head0)], axis=0)
            aa.append(_dot_nt(_bf(jnp.concatenate([at[i], rt[i]], axis=0)), rhs))
        gap()
        t_cat = [jnp.where(strict4, x[0:c, :], 0.0) for x in aa]
        q_cat = [_bf(jnp.where(incl4, x[c:2 * c, :], 0.0)) for x in aa]
        akv = [_dot(_bf(t_cat[i][:, LANES:]), by_head(v_bf[i], head0)) for i in range(n_it)]
        n_grp = n_it // 2

        def bdiag(x):
            return jnp.where(bd4, jnp.concatenate([x, x, x, x], axis=0), 0)

        p4 = [jnp.concatenate([t_cat[2 * g][:, 0:LANES], t_cat[2 * g + 1][:, 0:LANES]], axis=1) for g in range(n_grp)]
        inv4 = [eye4 + x for x in p4]
        p4_bf = [_bf(x) for x in p4]
        pw = [_bf(_dot(x, bdiag(x))) for x in p4_bf]
        gap()
        for _ in range(4):
            both = [_dot(jnp.concatenate([pw[g], _bf(inv4[g])], axis=0), bdiag(pw[g])) for g in range(n_grp)]
            inv4 = [inv4[g] + both[g][c:, :] for g in range(n_grp)]
            pw = [_bf(both[g][0:c, :]) for g in range(n_grp)]
            gap()
        inv4 = [_bf(inv4[g] + _dot(_bf(inv4[g]), bdiag(pw[g]))) for g in range(n_grp)]
        inv_cat = [inv4[i // 2][:, (i % 2) * LANES:(i % 2 + 1) * LANES] for i in range(n_it)]
        x_mat = [_bf(jnp.concatenate([at[i], akv[i]], axis=1)) for i in range(n_it)]
        tx = [_dot(inv_cat[i], by_head(x_mat[i], head0_2)) for i in range(n_it)]
        gap()
        tx_bf = [_bf(x) for x in tx]
        v_pad = [jnp.concatenate([zeros_cl, v_bf[i]], axis=1) for i in range(n_it)]
        w_mat = [jnp.concatenate([tx_bf[i], v_pad[i]], axis=0) for i in range(n_it)]
        qw = [_dot(q_cat[i], jnp.concatenate([by_head(tx_bf[i], head0_2), by_head(v_pad[i], head0_2)], axis=0))
              for i in range(n_it)]
        mn = []
        for i in range(n_it):
            e_end = jnp.exp(cum_end[i] - cum[i])
            bk_t = _bf(jnp.concatenate([b_c[i] * e_end, k_c[i] * e_end], axis=0).T)
            mn.append(_dot(bk_t, w_mat[i]))
        gap()
        g_col = [jnp.sum(jnp.where(diag_p, jnp.exp(cum_end[i]), 0.0), axis=1, keepdims=True) for i in range(n_it)]
        states = [st_ref[j] for j in range(n_pairs)]
        for i, (rows, lanes, j) in enumerate(items):
            st = states[j]
            st_bf = _bf(st)
            y_s[rows, lanes] = _dot(_bf(rt[i] + qw[i][:, 0:LANES]), st_bf) + qw[i][:, LANES:]
            m_low = _bf(jnp.where(same_head, mn[i][:, 0:LANES], 0.0))
            states[j] = g_col[i] * st + _dot(m_low, st_bf) + jnp.where(same_head, mn[i][:, LANES:], 0.0)
        for j in range(n_pairs):
            st_ref[j] = jnp.where(ends_sequence, 0.0, states[j])

    group_body()

    y = y_s[...]
    inv_n = 1.0 / HEAD
    mean = _head_sums(y, bd) * inv_n
    d = y - mean
    var = _head_sums(d * d, bd) * inv_n
    y = d * lax.rsqrt(var + RWKV_GN_EPS) * lnw_ref[...] + lnb_ref[...]
    r = r_s[...]
    bonus = _head_sums(r * k_s[...] * rk_ref[...], bd) * v_s[...]
    o_ref[...] = ((y + bonus) * g).astype(o_ref.dtype)


def _conv_steps(p_conv, ends_sequence, w_ref, b_ref, lnw_ref, lnb_ref, shift_ref, o_ref, hbuf):
    tt, dim = o_ref.shape
    sub = CONV_SUB
    ext = sub + 8

    hbuf[CONV_HALO:CONV_HALO + tt, :] = p_conv[:, 0:dim] * _sigmoid(p_conv[:, dim:2 * dim])
    yield
    for s in range(tt // sub):
        base = CONV_HALO + s * sub - 8
        wins = [hbuf[base - 8 * a:base - 8 * a + ext, :] for a in range(CONV_HALO // 8)]
        parts = []
        for r in range(8):
            lags = [8 * a + r for a in range(len(wins)) if 8 * a + r < CONV_WIDTH]
            terms = [wins[m // 8] * w_ref[CONV_WIDTH - 1 - m:CONV_WIDTH - m, :] for m in lags]
            parts.append(functools.reduce(jnp.add, terms))
        stacked = _bf(jnp.concatenate(parts, axis=0))
        acc = _dot(shift_ref[...], stacked) + b_ref[...]
        mu = jnp.mean(acc, axis=-1, keepdims=True)
        d = acc - mu
        var = jnp.mean(d * d, axis=-1, keepdims=True)
        y = d * lax.rsqrt(var + LN_EPS) * lnw_ref[...] + lnb_ref[...]
        o_ref[s * sub:(s + 1) * sub, :] = (y * _sigmoid(y)).astype(o_ref.dtype)
        if s + 1 < tt // sub:
            yield
    hbuf[0:CONV_HALO, :] = jnp.where(ends_sequence, 0.0, hbuf[tt:tt + CONV_HALO, :])


def _conv_shift_matrix():
    ext = CONV_SUB + 8
    u_idx = jnp.arange(CONV_SUB)[:, None]
    col = jnp.arange(8 * ext)[None, :]
    return (col % ext == u_idx + 8 - col // ext).astype(BF16)


def _flash_kernel(q_ref, k_ref, v_ref, o_ref, m_s, acc_s, s_buf, p_buf, al_buf):
    i = pl.program_id(2)
    t = FLASH_BLOCK
    m_s[...] = jnp.full(m_s.shape, -jnp.inf, F32)
    acc_s[...] = jnp.zeros(acc_s.shape, F32)
    p_buf[1] = jnp.zeros(p_buf.shape[1:], BF16)
    al_buf[1] = jnp.ones(al_buf.shape[1:], F32)
    causal = lax.broadcasted_iota(jnp.int32, (t, t), 0) <= lax.broadcasted_iota(jnp.int32, (t, t), 1)

    def keys_of(j):
        return pl.ds(pl.multiple_of(j * t, t), t)

    def scores(j, slot, u0):
        for h in range(2):
            cols = slice(h * LANES, (h + 1) * LANES)
            s_buf[slot, h, :, u0 * t:] = _dot_nt(k_ref[keys_of(j), cols], q_ref[u0 * t:, cols])

    def softmax(slot, u0, masked_sub):
        for u in range(u0, FLASH_SUBS):
            qs = slice(u * t, (u + 1) * t)
            for h in range(2):
                s = s_buf[slot, h, :, qs]
                if u == masked_sub:
                    s = jnp.where(causal, s, -jnp.inf)
                m_prev = m_s[h, :, qs]
                m_new = jnp.maximum(m_prev, jnp.max(s, axis=0, keepdims=True))
                m_s[h, :, qs] = m_new
                al_buf[slot, h, :, qs] = jnp.exp2(m_prev - m_new)
                p_buf[slot, h, :, qs] = jnp.exp2((s - m_new).astype(BF16))

    def accumulate(j, slot, u0):
        for h in range(2):
            v_blk = v_ref[keys_of(j), h * LANES:h * LANES + FLASH_VROWS]
            pv = lax.dot_general(v_blk, p_buf[slot, h, :, u0 * t:], (((0,), (0,)), ((), ())),
                                 preferred_element_type=F32)
            acc_s[h, :, u0 * t:] = al_buf[slot, h, :, u0 * t:] * acc_s[h, :, u0 * t:] + pv

    n_full = FLASH_SUBS * i
    scores(0, 0, 0)

    def body(g, carry):
        for par in range(2):
            k = 2 * g + par
            scores(k + 1, 1 - par, 0)
            accumulate(jnp.maximum(k - 1, 0), 1 - par, 0)
            softmax(par, 0, None)
        return carry

    lax.fori_loop(0, n_full // 2, body, 0)
    accumulate(jnp.maximum(n_full - 1, 0), 1, 0)
    for d in range(FLASH_SUBS):
        if d + 1 < FLASH_SUBS:
            scores(n_full + d + 1, (d + 1) % 2, d + 1)
        softmax(d % 2, d, d)
        accumulate(n_full + d, d % 2, d)
    out_t = jnp.concatenate([acc_s[h, 0:HEAD, :] / acc_s[h, HEAD:HEAD + 1, :] for h in range(2)], axis=0)
    o_ref[...] = out_t.T.astype(o_ref.dtype)


def _flash(q, k, v, batch, seq, n_heads):
    n = q.shape[0]
    tq = FLASH_SUBS * FLASH_BLOCK
    n_q = seq // tq
    return pl.pallas_call(
        _flash_kernel,
        grid=(batch, n_heads // 2, n_q),
        in_specs=[pl.BlockSpec((tq, 2 * LANES), lambda b, h, i: (b * n_q + i, h)),
                  pl.BlockSpec((seq, 2 * LANES), lambda b, h, i: (b, h)),
                  pl.BlockSpec((seq, 2 * LANES), lambda b, h, i: (b, h))],
        out_specs=pl.BlockSpec((tq, LANES), lambda b, h, i: (b * n_q + i, h)),
        out_shape=jax.ShapeDtypeStruct((n, n_heads * HEAD), BF16),
        scratch_shapes=[pltpu.VMEM((2, 1, tq), F32),
                        pltpu.VMEM((2, FLASH_VROWS, tq), F32),
                        pltpu.VMEM((2, 2, FLASH_BLOCK, tq), F32),
                        pltpu.VMEM((2, 2, FLASH_BLOCK, tq), BF16),
                        pltpu.VMEM((2, 2, 1, tq), F32)],
        compiler_params=_params("parallel", "parallel", "arbitrary"),
        name="mla_flash",
    )(q, k, v)


def _outffn_kernel(ya_ref, yb_ref, yc_ref, x_ref, mod_ref, gpm_ref, gpf_ref, gqf_ref, wo_ref, w1_ref, w2_ref,
                   o_ref, x1_s, hid_s, *, da, db, tf, layer):
    gpm_ref, gpf_ref, gqf_ref = _pick_row((gpm_ref, gpf_ref, gqf_ref), layer)
    half = x_ref.shape[0] // 2
    halves = [slice(0, half), slice(half, 2 * half)]
    ys = [_dot(ya_ref[r, :], wo_ref[0:da, :]) + _dot(yb_ref[r, :], wo_ref[da:da + db, :])
          + _dot(yc_ref[r, :], wo_ref[da + db:, :]) for r in halves]
    hs = []
    for r, y in zip(halves, ys):
        x1 = x_ref[r, :] + mod_ref[0, 2:3, :] * (_rms(y) * gpm_ref[...])
        x1_s[r, :] = x1
        h = _rms(x1) * gpf_ref[...]
        hs.append((h * (1.0 + mod_ref[0, 4:5, :]) + mod_ref[0, 3:4, :]).astype(BF16))
    for f in range(hid_s.shape[1] // tf):
        cols = slice(f * tf, (f + 1) * tf)
        for r, h in zip(halves, hs):
            hid = jnp.maximum(_dot(h, w1_ref[:, cols]), 0.0)
            hid_s[r, cols] = (hid * hid).astype(BF16)
    for r in halves:
        y2 = _dot(hid_s[r, :], w2_ref[...])
        o_ref[r, :] = x1_s[r, :] + mod_ref[0, 5:6, :] * (_rms(y2) * gqf_ref[...])


def _outffn(ya, yb, yc, x, layer, mod_all, gains, w_out, w_ff1, w_ff2, seq):
    n, d = x.shape
    d_ff = w_ff1.shape[2]
    tm, tf = 512, 512
    per_b = seq // tm
    row = lambda w: pl.BlockSpec((tm, w), lambda i: (i, 0))
    resident = lambda a: pl.BlockSpec((None,) + a.shape[1:], lambda i: (layer, 0, 0), pipeline_mode=pl.Buffered(1))
    return pl.pallas_call(
        functools.partial(_outffn_kernel, da=ya.shape[1], db=yb.shape[1], tf=tf, layer=layer),
        grid=(n // tm,),
        in_specs=[row(ya.shape[1]), row(yb.shape[1]), row(yc.shape[1]), row(d),
                  pl.BlockSpec((None, 1, 6, d), lambda i: (layer, i // per_b, 0, 0))]
                 + [_layer_spec(a, layer) for a in gains] + [resident(w_out), resident(w_ff1), resident(w_ff2)],
        out_specs=row(d),
        out_shape=jax.ShapeDtypeStruct((n, d), F32),
        scratch_shapes=[pltpu.VMEM((tm, d), F32), pltpu.VMEM((tm, d_ff), BF16)],
        compiler_params=_params("parallel"),
        name="out_ffn",
    )(ya, yb, yc, x, mod_all, *gains, w_out, w_ff1, w_ff2)


def _rot_cols(w):
    half = w.shape[-1] // 2
    return jnp.concatenate([-w[..., half:], w[..., :half]], axis=-1)


def kernel(x, c, positions, g_pre_mix, g_post_mix, g_pre_ffn, g_post_ffn, w_ada, b_ada, w_in, w_out, rwkv_mu, rwkv_w0, rwkv_w2, rwkv_a0, rwkv_a2, rwkv_g2, rwkv_k_k, rwkv_k_a, rwkv_r_k, rwkv_ln_w, rwkv_ln_b, conv_w, conv_b, conv_ln_w, conv_ln_b, mla_q_norm, mla_w_uq, mla_kv_norm, mla_w_ukv, w_ff1, w_ff2):
    batch, seq, d = x.shape
    n = batch * seq
    n_layers = w_ada.shape[0]
    r_dim = rwkv_w0.shape[1]
    w_lora, a_lora, g_lora = rwkv_w2.shape[1], rwkv_a2.shape[1], rwkv_g2.shape[1]
    assert w_lora + a_lora + g_lora == LANES and r_dim % LANES == 0
    n_r = 3 * r_dim + LANES
    c_dim = conv_w.shape[2]
    n_c = 2 * c_dim
    lora = mla_q_norm.shape[1]
    n_heads = mla_w_ukv.shape[2] // (2 * HEAD)
    rope = mla_w_uq.shape[2] // n_heads - HEAD
    assert HEAD + 2 * rope == LANES
    scale = float(HEAD + rope) ** -0.5 * LOG2_E

    xf = x.reshape(n, d)
    c8 = jnp.pad(c, ((0, 8 - batch), (0, 0)))
    mod_all = _ada(c8, w_ada, b_ada.reshape(n_layers, 1, 6 * d))[:, :batch].reshape(n_layers, batch, 6, d)

    inv_freq = 1.0 / (ROPE_THETA ** (jnp.arange(0, rope, 2, dtype=F32) / rope))
    invf = jnp.tile(inv_freq, 2 * rope // inv_freq.shape[0]).reshape(2 * rope, 1)
    phase = jnp.where(jnp.arange(2 * rope) < rope, 0.0, -0.5 * jnp.pi).astype(F32).reshape(2 * rope, 1)
    cs = _rope_table(positions.reshape(1, n), invf, phase)

    head_id = jnp.arange(2 * LANES) // HEAD
    bd = (head_id[:, None] == head_id[None, :]).astype(BF16)
    conv_shift = _conv_shift_matrix()

    vec = lambda a: a.reshape(n_layers, -1)
    w_in_bf, w_out_bf, w_ff1_bf, w_ff2_bf = [w.astype(BF16) for w in (w_in, w_out, w_ff1, w_ff2)]
    k_pe = w_in_bf[:, :, n_r + n_c + 2 * lora:]
    w_extra = jnp.concatenate([jnp.zeros((n_layers, d, HEAD), BF16), k_pe, _rot_cols(k_pe)], axis=-1)
    wq = mla_w_uq.reshape(n_layers, lora, n_heads, HEAD + rope)
    wuq = jnp.concatenate([wq, _rot_cols(wq[..., HEAD:])], axis=-1).reshape(n_layers, lora, n_heads * LANES)
    wkv = mla_w_ukv.reshape(n_layers, lora, n_heads, 2 * HEAD)
    pad = jnp.zeros_like(wkv[..., :HEAD])
    wuk = jnp.concatenate([wkv[..., :HEAD], pad], axis=-1).reshape(n_layers, lora, n_heads * LANES)
    wuv = jnp.concatenate([wkv[..., HEAD:], pad], axis=-1).reshape(n_layers, lora, n_heads * LANES)
    lora_w = jnp.zeros((n_layers, LANES, 3 * r_dim), F32)
    lora_w = lora_w.at[:, 0:w_lora, 0:r_dim].set(rwkv_w2)
    lora_w = lora_w.at[:, w_lora:w_lora + a_lora, r_dim:2 * r_dim].set(rwkv_a2)
    lora_w = lora_w.at[:, w_lora + a_lora:, 2 * r_dim:].set(rwkv_g2)
    layered = ([vec(g_pre_mix), w_in_bf, w_extra],
               [vec(mla_q_norm), vec(mla_kv_norm), wuq.astype(BF16), wuk.astype(BF16), wuv.astype(BF16)],
               [conv_w, vec(conv_b), vec(conv_ln_w), vec(conv_ln_b)],
               [vec(rwkv_mu), vec(rwkv_w0), vec(rwkv_a0), vec(rwkv_k_k), vec(rwkv_k_a), vec(rwkv_r_k),
                vec(rwkv_ln_w), vec(rwkv_ln_b), lora_w.astype(BF16)])
    gains = [vec(g_post_mix), vec(g_pre_ffn), vec(g_post_ffn)]
    for l in range(n_layers):
        y_a, y_b, q, k, v = _inproj(xf, l, mod_all, cs, layered, (conv_shift, bd),
                                    seq, n_r, n_c, lora, n_heads, rope, scale, r_dim, w_lora, a_lora)
        y_c = _flash(q, k, v, batch, seq, n_heads)
        xf = _outffn(y_a, y_b, y_c, xf, l, mod_all, gains, w_out_bf, w_ff1_bf, w_ff2_bf, seq)
    return xf.reshape(batch, seq, d)
```

```python
import functools
import itertools

import jax
import jax.numpy as jnp
from jax import lax
from jax.experimental import pallas as pl
from jax.experimental.pallas import tpu as pltpu

F32 = jnp.float32
BF16 = jnp.bfloat16

LANES = 128
HEAD = 64
NORM_EPS = 1e-6
LN_EPS = 1e-5
RWKV_GN_EPS = 64e-5
ROPE_THETA = 10000.0
CONV_WIDTH = 31
CONV_HALO = 32
CONV_SUB = 128
CHUNK = 64
FLASH_BLOCK = 256
FLASH_VROWS = HEAD + 16
FLASH_SUBS = 8
GROUP_CHUNKS = 8
FFN_CAST_STEPS = 8
VMEM_LIMIT = 56 * 1024 * 1024
LOG2_E = 1.4426950408889634


def _dot(a, b):
    return jnp.dot(a, b, preferred_element_type=F32)


def _dot_nt(a, b):
    return lax.dot_general(a, b, (((1,), (1,)), ((), ())), preferred_element_type=F32)


def _bf(x):
    return x.astype(BF16)


def _split(x):
    hi = x.astype(BF16)
    return hi, (x - hi.astype(F32)).astype(BF16)


def _head_sums(x, ones_bd):
    width = x.shape[1]
    groups = [_dot(_bf(x[:, lo:lo + 2 * LANES]), ones_bd) for lo in range(0, width - 2 * LANES + 1, 2 * LANES)]
    if width % (2 * LANES):
        groups.append(_dot(_bf(x[:, width - LANES:]), ones_bd[0:LANES, 0:LANES]))
    return jnp.concatenate(groups, axis=1)


def _rms(x):
    return x * lax.rsqrt(jnp.mean(x * x, axis=-1, keepdims=True) + NORM_EPS)


def _sigmoid(x):
    return 1.0 / (1.0 + jnp.exp(-x))


def _params(*semantics):
    return pltpu.CompilerParams(dimension_semantics=semantics, vmem_limit_bytes=VMEM_LIMIT)


def _ada_kernel(c_ref, w_ref, b_ref, o_ref):
    c = c_ref[...]
    cs = c * _sigmoid(c)
    cs_hi, cs_lo = _split(cs)
    w_hi, w_lo = _split(w_ref[0])
    o_ref[0] = _dot(cs_hi, w_hi) + _dot(cs_hi, w_lo) + _dot(cs_lo, w_hi) + b_ref[0]


def _ada(c8, w_ada, b_ada):
    n_layers, d, d6 = w_ada.shape
    tn = 1536
    return pl.pallas_call(
        _ada_kernel,
        grid=(n_layers, d6 // tn),
        in_specs=[pl.BlockSpec((8, d), lambda l, j: (0, 0)),
                  pl.BlockSpec((1, d, tn), lambda l, j: (l, 0, j)),
                  pl.BlockSpec((1, 1, tn), lambda l, j: (l, 0, j))],
        out_specs=pl.BlockSpec((1, 8, tn), lambda l, j: (l, 0, j)),
        out_shape=jax.ShapeDtypeStruct((n_layers, 8, d6), F32),
        compiler_params=_params("parallel", "parallel"),
        name="ada_mod",
    )(c8, w_ada, b_ada)


def _rope_kernel(pos_ref, invf_ref, phase_ref, cs_ref):
    cs_ref[...] = jnp.cos(invf_ref[...] * pos_ref[...].astype(F32) + phase_ref[...])


def _rope_table(pos_row, invf, phase):
    n = pos_row.shape[1]
    rows = invf.shape[0]
    tn = 2048
    col = pl.BlockSpec((rows, 1), lambda i: (0, 0))
    return pl.pallas_call(
        _rope_kernel,
        grid=(n // tn,),
        in_specs=[pl.BlockSpec((1, tn), lambda i: (0, i)), col, col],
        out_specs=pl.BlockSpec((rows, tn), lambda i: (0, i)),
        out_shape=jax.ShapeDtypeStruct((rows, n), F32),
        compiler_params=_params("parallel"),
        name="rope_table",
    )(pos_row, invf, phase)


def _inproj_kernel(x_ref, mod_ref, g_ref, w_ref, wx_ref, cs_ref, qn_ref, kvn_ref, wuq_ref, wuk_ref, wuv_ref,
                   cw_ref, cb_ref, clw_ref, clb_ref, shift_ref, *rest,
                   n_r, n_c, lora, n_heads, rope, scale, per_b, w_lora, a_lora, layer):
    g_ref, qn_ref, kvn_ref, cb_ref, clw_ref, clb_ref = _pick_row(
        (g_ref, qn_ref, kvn_ref, cb_ref, clw_ref, clb_ref), layer)
    rwkv_in = _pick_row(rest[:8], layer) + list(rest[8:10])
    ya_ref, yb_ref, q_ref, k_ref, v_ref, hbuf, zbuf, st_ref = rest[10:18]
    rwkv_scratch = rest[18:]

    @pl.when(pl.program_id(0) == 0)
    def _():
        hbuf[0:CONV_HALO, :] = jnp.zeros((CONV_HALO, hbuf.shape[1]), F32)
        zbuf[0:8, :] = jnp.zeros((8, zbuf.shape[1]), F32)
        st_ref[...] = jnp.zeros(st_ref.shape, F32)

    ends_sequence = (pl.program_id(0) + 1) % per_b == 0

    h = _rms(x_ref[...]) * g_ref[...]
    h = (h * (1.0 + mod_ref[0, 1:2, :]) + mod_ref[0, 0:1, :]).astype(BF16)
    p_rwkv = _dot(h, w_ref[:, :n_r])
    p_conv = _dot(h, w_ref[:, n_r:n_r + n_c])

    def mla_steps():
        lat = _dot(h, w_ref[:, n_r + n_c:n_r + n_c + 2 * lora])
        k_rot = _dot(h, wx_ref[...])
        cs_half = cs_ref[...].T
        cs = jnp.concatenate([cs_half, cs_half], axis=1)
        qn = (_rms(lat[:, 0:lora]) * qn_ref[...]).astype(BF16)
        kvn = (_rms(lat[:, lora:]) * kvn_ref[...]).astype(BF16)
        lane = lax.broadcasted_iota(jnp.int32, cs.shape, 1)
        t = k_rot * cs
        kpe = jnp.where(lane < HEAD, 0.0, t + pltpu.roll(t, rope, 1) + pltpu.roll(t, LANES - rope, 1))
        q_mul = jnp.where(lane < HEAD, 1.0, cs) * scale
        q_mul2 = jnp.concatenate([q_mul, q_mul], axis=1)
        kpe2 = jnp.concatenate([kpe, kpe], axis=1)
        is_v = jnp.concatenate([lane < HEAD, lane < HEAD], axis=1)
        yield
        for j in range(n_heads // 2):
            cols = slice(2 * j * LANES, 2 * (j + 1) * LANES)
            q_ref[:, cols] = (_dot(qn, wuq_ref[:, cols]) * q_mul2).astype(BF16)
            k_ref[:, cols] = (_dot(kvn, wuk_ref[:, cols]) + kpe2).astype(BF16)
            v_ref[:, cols] = jnp.where(is_v, _dot(kvn, wuv_ref[:, cols]), 1.0).astype(BF16)
            yield

    fill = itertools.chain(mla_steps(),
                           _conv_steps(p_conv, ends_sequence, cw_ref, cb_ref, clw_ref, clb_ref, shift_ref, yb_ref, hbuf))
    _rwkv_tile(p_rwkv, ends_sequence, *rwkv_in, ya_ref, zbuf, st_ref, *rwkv_scratch,
               w_lora=w_lora, a_lora=a_lora, fill=fill)
    for _ in fill:
        pass


def _layer_spec(a, layer):
    if a.ndim == 2:
        return pl.BlockSpec(a.shape, lambda i: (0, 0))
    return pl.BlockSpec((None,) + a.shape[1:], lambda i: (layer,) + (0,) * (a.ndim - 1))


def _pick_row(refs, layer):
    return [r.at[layer:layer + 1, :] for r in refs]


def _inproj(x, layer, mod_all, cs, layered, shared, seq, n_r, n_c, lora, n_heads, rope, scale, r_dim, w_lora, a_lora):
    n, d = x.shape
    tm = CHUNK * GROUP_CHUNKS
    per_b = seq // tm
    row = lambda w: pl.BlockSpec((tm, w), lambda i: (i, 0))
    full = lambda a: pl.BlockSpec(a.shape, lambda i: (0, 0))
    act = pltpu.VMEM((tm, r_dim), F32)
    g_w_wx, mla, conv, rwkv = layered
    conv_shift, bd = shared
    return pl.pallas_call(
        functools.partial(_inproj_kernel, n_r=n_r, n_c=n_c, lora=lora, n_heads=n_heads, rope=rope, scale=scale,
                          per_b=per_b, w_lora=w_lora, a_lora=a_lora, layer=layer),
        grid=(n // tm,),
        in_specs=[row(d), pl.BlockSpec((None, 1, 6, d), lambda i: (layer, i // per_b, 0, 0))]
                 + [_layer_spec(a, layer) for a in g_w_wx] + [pl.BlockSpec((cs.shape[0], tm), lambda i: (0, i))]
                 + [_layer_spec(a, layer) for a in mla]
                 + [_layer_spec(a, layer) for a in conv] + [full(conv_shift)]
                 + [_layer_spec(a, layer) for a in rwkv] + [full(bd)],
        out_specs=[row(r_dim), row(n_c // 2)] + [row(n_heads * LANES)] * 3,
        out_shape=[jax.ShapeDtypeStruct((n, r_dim), BF16), jax.ShapeDtypeStruct((n, n_c // 2), BF16)]
                  + [jax.ShapeDtypeStruct((n, n_heads * LANES), BF16)] * 3,
        scratch_shapes=[pltpu.VMEM((tm + CONV_HALO, n_c // 2), F32),
                        pltpu.VMEM((tm + 8, n_r), F32),
                        pltpu.VMEM((r_dim // LANES, LANES, LANES), F32),
                        act, act, act, act, act, act, act],
        compiler_params=_params("arbitrary"),
        name="in_proj",
    )(x, mod_all, *g_w_wx, cs, *mla, *conv, conv_shift, *rwkv, bd)


def _rwkv_tile(p, ends_sequence, mu_ref, w0_ref, a0_ref, kk_ref, ka_ref, rk_ref, lnw_ref, lnb_ref, lora_ref, bd_ref,
               o_ref, zbuf, st_ref, r_s, k_s, v_s, a_s, b_s, ld_s, y_s, *, w_lora, a_lora, fill):
    tt, dim = o_ref.shape
    assert tt == CHUNK * GROUP_CHUNKS

    def gap():
        next(fill, None)

    n_pairs = dim // LANES
    c = CHUNK

    zbuf[8:8 + tt, :] = p
    prev = zbuf[7:7 + tt, :]
    zbuf[7:8, :] = jnp.where(ends_sequence, 0.0, zbuf[tt + 7:tt + 8, :])
    z = p + (prev - p) * mu_ref[...]

    r = z[:, 0:dim]
    k = z[:, dim:2 * dim]
    v = z[:, 2 * dim:3 * dim]
    lo = z[:, 3 * dim:3 * dim + LANES]
    lane = lax.broadcasted_iota(jnp.int32, lo.shape, 1)
    act = jnp.where(lane < w_lora, jnp.tanh(lo), jnp.where(lane < w_lora + a_lora, lo, _sigmoid(lo)))
    lora = _dot(_bf(act), lora_ref[...])
    x_w = w0_ref[...] + lora[:, 0:dim]
    w_log = -(jnp.maximum(-x_w, 0.0) + jnp.log(1.0 + jnp.exp(-jnp.abs(x_w)))) - 0.5
    a = _sigmoid(a0_ref[...] + lora[:, dim:2 * dim])
    g = lora[:, 2 * dim:3 * dim]
    bd = bd_ref[...]
    kk = k * kk_ref[...]
    kk = kk * lax.rsqrt(jnp.maximum(_head_sums(kk * kk, bd), 1e-24))
    k = k * (1.0 + (a - 1.0) * ka_ref[...])
    r_s[...] = r
    k_s[...] = k
    v_s[...] = v
    a_s[...] = -kk
    b_s[...] = kk * a
    ld_s[...] = -jnp.exp(w_log)

    row_c = lax.broadcasted_iota(jnp.int32, (c, c), 0)
    col_c = lax.broadcasted_iota(jnp.int32, (c, c), 1)
    tri_incl = (row_c >= col_c).astype(BF16)
    head0 = lax.broadcasted_iota(jnp.int32, (c, LANES), 1) < HEAD
    lane_2 = lax.broadcasted_iota(jnp.int32, (c, 2 * LANES), 1)
    head0_2 = jnp.where(lane_2 >= LANES, lane_2 - LANES, lane_2) < HEAD
    row_4 = lax.broadcasted_iota(jnp.int32, (c, 4 * c), 0)
    col_4 = jnp.bitwise_and(lax.broadcasted_iota(jnp.int32, (c, 4 * c), 1), c - 1)
    strict4 = row_4 > col_4
    incl4 = row_4 >= col_4
    eye4 = (row_4 == col_4).astype(F32)
    bd4 = jnp.bitwise_xor(lax.broadcasted_iota(jnp.int32, (4 * c, 4 * c), 0),
                          lax.broadcasted_iota(jnp.int32, (4 * c, 4 * c), 1)) < c
    row_p = lax.broadcasted_iota(jnp.int32, (LANES, LANES), 0)
    col_p = lax.broadcasted_iota(jnp.int32, (LANES, LANES), 1)
    same_head = (row_p < HEAD) == (col_p < HEAD)
    diag_p = row_p == col_p
    zeros_cl = jnp.zeros((c, LANES), BF16)

    def group_body():
        items = [(slice(q * c, (q + 1) * c), slice(j * LANES, (j + 1) * LANES), j)
                 for q in range(GROUP_CHUNKS) for j in range(n_pairs)]
        n_it = len(items)
        loaded = [[ref[rows, lanes] for ref in (r_s, k_s, v_s, a_s, b_s, ld_s)] for rows, lanes, _ in items]
        r_c, k_c, v_c, a_c, b_c, ld_c = [list(t) for t in zip(*loaded)]
        cum = []
        for i in range(n_it):
            ld_hi, ld_lo = _split(ld_c[i])
            both = _dot(tri_incl, jnp.concatenate([ld_hi, ld_lo], axis=1))
            cum.append(both[:, 0:LANES] + both[:, LANES:])
        gap()
        cum_end = [x[c - 1:c, :] for x in cum]
        at = [a_c[i] * jnp.exp(cum[i] - ld_c[i]) for i in range(n_it)]
        rt = [r_c[i] * jnp.exp(cum[i]) for i in range(n_it)]
        v_bf = [_bf(x) for x in v_c]
        def by_head(x, mask):
            return jnp.concatenate([jnp.where(mask, x, 0), jnp.where(mask, 0, x)], axis=0)

        aa = []
        for i in range(n_it):
            e_neg = jnp.exp(-cum[i])
            rhs = jnp.concatenate([by_head(_bf(b_c[i] * e_neg), head0), by_head(_bf(k_c[i] * e_neg), head0)], axis=0)
            aa.append(_dot_nt(_bf(jnp.concatenate([at[i], rt[i]], axis=0)), rhs))
        gap()
        t_cat = [jnp.where(strict4, x[0:c, :], 0.0) for x in aa]
        q_cat = [_bf(jnp.where(incl4, x[c:2 * c, :], 0.0)) for x in aa]
        akv = [_dot(_bf(t_cat[i][:, LANES:]), by_head(v_bf[i], head0)) for i in range(n_it)]
        n_grp = n_it // 2

        def bdiag(x):
            return jnp.where(bd4, jnp.concatenate([x, x, x, x], axis=0), 0)

        p4 = [jnp.concatenate([t_cat[2 * g][:, 0:LANES], t_cat[2 * g + 1][:, 0:LANES]], axis=1) for g in range(n_grp)]
        inv4 = [eye4 + x for x in p4]
        p4_bf = [_bf(x) for x in p4]
        pw = [_bf(_dot(x, bdiag(x))) for x in p4_bf]
        gap()
        for _ in range(4):
            both = [_dot(jnp.concatenate([pw[g], _bf(inv4[g])], axis=0), bdiag(pw[g])) for g in range(n_grp)]
            inv4 = [inv4[g] + both[g][c:, :] for g in range(n_grp)]
            pw = [_bf(both[g][0:c, :]) for g in range(n_grp)]
            gap()
        inv4 = [_bf(inv4[g] + _dot(_bf(inv4[g]), bdiag(pw[g]))) for g in range(n_grp)]
        inv_cat = [inv4[i // 2][:, (i % 2) * LANES:(i % 2 + 1) * LANES] for i in range(n_it)]
        x_mat = [_bf(jnp.concatenate([at[i], akv[i]], axis=1)) for i in range(n_it)]
        tx = [_dot(inv_cat[i], by_head(x_mat[i], head0_2)) for i in range(n_it)]
        gap()
        tx_bf = [_bf(x) for x in tx]
        v_pad = [jnp.concatenate([zeros_cl, v_bf[i]], axis=1) for i in range(n_it)]
        w_mat = [jnp.concatenate([tx_bf[i], v_pad[i]], axis=0) for i in range(n_it)]
        qw = [_dot(q_cat[i], jnp.concatenate([by_head(tx_bf[i], head0_2), by_head(v_pad[i], head0_2)], axis=0))
              for i in range(n_it)]
        mn = []
        for i in range(n_it):
            e_end = jnp.exp(cum_end[i] - cum[i])
            bk_t = _bf(jnp.concatenate([b_c[i] * e_end, k_c[i] * e_end], axis=0).T)
            mn.append(_dot(bk_t, w_mat[i]))
        gap()
        g_col = [jnp.sum(jnp.where(diag_p, jnp.exp(cum_end[i]), 0.0), axis=1, keepdims=True) for i in range(n_it)]
        states = [st_ref[j] for j in range(n_pairs)]
        for i, (rows, lanes, j) in enumerate(items):
            st = states[j]
            st_bf = _bf(st)
            y_s[rows, lanes] = _dot(_bf(rt[i] + qw[i][:, 0:LANES]), st_bf) + qw[i][:, LANES:]
            m_low = _bf(jnp.where(same_head, mn[i][:, 0:LANES], 0.0))
            states[j] = g_col[i] * st + _dot(m_low, st_bf) + jnp.where(same_head, mn[i][:, LANES:], 0.0)
        for j in range(n_pairs):
            st_ref[j] = jnp.where(ends_sequence, 0.0, states[j])

    group_body()

    y = y_s[...]
    inv_n = 1.0 / HEAD
    mean = _head_sums(y, bd) * inv_n
    d = y - mean
    var = _head_sums(d * d, bd) * inv_n
    y = d * lax.rsqrt(var + RWKV_GN_EPS) * lnw_ref[...] + lnb_ref[...]
    r = r_s[...]
    bonus = _head_sums(r * k_s[...] * rk_ref[...], bd) * v_s[...]
    o_ref[...] = ((y + bonus) * g).astype(o_ref.dtype)


def _conv_steps(p_conv, ends_sequence, w_ref, b_ref, lnw_ref, lnb_ref, shift_ref, o_ref, hbuf):
    tt, dim = o_ref.shape
    sub = CONV_SUB
    ext = sub + 8

    hbuf[CONV_HALO:CONV_HALO + tt, :] = p_conv[:, 0:dim] * _sigmoid(p_conv[:, dim:2 * dim])
    yield
    for s in range(tt // sub):
        base = CONV_HALO + s * sub - 8
        wins = [hbuf[base - 8 * a:base - 8 * a + ext, :] for a in range(CONV_HALO // 8)]
        parts = []
        for r in range(8):
            lags = [8 * a + r for a in range(len(wins)) if 8 * a + r < CONV_WIDTH]
            terms = [wins[m // 8] * w_ref[CONV_WIDTH - 1 - m:CONV_WIDTH - m, :] for m in lags]
            parts.append(functools.reduce(jnp.add, terms))
        stacked = _bf(jnp.concatenate(parts, axis=0))
        acc = _dot(shift_ref[...], stacked) + b_ref[...]
        mu = jnp.mean(acc, axis=-1, keepdims=True)
        d = acc - mu
        var = jnp.mean(d * d, axis=-1, keepdims=True)
        y = d * lax.rsqrt(var + LN_EPS) * lnw_ref[...] + lnb_ref[...]
        o_ref[s * sub:(s + 1) * sub, :] = (y * _sigmoid(y)).astype(o_ref.dtype)
        if s + 1 < tt // sub:
            yield
    hbuf[0:CONV_HALO, :] = jnp.where(ends_sequence, 0.0, hbuf[tt:tt + CONV_HALO, :])


def _conv_shift_matrix():
    ext = CONV_SUB + 8
    u_idx = jnp.arange(CONV_SUB)[:, None]
    col = jnp.arange(8 * ext)[None, :]
    return (col % ext == u_idx + 8 - col // ext).astype(BF16)


def _flash_kernel(q_ref, k_ref, v_ref, o_ref, m_s, acc_s, s_buf, p_buf, al_buf):
    i = pl.program_id(2)
    t = FLASH_BLOCK
    m_s[...] = jnp.full(m_s.shape, -jnp.inf, F32)
    acc_s[...] = jnp.zeros(acc_s.shape, F32)
    p_buf[1] = jnp.zeros(p_buf.shape[1:], BF16)
    al_buf[1] = jnp.ones(al_buf.shape[1:], F32)
    causal = lax.broadcasted_iota(jnp.int32, (t, t), 0) <= lax.broadcasted_iota(jnp.int32, (t, t), 1)

    def keys_of(j):
        return pl.ds(pl.multiple_of(j * t, t), t)

    def scores(j, slot, u0):
        for h in range(2):
            cols = slice(h * LANES, (h + 1) * LANES)
            s_buf[slot, h, :, u0 * t:] = _dot_nt(k_ref[keys_of(j), cols], q_ref[u0 * t:, cols])

    def softmax(slot, u0, masked_sub):
        for u in range(u0, FLASH_SUBS):
            qs = slice(u * t, (u + 1) * t)
            for h in range(2):
                s = s_buf[slot, h, :, qs]
                if u == masked_sub:
                    s = jnp.where(causal, s, -jnp.inf)
                m_prev = m_s[h, :, qs]
                m_new = jnp.maximum(m_prev, jnp.max(s, axis=0, keepdims=True))
                m_s[h, :, qs] = m_new
                al_buf[slot, h, :, qs] = jnp.exp2(m_prev - m_new)
                p_buf[slot, h, :, qs] = jnp.exp2((s - m_new).astype(BF16))

    def accumulate(j, slot, u0):
        for h in range(2):
            v_blk = v_ref[keys_of(j), h * LANES:h * LANES + FLASH_VROWS]
            pv = lax.dot_general(v_blk, p_buf[slot, h, :, u0 * t:], (((0,), (0,)), ((), ())),
                                 preferred_element_type=F32)
            acc_s[h, :, u0 * t:] = al_buf[slot, h, :, u0 * t:] * acc_s[h, :, u0 * t:] + pv

    n_full = FLASH_SUBS * i
    scores(0, 0, 0)

    def body(g, carry):
        for par in range(2):
            k = 2 * g + par
            scores(k + 1, 1 - par, 0)
            accumulate(jnp.maximum(k - 1, 0), 1 - par, 0)
            softmax(par, 0, None)
        return carry

    lax.fori_loop(0, n_full // 2, body, 0)
    accumulate(jnp.maximum(n_full - 1, 0), 1, 0)
    for d in range(FLASH_SUBS):
        if d + 1 < FLASH_SUBS:
            scores(n_full + d + 1, (d + 1) % 2, d + 1)
        softmax(d % 2, d, d)
        accumulate(n_full + d, d % 2, d)
    out_t = jnp.concatenate([acc_s[h, 0:HEAD, :] / acc_s[h, HEAD:HEAD + 1, :] for h in range(2)], axis=0)
    o_ref[...] = out_t.T.astype(o_ref.dtype)


def _flash(q, k, v, batch, seq, n_heads):
    n = q.shape[0]
    tq = FLASH_SUBS * FLASH_BLOCK
    n_q = seq // tq
    return pl.pallas_call(
        _flash_kernel,
        grid=(batch, n_heads // 2, n_q),
        in_specs=[pl.BlockSpec((tq, 2 * LANES), lambda b, h, i: (b * n_q + i, h)),
                  pl.BlockSpec((seq, 2 * LANES), lambda b, h, i: (b, h)),
                  pl.BlockSpec((seq, 2 * LANES), lambda b, h, i: (b, h))],
        out_specs=pl.BlockSpec((tq, LANES), lambda b, h, i: (b * n_q + i, h)),
        out_shape=jax.ShapeDtypeStruct((n, n_heads * HEAD), BF16),
        scratch_shapes=[pltpu.VMEM((2, 1, tq), F32),
                        pltpu.VMEM((2, FLASH_VROWS, tq), F32),
                        pltpu.VMEM((2, 2, FLASH_BLOCK, tq), F32),
                        pltpu.VMEM((2, 2, FLASH_BLOCK, tq), BF16),
                        pltpu.VMEM((2, 2, 1, tq), F32)],
        compiler_params=_params("parallel", "parallel", "arbitrary"),
        name="mla_flash",
    )(q, k, v)


def _outffn_kernel(ya_ref, yb_ref, yc_ref, x_ref, mod_ref, gpm_ref, gpf_ref, gqf_ref, wo_ref, w1_ref, w2_ref,
                   o_ref, x1_s, hid_s, wo_s, w1_s, w2_s, *, da, db, tf, layer):
    step = pl.program_id(0)

    @pl.when(step < FFN_CAST_STEPS)
    def _():
        n_o, n_1, n_2 = wo_ref.shape[0], w1_ref.shape[1], w2_ref.shape[0]
        wo_s[pl.ds(pl.multiple_of(step * n_o, n_o), n_o), :] = wo_ref[...].astype(BF16)
        w1_s[:, pl.ds(pl.multiple_of(step * n_1, n_1), n_1)] = w1_ref[...].astype(BF16)
        w2_s[pl.ds(pl.multiple_of(step * n_2, n_2), n_2), :] = w2_ref[...].astype(BF16)

    @pl.when(step >= FFN_CAST_STEPS)
    def _():
        gpm, gpf, gqf = _pick_row((gpm_ref, gpf_ref, gqf_ref), layer)
        half = x_ref.shape[0] // 2
        halves = [slice(0, half), slice(half, 2 * half)]
        ys = [_dot(ya_ref[r, :], wo_s[0:da, :]) + _dot(yb_ref[r, :], wo_s[da:da + db, :])
              + _dot(yc_ref[r, :], wo_s[da + db:, :]) for r in halves]
        hs = []
        for r, y in zip(halves, ys):
            x1 = x_ref[r, :] + mod_ref[0, 2:3, :] * (_rms(y) * gpm[...])
            x1_s[r, :] = x1
            h = _rms(x1) * gpf[...]
            hs.append((h * (1.0 + mod_ref[0, 4:5, :]) + mod_ref[0, 3:4, :]).astype(BF16))
        for f in range(hid_s.shape[1] // tf):
            cols = slice(f * tf, (f + 1) * tf)
            for r, h in zip(halves, hs):
                hid = jnp.maximum(_dot(h, w1_s[:, cols]), 0.0)
                hid_s[r, cols] = (hid * hid).astype(BF16)
        for r in halves:
            y2 = _dot(hid_s[r, :], w2_s[...])
            o_ref[r, :] = x1_s[r, :] + mod_ref[0, 5:6, :] * (_rms(y2) * gqf[...])


def _outffn(ya, yb, yc, x, layer, mod_all, gains, w_out, w_ff1, w_ff2, seq):
    n, d = x.shape
    d_ff = w_ff1.shape[2]
    tm, tf = 512, 512
    per_b = seq // tm
    cast = FFN_CAST_STEPS
    tile = lambda i: jnp.maximum(i - cast, 0)
    chunk = lambda i: jnp.minimum(i, cast - 1)
    row = lambda w: pl.BlockSpec((tm, w), lambda i: (tile(i), 0))
    return pl.pallas_call(
        functools.partial(_outffn_kernel, da=ya.shape[1], db=yb.shape[1], tf=tf, layer=layer),
        grid=(cast + n // tm,),
        in_specs=[row(ya.shape[1]), row(yb.shape[1]), row(yc.shape[1]), row(d),
                  pl.BlockSpec((None, 1, 6, d), lambda i: (layer, tile(i) // per_b, 0, 0))]
                 + [_layer_spec(a, layer) for a in gains]
                 + [pl.BlockSpec((None, w_out.shape[1] // cast, d), lambda i: (layer, chunk(i), 0)),
                    pl.BlockSpec((None, d, d_ff // cast), lambda i: (layer, 0, chunk(i))),
                    pl.BlockSpec((None, d_ff // cast, d), lambda i: (layer, chunk(i), 0))],
        out_specs=row(d),
        out_shape=jax.ShapeDtypeStruct((n, d), F32),
        scratch_shapes=[pltpu.VMEM((tm, d), F32), pltpu.VMEM((tm, d_ff), BF16),
                        pltpu.VMEM(w_out.shape[1:], BF16), pltpu.VMEM(w_ff1.shape[1:], BF16),
                        pltpu.VMEM(w_ff2.shape[1:], BF16)],
        compiler_params=_params("arbitrary"),
        name="out_ffn",
    )(ya, yb, yc, x, mod_all, *gains, w_out, w_ff1, w_ff2)


def _rot_cols(w):
    half = w.shape[-1] // 2
    return jnp.concatenate([-w[..., half:], w[..., :half]], axis=-1)


def kernel(x, c, positions, g_pre_mix, g_post_mix, g_pre_ffn, g_post_ffn, w_ada, b_ada, w_in, w_out, rwkv_mu, rwkv_w0, rwkv_w2, rwkv_a0, rwkv_a2, rwkv_g2, rwkv_k_k, rwkv_k_a, rwkv_r_k, rwkv_ln_w, rwkv_ln_b, conv_w, conv_b, conv_ln_w, conv_ln_b, mla_q_norm, mla_w_uq, mla_kv_norm, mla_w_ukv, w_ff1, w_ff2):
    batch, seq, d = x.shape
    n = batch * seq
    n_layers = w_ada.shape[0]
    r_dim = rwkv_w0.shape[1]
    w_lora, a_lora, g_lora = rwkv_w2.shape[1], rwkv_a2.shape[1], rwkv_g2.shape[1]
    assert w_lora + a_lora + g_lora == LANES and r_dim % LANES == 0
    n_r = 3 * r_dim + LANES
    c_dim = conv_w.shape[2]
    n_c = 2 * c_dim
    lora = mla_q_norm.shape[1]
    n_heads = mla_w_ukv.shape[2] // (2 * HEAD)
    rope = mla_w_uq.shape[2] // n_heads - HEAD
    assert HEAD + 2 * rope == LANES
    scale = float(HEAD + rope) ** -0.5 * LOG2_E

    xf = x.reshape(n, d)
    c8 = jnp.pad(c, ((0, 8 - batch), (0, 0)))
    mod_all = _ada(c8, w_ada, b_ada.reshape(n_layers, 1, 6 * d))[:, :batch].reshape(n_layers, batch, 6, d)

    inv_freq = 1.0 / (ROPE_THETA ** (jnp.arange(0, rope, 2, dtype=F32) / rope))
    invf = jnp.tile(inv_freq, 2 * rope // inv_freq.shape[0]).reshape(2 * rope, 1)
    phase = jnp.where(jnp.arange(2 * rope) < rope, 0.0, -0.5 * jnp.pi).astype(F32).reshape(2 * rope, 1)
    cs = _rope_table(positions.reshape(1, n), invf, phase)

    head_id = jnp.arange(2 * LANES) // HEAD
    bd = (head_id[:, None] == head_id[None, :]).astype(BF16)
    conv_shift = _conv_shift_matrix()

    vec = lambda a: a.reshape(n_layers, -1)
    w_in_bf = w_in.astype(BF16)
    k_pe = w_in_bf[:, :, n_r + n_c + 2 * lora:]
    w_extra = jnp.concatenate([jnp.zeros((n_layers, d, HEAD), BF16), k_pe, _rot_cols(k_pe)], axis=-1)
    wq = mla_w_uq.reshape(n_layers, lora, n_heads, HEAD + rope)
    wuq = jnp.concatenate([wq, _rot_cols(wq[..., HEAD:])], axis=-1).reshape(n_layers, lora, n_heads * LANES)
    wkv = mla_w_ukv.reshape(n_layers, lora, n_heads, 2 * HEAD)
    pad = jnp.zeros_like(wkv[..., :HEAD])
    wuk = jnp.concatenate([wkv[..., :HEAD], pad], axis=-1).reshape(n_layers, lora, n_heads * LANES)
    wuv = jnp.concatenate([wkv[..., HEAD:], pad], axis=-1).reshape(n_layers, lora, n_heads * LANES)
    lora_w = jnp.zeros((n_layers, LANES, 3 * r_dim), F32)
    lora_w = lora_w.at[:, 0:w_lora, 0:r_dim].set(rwkv_w2)
    lora_w = lora_w.at[:, w_lora:w_lora + a_lora, r_dim:2 * r_dim].set(rwkv_a2)
    lora_w = lora_w.at[:, w_lora + a_lora:, 2 * r_dim:].set(rwkv_g2)
    layered = ([vec(g_pre_mix), w_in_bf, w_extra],
               [vec(mla_q_norm), vec(mla_kv_norm), wuq.astype(BF16), wuk.astype(BF16), wuv.astype(BF16)],
               [conv_w, vec(conv_b), vec(conv_ln_w), vec(conv_ln_b)],
               [vec(rwkv_mu), vec(rwkv_w0), vec(rwkv_a0), vec(rwkv_k_k), vec(rwkv_k_a), vec(rwkv_r_k),
                vec(rwkv_ln_w), vec(rwkv_ln_b), lora_w.astype(BF16)])
    gains = [vec(g_post_mix), vec(g_pre_ffn), vec(g_post_ffn)]
    for l in range(n_layers):
        y_a, y_b, q, k, v = _inproj(xf, l, mod_all, cs, layered, (conv_shift, bd),
                                    seq, n_r, n_c, lora, n_heads, rope, scale, r_dim, w_lora, a_lora)
        y_c = _flash(q, k, v, batch, seq, n_heads)
        xf = _outffn(y_a, y_b, y_c, xf, l, mod_all, gains, w_out, w_ff1, w_ff2, seq)
    return xf.reshape(batch, seq, d)
```

```python
import functools
import itertools

import jax
import jax.numpy as jnp
from jax import lax
from jax.experimental import pallas as pl
from jax.experimental.pallas import tpu as pltpu

F32 = jnp.float32
BF16 = jnp.bfloat16

LANES = 128
HEAD = 64
NORM_EPS = 1e-6
LN_EPS = 1e-5
RWKV_GN_EPS = 64e-5
ROPE_THETA = 10000.0
CONV_WIDTH = 31
CONV_HALO = 32
CONV_SUB = 128
CHUNK = 64
FLASH_BLOCK = 256
FLASH_VROWS = HEAD + 16
FLASH_SUBS = 8
GROUP_CHUNKS = 8
FFN_CAST_STEPS = 8
VMEM_LIMIT = 56 * 1024 * 1024
LOG2_E = 1.4426950408889634


def _dot(a, b):
    return jnp.dot(a, b, preferred_element_type=F32)


def _dot_nt(a, b):
    return lax.dot_general(a, b, (((1,), (1,)), ((), ())), preferred_element_type=F32)


def _bf(x):
    return x.astype(BF16)


def _split(x):
    hi = x.astype(BF16)
    return hi, (x - hi.astype(F32)).astype(BF16)


def _head_sums(x, ones_bd):
    width = x.shape[1]
    groups = [_dot(_bf(x[:, lo:lo + 2 * LANES]), ones_bd) for lo in range(0, width - 2 * LANES + 1, 2 * LANES)]
    if width % (2 * LANES):
        groups.append(_dot(_bf(x[:, width - LANES:]), ones_bd[0:LANES, 0:LANES]))
    return jnp.concatenate(groups, axis=1)


def _rms(x):
    return x * lax.rsqrt(jnp.mean(x * x, axis=-1, keepdims=True) + NORM_EPS)


def _sigmoid(x):
    return 1.0 / (1.0 + jnp.exp(-x))


def _params(*semantics):
    return pltpu.CompilerParams(dimension_semantics=semantics, vmem_limit_bytes=VMEM_LIMIT)


def _ada_kernel(c_ref, w_ref, b_ref, o_ref):
    c = c_ref[...]
    cs = c * _sigmoid(c)
    cs_hi, cs_lo = _split(cs)
    w_hi, w_lo = _split(w_ref[0])
    o_ref[0] = _dot(cs_hi, w_hi) + _dot(cs_hi, w_lo) + _dot(cs_lo, w_hi) + b_ref[0]


def _ada(c8, w_ada, b_ada):
    n_layers, d, d6 = w_ada.shape
    tn = 1536
    return pl.pallas_call(
        _ada_kernel,
        grid=(n_layers, d6 // tn),
        in_specs=[pl.BlockSpec((8, d), lambda l, j: (0, 0)),
                  pl.BlockSpec((1, d, tn), lambda l, j: (l, 0, j)),
                  pl.BlockSpec((1, 1, tn), lambda l, j: (l, 0, j))],
        out_specs=pl.BlockSpec((1, 8, tn), lambda l, j: (l, 0, j)),
        out_shape=jax.ShapeDtypeStruct((n_layers, 8, d6), F32),
        compiler_params=_params("parallel", "parallel"),
        name="ada_mod",
    )(c8, w_ada, b_ada)


def _rope_kernel(pos_ref, invf_ref, phase_ref, cs_ref):
    t = jnp.cos(invf_ref[...] * pos_ref[...].astype(F32) + phase_ref[...])
    half = t.shape[0] // 2
    for j in range(4):
        cs_ref[j * half:(j + 1) * half, :] = t[(j // 2) * half:(j // 2 + 1) * half]


def _rope_table(pos_row, invf, phase):
    n = pos_row.shape[1]
    rows = 2 * invf.shape[0]
    tn = 2048
    col = pl.BlockSpec((rows // 2, 1), lambda i: (0, 0))
    return pl.pallas_call(
        _rope_kernel,
        grid=(n // tn,),
        in_specs=[pl.BlockSpec((1, tn), lambda i: (0, i)), col, col],
        out_specs=pl.BlockSpec((rows, tn), lambda i: (0, i)),
        out_shape=jax.ShapeDtypeStruct((rows, n), F32),
        compiler_params=_params("parallel"),
        name="rope_table",
    )(pos_row, invf, phase)


def _inproj_kernel(x_ref, mod_ref, g_ref, w_ref, wx_ref, cs_ref, qn_ref, kvn_ref, wuq_ref, wuk_ref, wuv_ref,
                   cw_ref, cb_ref, clw_ref, clb_ref, shift_ref, *rest,
                   n_r, n_c, lora, n_heads, rope, scale, per_b, w_lora, a_lora, layer):
    g_ref, qn_ref, kvn_ref, cb_ref, clw_ref, clb_ref = _pick_row(
        (g_ref, qn_ref, kvn_ref, cb_ref, clw_ref, clb_ref), layer)
    rwkv_in = _pick_row(rest[:8], layer) + list(rest[8:10])
    ya_ref, yb_ref, q_ref, k_ref, v_ref, hbuf, zbuf, st_ref = rest[10:18]
    rwkv_scratch = rest[18:]

    @pl.when(pl.program_id(0) == 0)
    def _():
        hbuf[0:CONV_HALO, :] = jnp.zeros((CONV_HALO, hbuf.shape[1]), F32)
        zbuf[0:8, :] = jnp.zeros((8, zbuf.shape[1]), F32)
        st_ref[...] = jnp.zeros(st_ref.shape, F32)

    ends_sequence = (pl.program_id(0) + 1) % per_b == 0

    h = _rms(x_ref[...]) * g_ref[...]
    h = (h * (1.0 + mod_ref[0, 1:2, :]) + mod_ref[0, 0:1, :]).astype(BF16)
    p_rwkv = _dot(h, w_ref[:, :n_r])
    p_conv = _dot(h, w_ref[:, n_r:n_r + n_c])

    def mla_steps():
        lat = _dot(h, w_ref[:, n_r + n_c:n_r + n_c + 2 * lora])
        k_rot = _dot(h, wx_ref[...])
        cs_half = cs_ref[...].T
        cs = jnp.concatenate([cs_half, cs_half], axis=1)
        qn = (_rms(lat[:, 0:lora]) * qn_ref[...]).astype(BF16)
        kvn = (_rms(lat[:, lora:]) * kvn_ref[...]).astype(BF16)
        lane = lax.broadcasted_iota(jnp.int32, cs.shape, 1)
        t = k_rot * cs
        kpe = jnp.where(lane < HEAD, 0.0, t + pltpu.roll(t, rope, 1) + pltpu.roll(t, LANES - rope, 1))
        q_mul = jnp.where(lane < HEAD, 1.0, cs) * scale
        q_mul2 = jnp.concatenate([q_mul, q_mul], axis=1)
        kpe2 = jnp.concatenate([kpe, kpe], axis=1)
        is_v = jnp.concatenate([lane < HEAD, lane < HEAD], axis=1)
        yield
        for j in range(n_heads // 2):
            cols = slice(2 * j * LANES, 2 * (j + 1) * LANES)
            q_ref[:, cols] = (_dot(qn, wuq_ref[:, cols]) * q_mul2).astype(BF16)
            k_ref[:, cols] = (_dot(kvn, wuk_ref[:, cols]) + kpe2).astype(BF16)
            v_ref[:, cols] = jnp.where(is_v, _dot(kvn, wuv_ref[:, cols]), 1.0).astype(BF16)
            yield

    fill = itertools.chain(mla_steps(),
                           _conv_steps(p_conv, ends_sequence, cw_ref, cb_ref, clw_ref, clb_ref, shift_ref, yb_ref, hbuf))
    _rwkv_tile(p_rwkv, ends_sequence, *rwkv_in, ya_ref, zbuf, st_ref, *rwkv_scratch,
               w_lora=w_lora, a_lora=a_lora, fill=fill)
    for _ in fill:
        pass


def _layer_spec(a, layer):
    if a.ndim == 2:
        return pl.BlockSpec(a.shape, lambda i: (0, 0))
    return pl.BlockSpec((None,) + a.shape[1:], lambda i: (layer,) + (0,) * (a.ndim - 1))


def _pick_row(refs, layer):
    return [r.at[layer:layer + 1, :] for r in refs]


def _inproj(x, layer, mod_all, cs, layered, shared, seq, n_r, n_c, lora, n_heads, rope, scale, r_dim, w_lora, a_lora):
    n, d = x.shape
    tm = CHUNK * GROUP_CHUNKS
    per_b = seq // tm
    row = lambda w: pl.BlockSpec((tm, w), lambda i: (i, 0))
    full = lambda a: pl.BlockSpec(a.shape, lambda i: (0, 0))
    act = pltpu.VMEM((tm, r_dim), F32)
    g_w_wx, mla, conv, rwkv = layered
    conv_shift, bd = shared
    return pl.pallas_call(
        functools.partial(_inproj_kernel, n_r=n_r, n_c=n_c, lora=lora, n_heads=n_heads, rope=rope, scale=scale,
                          per_b=per_b, w_lora=w_lora, a_lora=a_lora, layer=layer),
        grid=(n // tm,),
        in_specs=[row(d), pl.BlockSpec((None, 1, 6, d), lambda i: (layer, i // per_b, 0, 0))]
                 + [_layer_spec(a, layer) for a in g_w_wx] + [pl.BlockSpec((cs.shape[0], tm), lambda i: (0, i))]
                 + [_layer_spec(a, layer) for a in mla]
                 + [_layer_spec(a, layer) for a in conv] + [full(conv_shift)]
                 + [_layer_spec(a, layer) for a in rwkv] + [full(bd)],
        out_specs=[row(r_dim), row(n_c // 2)] + [row(n_heads * LANES)] * 3,
        out_shape=[jax.ShapeDtypeStruct((n, r_dim), BF16), jax.ShapeDtypeStruct((n, n_c // 2), BF16)]
                  + [jax.ShapeDtypeStruct((n, n_heads * LANES), BF16)] * 3,
        scratch_shapes=[pltpu.VMEM((tm + CONV_HALO, n_c // 2), F32),
                        pltpu.VMEM((tm + 8, n_r), F32),
                        pltpu.VMEM((r_dim // LANES, LANES, LANES), F32),
                        act, act, act, act, act, act, act],
        compiler_params=_params("arbitrary"),
        name="in_proj",
    )(x, mod_all, *g_w_wx, cs, *mla, *conv, conv_shift, *rwkv, bd)


def _rwkv_tile(p, ends_sequence, mu_ref, w0_ref, a0_ref, kk_ref, ka_ref, rk_ref, lnw_ref, lnb_ref, lora_ref, bd_ref,
               o_ref, zbuf, st_ref, r_s, k_s, v_s, a_s, b_s, ld_s, y_s, *, w_lora, a_lora, fill):
    tt, dim = o_ref.shape
    assert tt == CHUNK * GROUP_CHUNKS

    def gap():
        next(fill, None)

    n_pairs = dim // LANES
    c = CHUNK

    zbuf[8:8 + tt, :] = p
    prev = zbuf[7:7 + tt, :]
    zbuf[7:8, :] = jnp.where(ends_sequence, 0.0, zbuf[tt + 7:tt + 8, :])
    z = p + (prev - p) * mu_ref[...]

    r = z[:, 0:dim]
    k = z[:, dim:2 * dim]
    v = z[:, 2 * dim:3 * dim]
    lo = z[:, 3 * dim:3 * dim + LANES]
    lane = lax.broadcasted_iota(jnp.int32, lo.shape, 1)
    act = jnp.where(lane < w_lora, jnp.tanh(lo), jnp.where(lane < w_lora + a_lora, lo, _sigmoid(lo)))
    lora = _dot(_bf(act), lora_ref[...])
    x_w = w0_ref[...] + lora[:, 0:dim]
    w_log = -(jnp.maximum(-x_w, 0.0) + jnp.log(1.0 + jnp.exp(-jnp.abs(x_w)))) - 0.5
    a = _sigmoid(a0_ref[...] + lora[:, dim:2 * dim])
    g = lora[:, 2 * dim:3 * dim]
    bd = bd_ref[...]
    kk = k * kk_ref[...]
    kk = kk * lax.rsqrt(jnp.maximum(_head_sums(kk * kk, bd), 1e-24))
    k = k * (1.0 + (a - 1.0) * ka_ref[...])
    r_s[...] = r
    k_s[...] = k
    v_s[...] = v
    a_s[...] = -kk
    b_s[...] = kk * a
    ld_s[...] = -jnp.exp(w_log)

    row_c = lax.broadcasted_iota(jnp.int32, (c, c), 0)
    col_c = lax.broadcasted_iota(jnp.int32, (c, c), 1)
    tri_incl = (row_c >= col_c).astype(BF16)
    head0 = lax.broadcasted_iota(jnp.int32, (c, LANES), 1) < HEAD
    lane_2 = lax.broadcasted_iota(jnp.int32, (c, 2 * LANES), 1)
    head0_2 = jnp.where(lane_2 >= LANES, lane_2 - LANES, lane_2) < HEAD
    row_4 = lax.broadcasted_iota(jnp.int32, (c, 4 * c), 0)
    col_4 = jnp.bitwise_and(lax.broadcasted_iota(jnp.int32, (c, 4 * c), 1), c - 1)
    strict4 = row_4 > col_4
    incl4 = row_4 >= col_4
    eye4 = (row_4 == col_4).astype(F32)
    bd4 = jnp.bitwise_xor(lax.broadcasted_iota(jnp.int32, (4 * c, 4 * c), 0),
                          lax.broadcasted_iota(jnp.int32, (4 * c, 4 * c), 1)) < c
    row_p = lax.broadcasted_iota(jnp.int32, (LANES, LANES), 0)
    col_p = lax.broadcasted_iota(jnp.int32, (LANES, LANES), 1)
    same_head = (row_p < HEAD) == (col_p < HEAD)
    diag_p = row_p == col_p
    zeros_cl = jnp.zeros((c, LANES), BF16)

    def group_body():
        items = [(slice(q * c, (q + 1) * c), slice(j * LANES, (j + 1) * LANES), j)
                 for q in range(GROUP_CHUNKS) for j in range(n_pairs)]
        n_it = len(items)
        loaded = [[ref[rows, lanes] for ref in (r_s, k_s, v_s, a_s, b_s, ld_s)] for rows, lanes, _ in items]
        r_c, k_c, v_c, a_c, b_c, ld_c = [list(t) for t in zip(*loaded)]
        cum = []
        for i in range(n_it):
            ld_hi, ld_lo = _split(ld_c[i])
            both = _dot(tri_incl, jnp.concatenate([ld_hi, ld_lo], axis=1))
            cum.append(both[:, 0:LANES] + both[:, LANES:])
        gap()
        cum_end = [x[c - 1:c, :] for x in cum]
        at = [a_c[i] * jnp.exp(cum[i] - ld_c[i]) for i in range(n_it)]
        rt = [r_c[i] * jnp.exp(cum[i]) for i in range(n_it)]
        v_bf = [_bf(x) for x in v_c]
        def by_head(x, mask):
            return jnp.concatenate([jnp.where(mask, x, 0), jnp.where(mask, 0, x)], axis=0)

        aa = []
        for i in range(n_it):
            e_neg = jnp.exp(-cum[i])
            rhs = jnp.concatenate([by_head(_bf(b_c[i] * e_neg), head0), by_head(_bf(k_c[i] * e_neg), head0)], axis=0)
            aa.append(_dot_nt(_bf(jnp.concatenate([at[i], rt[i]], axis=0)), rhs))
        gap()
        t_cat = [jnp.where(strict4, x[0:c, :], 0.0) for x in aa]
        q_cat = [_bf(jnp.where(incl4, x[c:2 * c, :], 0.0)) for x in aa]
        akv = [_dot(_bf(t_cat[i][:, LANES:]), by_head(v_bf[i], head0)) for i in range(n_it)]
        n_grp = n_it // 2

        def bdiag(x):
            return jnp.where(bd4, jnp.concatenate([x, x, x, x], axis=0), 0)

        p4 = [jnp.concatenate([t_cat[2 * g][:, 0:LANES], t_cat[2 * g + 1][:, 0:LANES]], axis=1) for g in range(n_grp)]
        inv4 = [eye4 + x for x in p4]
        p4_bf = [_bf(x) for x in p4]
        pw = [_bf(_dot(x, bdiag(x))) for x in p4_bf]
        gap()
        for _ in range(4):
            both = [_dot(jnp.concatenate([pw[g], _bf(inv4[g])], axis=0), bdiag(pw[g])) for g in range(n_grp)]
            inv4 = [inv4[g] + both[g][c:, :] for g in range(n_grp)]
            pw = [_bf(both[g][0:c, :]) for g in range(n_grp)]
            gap()
        inv4 = [_bf(inv4[g] + _dot(_bf(inv4[g]), bdiag(pw[g]))) for g in range(n_grp)]
        inv_cat = [inv4[i // 2][:, (i % 2) * LANES:(i % 2 + 1) * LANES] for i in range(n_it)]
        x_mat = [_bf(jnp.concatenate([at[i], akv[i]], axis=1)) for i in range(n_it)]
        tx = [_dot(inv_cat[i], by_head(x_mat[i], head0_2)) for i in range(n_it)]
        gap()
        tx_bf = [_bf(x) for x in tx]
        v_pad = [jnp.concatenate([zeros_cl, v_bf[i]], axis=1) for i in range(n_it)]
        w_mat = [jnp.concatenate([tx_bf[i], v_pad[i]], axis=0) for i in range(n_it)]
        qw = [_dot(q_cat[i], jnp.concatenate([by_head(tx_bf[i], head0_2), by_head(v_pad[i], head0_2)], axis=0))
              for i in range(n_it)]
        mn = []
        for i in range(n_it):
            e_end = jnp.exp(cum_end[i] - cum[i])
            bk_t = _bf(jnp.concatenate([b_c[i] * e_end, k_c[i] * e_end], axis=0).T)
            mn.append(_dot(bk_t, w_mat[i]))
        gap()
        g_col = [jnp.sum(jnp.where(diag_p, jnp.exp(cum_end[i]), 0.0), axis=1, keepdims=True) for i in range(n_it)]
        states = [st_ref[j] for j in range(n_pairs)]
        for i, (rows, lanes, j) in enumerate(items):
            st = states[j]
            st_bf = _bf(st)
            y_s[rows, lanes] = _dot(_bf(rt[i] + qw[i][:, 0:LANES]), st_bf) + qw[i][:, LANES:]
            m_low = _bf(jnp.where(same_head, mn[i][:, 0:LANES], 0.0))
            states[j] = g_col[i] * st + _dot(m_low, st_bf) + jnp.where(same_head, mn[i][:, LANES:], 0.0)
        for j in range(n_pairs):
            st_ref[j] = jnp.where(ends_sequence, 0.0, states[j])

    group_body()

    y = y_s[...]
    inv_n = 1.0 / HEAD
    mean = _head_sums(y, bd) * inv_n
    d = y - mean
    var = _head_sums(d * d, bd) * inv_n
    y = d * lax.rsqrt(var + RWKV_GN_EPS) * lnw_ref[...] + lnb_ref[...]
    r = r_s[...]
    bonus = _head_sums(r * k_s[...] * rk_ref[...], bd) * v_s[...]
    o_ref[...] = ((y + bonus) * g).astype(o_ref.dtype)


def _conv_steps(p_conv, ends_sequence, w_ref, b_ref, lnw_ref, lnb_ref, shift_ref, o_ref, hbuf):
    tt, dim = o_ref.shape
    sub = CONV_SUB
    ext = sub + 8

    hbuf[CONV_HALO:CONV_HALO + tt, :] = p_conv[:, 0:dim] * _sigmoid(p_conv[:, dim:2 * dim])
    yield
    for s in range(tt // sub):
        base = CONV_HALO + s * sub - 8
        wins = [hbuf[base - 8 * a:base - 8 * a + ext, :] for a in range(CONV_HALO // 8)]
        parts = []
        for r in range(8):
            lags = [8 * a + r for a in range(len(wins)) if 8 * a + r < CONV_WIDTH]
            terms = [wins[m // 8] * w_ref[CONV_WIDTH - 1 - m:CONV_WIDTH - m, :] for m in lags]
            parts.append(functools.reduce(jnp.add, terms))
        stacked = _bf(jnp.concatenate(parts, axis=0))
        acc = _dot(shift_ref[...], stacked) + b_ref[...]
        mu = jnp.mean(acc, axis=-1, keepdims=True)
        d = acc - mu
        var = jnp.mean(d * d, axis=-1, keepdims=True)
        y = d * lax.rsqrt(var + LN_EPS) * lnw_ref[...] + lnb_ref[...]
        o_ref[s * sub:(s + 1) * sub, :] = (y * _sigmoid(y)).astype(o_ref.dtype)
        if s + 1 < tt // sub:
            yield
    hbuf[0:CONV_HALO, :] = jnp.where(ends_sequence, 0.0, hbuf[tt:tt + CONV_HALO, :])


def _conv_shift_matrix():
    ext = CONV_SUB + 8
    u_idx = jnp.arange(CONV_SUB)[:, None]
    col = jnp.arange(8 * ext)[None, :]
    return (col % ext == u_idx + 8 - col // ext).astype(BF16)


def _flash_kernel(q_ref, k_ref, v_ref, o_ref, m_s, acc_s, s_buf, p_buf, al_buf):
    i = pl.program_id(2)
    t = FLASH_BLOCK
    m_s[...] = jnp.full(m_s.shape, -jnp.inf, F32)
    acc_s[...] = jnp.zeros(acc_s.shape, F32)
    p_buf[1] = jnp.zeros(p_buf.shape[1:], BF16)
    al_buf[1] = jnp.ones(al_buf.shape[1:], F32)
    causal = lax.broadcasted_iota(jnp.int32, (t, t), 0) <= lax.broadcasted_iota(jnp.int32, (t, t), 1)

    def keys_of(j):
        return pl.ds(pl.multiple_of(j * t, t), t)

    def scores(j, slot, u0):
        for h in range(2):
            cols = slice(h * LANES, (h + 1) * LANES)
            s_buf[slot, h, :, u0 * t:] = _dot_nt(k_ref[keys_of(j), cols], q_ref[u0 * t:, cols])

    def softmax(slot, u0, masked_sub):
        for u in range(u0, FLASH_SUBS):
            qs = slice(u * t, (u + 1) * t)
            for h in range(2):
                s = s_buf[slot, h, :, qs]
                if u == masked_sub:
                    s = jnp.where(causal, s, -jnp.inf)
                m_prev = m_s[h, :, qs]
                m_new = jnp.maximum(m_prev, jnp.max(s, axis=0, keepdims=True))
                m_s[h, :, qs] = m_new
                al_buf[slot, h, :, qs] = jnp.exp2(m_prev - m_new)
                p_buf[slot, h, :, qs] = jnp.exp2((s - m_new).astype(BF16))

    def accumulate(j, slot, u0):
        for h in range(2):
            v_blk = v_ref[keys_of(j), h * LANES:h * LANES + FLASH_VROWS]
            pv = lax.dot_general(v_blk, p_buf[slot, h, :, u0 * t:], (((0,), (0,)), ((), ())),
                                 preferred_element_type=F32)
            acc_s[h, :, u0 * t:] = al_buf[slot, h, :, u0 * t:] * acc_s[h, :, u0 * t:] + pv

    n_full = FLASH_SUBS * i
    scores(0, 0, 0)

    def body(g, carry):
        for par in range(2):
            k = 2 * g + par
            scores(k + 1, 1 - par, 0)
            accumulate(jnp.maximum(k - 1, 0), 1 - par, 0)
            softmax(par, 0, None)
        return carry

    lax.fori_loop(0, n_full // 2, body, 0)
    accumulate(jnp.maximum(n_full - 1, 0), 1, 0)
    for d in range(FLASH_SUBS):
        if d + 1 < FLASH_SUBS:
            scores(n_full + d + 1, (d + 1) % 2, d + 1)
        softmax(d % 2, d, d)
        accumulate(n_full + d, d % 2, d)
    out_t = jnp.concatenate([acc_s[h, 0:HEAD, :] / acc_s[h, HEAD:HEAD + 1, :] for h in range(2)], axis=0)
    o_ref[...] = out_t.T.astype(o_ref.dtype)


def _flash(q, k, v, batch, seq, n_heads):
    n = q.shape[0]
    tq = FLASH_SUBS * FLASH_BLOCK
    n_q = seq // tq
    return pl.pallas_call(
        _flash_kernel,
        grid=(batch, n_heads // 2, n_q),
        in_specs=[pl.BlockSpec((tq, 2 * LANES), lambda b, h, i: (b * n_q + i, h)),
                  pl.BlockSpec((seq, 2 * LANES), lambda b, h, i: (b, h)),
                  pl.BlockSpec((seq, 2 * LANES), lambda b, h, i: (b, h))],
        out_specs=pl.BlockSpec((tq, LANES), lambda b, h, i: (b * n_q + i, h)),
        out_shape=jax.ShapeDtypeStruct((n, n_heads * HEAD), BF16),
        scratch_shapes=[pltpu.VMEM((2, 1, tq), F32),
                        pltpu.VMEM((2, FLASH_VROWS, tq), F32),
                        pltpu.VMEM((2, 2, FLASH_BLOCK, tq), F32),
                        pltpu.VMEM((2, 2, FLASH_BLOCK, tq), BF16),
                        pltpu.VMEM((2, 2, 1, tq), F32)],
        compiler_params=_params("parallel", "parallel", "arbitrary"),
        name="mla_flash",
    )(q, k, v)


def _outffn_kernel(ya_ref, yb_ref, yc_ref, x_ref, mod_ref, gpm_ref, gpf_ref, gqf_ref, wo_ref, w1_ref, w2_ref,
                   o_ref, x1_s, hid_s, wo_s, w1_s, w2_s, *, da, db, tf, layer):
    step = pl.program_id(0)

    @pl.when(step < FFN_CAST_STEPS)
    def _():
        n_o, n_1, n_2 = wo_ref.shape[0], w1_ref.shape[1], w2_ref.shape[0]
        wo_s[pl.ds(pl.multiple_of(step * n_o, n_o), n_o), :] = wo_ref[...].astype(BF16)
        w1_s[:, pl.ds(pl.multiple_of(step * n_1, n_1), n_1)] = w1_ref[...].astype(BF16)
        w2_s[pl.ds(pl.multiple_of(step * n_2, n_2), n_2), :] = w2_ref[...].astype(BF16)

    @pl.when(step >= FFN_CAST_STEPS)
    def _():
        gpm, gpf, gqf = _pick_row((gpm_ref, gpf_ref, gqf_ref), layer)
        half = x_ref.shape[0] // 2
        halves = [slice(0, half), slice(half, 2 * half)]
        ys = [_dot(ya_ref[r, :], wo_s[0:da, :]) + _dot(yb_ref[r, :], wo_s[da:da + db, :])
              + _dot(yc_ref[r, :], wo_s[da + db:, :]) for r in halves]
        hs = []
        for r, y in zip(halves, ys):
            x1 = x_ref[r, :] + mod_ref[0, 2:3, :] * (_rms(y) * gpm[...])
            x1_s[r, :] = x1
            h = _rms(x1) * gpf[...]
            hs.append((h * (1.0 + mod_ref[0, 4:5, :]) + mod_ref[0, 3:4, :]).astype(BF16))
        for f in range(hid_s.shape[1] // tf):
            cols = slice(f * tf, (f + 1) * tf)
            for r, h in zip(halves, hs):
                hid = jnp.maximum(_dot(h, w1_s[:, cols]), 0.0)
                hid_s[r, cols] = (hid * hid).astype(BF16)
        for r in halves:
            y2 = _dot(hid_s[r, :], w2_s[...])
            o_ref[r, :] = x1_s[r, :] + mod_ref[0, 5:6, :] * (_rms(y2) * gqf[...])


def _outffn(ya, yb, yc, x, layer, mod_all, gains, w_out, w_ff1, w_ff2, seq):
    n, d = x.shape
    d_ff = w_ff1.shape[2]
    tm, tf = 512, 512
    per_b = seq // tm
    cast = FFN_CAST_STEPS
    tile = lambda i: jnp.maximum(i - cast, 0)
    chunk = lambda i: jnp.minimum(i, cast - 1)
    row = lambda w: pl.BlockSpec((tm, w), lambda i: (tile(i), 0))
    return pl.pallas_call(
        functools.partial(_outffn_kernel, da=ya.shape[1], db=yb.shape[1], tf=tf, layer=layer),
        grid=(cast + n // tm,),
        in_specs=[row(ya.shape[1]), row(yb.shape[1]), row(yc.shape[1]), row(d),
                  pl.BlockSpec((None, 1, 6, d), lambda i: (layer, tile(i) // per_b, 0, 0))]
                 + [_layer_spec(a, layer) for a in gains]
                 + [pl.BlockSpec((None, w_out.shape[1] // cast, d), lambda i: (layer, chunk(i), 0)),
                    pl.BlockSpec((None, d, d_ff // cast), lambda i: (layer, 0, chunk(i))),
                    pl.BlockSpec((None, d_ff // cast, d), lambda i: (layer, chunk(i), 0))],
        out_specs=row(d),
        out_shape=jax.ShapeDtypeStruct((n, d), F32),
        scratch_shapes=[pltpu.VMEM((tm, d), F32), pltpu.VMEM((tm, d_ff), BF16),
                        pltpu.VMEM(w_out.shape[1:], BF16), pltpu.VMEM(w_ff1.shape[1:], BF16),
                        pltpu.VMEM(w_ff2.shape[1:], BF16)],
        compiler_params=_params("arbitrary"),
        name="out_ffn",
    )(ya, yb, yc, x, mod_all, *gains, w_out, w_ff1, w_ff2)


def _rot_cols(w):
    half = w.shape[-1] // 2
    return jnp.concatenate([-w[..., half:], w[..., :half]], axis=-1)


def kernel(x, c, positions, g_pre_mix, g_post_mix, g_pre_ffn, g_post_ffn, w_ada, b_ada, w_in, w_out, rwkv_mu, rwkv_w0, rwkv_w2, rwkv_a0, rwkv_a2, rwkv_g2, rwkv_k_k, rwkv_k_a, rwkv_r_k, rwkv_ln_w, rwkv_ln_b, conv_w, conv_b, conv_ln_w, conv_ln_b, mla_q_norm, mla_w_uq, mla_kv_norm, mla_w_ukv, w_ff1, w_ff2):
    batch, seq, d = x.shape
    n = batch * seq
    n_layers = w_ada.shape[0]
    r_dim = rwkv_w0.shape[1]
    w_lora, a_lora, g_lora = rwkv_w2.shape[1], rwkv_a2.shape[1], rwkv_g2.shape[1]
    assert w_lora + a_lora + g_lora == LANES and r_dim % LANES == 0
    n_r = 3 * r_dim + LANES
    c_dim = conv_w.shape[2]
    n_c = 2 * c_dim
    lora = mla_q_norm.shape[1]
    n_heads = mla_w_ukv.shape[2] // (2 * HEAD)
    rope = mla_w_uq.shape[2] // n_heads - HEAD
    assert HEAD + 2 * rope == LANES
    scale = float(HEAD + rope) ** -0.5 * LOG2_E

    xf = x.reshape(n, d)
    c8 = jnp.pad(c, ((0, 8 - batch), (0, 0)))
    mod_all = _ada(c8, w_ada, b_ada.reshape(n_layers, 1, 6 * d))[:, :batch].reshape(n_layers, batch, 6, d)

    inv_freq = 1.0 / (ROPE_THETA ** (jnp.arange(0, rope, 2, dtype=F32) / rope))
    invf = jnp.tile(inv_freq, 2).reshape(rope, 1)
    phase = jnp.where(jnp.arange(rope) < rope // 2, 0.0, -0.5 * jnp.pi).astype(F32).reshape(rope, 1)
    cs = _rope_table(positions.reshape(1, n), invf, phase)

    head_id = jnp.arange(2 * LANES) // HEAD
    bd = (head_id[:, None] == head_id[None, :]).astype(BF16)
    conv_shift = _conv_shift_matrix()

    vec = lambda a: a.reshape(n_layers, -1)
    w_in_bf = w_in.astype(BF16)
    k_pe = w_in_bf[:, :, n_r + n_c + 2 * lora:]
    w_extra = jnp.concatenate([jnp.zeros((n_layers, d, HEAD), BF16), k_pe, _rot_cols(k_pe)], axis=-1)
    wq = mla_w_uq.reshape(n_layers, lora, n_heads, HEAD + rope)
    wuq = jnp.concatenate([wq, _rot_cols(wq[..., HEAD:])], axis=-1).reshape(n_layers, lora, n_heads * LANES)
    wkv = mla_w_ukv.reshape(n_layers, lora, n_heads, 2 * HEAD)
    pad = jnp.zeros_like(wkv[..., :HEAD])
    wuk = jnp.concatenate([wkv[..., :HEAD], pad], axis=-1).reshape(n_layers, lora, n_heads * LANES)
    wuv = jnp.concatenate([wkv[..., HEAD:], pad], axis=-1).reshape(n_layers, lora, n_heads * LANES)
    lora_w = jnp.zeros((n_layers, LANES, 3 * r_dim), F32)
    lora_w = lora_w.at[:, 0:w_lora, 0:r_dim].set(rwkv_w2)
    lora_w = lora_w.at[:, w_lora:w_lora + a_lora, r_dim:2 * r_dim].set(rwkv_a2)
    lora_w = lora_w.at[:, w_lora + a_lora:, 2 * r_dim:].set(rwkv_g2)
    layered = ([vec(g_pre_mix), w_in_bf, w_extra],
               [vec(mla_q_norm), vec(mla_kv_norm), wuq.astype(BF16), wuk.astype(BF16), wuv.astype(BF16)],
               [conv_w, vec(conv_b), vec(conv_ln_w), vec(conv_ln_b)],
               [vec(rwkv_mu), vec(rwkv_w0), vec(rwkv_a0), vec(rwkv_k_k), vec(rwkv_k_a), vec(rwkv_r_k),
                vec(rwkv_ln_w), vec(rwkv_ln_b), lora_w.astype(BF16)])
    gains = [vec(g_post_mix), vec(g_pre_ffn), vec(g_post_ffn)]
    for l in range(n_layers):
        y_a, y_b, q, k, v = _inproj(xf, l, mod_all, cs, layered, (conv_shift, bd),
                                    seq, n_r, n_c, lora, n_heads, rope, scale, r_dim, w_lora, a_lora)
        y_c = _flash(q, k, v, batch, seq, n_heads)
        xf = _outffn(y_a, y_b, y_c, xf, l, mod_all, gains, w_out, w_ff1, w_ff2, seq)
    return xf.reshape(batch, seq, d)
```

```python
import functools
import itertools

import jax
import jax.numpy as jnp
from jax import lax
from jax.experimental import pallas as pl
from jax.experimental.pallas import tpu as pltpu

F32 = jnp.float32
BF16 = jnp.bfloat16

LANES = 128
HEAD = 64
NORM_EPS = 1e-6
LN_EPS = 1e-5
RWKV_GN_EPS = 64e-5
ROPE_THETA = 10000.0
CONV_WIDTH = 31
CONV_HALO = 32
CONV_SUB = 128
CHUNK = 64
FLASH_BLOCK = 256
FLASH_VROWS = HEAD + 16
FLASH_SUBS = 8
GROUP_CHUNKS = 8
ADA_SLOTS = 3
FFN_CAST_STEPS = 8
VMEM_LIMIT = 56 * 1024 * 1024
LOG2_E = 1.4426950408889634


def _dot(a, b):
    return jnp.dot(a, b, preferred_element_type=F32)


def _dot_nt(a, b):
    return lax.dot_general(a, b, (((1,), (1,)), ((), ())), preferred_element_type=F32)


def _bf(x):
    return x.astype(BF16)


def _split(x):
    hi = x.astype(BF16)
    return hi, (x - hi.astype(F32)).astype(BF16)


def _head_sums(x, ones_bd):
    width = x.shape[1]
    groups = [_dot(_bf(x[:, lo:lo + 2 * LANES]), ones_bd) for lo in range(0, width - 2 * LANES + 1, 2 * LANES)]
    if width % (2 * LANES):
        groups.append(_dot(_bf(x[:, width - LANES:]), ones_bd[0:LANES, 0:LANES]))
    return jnp.concatenate(groups, axis=1)


def _rms(x):
    return x * lax.rsqrt(jnp.mean(x * x, axis=-1, keepdims=True) + NORM_EPS)


def _sigmoid(x):
    return 1.0 / (1.0 + jnp.exp(-x))


def _params(*semantics):
    return pltpu.CompilerParams(dimension_semantics=semantics, vmem_limit_bytes=VMEM_LIMIT)


def _ada_kernel(c_ref, w_hbm, b_ref, o_ref, w_buf, sem, *, tn):
    n_layers, _, d6 = w_hbm.shape
    blocks = [(l, j * tn) for l in range(n_layers) for j in range(d6 // tn)]

    def fetch(i):
        l, col = blocks[i]
        slot = i % ADA_SLOTS
        return pltpu.make_async_copy(w_hbm.at[l, :, col:col + tn], w_buf.at[slot], sem.at[slot])

    for i in range(min(ADA_SLOTS, len(blocks))):
        fetch(i).start()
    c = c_ref[...]
    cs = c * _sigmoid(c)
    cs_hi, cs_lo = _split(cs)
    for i, (l, col) in enumerate(blocks):
        fetch(i).wait()
        w_hi, w_lo = _split(w_buf[i % ADA_SLOTS])
        o_ref[l, :, col:col + tn] = (_dot(cs_hi, w_hi) + _dot(cs_hi, w_lo) + _dot(cs_lo, w_hi)
                                     + b_ref[l, :, col:col + tn])
        if i + ADA_SLOTS < len(blocks):
            fetch(i + ADA_SLOTS).start()


def _ada(c8, w_ada, b_ada):
    n_layers, d, d6 = w_ada.shape
    tn = 1536
    return pl.pallas_call(
        functools.partial(_ada_kernel, tn=tn),
        grid=(1,),
        in_specs=[pl.BlockSpec((8, d), lambda i: (0, 0)),
                  pl.BlockSpec(memory_space=pl.ANY),
                  pl.BlockSpec((n_layers, 1, d6), lambda i: (0, 0, 0))],
        out_specs=pl.BlockSpec((n_layers, 8, d6), lambda i: (0, 0, 0)),
        out_shape=jax.ShapeDtypeStruct((n_layers, 8, d6), F32),
        scratch_shapes=[pltpu.VMEM((ADA_SLOTS, d, tn), F32), pltpu.SemaphoreType.DMA((ADA_SLOTS,))],
        compiler_params=_params("arbitrary"),
        name="ada_mod",
    )(c8, w_ada, b_ada)


def _rope_kernel(pos_ref, invf_ref, phase_ref, cs_ref):
    t = jnp.cos(invf_ref[...] * pos_ref[...].astype(F32) + phase_ref[...])
    half = t.shape[0] // 2
    for j in range(4):
        cs_ref[j * half:(j + 1) * half, :] = t[(j // 2) * half:(j // 2 + 1) * half]


def _rope_table(pos_row, invf, phase):
    n = pos_row.shape[1]
    rows = 2 * invf.shape[0]
    tn = 2048
    col = pl.BlockSpec((rows // 2, 1), lambda i: (0, 0))
    return pl.pallas_call(
        _rope_kernel,
        grid=(n // tn,),
        in_specs=[pl.BlockSpec((1, tn), lambda i: (0, i)), col, col],
        out_specs=pl.BlockSpec((rows, tn), lambda i: (0, i)),
        out_shape=jax.ShapeDtypeStruct((rows, n), F32),
        compiler_params=_params("parallel"),
        name="rope_table",
    )(pos_row, invf, phase)


def _inproj_kernel(x_ref, mod_ref, g_ref, w_ref, wx_ref, cs_ref, qn_ref, kvn_ref, wuq_ref, wuk_ref, wuv_ref,
                   cw_ref, cb_ref, clw_ref, clb_ref, shift_ref, *rest,
                   n_r, n_c, lora, n_heads, rope, scale, per_b, w_lora, a_lora, layer):
    g_ref, qn_ref, kvn_ref, cb_ref, clw_ref, clb_ref = _pick_row(
        (g_ref, qn_ref, kvn_ref, cb_ref, clw_ref, clb_ref), layer)
    rwkv_in = _pick_row(rest[:8], layer) + list(rest[8:10])
    ya_ref, yb_ref, q_ref, k_ref, v_ref, hbuf, zbuf, st_ref = rest[10:18]
    rwkv_scratch = rest[18:]

    @pl.when(pl.program_id(0) == 0)
    def _():
        hbuf[0:CONV_HALO, :] = jnp.zeros((CONV_HALO, hbuf.shape[1]), F32)
        zbuf[0:8, :] = jnp.zeros((8, zbuf.shape[1]), F32)
        st_ref[...] = jnp.zeros(st_ref.shape, F32)

    ends_sequence = (pl.program_id(0) + 1) % per_b == 0

    h = _rms(x_ref[...]) * g_ref[...]
    h = (h * (1.0 + mod_ref[0, 1:2, :]) + mod_ref[0, 0:1, :]).astype(BF16)
    p_rwkv = _dot(h, w_ref[:, :n_r])
    p_conv = _dot(h, w_ref[:, n_r:n_r + n_c])

    def mla_steps():
        lat = _dot(h, w_ref[:, n_r + n_c:n_r + n_c + 2 * lora])
        k_rot = _dot(h, wx_ref[...])
        cs_half = cs_ref[...].T
        cs = jnp.concatenate([cs_half, cs_half], axis=1)
        qn = (_rms(lat[:, 0:lora]) * qn_ref[...]).astype(BF16)
        kvn = (_rms(lat[:, lora:]) * kvn_ref[...]).astype(BF16)
        lane = lax.broadcasted_iota(jnp.int32, cs.shape, 1)
        t = k_rot * cs
        kpe = jnp.where(lane < HEAD, 0.0, t + pltpu.roll(t, rope, 1) + pltpu.roll(t, LANES - rope, 1))
        q_mul = jnp.where(lane < HEAD, 1.0, cs) * scale
        q_mul2 = jnp.concatenate([q_mul, q_mul], axis=1)
        kpe2 = jnp.concatenate([kpe, kpe], axis=1)
        is_v = jnp.concatenate([lane < HEAD, lane < HEAD], axis=1)
        yield
        for j in range(n_heads // 2):
            cols = slice(2 * j * LANES, 2 * (j + 1) * LANES)
            q_ref[:, cols] = (_dot(qn, wuq_ref[:, cols]) * q_mul2).astype(BF16)
            k_ref[:, cols] = (_dot(kvn, wuk_ref[:, cols]) + kpe2).astype(BF16)
            v_ref[:, cols] = jnp.where(is_v, _dot(kvn, wuv_ref[:, cols]), 1.0).astype(BF16)
            yield

    fill = itertools.chain(mla_steps(),
                           _conv_steps(p_conv, ends_sequence, cw_ref, cb_ref, clw_ref, clb_ref, shift_ref, yb_ref, hbuf))
    _rwkv_tile(p_rwkv, ends_sequence, *rwkv_in, ya_ref, zbuf, st_ref, *rwkv_scratch,
               w_lora=w_lora, a_lora=a_lora, fill=fill)
    for _ in fill:
        pass


def _layer_spec(a, layer):
    if a.ndim == 2:
        return pl.BlockSpec(a.shape, lambda i: (0, 0))
    return pl.BlockSpec((None,) + a.shape[1:], lambda i: (layer,) + (0,) * (a.ndim - 1))


def _pick_row(refs, layer):
    return [r.at[layer:layer + 1, :] for r in refs]


def _inproj(x, layer, mod_all, cs, layered, shared, seq, n_r, n_c, lora, n_heads, rope, scale, r_dim, w_lora, a_lora):
    n, d = x.shape
    tm = CHUNK * GROUP_CHUNKS
    per_b = seq // tm
    row = lambda w: pl.BlockSpec((tm, w), lambda i: (i, 0))
    full = lambda a: pl.BlockSpec(a.shape, lambda i: (0, 0))
    act = pltpu.VMEM((tm, r_dim), F32)
    g_w_wx, mla, conv, rwkv = layered
    conv_shift, bd = shared
    return pl.pallas_call(
        functools.partial(_inproj_kernel, n_r=n_r, n_c=n_c, lora=lora, n_heads=n_heads, rope=rope, scale=scale,
                          per_b=per_b, w_lora=w_lora, a_lora=a_lora, layer=layer),
        grid=(n // tm,),
        in_specs=[row(d), pl.BlockSpec((None, 1, 6, d), lambda i: (layer, i // per_b, 0, 0))]
                 + [_layer_spec(a, layer) for a in g_w_wx] + [pl.BlockSpec((cs.shape[0], tm), lambda i: (0, i))]
                 + [_layer_spec(a, layer) for a in mla]
                 + [_layer_spec(a, layer) for a in conv] + [full(conv_shift)]
                 + [_layer_spec(a, layer) for a in rwkv] + [full(bd)],
        out_specs=[row(r_dim), row(n_c // 2)] + [row(n_heads * LANES)] * 3,
        out_shape=[jax.ShapeDtypeStruct((n, r_dim), BF16), jax.ShapeDtypeStruct((n, n_c // 2), BF16)]
                  + [jax.ShapeDtypeStruct((n, n_heads * LANES), BF16)] * 3,
        scratch_shapes=[pltpu.VMEM((tm + CONV_HALO, n_c // 2), F32),
                        pltpu.VMEM((tm + 8, n_r), F32),
                        pltpu.VMEM((r_dim // LANES, LANES, LANES), F32),
                        act, act, act, act, act, act, act],
        compiler_params=_params("arbitrary"),
        name="in_proj",
    )(x, mod_all, *g_w_wx, cs, *mla, *conv, conv_shift, *rwkv, bd)


def _rwkv_tile(p, ends_sequence, mu_ref, w0_ref, a0_ref, kk_ref, ka_ref, rk_ref, lnw_ref, lnb_ref, lora_ref, bd_ref,
               o_ref, zbuf, st_ref, r_s, k_s, v_s, a_s, b_s, ld_s, y_s, *, w_lora, a_lora, fill):
    tt, dim = o_ref.shape
    assert tt == CHUNK * GROUP_CHUNKS

    def gap():
        next(fill, None)

    n_pairs = dim // LANES
    c = CHUNK

    zbuf[8:8 + tt, :] = p
    prev = zbuf[7:7 + tt, :]
    zbuf[7:8, :] = jnp.where(ends_sequence, 0.0, zbuf[tt + 7:tt + 8, :])
    z = p + (prev - p) * mu_ref[...]

    r = z[:, 0:dim]
    k = z[:, dim:2 * dim]
    v = z[:, 2 * dim:3 * dim]
    lo = z[:, 3 * dim:3 * dim + LANES]
    lane = lax.broadcasted_iota(jnp.int32, lo.shape, 1)
    act = jnp.where(lane < w_lora, jnp.tanh(lo), jnp.where(lane < w_lora + a_lora, lo, _sigmoid(lo)))
    lora = _dot(_bf(act), lora_ref[...])
    x_w = w0_ref[...] + lora[:, 0:dim]
    w_log = -(jnp.maximum(-x_w, 0.0) + jnp.log(1.0 + jnp.exp(-jnp.abs(x_w)))) - 0.5
    a = _sigmoid(a0_ref[...] + lora[:, dim:2 * dim])
    g = lora[:, 2 * dim:3 * dim]
    bd = bd_ref[...]
    kk = k * kk_ref[...]
    kk = kk * lax.rsqrt(jnp.maximum(_head_sums(kk * kk, bd), 1e-24))
    k = k * (1.0 + (a - 1.0) * ka_ref[...])
    r_s[...] = r
    k_s[...] = k
    v_s[...] = v
    a_s[...] = -kk
    b_s[...] = kk * a
    ld_s[...] = -jnp.exp(w_log)

    row_c = lax.broadcasted_iota(jnp.int32, (c, c), 0)
    col_c = lax.broadcasted_iota(jnp.int32, (c, c), 1)
    tri_incl = (row_c >= col_c).astype(BF16)
    head0 = lax.broadcasted_iota(jnp.int32, (c, LANES), 1) < HEAD
    lane_2 = lax.broadcasted_iota(jnp.int32, (c, 2 * LANES), 1)
    head0_2 = jnp.where(lane_2 >= LANES, lane_2 - LANES, lane_2) < HEAD
    row_4 = lax.broadcasted_iota(jnp.int32, (c, 4 * c), 0)
    col_4 = jnp.bitwise_and(lax.broadcasted_iota(jnp.int32, (c, 4 * c), 1), c - 1)
    strict4 = row_4 > col_4
    incl4 = row_4 >= col_4
    eye4 = (row_4 == col_4).astype(F32)
    bd4 = jnp.bitwise_xor(lax.broadcasted_iota(jnp.int32, (4 * c, 4 * c), 0),
                          lax.broadcasted_iota(jnp.int32, (4 * c, 4 * c), 1)) < c
    row_p = lax.broadcasted_iota(jnp.int32, (LANES, LANES), 0)
    col_p = lax.broadcasted_iota(jnp.int32, (LANES, LANES), 1)
    same_head = (row_p < HEAD) == (col_p < HEAD)
    diag_p = row_p == col_p
    zeros_cl = jnp.zeros((c, LANES), BF16)

    def group_body():
        items = [(slice(q * c, (q + 1) * c), slice(j * LANES, (j + 1) * LANES), j)
                 for q in range(GROUP_CHUNKS) for j in range(n_pairs)]
        n_it = len(items)
        loaded = [[ref[rows, lanes] for ref in (r_s, k_s, v_s, a_s, b_s, ld_s)] for rows, lanes, _ in items]
        r_c, k_c, v_c, a_c, b_c, ld_c = [list(t) for t in zip(*loaded)]
        cum = []
        for i in range(n_it):
            ld_hi, ld_lo = _split(ld_c[i])
            both = _dot(tri_incl, jnp.concatenate([ld_hi, ld_lo], axis=1))
            cum.append(both[:, 0:LANES] + both[:, LANES:])
        gap()
        cum_end = [x[c - 1:c, :] for x in cum]
        at = [a_c[i] * jnp.exp(cum[i] - ld_c[i]) for i in range(n_it)]
        rt = [r_c[i] * jnp.exp(cum[i]) for i in range(n_it)]
        v_bf = [_bf(x) for x in v_c]
        def by_head(x, mask):
            return jnp.concatenate([jnp.where(mask, x, 0), jnp.where(mask, 0, x)], axis=0)

        aa = []
        for i in range(n_it):
            e_neg = jnp.exp(-cum[i])
            rhs = jnp.concatenate([by_head(_bf(b_c[i] * e_neg), head0), by_head(_bf(k_c[i] * e_neg), head0)], axis=0)
            aa.append(_dot_nt(_bf(jnp.concatenate([at[i], rt[i]], axis=0)), rhs))
        gap()
        t_cat = [jnp.where(strict4, x[0:c, :], 0.0) for x in aa]
        q_cat = [_bf(jnp.where(incl4, x[c:2 * c, :], 0.0)) for x in aa]
        akv = [_dot(_bf(t_cat[i][:, LANES:]), by_head(v_bf[i], head0)) for i in range(n_it)]
        n_grp = n_it // 2

        def bdiag(x):
            return jnp.where(bd4, jnp.concatenate([x, x, x, x], axis=0), 0)

        p4 = [jnp.concatenate([t_cat[2 * g][:, 0:LANES], t_cat[2 * g + 1][:, 0:LANES]], axis=1) for g in range(n_grp)]
        inv4 = [eye4 + x for x in p4]
        p4_bf = [_bf(x) for x in p4]
        pw = [_bf(_dot(x, bdiag(x))) for x in p4_bf]
        gap()
        for _ in range(4):
            both = [_dot(jnp.concatenate([pw[g], _bf(inv4[g])], axis=0), bdiag(pw[g])) for g in range(n_grp)]
            inv4 = [inv4[g] + both[g][c:, :] for g in range(n_grp)]
            pw = [_bf(both[g][0:c, :]) for g in range(n_grp)]
            gap()
        inv4 = [_bf(inv4[g] + _dot(_bf(inv4[g]), bdiag(pw[g]))) for g in range(n_grp)]
        inv_cat = [inv4[i // 2][:, (i % 2) * LANES:(i % 2 + 1) * LANES] for i in range(n_it)]
        x_mat = [_bf(jnp.concatenate([at[i], akv[i]], axis=1)) for i in range(n_it)]
        tx = [_dot(inv_cat[i], by_head(x_mat[i], head0_2)) for i in range(n_it)]
        gap()
        tx_bf = [_bf(x) for x in tx]
        v_pad = [jnp.concatenate([zeros_cl, v_bf[i]], axis=1) for i in range(n_it)]
        w_mat = [jnp.concatenate([tx_bf[i], v_pad[i]], axis=0) for i in range(n_it)]
        qw = [_dot(q_cat[i], jnp.concatenate([by_head(tx_bf[i], head0_2), by_head(v_pad[i], head0_2)], axis=0))
              for i in range(n_it)]
        mn = []
        for i in range(n_it):
            e_end = jnp.exp(cum_end[i] - cum[i])
            bk_t = _bf(jnp.concatenate([b_c[i] * e_end, k_c[i] * e_end], axis=0).T)
            mn.append(_dot(bk_t, w_mat[i]))
        gap()
        g_col = [jnp.sum(jnp.where(diag_p, jnp.exp(cum_end[i]), 0.0), axis=1, keepdims=True) for i in range(n_it)]
        states = [st_ref[j] for j in range(n_pairs)]
        for i, (rows, lanes, j) in enumerate(items):
            st = states[j]
            st_bf = _bf(st)
            y_s[rows, lanes] = _dot(_bf(rt[i] + qw[i][:, 0:LANES]), st_bf) + qw[i][:, LANES:]
            m_low = _bf(jnp.where(same_head, mn[i][:, 0:LANES], 0.0))
            states[j] = g_col[i] * st + _dot(m_low, st_bf) + jnp.where(same_head, mn[i][:, LANES:], 0.0)
        for j in range(n_pairs):
            st_ref[j] = jnp.where(ends_sequence, 0.0, states[j])

    group_body()

    y = y_s[...]
    inv_n = 1.0 / HEAD
    mean = _head_sums(y, bd) * inv_n
    d = y - mean
    var = _head_sums(d * d, bd) * inv_n
    y = d * lax.rsqrt(var + RWKV_GN_EPS) * lnw_ref[...] + lnb_ref[...]
    r = r_s[...]
    bonus = _head_sums(r * k_s[...] * rk_ref[...], bd) * v_s[...]
    o_ref[...] = ((y + bonus) * g).astype(o_ref.dtype)


def _conv_steps(p_conv, ends_sequence, w_ref, b_ref, lnw_ref, lnb_ref, shift_ref, o_ref, hbuf):
    tt, dim = o_ref.shape
    sub = CONV_SUB
    ext = sub + 8

    hbuf[CONV_HALO:CONV_HALO + tt, :] = p_conv[:, 0:dim] * _sigmoid(p_conv[:, dim:2 * dim])
    yield
    for s in range(tt // sub):
        base = CONV_HALO + s * sub - 8
        wins = [hbuf[base - 8 * a:base - 8 * a + ext, :] for a in range(CONV_HALO // 8)]
        parts = []
        for r in range(8):
            lags = [8 * a + r for a in range(len(wins)) if 8 * a + r < CONV_WIDTH]
            terms = [wins[m // 8] * w_ref[CONV_WIDTH - 1 - m:CONV_WIDTH - m, :] for m in lags]
            parts.append(functools.reduce(jnp.add, terms))
        stacked = _bf(jnp.concatenate(parts, axis=0))
        acc = _dot(shift_ref[...], stacked) + b_ref[...]
        mu = jnp.mean(acc, axis=-1, keepdims=True)
        d = acc - mu
        var = jnp.mean(d * d, axis=-1, keepdims=True)
        y = d * lax.rsqrt(var + LN_EPS) * lnw_ref[...] + lnb_ref[...]
        o_ref[s * sub:(s + 1) * sub, :] = (y * _sigmoid(y)).astype(o_ref.dtype)
        if s + 1 < tt // sub:
            yield
    hbuf[0:CONV_HALO, :] = jnp.where(ends_sequence, 0.0, hbuf[tt:tt + CONV_HALO, :])


def _conv_shift_matrix():
    ext = CONV_SUB + 8
    u_idx = jnp.arange(CONV_SUB)[:, None]
    col = jnp.arange(8 * ext)[None, :]
    return (col % ext == u_idx + 8 - col // ext).astype(BF16)


def _flash_kernel(q_ref, k_ref, v_ref, o_ref, m_s, acc_s, s_buf, p_buf, al_buf):
    i = pl.program_id(2)
    t = FLASH_BLOCK
    m_s[...] = jnp.full(m_s.shape, -jnp.inf, F32)
    acc_s[...] = jnp.zeros(acc_s.shape, F32)
    p_buf[1] = jnp.zeros(p_buf.shape[1:], BF16)
    al_buf[1] = jnp.ones(al_buf.shape[1:], F32)
    causal = lax.broadcasted_iota(jnp.int32, (t, t), 0) <= lax.broadcasted_iota(jnp.int32, (t, t), 1)

    def keys_of(j):
        return pl.ds(pl.multiple_of(j * t, t), t)

    def scores(j, slot, u0):
        for h in range(2):
            cols = slice(h * LANES, (h + 1) * LANES)
            s_buf[slot, h, :, u0 * t:] = _dot_nt(k_ref[keys_of(j), cols], q_ref[u0 * t:, cols])

    def softmax(slot, u0, masked_sub):
        for u in range(u0, FLASH_SUBS):
            qs = slice(u * t, (u + 1) * t)
            for h in range(2):
                s = s_buf[slot, h, :, qs]
                if u == masked_sub:
                    s = jnp.where(causal, s, -jnp.inf)
                m_prev = m_s[h, :, qs]
                m_new = jnp.maximum(m_prev, jnp.max(s, axis=0, keepdims=True))
                m_s[h, :, qs] = m_new
                al_buf[slot, h, :, qs] = jnp.exp2(m_prev - m_new)
                p_buf[slot, h, :, qs] = jnp.exp2((s - m_new).astype(BF16))

    def accumulate(j, slot, u0):
        for h in range(2):
            v_blk = v_ref[keys_of(j), h * LANES:h * LANES + FLASH_VROWS]
            pv = lax.dot_general(v_blk, p_buf[slot, h, :, u0 * t:], (((0,), (0,)), ((), ())),
                                 preferred_element_type=F32)
            acc_s[h, :, u0 * t:] = al_buf[slot, h, :, u0 * t:] * acc_s[h, :, u0 * t:] + pv

    n_full = FLASH_SUBS * i
    scores(0, 0, 0)

    def body(g, carry):
        for par in range(2):
            k = 2 * g + par
            scores(k + 1, 1 - par, 0)
            accumulate(jnp.maximum(k - 1, 0), 1 - par, 0)
            softmax(par, 0, None)
        return carry

    lax.fori_loop(0, n_full // 2, body, 0)
    accumulate(jnp.maximum(n_full - 1, 0), 1, 0)
    for d in range(FLASH_SUBS):
        if d + 1 < FLASH_SUBS:
            scores(n_full + d + 1, (d + 1) % 2, d + 1)
        softmax(d % 2, d, d)
        accumulate(n_full + d, d % 2, d)
    out_t = jnp.concatenate([acc_s[h, 0:HEAD, :] / acc_s[h, HEAD:HEAD + 1, :] for h in range(2)], axis=0)
    o_ref[...] = out_t.T.astype(o_ref.dtype)


def _flash(q, k, v, batch, seq, n_heads):
    n = q.shape[0]
    tq = FLASH_SUBS * FLASH_BLOCK
    n_q = seq // tq
    return pl.pallas_call(
        _flash_kernel,
        grid=(batch, n_heads // 2, n_q),
        in_specs=[pl.BlockSpec((tq, 2 * LANES), lambda b, h, i: (b * n_q + i, h)),
                  pl.BlockSpec((seq, 2 * LANES), lambda b, h, i: (b, h)),
                  pl.BlockSpec((seq, 2 * LANES), lambda b, h, i: (b, h))],
        out_specs=pl.BlockSpec((tq, LANES), lambda b, h, i: (b * n_q + i, h)),
        out_shape=jax.ShapeDtypeStruct((n, n_heads * HEAD), BF16),
        scratch_shapes=[pltpu.VMEM((2, 1, tq), F32),
                        pltpu.VMEM((2, FLASH_VROWS, tq), F32),
                        pltpu.VMEM((2, 2, FLASH_BLOCK, tq), F32),
                        pltpu.VMEM((2, 2, FLASH_BLOCK, tq), BF16),
                        pltpu.VMEM((2, 2, 1, tq), F32)],
        compiler_params=_params("parallel", "parallel", "arbitrary"),
        name="mla_flash",
    )(q, k, v)


def _outffn_kernel(ya_ref, yb_ref, yc_ref, x_ref, mod_ref, gpm_ref, gpf_ref, gqf_ref, wo_ref, w1_ref, w2_ref,
                   o_ref, x1_s, hid_s, wo_s, w1_s, w2_s, *, da, db, tf, layer):
    step = pl.program_id(0)

    @pl.when(step < FFN_CAST_STEPS)
    def _():
        n_o, n_1, n_2 = wo_ref.shape[0], w1_ref.shape[1], w2_ref.shape[0]
        wo_s[pl.ds(pl.multiple_of(step * n_o, n_o), n_o), :] = wo_ref[...].astype(BF16)
        w1_s[:, pl.ds(pl.multiple_of(step * n_1, n_1), n_1)] = w1_ref[...].astype(BF16)
        w2_s[pl.ds(pl.multiple_of(step * n_2, n_2), n_2), :] = w2_ref[...].astype(BF16)

    @pl.when(step >= FFN_CAST_STEPS)
    def _():
        gpm, gpf, gqf = _pick_row((gpm_ref, gpf_ref, gqf_ref), layer)
        half = x_ref.shape[0] // 2
        halves = [slice(0, half), slice(half, 2 * half)]
        ys = [_dot(ya_ref[r, :], wo_s[0:da, :]) + _dot(yb_ref[r, :], wo_s[da:da + db, :])
              + _dot(yc_ref[r, :], wo_s[da + db:, :]) for r in halves]
        hs = []
        for r, y in zip(halves, ys):
            x1 = x_ref[r, :] + mod_ref[0, 2:3, :] * (_rms(y) * gpm[...])
            x1_s[r, :] = x1
            h = _rms(x1) * gpf[...]
            hs.append((h * (1.0 + mod_ref[0, 4:5, :]) + mod_ref[0, 3:4, :]).astype(BF16))
        for f in range(hid_s.shape[1] // tf):
            cols = slice(f * tf, (f + 1) * tf)
            for r, h in zip(halves, hs):
                hid = jnp.maximum(_dot(h, w1_s[:, cols]), 0.0)
                hid_s[r, cols] = (hid * hid).astype(BF16)
        for r in halves:
            y2 = _dot(hid_s[r, :], w2_s[...])
            o_ref[r, :] = x1_s[r, :] + mod_ref[0, 5:6, :] * (_rms(y2) * gqf[...])


def _outffn(ya, yb, yc, x, layer, mod_all, gains, w_out, w_ff1, w_ff2, seq):
    n, d = x.shape
    d_ff = w_ff1.shape[2]
    tm, tf = 512, 512
    per_b = seq // tm
    cast = FFN_CAST_STEPS
    tile = lambda i: jnp.maximum(i - cast, 0)
    chunk = lambda i: jnp.minimum(i, cast - 1)
    row = lambda w: pl.BlockSpec((tm, w), lambda i: (tile(i), 0))
    return pl.pallas_call(
        functools.partial(_outffn_kernel, da=ya.shape[1], db=yb.shape[1], tf=tf, layer=layer),
        grid=(cast + n // tm,),
        in_specs=[row(ya.shape[1]), row(yb.shape[1]), row(yc.shape[1]), row(d),
                  pl.BlockSpec((None, 1, 6, d), lambda i: (layer, tile(i) // per_b, 0, 0))]
                 + [_layer_spec(a, layer) for a in gains]
                 + [pl.BlockSpec((None, w_out.shape[1] // cast, d), lambda i: (layer, chunk(i), 0)),
                    pl.BlockSpec((None, d, d_ff // cast), lambda i: (layer, 0, chunk(i))),
                    pl.BlockSpec((None, d_ff // cast, d), lambda i: (layer, chunk(i), 0))],
        out_specs=row(d),
        out_shape=jax.ShapeDtypeStruct((n, d), F32),
        scratch_shapes=[pltpu.VMEM((tm, d), F32), pltpu.VMEM((tm, d_ff), BF16),
                        pltpu.VMEM(w_out.shape[1:], BF16), pltpu.VMEM(w_ff1.shape[1:], BF16),
                        pltpu.VMEM(w_ff2.shape[1:], BF16)],
        compiler_params=_params("arbitrary"),
        name="out_ffn",
    )(ya, yb, yc, x, mod_all, *gains, w_out, w_ff1, w_ff2)


def _rot_cols(w):
    half = w.shape[-1] // 2
    return jnp.concatenate([-w[..., half:], w[..., :half]], axis=-1)


def kernel(x, c, positions, g_pre_mix, g_post_mix, g_pre_ffn, g_post_ffn, w_ada, b_ada, w_in, w_out, rwkv_mu, rwkv_w0, rwkv_w2, rwkv_a0, rwkv_a2, rwkv_g2, rwkv_k_k, rwkv_k_a, rwkv_r_k, rwkv_ln_w, rwkv_ln_b, conv_w, conv_b, conv_ln_w, conv_ln_b, mla_q_norm, mla_w_uq, mla_kv_norm, mla_w_ukv, w_ff1, w_ff2):
    batch, seq, d = x.shape
    n = batch * seq
    n_layers = w_ada.shape[0]
    r_dim = rwkv_w0.shape[1]
    w_lora, a_lora, g_lora = rwkv_w2.shape[1], rwkv_a2.shape[1], rwkv_g2.shape[1]
    assert w_lora + a_lora + g_lora == LANES and r_dim % LANES == 0
    n_r = 3 * r_dim + LANES
    c_dim = conv_w.shape[2]
    n_c = 2 * c_dim
    lora = mla_q_norm.shape[1]
    n_heads = mla_w_ukv.shape[2] // (2 * HEAD)
    rope = mla_w_uq.shape[2] // n_heads - HEAD
    assert HEAD + 2 * rope == LANES
    scale = float(HEAD + rope) ** -0.5 * LOG2_E

    xf = x.reshape(n, d)
    c8 = jnp.pad(c, ((0, 8 - batch), (0, 0)))
    mod_all = _ada(c8, w_ada, b_ada.reshape(n_layers, 1, 6 * d))[:, :batch].reshape(n_layers, batch, 6, d)

    inv_freq = 1.0 / (ROPE_THETA ** (jnp.arange(0, rope, 2, dtype=F32) / rope))
    invf = jnp.tile(inv_freq, 2).reshape(rope, 1)
    phase = jnp.where(jnp.arange(rope) < rope // 2, 0.0, -0.5 * jnp.pi).astype(F32).reshape(rope, 1)
    cs = _rope_table(positions.reshape(1, n), invf, phase)

    head_id = jnp.arange(2 * LANES) // HEAD
    bd = (head_id[:, None] == head_id[None, :]).astype(BF16)
    conv_shift = _conv_shift_matrix()

    vec = lambda a: a.reshape(n_layers, -1)
    w_in_bf = w_in.astype(BF16)
    k_pe = w_in_bf[:, :, n_r + n_c + 2 * lora:]
    w_extra = jnp.concatenate([jnp.zeros((n_layers, d, HEAD), BF16), k_pe, _rot_cols(k_pe)], axis=-1)
    wq = mla_w_uq.reshape(n_layers, lora, n_heads, HEAD + rope)
    wuq = jnp.concatenate([wq, _rot_cols(wq[..., HEAD:])], axis=-1).reshape(n_layers, lora, n_heads * LANES)
    wkv = mla_w_ukv.reshape(n_layers, lora, n_heads, 2 * HEAD)
    pad = jnp.zeros_like(wkv[..., :HEAD])
    wuk = jnp.concatenate([wkv[..., :HEAD], pad], axis=-1).reshape(n_layers, lora, n_heads * LANES)
    wuv = jnp.concatenate([wkv[..., HEAD:], pad], axis=-1).reshape(n_layers, lora, n_heads * LANES)
    lora_w = jnp.zeros((n_layers, LANES, 3 * r_dim), F32)
    lora_w = lora_w.at[:, 0:w_lora, 0:r_dim].set(rwkv_w2)
    lora_w = lora_w.at[:, w_lora:w_lora + a_lora, r_dim:2 * r_dim].set(rwkv_a2)
    lora_w = lora_w.at[:, w_lora + a_lora:, 2 * r_dim:].set(rwkv_g2)
    layered = ([vec(g_pre_mix), w_in_bf, w_extra],
               [vec(mla_q_norm), vec(mla_kv_norm), wuq.astype(BF16), wuk.astype(BF16), wuv.astype(BF16)],
               [conv_w, vec(conv_b), vec(conv_ln_w), vec(conv_ln_b)],
               [vec(rwkv_mu), vec(rwkv_w0), vec(rwkv_a0), vec(rwkv_k_k), vec(rwkv_k_a), vec(rwkv_r_k),
                vec(rwkv_ln_w), vec(rwkv_ln_b), lora_w.astype(BF16)])
    gains = [vec(g_post_mix), vec(g_pre_ffn), vec(g_post_ffn)]
    for l in range(n_layers):
        y_a, y_b, q, k, v = _inproj(xf, l, mod_all, cs, layered, (conv_shift, bd),
                                    seq, n_r, n_c, lora, n_heads, rope, scale, r_dim, w_lora, a_lora)
        y_c = _flash(q, k, v, batch, seq, n_heads)
        xf = _outffn(y_a, y_b, y_c, xf, l, mod_all, gains, w_out, w_ff1, w_ff2, seq)
    return xf.reshape(batch, seq, d)
```

```python
import functools
import itertools

import jax
import jax.numpy as jnp
from jax import lax
from jax.experimental import pallas as pl
from jax.experimental.pallas import tpu as pltpu

F32 = jnp.float32
BF16 = jnp.bfloat16

LANES = 128
HEAD = 64
NORM_EPS = 1e-6
LN_EPS = 1e-5
RWKV_GN_EPS = 64e-5
ROPE_THETA = 10000.0
CONV_WIDTH = 31
CONV_HALO = 32
CONV_SUB = 128
CHUNK = 64
FLASH_BLOCK = 256
FLASH_VROWS = HEAD + 16
FLASH_SUBS = 8
GROUP_CHUNKS = 8
ADA_SLOTS = 3
FFN_CAST_STEPS = 8
VMEM_LIMIT = 56 * 1024 * 1024
LOG2_E = 1.4426950408889634


def _dot(a, b):
    return jnp.dot(a, b, preferred_element_type=F32)


def _dot_nt(a, b):
    return lax.dot_general(a, b, (((1,), (1,)), ((), ())), preferred_element_type=F32)


def _bf(x):
    return x.astype(BF16)


def _split(x):
    hi = x.astype(BF16)
    return hi, (x - hi.astype(F32)).astype(BF16)


def _head_sums(x, ones_bd):
    width = x.shape[1]
    groups = [_dot(_bf(x[:, lo:lo + 2 * LANES]), ones_bd) for lo in range(0, width - 2 * LANES + 1, 2 * LANES)]
    if width % (2 * LANES):
        groups.append(_dot(_bf(x[:, width - LANES:]), ones_bd[0:LANES, 0:LANES]))
    return jnp.concatenate(groups, axis=1)


def _rms(x):
    return x * lax.rsqrt(jnp.mean(x * x, axis=-1, keepdims=True) + NORM_EPS)


def _sigmoid(x):
    return 1.0 / (1.0 + jnp.exp(-x))


def _params(*semantics):
    return pltpu.CompilerParams(dimension_semantics=semantics, vmem_limit_bytes=VMEM_LIMIT)


def _ada_kernel(c_ref, w_hbm, b_ref, o_ref, w_buf, sem, *, tn):
    n_layers, _, d6 = w_hbm.shape
    blocks = [(l, j * tn) for l in range(n_layers) for j in range(d6 // tn)]

    def fetch(i):
        l, col = blocks[i]
        slot = i % ADA_SLOTS
        return pltpu.make_async_copy(w_hbm.at[l, :, col:col + tn], w_buf.at[slot], sem.at[slot])

    for i in range(min(ADA_SLOTS, len(blocks))):
        fetch(i).start()
    c = c_ref[...]
    cs = c * _sigmoid(c)
    hi = cs.astype(BF16).astype(F32)
    rows = c.shape[0]
    cs_both = jnp.concatenate([hi, cs - hi], axis=0).astype(BF16)
    cs_hi = cs_both[:rows]
    for i, (l, col) in enumerate(blocks):
        fetch(i).wait()
        w_hi, w_lo = _split(w_buf[i % ADA_SLOTS])
        both = _dot(cs_both, w_hi)
        o_ref[l, :, col:col + tn] = (both[:rows] + both[rows:] + _dot(cs_hi, w_lo)
                                     + b_ref[l, :, col:col + tn])
        if i + ADA_SLOTS < len(blocks):
            fetch(i + ADA_SLOTS).start()


def _ada(c8, w_ada, b_ada):
    n_layers, d, d6 = w_ada.shape
    tn = 768
    return pl.pallas_call(
        functools.partial(_ada_kernel, tn=tn),
        grid=(1,),
        in_specs=[pl.BlockSpec((8, d), lambda i: (0, 0)),
                  pl.BlockSpec(memory_space=pl.ANY),
                  pl.BlockSpec((n_layers, 1, d6), lambda i: (0, 0, 0))],
        out_specs=pl.BlockSpec((n_layers, 8, d6), lambda i: (0, 0, 0)),
        out_shape=jax.ShapeDtypeStruct((n_layers, 8, d6), F32),
        scratch_shapes=[pltpu.VMEM((ADA_SLOTS, d, tn), F32), pltpu.SemaphoreType.DMA((ADA_SLOTS,))],
        compiler_params=_params("arbitrary"),
        name="ada_mod",
    )(c8, w_ada, b_ada)


def _rope_kernel(pos_ref, invf_ref, phase_ref, cs_ref):
    t = jnp.cos(invf_ref[...] * pos_ref[...].astype(F32) + phase_ref[...])
    half = t.shape[0] // 2
    for j in range(4):
        cs_ref[j * half:(j + 1) * half, :] = t[(j // 2) * half:(j // 2 + 1) * half]


def _rope_table(pos_row, invf, phase):
    n = pos_row.shape[1]
    rows = 2 * invf.shape[0]
    tn = 2048
    col = pl.BlockSpec((rows // 2, 1), lambda i: (0, 0))
    return pl.pallas_call(
        _rope_kernel,
        grid=(n // tn,),
        in_specs=[pl.BlockSpec((1, tn), lambda i: (0, i)), col, col],
        out_specs=pl.BlockSpec((rows, tn), lambda i: (0, i)),
        out_shape=jax.ShapeDtypeStruct((rows, n), F32),
        compiler_params=_params("parallel"),
        name="rope_table",
    )(pos_row, invf, phase)


def _inproj_kernel(x_ref, mod_ref, g_ref, w_ref, wx_ref, cs_ref, qn_ref, kvn_ref, wuq_ref, wuk_ref, wuv_ref,
                   cw_ref, cb_ref, clw_ref, clb_ref, shift_ref, *rest,
                   n_r, n_c, lora, n_heads, rope, scale, per_b, w_lora, a_lora, layer):
    g_ref, qn_ref, kvn_ref, cb_ref, clw_ref, clb_ref = _pick_row(
        (g_ref, qn_ref, kvn_ref, cb_ref, clw_ref, clb_ref), layer)
    rwkv_in = _pick_row(rest[:8], layer) + list(rest[8:10])
    ya_ref, yb_ref, q_ref, k_ref, v_ref, hbuf, zbuf, st_ref = rest[10:18]
    rwkv_scratch = rest[18:]

    @pl.when(pl.program_id(0) == 0)
    def _():
        hbuf[0:CONV_HALO, :] = jnp.zeros((CONV_HALO, hbuf.shape[1]), F32)
        zbuf[0:8, :] = jnp.zeros((8, zbuf.shape[1]), F32)
        st_ref[...] = jnp.zeros(st_ref.shape, F32)

    ends_sequence = (pl.program_id(0) + 1) % per_b == 0

    h = _rms(x_ref[...]) * g_ref[...]
    h = (h * (1.0 + mod_ref[0, 1:2, :]) + mod_ref[0, 0:1, :]).astype(BF16)
    p_rwkv = _dot(h, w_ref[:, :n_r])
    p_conv = _dot(h, w_ref[:, n_r:n_r + n_c])

    def mla_steps():
        lat = _dot(h, w_ref[:, n_r + n_c:n_r + n_c + 2 * lora])
        k_rot = _dot(h, wx_ref[...])
        cs_half = cs_ref[...].T
        cs = jnp.concatenate([cs_half, cs_half], axis=1)
        qn = (_rms(lat[:, 0:lora]) * qn_ref[...]).astype(BF16)
        kvn = (_rms(lat[:, lora:]) * kvn_ref[...]).astype(BF16)
        lane = lax.broadcasted_iota(jnp.int32, cs.shape, 1)
        t = k_rot * cs
        kpe = jnp.where(lane < HEAD, 0.0, t + pltpu.roll(t, rope, 1) + pltpu.roll(t, LANES - rope, 1))
        q_mul = jnp.where(lane < HEAD, 1.0, cs) * scale
        q_mul2 = jnp.concatenate([q_mul, q_mul], axis=1)
        kpe2 = jnp.concatenate([kpe, kpe], axis=1)
        is_v = jnp.concatenate([lane < HEAD, lane < HEAD], axis=1)
        yield
        for j in range(n_heads // 2):
            cols = slice(2 * j * LANES, 2 * (j + 1) * LANES)
            q_ref[:, cols] = (_dot(qn, wuq_ref[:, cols]) * q_mul2).astype(BF16)
            k_ref[:, cols] = (_dot(kvn, wuk_ref[:, cols]) + kpe2).astype(BF16)
            v_ref[:, cols] = jnp.where(is_v, _dot(kvn, wuv_ref[:, cols]), 1.0).astype(BF16)
            yield

    fill = itertools.chain(mla_steps(),
                           _conv_steps(p_conv, ends_sequence, cw_ref, cb_ref, clw_ref, clb_ref, shift_ref, yb_ref, hbuf))
    _rwkv_tile(p_rwkv, ends_sequence, *rwkv_in, ya_ref, zbuf, st_ref, *rwkv_scratch,
               w_lora=w_lora, a_lora=a_lora, fill=fill)
    for _ in fill:
        pass


def _layer_spec(a, layer):
    if a.ndim == 2:
        return pl.BlockSpec(a.shape, lambda i: (0, 0))
    return pl.BlockSpec((None,) + a.shape[1:], lambda i: (layer,) + (0,) * (a.ndim - 1))


def _pick_row(refs, layer):
    return [r.at[layer:layer + 1, :] for r in refs]


def _inproj(x, layer, mod_all, cs, layered, shared, seq, n_r, n_c, lora, n_heads, rope, scale, r_dim, w_lora, a_lora):
    n, d = x.shape
    tm = CHUNK * GROUP_CHUNKS
    per_b = seq // tm
    row = lambda w: pl.BlockSpec((tm, w), lambda i: (i, 0))
    full = lambda a: pl.BlockSpec(a.shape, lambda i: (0, 0))
    act = pltpu.VMEM((tm, r_dim), F32)
    g_w_wx, mla, conv, rwkv = layered
    conv_shift, bd = shared
    return pl.pallas_call(
        functools.partial(_inproj_kernel, n_r=n_r, n_c=n_c, lora=lora, n_heads=n_heads, rope=rope, scale=scale,
                          per_b=per_b, w_lora=w_lora, a_lora=a_lora, layer=layer),
        grid=(n // tm,),
        in_specs=[row(d), pl.BlockSpec((None, 1, 6, d), lambda i: (layer, i // per_b, 0, 0))]
                 + [_layer_spec(a, layer) for a in g_w_wx] + [pl.BlockSpec((cs.shape[0], tm), lambda i: (0, i))]
                 + [_layer_spec(a, layer) for a in mla]
                 + [_layer_spec(a, layer) for a in conv] + [full(conv_shift)]
                 + [_layer_spec(a, layer) for a in rwkv] + [full(bd)],
        out_specs=[row(r_dim), row(n_c // 2)] + [row(n_heads * LANES)] * 3,
        out_shape=[jax.ShapeDtypeStruct((n, r_dim), BF16), jax.ShapeDtypeStruct((n, n_c // 2), BF16)]
                  + [jax.ShapeDtypeStruct((n, n_heads * LANES), BF16)] * 3,
        scratch_shapes=[pltpu.VMEM((tm + CONV_HALO, n_c // 2), F32),
                        pltpu.VMEM((tm + 8, n_r), F32),
                        pltpu.VMEM((r_dim // LANES, LANES, LANES), F32),
                        act, act, act, act, act, act, act],
        compiler_params=_params("arbitrary"),
        name="in_proj",
    )(x, mod_all, *g_w_wx, cs, *mla, *conv, conv_shift, *rwkv, bd)


def _rwkv_tile(p, ends_sequence, mu_ref, w0_ref, a0_ref, kk_ref, ka_ref, rk_ref, lnw_ref, lnb_ref, lora_ref, bd_ref,
               o_ref, zbuf, st_ref, r_s, k_s, v_s, a_s, b_s, ld_s, y_s, *, w_lora, a_lora, fill):
    tt, dim = o_ref.shape
    assert tt == CHUNK * GROUP_CHUNKS

    def gap():
        next(fill, None)

    n_pairs = dim // LANES
    c = CHUNK

    zbuf[8:8 + tt, :] = p
    prev = zbuf[7:7 + tt, :]
    zbuf[7:8, :] = jnp.where(ends_sequence, 0.0, zbuf[tt + 7:tt + 8, :])
    z = p + (prev - p) * mu_ref[...]

    r = z[:, 0:dim]
    k = z[:, dim:2 * dim]
    v = z[:, 2 * dim:3 * dim]
    lo = z[:, 3 * dim:3 * dim + LANES]
    lane = lax.broadcasted_iota(jnp.int32, lo.shape, 1)
    act = jnp.where(lane < w_lora, jnp.tanh(lo), jnp.where(lane < w_lora + a_lora, lo, _sigmoid(lo)))
    lora = _dot(_bf(act), lora_ref[...])
    x_w = w0_ref[...] + lora[:, 0:dim]
    w_log = -(jnp.maximum(-x_w, 0.0) + jnp.log(1.0 + jnp.exp(-jnp.abs(x_w)))) - 0.5
    a = _sigmoid(a0_ref[...] + lora[:, dim:2 * dim])
    g = lora[:, 2 * dim:3 * dim]
    bd = bd_ref[...]
    kk = k * kk_ref[...]
    kk = kk * lax.rsqrt(jnp.maximum(_head_sums(kk * kk, bd), 1e-24))
    k = k * (1.0 + (a - 1.0) * ka_ref[...])
    r_s[...] = r
    k_s[...] = k
    v_s[...] = v
    a_s[...] = -kk
    b_s[...] = kk * a
    ld_s[...] = -jnp.exp(w_log)

    row_c = lax.broadcasted_iota(jnp.int32, (c, c), 0)
    col_c = lax.broadcasted_iota(jnp.int32, (c, c), 1)
    tri_incl = (row_c >= col_c).astype(BF16)
    head0 = lax.broadcasted_iota(jnp.int32, (c, LANES), 1) < HEAD
    lane_2 = lax.broadcasted_iota(jnp.int32, (c, 2 * LANES), 1)
    head0_2 = jnp.where(lane_2 >= LANES, lane_2 - LANES, lane_2) < HEAD
    row_4 = lax.broadcasted_iota(jnp.int32, (c, 4 * c), 0)
    col_4 = jnp.bitwise_and(lax.broadcasted_iota(jnp.int32, (c, 4 * c), 1), c - 1)
    strict4 = row_4 > col_4
    incl4 = row_4 >= col_4
    eye4 = (row_4 == col_4).astype(F32)
    bd4 = jnp.bitwise_xor(lax.broadcasted_iota(jnp.int32, (4 * c, 4 * c), 0),
                          lax.broadcasted_iota(jnp.int32, (4 * c, 4 * c), 1)) < c
    row_p = lax.broadcasted_iota(jnp.int32, (LANES, LANES), 0)
    col_p = lax.broadcasted_iota(jnp.int32, (LANES, LANES), 1)
    same_head = (row_p < HEAD) == (col_p < HEAD)
    diag_p = row_p == col_p
    zeros_cl = jnp.zeros((c, LANES), BF16)

    def group_body():
        items = [(slice(q * c, (q + 1) * c), slice(j * LANES, (j + 1) * LANES), j)
                 for q in range(GROUP_CHUNKS) for j in range(n_pairs)]
        n_it = len(items)
        loaded = [[ref[rows, lanes] for ref in (r_s, k_s, v_s, a_s, b_s, ld_s)] for rows, lanes, _ in items]
        r_c, k_c, v_c, a_c, b_c, ld_c = [list(t) for t in zip(*loaded)]
        cum = []
        for i in range(n_it):
            ld_hi, ld_lo = _split(ld_c[i])
            both = _dot(tri_incl, jnp.concatenate([ld_hi, ld_lo], axis=1))
            cum.append(both[:, 0:LANES] + both[:, LANES:])
        gap()
        cum_end = [x[c - 1:c, :] for x in cum]
        at = [a_c[i] * jnp.exp(cum[i] - ld_c[i]) for i in range(n_it)]
        rt = [r_c[i] * jnp.exp(cum[i]) for i in range(n_it)]
        v_bf = [_bf(x) for x in v_c]
        def by_head(x, mask):
            return jnp.concatenate([jnp.where(mask, x, 0), jnp.where(mask, 0, x)], axis=0)

        aa = []
        for i in range(n_it):
            e_neg = jnp.exp(-cum[i])
            rhs = jnp.concatenate([by_head(_bf(b_c[i] * e_neg), head0), by_head(_bf(k_c[i] * e_neg), head0)], axis=0)
            aa.append(_dot_nt(_bf(jnp.concatenate([at[i], rt[i]], axis=0)), rhs))
        gap()
        t_cat = [jnp.where(strict4, x[0:c, :], 0.0) for x in aa]
        q_cat = [_bf(jnp.where(incl4, x[c:2 * c, :], 0.0)) for x in aa]
        akv = [_dot(_bf(t_cat[i][:, LANES:]), by_head(v_bf[i], head0)) for i in range(n_it)]
        n_grp = n_it // 2

        def bdiag(x):
            return jnp.where(bd4, jnp.concatenate([x, x, x, x], axis=0), 0)

        p4 = [jnp.concatenate([t_cat[2 * g][:, 0:LANES], t_cat[2 * g + 1][:, 0:LANES]], axis=1) for g in range(n_grp)]
        inv4 = [eye4 + x for x in p4]
        p4_bf = [_bf(x) for x in p4]
        pw = [_bf(_dot(x, bdiag(x))) for x in p4_bf]
        gap()
        for _ in range(4):
            both = [_dot(jnp.concatenate([pw[g], _bf(inv4[g])], axis=0), bdiag(pw[g])) for g in range(n_grp)]
            inv4 = [inv4[g] + both[g][c:, :] for g in range(n_grp)]
            pw = [_bf(both[g][0:c, :]) for g in range(n_grp)]
            gap()
        inv4 = [_bf(inv4[g] + _dot(_bf(inv4[g]), bdiag(pw[g]))) for g in range(n_grp)]
        inv_cat = [inv4[i // 2][:, (i % 2) * LANES:(i % 2 + 1) * LANES] for i in range(n_it)]
        x_mat = [_bf(jnp.concatenate([at[i], akv[i]], axis=1)) for i in range(n_it)]
        tx = [_dot(inv_cat[i], by_head(x_mat[i], head0_2)) for i in range(n_it)]
        gap()
        tx_bf = [_bf(x) for x in tx]
        v_pad = [jnp.concatenate([zeros_cl, v_bf[i]], axis=1) for i in range(n_it)]
        w_mat = [jnp.concatenate([tx_bf[i], v_pad[i]], axis=0) for i in range(n_it)]
        qw = [_dot(q_cat[i], jnp.concatenate([by_head(tx_bf[i], head0_2), by_head(v_pad[i], head0_2)], axis=0))
              for i in range(n_it)]
        mn = []
        for i in range(n_it):
            e_end = jnp.exp(cum_end[i] - cum[i])
            bk_t = _bf(jnp.concatenate([b_c[i] * e_end, k_c[i] * e_end], axis=0).T)
            mn.append(_dot(bk_t, w_mat[i]))
        gap()
        g_col = [jnp.sum(jnp.where(diag_p, jnp.exp(cum_end[i]), 0.0), axis=1, keepdims=True) for i in range(n_it)]
        states = [st_ref[j] for j in range(n_pairs)]
        for i, (rows, lanes, j) in enumerate(items):
            st = states[j]
            st_bf = _bf(st)
            y_s[rows, lanes] = _dot(_bf(rt[i] + qw[i][:, 0:LANES]), st_bf) + qw[i][:, LANES:]
            m_low = _bf(jnp.where(same_head, mn[i][:, 0:LANES], 0.0))
            states[j] = g_col[i] * st + _dot(m_low, st_bf) + jnp.where(same_head, mn[i][:, LANES:], 0.0)
        for j in range(n_pairs):
            st_ref[j] = jnp.where(ends_sequence, 0.0, states[j])

    group_body()

    y = y_s[...]
    inv_n = 1.0 / HEAD
    mean = _head_sums(y, bd) * inv_n
    d = y - mean
    var = _head_sums(d * d, bd) * inv_n
    y = d * lax.rsqrt(var + RWKV_GN_EPS) * lnw_ref[...] + lnb_ref[...]
    r = r_s[...]
    bonus = _head_sums(r * k_s[...] * rk_ref[...], bd) * v_s[...]
    o_ref[...] = ((y + bonus) * g).astype(o_ref.dtype)


def _conv_steps(p_conv, ends_sequence, w_ref, b_ref, lnw_ref, lnb_ref, shift_ref, o_ref, hbuf):
    tt, dim = o_ref.shape
    sub = CONV_SUB
    ext = sub + 8

    hbuf[CONV_HALO:CONV_HALO + tt, :] = p_conv[:, 0:dim] * _sigmoid(p_conv[:, dim:2 * dim])
    yield
    for s in range(tt // sub):
        base = CONV_HALO + s * sub - 8
        wins = [hbuf[base - 8 * a:base - 8 * a + ext, :] for a in range(CONV_HALO // 8)]
        parts = []
        for r in range(8):
            lags = [8 * a + r for a in range(len(wins)) if 8 * a + r < CONV_WIDTH]
            terms = [wins[m // 8] * w_ref[CONV_WIDTH - 1 - m:CONV_WIDTH - m, :] for m in lags]
            parts.append(functools.reduce(jnp.add, terms))
        stacked = _bf(jnp.concatenate(parts, axis=0))
        acc = _dot(shift_ref[...], stacked) + b_ref[...]
        mu = jnp.mean(acc, axis=-1, keepdims=True)
        d = acc - mu
        var = jnp.mean(d * d, axis=-1, keepdims=True)
        y = d * lax.rsqrt(var + LN_EPS) * lnw_ref[...] + lnb_ref[...]
        o_ref[s * sub:(s + 1) * sub, :] = (y * _sigmoid(y)).astype(o_ref.dtype)
        if s + 1 < tt // sub:
            yield
    hbuf[0:CONV_HALO, :] = jnp.where(ends_sequence, 0.0, hbuf[tt:tt + CONV_HALO, :])


def _conv_shift_matrix():
    ext = CONV_SUB + 8
    u_idx = jnp.arange(CONV_SUB)[:, None]
    col = jnp.arange(8 * ext)[None, :]
    return (col % ext == u_idx + 8 - col // ext).astype(BF16)


def _flash_kernel(q_ref, k_ref, v_ref, o_ref, m_s, acc_s, s_buf, p_buf, al_buf):
    i = pl.program_id(2)
    t = FLASH_BLOCK
    m_s[...] = jnp.full(m_s.shape, -jnp.inf, F32)
    acc_s[...] = jnp.zeros(acc_s.shape, F32)
    p_buf[1] = jnp.zeros(p_buf.shape[1:], BF16)
    al_buf[1] = jnp.ones(al_buf.shape[1:], F32)
    causal = lax.broadcasted_iota(jnp.int32, (t, t), 0) <= lax.broadcasted_iota(jnp.int32, (t, t), 1)

    def keys_of(j):
        return pl.ds(pl.multiple_of(j * t, t), t)

    def scores(j, slot, u0):
        for h in range(2):
            cols = slice(h * LANES, (h + 1) * LANES)
            s_buf[slot, h, :, u0 * t:] = _dot_nt(k_ref[keys_of(j), cols], q_ref[u0 * t:, cols])

    def softmax(slot, u0, masked_sub):
        for u in range(u0, FLASH_SUBS):
            qs = slice(u * t, (u + 1) * t)
            for h in range(2):
                s = s_buf[slot, h, :, qs]
                if u == masked_sub:
                    s = jnp.where(causal, s, -jnp.inf)
                m_prev = m_s[h, :, qs]
                m_new = jnp.maximum(m_prev, jnp.max(s, axis=0, keepdims=True))
                m_s[h, :, qs] = m_new
                al_buf[slot, h, :, qs] = jnp.exp2(m_prev - m_new)
                p_buf[slot, h, :, qs] = jnp.exp2((s - m_new).astype(BF16))

    def accumulate(j, slot, u0):
        for h in range(2):
            v_blk = v_ref[keys_of(j), h * LANES:h * LANES + FLASH_VROWS]
            pv = lax.dot_general(v_blk, p_buf[slot, h, :, u0 * t:], (((0,), (0,)), ((), ())),
                                 preferred_element_type=F32)
            acc_s[h, :, u0 * t:] = al_buf[slot, h, :, u0 * t:] * acc_s[h, :, u0 * t:] + pv

    n_full = FLASH_SUBS * i
    scores(0, 0, 0)

    def body(g, carry):
        for par in range(2):
            k = 2 * g + par
            scores(k + 1, 1 - par, 0)
            accumulate(jnp.maximum(k - 1, 0), 1 - par, 0)
            softmax(par, 0, None)
        return carry

    lax.fori_loop(0, n_full // 2, body, 0)
    accumulate(jnp.maximum(n_full - 1, 0), 1, 0)
    for d in range(FLASH_SUBS):
        if d + 1 < FLASH_SUBS:
            scores(n_full + d + 1, (d + 1) % 2, d + 1)
        softmax(d % 2, d, d)
        accumulate(n_full + d, d % 2, d)
    out_t = jnp.concatenate([acc_s[h, 0:HEAD, :] / acc_s[h, HEAD:HEAD + 1, :] for h in range(2)], axis=0)
    o_ref[...] = out_t.T.astype(o_ref.dtype)


def _flash(q, k, v, batch, seq, n_heads):
    n = q.shape[0]
    tq = FLASH_SUBS * FLASH_BLOCK
    n_q = seq // tq
    return pl.pallas_call(
        _flash_kernel,
        grid=(batch, n_heads // 2, n_q),
        in_specs=[pl.BlockSpec((tq, 2 * LANES), lambda b, h, i: (b * n_q + i, h)),
                  pl.BlockSpec((seq, 2 * LANES), lambda b, h, i: (b, h)),
                  pl.BlockSpec((seq, 2 * LANES), lambda b, h, i: (b, h))],
        out_specs=pl.BlockSpec((tq, LANES), lambda b, h, i: (b * n_q + i, h)),
        out_shape=jax.ShapeDtypeStruct((n, n_heads * HEAD), BF16),
        scratch_shapes=[pltpu.VMEM((2, 1, tq), F32),
                        pltpu.VMEM((2, FLASH_VROWS, tq), F32),
                        pltpu.VMEM((2, 2, FLASH_BLOCK, tq), F32),
                        pltpu.VMEM((2, 2, FLASH_BLOCK, tq), BF16),
                        pltpu.VMEM((2, 2, 1, tq), F32)],
        compiler_params=_params("parallel", "parallel", "arbitrary"),
        name="mla_flash",
    )(q, k, v)


def _outffn_kernel(ya_ref, yb_ref, yc_ref, x_ref, mod_ref, gpm_ref, gpf_ref, gqf_ref, wo_ref, w1_ref, w2_ref,
                   o_ref, x1_s, hid_s, wo_s, w1_s, w2_s, *, da, db, tf, layer):
    step = pl.program_id(0)

    @pl.when(step < FFN_CAST_STEPS)
    def _():
        n_o, n_1, n_2 = wo_ref.shape[0], w1_ref.shape[1], w2_ref.shape[0]
        wo_s[pl.ds(pl.multiple_of(step * n_o, n_o), n_o), :] = wo_ref[...].astype(BF16)
        w1_s[:, pl.ds(pl.multiple_of(step * n_1, n_1), n_1)] = w1_ref[...].astype(BF16)
        w2_s[pl.ds(pl.multiple_of(step * n_2, n_2), n_2), :] = w2_ref[...].astype(BF16)

    @pl.when(step >= FFN_CAST_STEPS)
    def _():
        gpm, gpf, gqf = _pick_row((gpm_ref, gpf_ref, gqf_ref), layer)
        half = x_ref.shape[0] // 2
        halves = [slice(0, half), slice(half, 2 * half)]
        ys = [_dot(ya_ref[r, :], wo_s[0:da, :]) + _dot(yb_ref[r, :], wo_s[da:da + db, :])
              + _dot(yc_ref[r, :], wo_s[da + db:, :]) for r in halves]
        hs = []
        for r, y in zip(halves, ys):
            x1 = x_ref[r, :] + mod_ref[0, 2:3, :] * (_rms(y) * gpm[...])
            x1_s[r, :] = x1
            h = _rms(x1) * gpf[...]
            hs.append((h * (1.0 + mod_ref[0, 4:5, :]) + mod_ref[0, 3:4, :]).astype(BF16))
        for f in range(hid_s.shape[1] // tf):
            cols = slice(f * tf, (f + 1) * tf)
            for r, h in zip(halves, hs):
                hid = jnp.maximum(_dot(h, w1_s[:, cols]), 0.0)
                hid_s[r, cols] = (hid * hid).astype(BF16)
        for r in halves:
            y2 = _dot(hid_s[r, :], w2_s[...])
            o_ref[r, :] = x1_s[r, :] + mod_ref[0, 5:6, :] * (_rms(y2) * gqf[...])


def _outffn(ya, yb, yc, x, layer, mod_all, gains, w_out, w_ff1, w_ff2, seq):
    n, d = x.shape
    d_ff = w_ff1.shape[2]
    tm, tf = 512, 512
    per_b = seq // tm
    cast = FFN_CAST_STEPS
    tile = lambda i: jnp.maximum(i - cast, 0)
    chunk = lambda i: jnp.minimum(i, cast - 1)
    row = lambda w: pl.BlockSpec((tm, w), lambda i: (tile(i), 0))
    return pl.pallas_call(
        functools.partial(_outffn_kernel, da=ya.shape[1], db=yb.shape[1], tf=tf, layer=layer),
        grid=(cast + n // tm,),
        in_specs=[row(ya.shape[1]), row(yb.shape[1]), row(yc.shape[1]), row(d),
                  pl.BlockSpec((None, 1, 6, d), lambda i: (layer, tile(i) // per_b, 0, 0))]
                 + [_layer_spec(a, layer) for a in gains]
                 + [pl.BlockSpec((None, w_out.shape[1] // cast, d), lambda i: (layer, chunk(i), 0)),
                    pl.BlockSpec((None, d, d_ff // cast), lambda i: (layer, 0, chunk(i))),
                    pl.BlockSpec((None, d_ff // cast, d), lambda i: (layer, chunk(i), 0))],
        out_specs=row(d),
        out_shape=jax.ShapeDtypeStruct((n, d), F32),
        scratch_shapes=[pltpu.VMEM((tm, d), F32), pltpu.VMEM((tm, d_ff), BF16),
                        pltpu.VMEM(w_out.shape[1:], BF16), pltpu.VMEM(w_ff1.shape[1:], BF16),
                        pltpu.VMEM(w_ff2.shape[1:], BF16)],
        compiler_params=_params("arbitrary"),
        name="out_ffn",
    )(ya, yb, yc, x, mod_all, *gains, w_out, w_ff1, w_ff2)


def _rot_cols(w):
    half = w.shape[-1] // 2
    return jnp.concatenate([-w[..., half:], w[..., :half]], axis=-1)


def kernel(x, c, positions, g_pre_mix, g_post_mix, g_pre_ffn, g_post_ffn, w_ada, b_ada, w_in, w_out, rwkv_mu, rwkv_w0, rwkv_w2, rwkv_a0, rwkv_a2, rwkv_g2, rwkv_k_k, rwkv_k_a, rwkv_r_k, rwkv_ln_w, rwkv_ln_b, conv_w, conv_b, conv_ln_w, conv_ln_b, mla_q_norm, mla_w_uq, mla_kv_norm, mla_w_ukv, w_ff1, w_ff2):
    batch, seq, d = x.shape
    n = batch * seq
    n_layers = w_ada.shape[0]
    r_dim = rwkv_w0.shape[1]
    w_lora, a_lora, g_lora = rwkv_w2.shape[1], rwkv_a2.shape[1], rwkv_g2.shape[1]
    assert w_lora + a_lora + g_lora == LANES and r_dim % LANES == 0
    n_r = 3 * r_dim + LANES
    c_dim = conv_w.shape[2]
    n_c = 2 * c_dim
    lora = mla_q_norm.shape[1]
    n_heads = mla_w_ukv.shape[2] // (2 * HEAD)
    rope = mla_w_uq.shape[2] // n_heads - HEAD
    assert HEAD + 2 * rope == LANES
    scale = float(HEAD + rope) ** -0.5 * LOG2_E

    xf = x.reshape(n, d)
    c8 = jnp.pad(c, ((0, 8 - batch), (0, 0)))
    mod_all = _ada(c8, w_ada, b_ada.reshape(n_layers, 1, 6 * d))[:, :batch].reshape(n_layers, batch, 6, d)

    inv_freq = 1.0 / (ROPE_THETA ** (jnp.arange(0, rope, 2, dtype=F32) / rope))
    invf = jnp.tile(inv_freq, 2).reshape(rope, 1)
    phase = jnp.where(jnp.arange(rope) < rope // 2, 0.0, -0.5 * jnp.pi).astype(F32).reshape(rope, 1)
    cs = _rope_table(positions.reshape(1, n), invf, phase)

    head_id = jnp.arange(2 * LANES) // HEAD
    bd = (head_id[:, None] == head_id[None, :]).astype(BF16)
    conv_shift = _conv_shift_matrix()

    vec = lambda a: a.reshape(n_layers, -1)
    w_in_bf = w_in.astype(BF16)
    k_pe = w_in_bf[:, :, n_r + n_c + 2 * lora:]
    w_extra = jnp.concatenate([jnp.zeros((n_layers, d, HEAD), BF16), k_pe, _rot_cols(k_pe)], axis=-1)
    wq = mla_w_uq.reshape(n_layers, lora, n_heads, HEAD + rope)
    wuq = jnp.concatenate([wq, _rot_cols(wq[..., HEAD:])], axis=-1).reshape(n_layers, lora, n_heads * LANES)
    wkv = mla_w_ukv.reshape(n_layers, lora, n_heads, 2 * HEAD)
    pad = jnp.zeros_like(wkv[..., :HEAD])
    wuk = jnp.concatenate([wkv[..., :HEAD], pad], axis=-1).reshape(n_layers, lora, n_heads * LANES)
    wuv = jnp.concatenate([wkv[..., HEAD:], pad], axis=-1).reshape(n_layers, lora, n_heads * LANES)
    lora_w = jnp.zeros((n_layers, LANES, 3 * r_dim), F32)
    lora_w = lora_w.at[:, 0:w_lora, 0:r_dim].set(rwkv_w2)
    lora_w = lora_w.at[:, w_lora:w_lora + a_lora, r_dim:2 * r_dim].set(rwkv_a2)
    lora_w = lora_w.at[:, w_lora + a_lora:, 2 * r_dim:].set(rwkv_g2)
    layered = ([vec(g_pre_mix), w_in_bf, w_extra],
               [vec(mla_q_norm), vec(mla_kv_norm), wuq.astype(BF16), wuk.astype(BF16), wuv.astype(BF16)],
               [conv_w, vec(conv_b), vec(conv_ln_w), vec(conv_ln_b)],
               [vec(rwkv_mu), vec(rwkv_w0), vec(rwkv_a0), vec(rwkv_k_k), vec(rwkv_k_a), vec(rwkv_r_k),
                vec(rwkv_ln_w), vec(rwkv_ln_b), lora_w.astype(BF16)])
    gains = [vec(g_post_mix), vec(g_pre_ffn), vec(g_post_ffn)]
    for l in range(n_layers):
        y_a, y_b, q, k, v = _inproj(xf, l, mod_all, cs, layered, (conv_shift, bd),
                                    seq, n_r, n_c, lora, n_heads, rope, scale, r_dim, w_lora, a_lora)
        y_c = _flash(q, k, v, batch, seq, n_heads)
        xf = _outffn(y_a, y_b, y_c, xf, l, mod_all, gains, w_out, w_ff1, w_ff2, seq)
    return xf.reshape(batch, seq, d)
```
